```python
import jax
import jax.numpy as jnp
from jax import lax
import numpy as np

D_MODEL = 2048
BATCH = 4
SEQ = 2048
DEPTH = 1
DEC_BATCH = 128
DEC_SEQ = 8
PAST_LEN = 16384
PAGE_SIZE = 128

MEM_LEN = 256
CHUNK = 128
A_HEADS = 4
A_HEAD_DIM = 128
A_WIDTH = A_HEADS * A_HEAD_DIM
B_HEADS = 16
B_HEAD_DIM = 64
B_WIDTH = B_HEADS * B_HEAD_DIM
DECAY_LORA = 64
ICLR_LORA = 64
GATE_LORA = 64
B_PROJ = 3 * B_WIDTH + DECAY_LORA + ICLR_LORA + GATE_LORA
C_HEADS = 4
C_HEAD_DIM = 128
C_WIDTH = C_HEADS * C_HEAD_DIM
MIX_WIDTH = A_WIDTH + B_WIDTH + C_WIDTH
IN_COLS = 2 * A_WIDTH + B_PROJ + C_WIDTH
D_FF = 5632
CONV_WIDTH = 3
RMS_EPS = 1e-6
LN_EPS = 1e-5
GN_EPS = 64e-5
DECAY_OFFSET = 0.5

kernel_name = 'hymba_gmlp_rwkv7_memxattn_convffn_step'


def rms_norm(x, g):
    xf = x.astype(jnp.float32)
    y = xf * lax.rsqrt(jnp.mean(xf * xf, axis=-1, keepdims=True) + RMS_EPS)
    return (y * g.astype(jnp.float32)).astype(x.dtype)


def layer_norm(x, g, b):
    xf = x.astype(jnp.float32)
    mean = jnp.mean(xf, axis=-1, keepdims=True)
    var = jnp.mean(jnp.square(xf - mean), axis=-1, keepdims=True)
    y = (xf - mean) * lax.rsqrt(var + LN_EPS)
    return (y * g.astype(jnp.float32) + b.astype(jnp.float32)).astype(x.dtype)


def chunk_spatial_gate(u, v, w_s, b_s):
    Bn, L, H, Dh = v.shape
    n_chunks = -(-L // CHUNK)
    pad = n_chunks * CHUNK - L
    vp = jnp.pad(v, ((0, 0), (0, pad), (0, 0), (0, 0))).reshape(Bn, n_chunks, CHUNK, H, Dh)
    causal = jnp.tril(jnp.ones((CHUNK, CHUNK), dtype=bool))
    w = jnp.where(causal[None], w_s, 0.0).astype(v.dtype)
    mixed = jnp.einsum('hts,bnshd->bnthd', w, vp) + b_s.T.astype(v.dtype)[None, None, :, :, None]
    mixed = mixed.reshape(Bn, n_chunks * CHUNK, H, Dh)[:, :L]
    return u * mixed


def wkv7_scan(r, w, k, v, a_vec, b_vec, state):
    def step(S, inp):
        r_t, w_t, k_t, v_t, a_t, b_t = inp
        sa = jnp.einsum('bhij,bhj->bhi', S, a_t)
        S = S * w_t[:, :, None, :] + sa[..., None] * b_t[:, :, None, :] + v_t[..., None] * k_t[:, :, None, :]
        y = jnp.einsum('bhij,bhj->bhi', S, r_t)
        return S, y
    xs = tuple(jnp.moveaxis(t, 1, 0) for t in (r, w, k, v, a_vec, b_vec))
    S, ys = lax.scan(step, state, xs)
    return jnp.moveaxis(ys, 0, 1), S


def rwkv7_time_mix(p, shift_prev, wkv_prev, mu, w0, w2, a0, a2, g2, k_k, k_a, r_k, lnx_g, lnx_b):
    Bn, L, _ = p.shape
    f32 = lambda t: t.astype(jnp.float32)
    pf = f32(p)
    prev = jnp.concatenate([f32(shift_prev)[:, None], pf[:, :-1]], axis=1)
    ps = pf + (prev - pf) * f32(mu)
    o1, o2, o3 = B_WIDTH, 2 * B_WIDTH, 3 * B_WIDTH
    o4, o5 = o3 + DECAY_LORA, o3 + DECAY_LORA + ICLR_LORA
    r, k, v = ps[..., :o1], ps[..., o1:o2], ps[..., o2:o3]
    dw, da, dg = ps[..., o3:o4], ps[..., o4:o5], ps[..., o5:]
    w_log = -jax.nn.softplus(-(f32(w0) + jnp.tanh(dw) @ f32(w2))) - DECAY_OFFSET
    decay = jnp.exp(-jnp.exp(w_log))
    a = jax.nn.sigmoid(f32(a0) + da @ f32(a2))
    g = jax.nn.sigmoid(dg) @ f32(g2)
    heads = lambda t: t.reshape(Bn, L, B_HEADS, B_HEAD_DIM)
    kk = heads(k * f32(k_k))
    kk = kk / jnp.maximum(jnp.sqrt(jnp.sum(kk * kk, axis=-1, keepdims=True)), 1e-12)
    k = k * (1.0 + (a - 1.0) * f32(k_a))
    rh, kh, vh, wh, ah = heads(r), heads(k), heads(v), heads(decay), heads(a)
    y, wkv_new = wkv7_scan(rh, wh, kh, vh, -kk, kk * ah, f32(wkv_prev))
    mean = jnp.mean(y, axis=-1, keepdims=True)
    var = jnp.mean(jnp.square(y - mean), axis=-1, keepdims=True)
    y = ((y - mean) * lax.rsqrt(var + GN_EPS)).reshape(Bn, L, B_WIDTH) * f32(lnx_g) + f32(lnx_b)
    bonus = jnp.sum(rh * kh * f32(r_k), axis=-1, keepdims=True) * vh
    y = (y + bonus.reshape(Bn, L, B_WIDTH)) * g
    return y.astype(p.dtype), p[:, -1], wkv_new


def memory_kv(mem, g_mem, w_mk, w_mv):
    Bn, M, _ = mem.shape
    m = rms_norm(mem, g_mem)
    k = (m @ w_mk).reshape(Bn, M, C_HEADS, C_HEAD_DIM)
    v = (m @ w_mv).reshape(Bn, M, C_HEADS, C_HEAD_DIM)
    return k, v


def memory_attend(q, mem_k, mem_v):
    s = jnp.einsum('blhd,bmhd->bhlm', q, mem_k.astype(q.dtype)).astype(jnp.float32) * (C_HEAD_DIM ** -0.5)
    p = jax.nn.softmax(s, axis=-1).astype(q.dtype)
    return jnp.einsum('bhlm,bmhd->blhd', p, mem_v.astype(q.dtype))


def mixing_sublayer(x, mem_k, mem_v, shift_prev, wkv_prev, lw):
    Bn, L, _ = x.shape
    h = rms_norm(x, lw['norm_mix_pre'])
    proj = h @ lw['w_in']
    a_u = proj[..., :A_WIDTH]
    a_v = proj[..., A_WIDTH:2 * A_WIDTH]
    b_p = proj[..., 2 * A_WIDTH:2 * A_WIDTH + B_PROJ]
    c_q = proj[..., 2 * A_WIDTH + B_PROJ:]
    a_u = jax.nn.gelu(a_u, approximate=True).reshape(Bn, L, A_HEADS, A_HEAD_DIM)
    a_v = layer_norm(jax.nn.gelu(a_v, approximate=True), lw['gm_ln_g'], lw['gm_ln_b'])
    a_v = a_v.reshape(Bn, L, A_HEADS, A_HEAD_DIM)
    a_out = chunk_spatial_gate(a_u, a_v, lw['gm_ws'], lw['gm_bs']).reshape(Bn, L, A_WIDTH)
    chunk_start = ((L - 1) // CHUNK) * CHUNK
    chunk_v = a_v[:, chunk_start:]
    b_out, shift_new, wkv_new = rwkv7_time_mix(
        b_p, shift_prev, wkv_prev, lw['rk_mu'], lw['rk_w0'], lw['rk_w2'], lw['rk_a0'], lw['rk_a2'],
        lw['rk_g2'], lw['rk_kk'], lw['rk_ka'], lw['rk_rk'], lw['rk_lnx_g'], lw['rk_lnx_b'])
    c_out = memory_attend(c_q.reshape(Bn, L, C_HEADS, C_HEAD_DIM), mem_k, mem_v).reshape(Bn, L, C_WIDTH)
    mixed = jnp.concatenate([a_out, b_out.astype(x.dtype), c_out], axis=-1)
    out = rms_norm(mixed @ lw['w_out'], lw['norm_mix_post'])
    return x + out, chunk_v, shift_new, wkv_new


def conv_ffn_sublayer(x, conv_prev, lw):
    L = x.shape[1]
    h = rms_norm(x, lw['norm_ffn_pre'])
    up = h @ lw['ffn_w_up']
    ext = jnp.concatenate([conv_prev.astype(up.dtype), up], axis=1)
    cw = lw['ffn_conv_w'].astype(up.dtype)
    conv = lw['ffn_conv_b'].astype(up.dtype)
    for i in range(CONV_WIDTH):
        conv = conv + ext[:, i:i + L] * cw[i]
    gate, val = conv[..., :D_FF], conv[..., D_FF:]
    act = jax.nn.gelu(gate, approximate=True) * val
    out = rms_norm(act @ lw['ffn_w_down'], lw['norm_ffn_post'])
    return x + out, ext[:, -(CONV_WIDTH - 1):]


def decoder_layer(x, mem_k, mem_v, shift_prev, wkv_prev, conv_prev, lw):
    x, chunk_v, shift_new, wkv_new = mixing_sublayer(x, mem_k, mem_v, shift_prev, wkv_prev, lw)
    x, conv_new = conv_ffn_sublayer(x, conv_prev, lw)
    return x, chunk_v, shift_new, wkv_new, conv_new


def setup_inputs(seed: int = 0) -> dict:
    key = jax.random.key(seed)
    keys = iter(jax.random.split(key, 48))

    def nrm(shape, scale):
        return jax.random.normal(next(keys), shape, jnp.float32) * scale

    def gain(shape):
        return 1.0 + nrm(shape, 0.02)

    return {
        'x_prompt': nrm((BATCH, SEQ, D_MODEL), 1.0),
        'x_sample': nrm((DEC_BATCH, DEC_SEQ, D_MODEL), 1.0),
        'mem_prompt': nrm((BATCH, MEM_LEN, D_MODEL), 1.0),
        'cache_mem_k': nrm((DEPTH, DEC_BATCH, MEM_LEN, C_HEADS, C_HEAD_DIM), 1.0),
        'cache_mem_v': nrm((DEPTH, DEC_BATCH, MEM_LEN, C_HEADS, C_HEAD_DIM), 1.0),
        'state_shift': nrm((DEPTH, DEC_BATCH, B_PROJ), 1.0),
        'state_wkv': nrm((DEPTH, DEC_BATCH, B_HEADS, B_HEAD_DIM, B_HEAD_DIM), 0.1),
        'state_conv': nrm((DEPTH, DEC_BATCH, CONV_WIDTH - 1, 2 * D_FF), 1.0),
        'norm_mix_pre': gain((DEPTH, D_MODEL)),
        'norm_mix_post': gain((DEPTH, D_MODEL)),
        'norm_ffn_pre': gain((DEPTH, D_MODEL)),
        'norm_ffn_post': gain((DEPTH, D_MODEL)),
        'norm_mem': gain((DEPTH, D_MODEL)),
        'w_in': nrm((DEPTH, D_MODEL, IN_COLS), D_MODEL ** -0.5),
        'w_out': nrm((DEPTH, MIX_WIDTH, D_MODEL), MIX_WIDTH ** -0.5),
        'w_mem_k': nrm((DEPTH, D_MODEL, C_WIDTH), D_MODEL ** -0.5),
        'w_mem_v': nrm((DEPTH, D_MODEL, C_WIDTH), D_MODEL ** -0.5),
        'gm_ln_g': gain((DEPTH, A_WIDTH)),
        'gm_ln_b': nrm((DEPTH, A_WIDTH), 0.02),
        'gm_ws': nrm((DEPTH, A_HEADS, CHUNK, CHUNK), CHUNK ** -0.5),
        'gm_bs': 1.0 + nrm((DEPTH, A_HEADS, CHUNK), 0.1),
        'rk_mu': jax.random.uniform(next(keys), (DEPTH, B_PROJ), jnp.float32),
        'rk_w0': -1.0 + nrm((DEPTH, B_WIDTH), 0.5),
        'rk_w2': nrm((DEPTH, DECAY_LORA, B_WIDTH), 0.1),
        'rk_a0': nrm((DEPTH, B_WIDTH), 0.1),
        'rk_a2': nrm((DEPTH, ICLR_LORA, B_WIDTH), 0.1),
        'rk_g2': nrm((DEPTH, GATE_LORA, B_WIDTH), GATE_LORA ** -0.5),
        'rk_kk': 0.85 + nrm((DEPTH, B_WIDTH), 0.02),
        'rk_ka': 1.0 + nrm((DEPTH, B_WIDTH), 0.02),
        'rk_rk': nrm((DEPTH, B_HEADS, B_HEAD_DIM), 0.1),
        'rk_lnx_g': gain((DEPTH, B_WIDTH)),
        'rk_lnx_b': nrm((DEPTH, B_WIDTH), 0.02),
        'ffn_w_up': nrm((DEPTH, D_MODEL, 2 * D_FF), D_MODEL ** -0.5),
        'ffn_conv_w': nrm((DEPTH, CONV_WIDTH, 2 * D_FF), CONV_WIDTH ** -0.5),
        'ffn_conv_b': nrm((DEPTH, 2 * D_FF), 0.02),
        'ffn_w_down': nrm((DEPTH, D_FF, D_MODEL), D_FF ** -0.5),
    }


def reference(x_prompt, x_sample, mem_prompt, cache_mem_k, cache_mem_v, state_shift, state_wkv, state_conv,
              norm_mix_pre, norm_mix_post, norm_ffn_pre, norm_ffn_post, norm_mem, w_in, w_out, w_mem_k, w_mem_v,
              gm_ln_g, gm_ln_b, gm_ws, gm_bs, rk_mu, rk_w0, rk_w2, rk_a0, rk_a2, rk_g2, rk_kk, rk_ka, rk_rk,
              rk_lnx_g, rk_lnx_b, ffn_w_up, ffn_conv_w, ffn_conv_b, ffn_w_down):
    Bp = x_prompt.shape[0]
    y_p, y_s = x_prompt, x_sample
    p_mk, p_mv, p_cv, p_sh, p_wkv, p_conv = [], [], [], [], [], []
    s_cv, s_sh, s_wkv, s_conv = [], [], [], []
    for l in range(DEPTH):
        lw = {
            'norm_mix_pre': norm_mix_pre[l], 'norm_mix_post': norm_mix_post[l],
            'norm_ffn_pre': norm_ffn_pre[l], 'norm_ffn_post': norm_ffn_post[l],
            'w_in': w_in[l], 'w_out': w_out[l],
            'gm_ln_g': gm_ln_g[l], 'gm_ln_b': gm_ln_b[l], 'gm_ws': gm_ws[l], 'gm_bs': gm_bs[l],
            'rk_mu': rk_mu[l], 'rk_w0': rk_w0[l], 'rk_w2': rk_w2[l], 'rk_a0': rk_a0[l], 'rk_a2': rk_a2[l],
            'rk_g2': rk_g2[l], 'rk_kk': rk_kk[l], 'rk_ka': rk_ka[l], 'rk_rk': rk_rk[l],
            'rk_lnx_g': rk_lnx_g[l], 'rk_lnx_b': rk_lnx_b[l],
            'ffn_w_up': ffn_w_up[l], 'ffn_conv_w': ffn_conv_w[l], 'ffn_conv_b': ffn_conv_b[l],
            'ffn_w_down': ffn_w_down[l],
        }
        mk, mv = memory_kv(mem_prompt, norm_mem[l], w_mem_k[l], w_mem_v[l])
        zero_shift = jnp.zeros((Bp, B_PROJ), x_prompt.dtype)
        zero_wkv = jnp.zeros((Bp, B_HEADS, B_HEAD_DIM, B_HEAD_DIM), jnp.float32)
        zero_conv = jnp.zeros((Bp, CONV_WIDTH - 1, 2 * D_FF), x_prompt.dtype)
        y_p, cv, sh, wkv, conv = decoder_layer(y_p, mk, mv, zero_shift, zero_wkv, zero_conv, lw)
        p_mk.append(mk); p_mv.append(mv); p_cv.append(cv); p_sh.append(sh); p_wkv.append(wkv); p_conv.append(conv)
        y_s, cv, sh, wkv, conv = decoder_layer(y_s, cache_mem_k[l], cache_mem_v[l], state_shift[l],
                                               state_wkv[l], state_conv[l], lw)
        s_cv.append(cv); s_sh.append(sh); s_wkv.append(wkv); s_conv.append(conv)
    return (y_p, y_s,
            jnp.stack(p_mk), jnp.stack(p_mv), jnp.stack(p_cv), jnp.stack(p_sh), jnp.stack(p_wkv), jnp.stack(p_conv),
            jnp.stack(s_cv), jnp.stack(s_sh), jnp.stack(s_wkv), jnp.stack(s_conv))
```

```python
import functools
import math

import jax
import jax.numpy as jnp
from jax import lax
from jax.experimental import pallas as pl
from jax.experimental.pallas import tpu as pltpu

D_MODEL = 2048
MEM_LEN = 256
CHUNK = 128
A_HEADS, A_HEAD_DIM = 4, 128
A_WIDTH = A_HEADS * A_HEAD_DIM
B_HEADS, B_HEAD_DIM = 16, 64
B_WIDTH = B_HEADS * B_HEAD_DIM
LORA = 64
B_PROJ = 3 * B_WIDTH + 3 * LORA
C_HEADS, C_HEAD_DIM = 4, 128
C_WIDTH = C_HEADS * C_HEAD_DIM
D_FF = 5632
RMS_EPS = 1e-6
LN_EPS = 1e-5
GN_EPS = 64e-5
DECAY_OFFSET = 0.5

LANES = 128
SUBLANES = 8
VMEM_LIMIT_BYTES = 56 * 1024 * 1024

LORA_PAD = 256
COL_A = 0
COL_R = 2 * A_WIDTH
COL_Q = COL_R + 3 * B_WIDTH
COL_L = COL_Q + C_WIDTH
IN_COLS_PAD = 5120
PROJ_TN = 1024

WKV_CHUNK = 64
FFN_TN = 512
FFN_N_TILES = D_FF // FFN_TN

F32 = jnp.float32
BF16 = jnp.bfloat16


def _cparams(sem):
    return pltpu.CompilerParams(dimension_semantics=sem, vmem_limit_bytes=VMEM_LIMIT_BYTES)


def _rms(x, g):
    return x * lax.rsqrt(jnp.mean(x * x, axis=-1, keepdims=True) + RMS_EPS) * g


def _norm_matmul_kernel(x_ref, g_ref, w_ref, o_ref, h_ref):
    @pl.when(pl.program_id(1) == 0)
    def _():
        h_ref[...] = _rms(x_ref[...], g_ref[...]).astype(BF16)

    o_ref[...] = jnp.dot(h_ref[...], w_ref[...], preferred_element_type=F32)


def _norm_matmul(x, g, w, *, tm, tn):
    t, k = x.shape
    n = w.shape[1]
    return pl.pallas_call(
        _norm_matmul_kernel,
        grid=(t // tm, n // tn),
        in_specs=[
            pl.BlockSpec((tm, k), lambda i, j: (i, 0)),
            pl.BlockSpec((1, k), lambda i, j: (0, 0)),
            pl.BlockSpec((k, tn), lambda i, j: (0, j)),
        ],
        out_specs=pl.BlockSpec((tm, tn), lambda i, j: (i, j)),
        out_shape=jax.ShapeDtypeStruct((t, n), F32),
        scratch_shapes=[pltpu.VMEM((tm, k), BF16)],
        compiler_params=_cparams(("parallel", "arbitrary")),
        name="norm_matmul",
    )(x, g, w)


def _group_a_kernel(u_ref, v_ref, g_ref, b_ref, w_ref, bias_ref, o_ref, vout_ref):
    u = jax.nn.gelu(u_ref[...], approximate=True)
    v = jax.nn.gelu(v_ref[...], approximate=True)
    mean = jnp.mean(v, axis=-1, keepdims=True)
    d = v - mean
    var = jnp.mean(d * d, axis=-1, keepdims=True)
    vn = d * lax.rsqrt(var + LN_EPS) * g_ref[...] + b_ref[...]
    vout_ref[...] = vn
    vb = vn.astype(BF16)
    for h in range(A_HEADS):
        hs = slice(h * A_HEAD_DIM, (h + 1) * A_HEAD_DIM)
        mixed = jnp.dot(w_ref[h], vb[:, hs], preferred_element_type=F32) + bias_ref[:, hs]
        o_ref[:, hs] = (u[:, hs] * mixed).astype(BF16)


def _group_a(proj, ln_g, ln_b, w_sp, bias_full):
    t = proj.shape[0]
    return pl.pallas_call(
        _group_a_kernel,
        grid=(t // CHUNK,),
        in_specs=[
            pl.BlockSpec((CHUNK, A_WIDTH), lambda i: (i, COL_A // A_WIDTH)),
            pl.BlockSpec((CHUNK, A_WIDTH), lambda i: (i, COL_A // A_WIDTH + 1)),
            pl.BlockSpec((1, A_WIDTH), lambda i: (0, 0)),
            pl.BlockSpec((1, A_WIDTH), lambda i: (0, 0)),
            pl.BlockSpec((A_HEADS, CHUNK, CHUNK), lambda i: (0, 0, 0)),
            pl.BlockSpec((CHUNK, A_WIDTH), lambda i: (0, 0)),
        ],
        out_specs=[
            pl.BlockSpec((CHUNK, A_WIDTH), lambda i: (i, 0)),
            pl.BlockSpec((CHUNK, A_WIDTH), lambda i: (i, 0)),
        ],
        out_shape=[
            jax.ShapeDtypeStruct((t, A_WIDTH), BF16),
            jax.ShapeDtypeStruct((t, A_WIDTH), F32),
        ],
        compiler_params=_cparams(("parallel",)),
        name="group_a",
    )(proj, proj, ln_g, ln_b, w_sp, bias_full)


def _attn_kernel(q_ref, k_ref, v_ref, o_ref):
    q = q_ref[...]
    scale = C_HEAD_DIM ** -0.5
    for h in range(C_HEADS):
        hs = slice(h * C_HEAD_DIM, (h + 1) * C_HEAD_DIM)
        s = lax.dot_general(q[:, hs].astype(BF16), k_ref[:, hs].astype(BF16),
                            (((1,), (1,)), ((), ())), preferred_element_type=F32) * scale
        e = jnp.exp(s - jnp.max(s, axis=-1, keepdims=True))
        p = e / jnp.sum(e, axis=-1, keepdims=True)
        o = jnp.dot(p.astype(BF16), v_ref[:, hs].astype(BF16), preferred_element_type=F32)
        o_ref[:, hs] = o.astype(BF16)


def _attention(proj, mem_k, mem_v, *, batch, seq):
    tq = min(seq, 512)
    n_q = seq // tq
    return pl.pallas_call(
        _attn_kernel,
        grid=(batch, n_q),
        in_specs=[
            pl.BlockSpec((tq, C_WIDTH), lambda b, i: (b * n_q + i, COL_Q // C_WIDTH)),
            pl.BlockSpec((None, MEM_LEN, C_WIDTH), lambda b, i: (b, 0, 0)),
            pl.BlockSpec((None, MEM_LEN, C_WIDTH), lambda b, i: (b, 0, 0)),
        ],
        out_specs=pl.BlockSpec((tq, C_WIDTH), lambda b, i: (b * n_q + i, 0)),
        out_shape=jax.ShapeDtypeStruct((batch * seq, C_WIDTH), BF16),
        compiler_params=_cparams(("parallel", "arbitrary")),
        name="mem_attention",
    )(proj, mem_k, mem_v)


_V_MU_R, _V_MU_K, _V_MU_V, _V_W0, _V_A0, _V_KK, _V_KA, _V_RK, _V_LNG, _V_LNB = range(10)


def _dot_hi(a, b, dims=(((1,), (0,)), ((), ()))):
    return lax.dot_general(a, b, dims, precision=lax.Precision.HIGHEST, preferred_element_type=F32)


_NT = (((1,), (1,)), ((), ()))
_TN = (((0,), (0,)), ((), ()))


def _rwkv_kernel(pr_ref, pk_ref, pv_ref, pl_ref, sr_ref, sk_ref, sv_ref, sl_ref, wkv_ref,
                 vec_ref, mul_ref, w2_ref, a2_ref, g2_ref, e1_ref, e2_ref,
                 o_ref, so_ref, cr_ref, ck_ref, cv_ref, cl_ref, *, groups, chunk, whole_seq):
    rows = groups * chunk
    first = pl.program_id(1) == 0

    @pl.when(first)
    def _():
        so_ref[...] = wkv_ref[...]

    row_id = lax.broadcasted_iota(jnp.int32, (rows, 1), 0)

    def prev_rows(x, s_ref, carry_ref):
        width = x.shape[1]
        if whole_seq:
            start = jnp.broadcast_to(s_ref[...], (groups, chunk, width)).reshape(rows, width)
            return jnp.where((row_id & (chunk - 1)) == 0, start, pltpu.roll(x, 1, 0))

        @pl.when(first)
        def _():
            carry_ref[...] = jnp.broadcast_to(s_ref[0], (SUBLANES, width))

        ext = jnp.concatenate([carry_ref[...], x], axis=0)
        prev = pltpu.roll(ext, 1, 0)[SUBLANES:]
        carry_ref[...] = x[rows - SUBLANES:]
        return prev

    def shifted(p_ref, s_ref, carry_ref, mu):
        x = p_ref[...]
        return x + (prev_rows(x, s_ref, carry_ref) - x) * mu

    vec = lambda i: vec_ref[i:i + 1, :]
    r = shifted(pr_ref, sr_ref, cr_ref, vec(_V_MU_R))
    k = shifted(pk_ref, sk_ref, ck_ref, vec(_V_MU_K))
    v = shifted(pv_ref, sv_ref, cv_ref, vec(_V_MU_V))
    lo = shifted(pl_ref, sl_ref, cl_ref, mul_ref[0:1, :])

    def seg_sum(x):
        return _dot_hi(_dot_hi(x, e1_ref[...]), e2_ref[...])

    dw = jnp.dot(jnp.tanh(lo).astype(BF16), w2_ref[...], preferred_element_type=F32)
    da = jnp.dot(lo.astype(BF16), a2_ref[...], preferred_element_type=F32)
    gate = jnp.dot(jax.nn.sigmoid(lo).astype(BF16), g2_ref[...], preferred_element_type=F32)
    w_log = -jax.nn.softplus(-(vec(_V_W0) + dw)) - DECAY_OFFSET
    log_decay = -jnp.exp(w_log)
    a = jax.nn.sigmoid(vec(_V_A0) + da)
    kk = k * vec(_V_KK)
    kk = kk / jnp.maximum(jnp.sqrt(seg_sum(kk * kk)), 1e-12)
    k = k * (1.0 + (a - 1.0) * vec(_V_KA))
    bonus = seg_sum(r * k * vec(_V_RK)) * v

    col_id = lax.broadcasted_iota(jnp.int32, (1, rows), 1)
    chunk_bits = int(math.log2(chunk))
    same = (row_id >> chunk_bits) == (col_id >> chunk_bits)
    incl = same & (col_id <= row_id)
    strict = same & (col_id < row_id)
    cum = _dot_hi(incl.astype(F32), log_decay)
    p_incl = jnp.exp(cum)
    p_inv = jnp.exp(-cum)
    a_t = -kk * jnp.exp(cum - log_decay)
    b_t = kk * a * p_inv
    k_t = k * p_inv
    r_t = r * p_incl

    eye = (row_id == col_id).astype(F32)
    n_sq = int(math.log2(chunk)) - 1
    ys = []
    for h in range(B_HEADS):
        hs = slice(h * B_HEAD_DIM, (h + 1) * B_HEAD_DIM)
        ah, bh, kh, rh, vh = a_t[:, hs], b_t[:, hs], k_t[:, hs], r_t[:, hs], v[:, hs]
        m = _dot_hi(jnp.concatenate([ah, rh], axis=0), jnp.concatenate([bh, kh], axis=0), _NT)
        a_ab = jnp.where(strict, m[:rows, :rows], 0.0)
        a_ak = jnp.where(strict, m[:rows, rows:], 0.0)
        a_rb = jnp.where(incl, m[rows:, :rows], 0.0)
        a_rk = jnp.where(incl, m[rows:, rows:], 0.0)
        inv = eye + a_ab
        pw = a_ab
        for _ in range(n_sq):
            pw = _dot_hi(pw, pw)
            inv = inv + _dot_hi(pw, inv)
        xs, y0 = [], []
        for g in range(groups):
            gs = slice(g * chunk, (g + 1) * chunk)
            xy = _dot_hi(jnp.concatenate([ah[gs], rh[gs]], axis=0), so_ref[g, h], _NT)
            xs.append(xy[:chunk])
            y0.append(xy[chunk:])
        x0 = xs[0] if groups == 1 else jnp.concatenate(xs, axis=0)
        y0 = y0[0] if groups == 1 else jnp.concatenate(y0, axis=0)
        u = _dot_hi(inv, x0 + _dot_hi(a_ak, vh))
        ys.append(y0 + _dot_hi(a_rb, u) + _dot_hi(a_rk, vh))
        for g in range(groups):
            gs = slice(g * chunk, (g + 1) * chunk)
            ds = _dot_hi(jnp.concatenate([u[gs], vh[gs]], axis=0),
                         jnp.concatenate([bh[gs], kh[gs]], axis=0), _TN)
            end = (g + 1) * chunk - 1
            so_ref[g, h] = (so_ref[g, h] + ds) * p_incl[end:end + 1, hs]
    y = jnp.concatenate(ys, axis=1)

    inv_n = 1.0 / B_HEAD_DIM
    d = y - seg_sum(y) * inv_n
    var = seg_sum(d * d) * inv_n
    yn = d * lax.rsqrt(var + GN_EPS) * vec(_V_LNG) + vec(_V_LNB)
    o_ref[...] = ((yn + bonus) * gate).astype(BF16)


def _rwkv(proj, shift_parts, wkv_prev, lw, *, batch, seq):
    whole_seq = seq <= WKV_CHUNK
    if whole_seq:
        assert seq == SUBLANES, "whole-sequence blocks rely on one sublane tile per sequence"
        chunk, groups = seq, WKV_CHUNK // seq
    else:
        chunk, groups = WKV_CHUNK, 1
    rows = groups * chunk
    n_chunks = seq // chunk
    row_blk = lambda bi, c: bi * n_chunks + c
    in_specs = [
        pl.BlockSpec((rows, B_WIDTH), lambda bi, c: (row_blk(bi, c), COL_R // B_WIDTH)),
        pl.BlockSpec((rows, B_WIDTH), lambda bi, c: (row_blk(bi, c), COL_R // B_WIDTH + 1)),
        pl.BlockSpec((rows, B_WIDTH), lambda bi, c: (row_blk(bi, c), COL_R // B_WIDTH + 2)),
        pl.BlockSpec((rows, LORA_PAD), lambda bi, c: (row_blk(bi, c), COL_L // LORA_PAD)),
        pl.BlockSpec((groups, 1, B_WIDTH), lambda bi, c: (bi, 0, 0)),
        pl.BlockSpec((groups, 1, B_WIDTH), lambda bi, c: (bi, 0, 0)),
        pl.BlockSpec((groups, 1, B_WIDTH), lambda bi, c: (bi, 0, 0)),
        pl.BlockSpec((groups, 1, LORA_PAD), lambda bi, c: (bi, 0, 0)),
        pl.BlockSpec((groups, B_HEADS, B_HEAD_DIM, B_HEAD_DIM), lambda bi, c: (bi, 0, 0, 0)),
        pl.BlockSpec((16, B_WIDTH), lambda bi, c: (0, 0)),
        pl.BlockSpec((SUBLANES, LORA_PAD), lambda bi, c: (0, 0)),
        pl.BlockSpec((LORA_PAD, B_WIDTH), lambda bi, c: (0, 0)),
        pl.BlockSpec((LORA_PAD, B_WIDTH), lambda bi, c: (0, 0)),
        pl.BlockSpec((LORA_PAD, B_WIDTH), lambda bi, c: (0, 0)),
        pl.BlockSpec((B_WIDTH, LANES), lambda bi, c: (0, 0)),
        pl.BlockSpec((LANES, B_WIDTH), lambda bi, c: (0, 0)),
    ]
    kern = functools.partial(_rwkv_kernel, groups=groups, chunk=chunk, whole_seq=whole_seq)
    return pl.pallas_call(
        kern,
        grid=(batch // groups, n_chunks),
        in_specs=in_specs,
        out_specs=[
            pl.BlockSpec((rows, B_WIDTH), lambda bi, c: (row_blk(bi, c), 0)),
            pl.BlockSpec((groups, B_HEADS, B_HEAD_DIM, B_HEAD_DIM), lambda bi, c: (bi, 0, 0, 0)),
        ],
        out_shape=[
            jax.ShapeDtypeStruct((batch * seq, B_WIDTH), BF16),
            jax.ShapeDtypeStruct((batch, B_HEADS, B_HEAD_DIM, B_HEAD_DIM), F32),
        ],
        scratch_shapes=[
            pltpu.VMEM((SUBLANES, B_WIDTH), F32),
            pltpu.VMEM((SUBLANES, B_WIDTH), F32),
            pltpu.VMEM((SUBLANES, B_WIDTH), F32),
            pltpu.VMEM((SUBLANES, LORA_PAD), F32),
        ],
        compiler_params=_cparams(("parallel", "arbitrary")),
        name="rwkv7",
    )(proj, proj, proj, proj, *shift_parts, wkv_prev,
      lw["rk_vecs"], lw["rk_mu_l"], lw["rk_w2"], lw["rk_a2"], lw["rk_g2"], lw["seg_e1"], lw["seg_e2"])


def _out_proj_kernel(a_ref, b_ref, c_ref, w_ref, x_ref, gpost_ref, gffn_ref, x1_ref, h_ref):
    acc = jnp.dot(a_ref[...], w_ref[0:A_WIDTH, :], preferred_element_type=F32)
    acc += jnp.dot(b_ref[...], w_ref[A_WIDTH:A_WIDTH + B_WIDTH, :], preferred_element_type=F32)
    acc += jnp.dot(c_ref[...], w_ref[A_WIDTH + B_WIDTH:, :], preferred_element_type=F32)
    x1 = x_ref[...] + _rms(acc, gpost_ref[...])
    x1_ref[...] = x1
    h_ref[...] = _rms(x1, gffn_ref[...]).astype(BF16)


def _out_proj(a_out, b_out, c_out, w_out, x, g_post, g_ffn, *, tm=256):
    t = x.shape[0]
    row = lambda i: (i, 0)
    fixed = lambda i: (0, 0)
    return pl.pallas_call(
        _out_proj_kernel,
        grid=(t // tm,),
        in_specs=[
            pl.BlockSpec((tm, A_WIDTH), row),
            pl.BlockSpec((tm, B_WIDTH), row),
            pl.BlockSpec((tm, C_WIDTH), row),
            pl.BlockSpec((D_MODEL, D_MODEL), fixed),
            pl.BlockSpec((tm, D_MODEL), row),
            pl.BlockSpec((1, D_MODEL), fixed),
            pl.BlockSpec((1, D_MODEL), fixed),
        ],
        out_specs=[pl.BlockSpec((tm, D_MODEL), row), pl.BlockSpec((tm, D_MODEL), row)],
        out_shape=[jax.ShapeDtypeStruct((t, D_MODEL), F32), jax.ShapeDtypeStruct((t, D_MODEL), BF16)],
        compiler_params=_cparams(("parallel",)),
        name="out_proj",
    )(a_out, b_out, c_out, w_out, x, g_post, g_ffn)


def _ffn_up_kernel(h_ref, wg_ref, wv_ref, cwg_ref, cwv_ref, cbg_ref, cbv_ref, pg_ref, pv_ref,
                   act_ref, ng_ref, nv_ref, cg_ref, cv_ref, *, tm, seq):
    h = h_ref[...]
    tn = wg_ref.shape[1]
    row_id = lax.broadcasted_iota(jnp.int32, (tm, 1), 0)

    def conv_half(w_ref, cw_ref, cb_ref, p_ref, n_ref, carry_ref):
        up = jnp.dot(h, w_ref[...], preferred_element_type=F32)
        if seq <= tm:
            n_seq = tm // seq
            prev = p_ref[...]
            e0 = jnp.broadcast_to(prev[:, 0:1, :], (n_seq, seq, tn)).reshape(tm, tn)
            e1 = jnp.broadcast_to(prev[:, 1:2, :], (n_seq, seq, tn)).reshape(tm, tn)
            tau = row_id & (seq - 1)
            m1 = jnp.where(tau == 0, e1, pltpu.roll(up, 1, 0))
            m2 = jnp.where(tau == 0, e0, jnp.where(tau == 1, e1, pltpu.roll(up, 2, 0)))
            n_ref[...] = up.reshape(n_seq, seq, tn)[:, seq - 2:, :]
        else:
            @pl.when(pl.program_id(1) % (seq // tm) == 0)
            def _():
                carry_ref[...] = jnp.concatenate(
                    [jnp.zeros((SUBLANES - 2, tn), F32), p_ref[0]], axis=0)

            ext = jnp.concatenate([carry_ref[...], up], axis=0)
            m1 = pltpu.roll(ext, 1, 0)[SUBLANES:]
            m2 = pltpu.roll(ext, 2, 0)[SUBLANES:]
            carry_ref[...] = up[tm - SUBLANES:]
            n_ref[0] = up[tm - 2:]
        return cb_ref[...] + m2 * cw_ref[0:1, :] + m1 * cw_ref[1:2, :] + up * cw_ref[2:3, :]

    gate = conv_half(wg_ref, cwg_ref, cbg_ref, pg_ref, ng_ref, cg_ref)
    val = conv_half(wv_ref, cwv_ref, cbv_ref, pv_ref, nv_ref, cv_ref)
    act_ref[...] = (jax.nn.gelu(gate, approximate=True) * val).astype(BF16)


def _ffn_up(h, w_up, conv_w, conv_b, conv_prev, *, batch, seq):
    t = h.shape[0]
    if seq <= SUBLANES:
        assert seq == SUBLANES, "whole-sequence tiles rely on one sublane tile per sequence"
        tm = t
        state_blk = (batch, 2, FFN_TN)
        state_idx = lambda half: (lambda j, i: (0, 0, j + half * FFN_N_TILES))
    else:
        tm = 512
        tiles_per_seq = seq // tm
        state_blk = (1, 2, FFN_TN)
        state_idx = lambda half: (lambda j, i: (i // tiles_per_seq, 0, j + half * FFN_N_TILES))
    col = lambda half: (lambda j, i: (0, j + half * FFN_N_TILES))
    kern = functools.partial(_ffn_up_kernel, tm=tm, seq=seq)
    state_out = jax.ShapeDtypeStruct((batch, 2, D_FF), F32)
    return pl.pallas_call(
        kern,
        grid=(FFN_N_TILES, t // tm),
        in_specs=[
            pl.BlockSpec((tm, D_MODEL), lambda j, i: (i, 0)),
            pl.BlockSpec((D_MODEL, FFN_TN), col(0)),
            pl.BlockSpec((D_MODEL, FFN_TN), col(1)),
            pl.BlockSpec((3, FFN_TN), col(0)),
            pl.BlockSpec((3, FFN_TN), col(1)),
            pl.BlockSpec((1, FFN_TN), col(0)),
            pl.BlockSpec((1, FFN_TN), col(1)),
            pl.BlockSpec(state_blk, state_idx(0)),
            pl.BlockSpec(state_blk, state_idx(1)),
        ],
        out_specs=[
            pl.BlockSpec((tm, FFN_TN), lambda j, i: (i, j)),
            pl.BlockSpec(state_blk, state_idx(0)),
            pl.BlockSpec(state_blk, state_idx(0)),
        ],
        out_shape=[jax.ShapeDtypeStruct((t, D_FF), BF16), state_out, state_out],
        scratch_shapes=[pltpu.VMEM((SUBLANES, FFN_TN), F32), pltpu.VMEM((SUBLANES, FFN_TN), F32)],
        compiler_params=_cparams(("parallel", "arbitrary")),
        name="ffn_up_conv",
    )(h, w_up, w_up, conv_w, conv_w, conv_b, conv_b, conv_prev, conv_prev)


def _ffn_down_kernel(a_ref, w_ref, x_ref, g_ref, o_ref, acc_ref):
    kk = pl.program_id(1)

    @pl.when(kk == 0)
    def _():
        acc_ref[...] = jnp.zeros_like(acc_ref)

    acc_ref[...] += jnp.dot(a_ref[...], w_ref[...], preferred_element_type=F32)

    @pl.when(kk == pl.num_programs(1) - 1)
    def _():
        o_ref[...] = x_ref[...] + _rms(acc_ref[...], g_ref[...])


def _ffn_down(act, w_down, x1, g_post, *, tm=512, tk=D_FF // 4):
    t = x1.shape[0]
    return pl.pallas_call(
        _ffn_down_kernel,
        grid=(t // tm, D_FF // tk),
        in_specs=[
            pl.BlockSpec((tm, tk), lambda i, k: (i, k)),
            pl.BlockSpec((tk, D_MODEL), lambda i, k: (k, 0)),
            pl.BlockSpec((tm, D_MODEL), lambda i, k: (i, 0)),
            pl.BlockSpec((1, D_MODEL), lambda i, k: (0, 0)),
        ],
        out_specs=pl.BlockSpec((tm, D_MODEL), lambda i, k: (i, 0)),
        out_shape=jax.ShapeDtypeStruct((t, D_MODEL), F32),
        scratch_shapes=[pltpu.VMEM((tm, D_MODEL), F32)],
        compiler_params=_cparams(("parallel", "arbitrary")),
        name="ffn_down",
    )(act, w_down, x1, g_post)


def _pad_rows(w, row0, total):
    return jnp.zeros((total, w.shape[1]), w.dtype).at[row0:row0 + w.shape[0]].set(w)


def _prepare_layer(p, l):
    w_in = p["w_in"][l]
    b0 = 2 * A_WIDTH
    q0 = b0 + B_PROJ
    w_in_p = jnp.concatenate([
        w_in[:, :b0 + 3 * B_WIDTH],
        w_in[:, q0:],
        w_in[:, b0 + 3 * B_WIDTH:q0],
        jnp.zeros((D_MODEL, IN_COLS_PAD - COL_L - 3 * LORA), w_in.dtype),
    ], axis=1).astype(BF16)
    mu = p["rk_mu"][l]
    vec_rows = [mu[:B_WIDTH], mu[B_WIDTH:2 * B_WIDTH], mu[2 * B_WIDTH:3 * B_WIDTH], p["rk_w0"][l],
                p["rk_a0"][l], p["rk_kk"][l], p["rk_ka"][l], p["rk_rk"][l].reshape(B_WIDTH),
                p["rk_lnx_g"][l], p["rk_lnx_b"][l]]
    vecs = jnp.zeros((16, B_WIDTH), F32).at[:len(vec_rows)].set(jnp.stack(vec_rows))
    mu_l = jnp.zeros((SUBLANES, LORA_PAD), F32).at[0, :3 * LORA].set(mu[3 * B_WIDTH:])
    head_of = jnp.arange(B_WIDTH) // B_HEAD_DIM
    seg_e1 = (head_of[:, None] == jnp.arange(LANES)[None, :]).astype(F32)
    tril = jnp.tril(jnp.ones((CHUNK, CHUNK), bool))
    ws = jnp.where(tril[None], p["gm_ws"][l], 0.0)
    bs = p["gm_bs"][l]
    return {
        "g_mix_pre": p["norm_mix_pre"][l][None], "g_mix_post": p["norm_mix_post"][l][None],
        "g_ffn_pre": p["norm_ffn_pre"][l][None], "g_ffn_post": p["norm_ffn_post"][l][None],
        "g_mem": p["norm_mem"][l][None],
        "w_in": w_in_p,
        "w_out": p["w_out"][l].astype(BF16),
        "w_mkv": jnp.concatenate([p["w_mem_k"][l], p["w_mem_v"][l]], axis=1).astype(BF16),
        "gm_ln_g": p["gm_ln_g"][l][None], "gm_ln_b": p["gm_ln_b"][l][None],
        "gm_ws": ws, "gm_bs": bs,
        "rk_vecs": vecs, "rk_mu_l": mu_l,
        "rk_w2": _pad_rows(p["rk_w2"][l], 0, LORA_PAD).astype(BF16),
        "rk_a2": _pad_rows(p["rk_a2"][l], LORA, LORA_PAD).astype(BF16),
        "rk_g2": _pad_rows(p["rk_g2"][l], 2 * LORA, LORA_PAD).astype(BF16),
        "seg_e1": seg_e1, "seg_e2": seg_e1.T,
        "w_up": p["ffn_w_up"][l].astype(BF16),
        "conv_w": p["ffn_conv_w"][l], "conv_b": p["ffn_conv_b"][l][None],
        "w_down": p["ffn_w_down"][l].astype(BF16),
    }


def _spatial_weights(lw, seq):
    ws, bs = lw["gm_ws"], lw["gm_bs"]
    if seq >= CHUNK:
        w_blk, b_rows = ws, bs
    else:
        reps = CHUNK // seq
        eye = jnp.eye(reps, dtype=ws.dtype)
        w_blk = jnp.einsum("ab,hts->hatbs", eye, ws[:, :seq, :seq]).reshape(A_HEADS, CHUNK, CHUNK)
        b_rows = jnp.tile(bs[:, :seq], (1, reps))
    bias_full = jnp.repeat(b_rows.T, A_HEAD_DIM, axis=1)
    return w_blk.astype(BF16), bias_full


def _decoder_layer(x, mem_k, mem_v, shift_prev, wkv_prev, conv_prev, lw):
    batch, seq, _ = x.shape
    x2 = x.reshape(batch * seq, D_MODEL)
    proj = _norm_matmul(x2, lw["g_mix_pre"], lw["w_in"], tm=512, tn=PROJ_TN)
    w_sp, bias_full = _spatial_weights(lw, seq)
    a_out, a_v = _group_a(proj, lw["gm_ln_g"], lw["gm_ln_b"], w_sp, bias_full)
    shift_parts = [
        shift_prev[:, None, :B_WIDTH], shift_prev[:, None, B_WIDTH:2 * B_WIDTH],
        shift_prev[:, None, 2 * B_WIDTH:3 * B_WIDTH],
        jnp.pad(shift_prev[:, None, 3 * B_WIDTH:], ((0, 0), (0, 0), (0, LORA_PAD - 3 * LORA))),
    ]
    b_out, wkv_new = _rwkv(proj, shift_parts, wkv_prev, lw, batch=batch, seq=seq)
    c_out = _attention(proj, mem_k.reshape(batch, MEM_LEN, C_WIDTH), mem_v.reshape(batch, MEM_LEN, C_WIDTH),
                       batch=batch, seq=seq)
    x1, h = _out_proj(a_out, b_out, c_out, lw["w_out"], x2, lw["g_mix_post"], lw["g_ffn_pre"])
    act, conv_g, conv_v = _ffn_up(h, lw["w_up"], lw["conv_w"], lw["conv_b"], conv_prev, batch=batch, seq=seq)
    y = _ffn_down(act, lw["w_down"], x1, lw["g_ffn_post"])

    chunk_start = ((seq - 1) // CHUNK) * CHUNK
    chunk_v = a_v.reshape(batch, seq, A_HEADS, A_HEAD_DIM)[:, chunk_start:]
    last = proj.reshape(batch, seq, IN_COLS_PAD)[:, -1]
    shift_new = jnp.concatenate([last[:, COL_R:COL_R + 3 * B_WIDTH], last[:, COL_L:COL_L + 3 * LORA]], axis=1)
    conv_new = jnp.concatenate([conv_g, conv_v], axis=-1)
    return y.reshape(batch, seq, D_MODEL), chunk_v, shift_new, wkv_new, conv_new


def kernel(x_prompt, x_sample, mem_prompt, cache_mem_k, cache_mem_v, state_shift, state_wkv, state_conv,
           norm_mix_pre, norm_mix_post, norm_ffn_pre, norm_ffn_post, norm_mem, w_in, w_out, w_mem_k, w_mem_v,
           gm_ln_g, gm_ln_b, gm_ws, gm_bs, rk_mu, rk_w0, rk_w2, rk_a0, rk_a2, rk_g2, rk_kk, rk_ka, rk_rk,
           rk_lnx_g, rk_lnx_b, ffn_w_up, ffn_conv_w, ffn_conv_b, ffn_w_down):
    params = dict(
        norm_mix_pre=norm_mix_pre, norm_mix_post=norm_mix_post, norm_ffn_pre=norm_ffn_pre,
        norm_ffn_post=norm_ffn_post, norm_mem=norm_mem, w_in=w_in, w_out=w_out, w_mem_k=w_mem_k,
        w_mem_v=w_mem_v, gm_ln_g=gm_ln_g, gm_ln_b=gm_ln_b, gm_ws=gm_ws, gm_bs=gm_bs, rk_mu=rk_mu,
        rk_w0=rk_w0, rk_w2=rk_w2, rk_a0=rk_a0, rk_a2=rk_a2, rk_g2=rk_g2, rk_kk=rk_kk, rk_ka=rk_ka,
        rk_rk=rk_rk, rk_lnx_g=rk_lnx_g, rk_lnx_b=rk_lnx_b, ffn_w_up=ffn_w_up, ffn_conv_w=ffn_conv_w,
        ffn_conv_b=ffn_conv_b, ffn_w_down=ffn_w_down)
    depth = w_in.shape[0]
    bp = x_prompt.shape[0]
    y_p, y_s = x_prompt, x_sample
    outs = [[] for _ in range(10)]
    for l in range(depth):
        lw = _prepare_layer(params, l)
        mem2 = mem_prompt.reshape(bp * MEM_LEN, D_MODEL)
        mkv = _norm_matmul(mem2, lw["g_mem"], lw["w_mkv"], tm=512, tn=2 * C_WIDTH)
        mk = mkv[:, :C_WIDTH].reshape(bp, MEM_LEN, C_HEADS, C_HEAD_DIM)
        mv = mkv[:, C_WIDTH:].reshape(bp, MEM_LEN, C_HEADS, C_HEAD_DIM)
        zero_shift = jnp.zeros((bp, B_PROJ), x_prompt.dtype)
        zero_wkv = jnp.zeros((bp, B_HEADS, B_HEAD_DIM, B_HEAD_DIM), F32)
        zero_conv = jnp.zeros((bp, 2, 2 * D_FF), x_prompt.dtype)
        y_p, cv, sh, wkv, conv = _decoder_layer(y_p, mk, mv, zero_shift, zero_wkv, zero_conv, lw)
        for lst, val in zip(outs[:6], (mk, mv, cv, sh, wkv, conv)):
            lst.append(val)
        y_s, cv, sh, wkv, conv = _decoder_layer(y_s, cache_mem_k[l], cache_mem_v[l], state_shift[l],
                                                state_wkv[l], state_conv[l], lw)
        for lst, val in zip(outs[6:], (cv, sh, wkv, conv)):
            lst.append(val)
    return (y_p, y_s) + tuple(jnp.stack(o) for o in outs)
```

```python
import functools
import math

import jax
import jax.numpy as jnp
from jax import lax
from jax.experimental import pallas as pl
from jax.experimental.pallas import tpu as pltpu

D_MODEL = 2048
MEM_LEN = 256
CHUNK = 128
A_HEADS, A_HEAD_DIM = 4, 128
A_WIDTH = A_HEADS * A_HEAD_DIM
B_HEADS, B_HEAD_DIM = 16, 64
B_WIDTH = B_HEADS * B_HEAD_DIM
LORA = 64
B_PROJ = 3 * B_WIDTH + 3 * LORA
C_HEADS, C_HEAD_DIM = 4, 128
C_WIDTH = C_HEADS * C_HEAD_DIM
D_FF = 5632
RMS_EPS = 1e-6
LN_EPS = 1e-5
GN_EPS = 64e-5
DECAY_OFFSET = 0.5

LANES = 128
SUBLANES = 8
VMEM_LIMIT_BYTES = 56 * 1024 * 1024

LORA_PAD = 256
COL_A = 0
COL_R = 2 * A_WIDTH
COL_Q = COL_R + 3 * B_WIDTH
COL_L = COL_Q + C_WIDTH
IN_COLS_PAD = 5120
PROJ_TN = 1024

WKV_CHUNK = 64
FFN_TN = 512
FFN_N_TILES = D_FF // FFN_TN

F32 = jnp.float32
BF16 = jnp.bfloat16


def _cparams(sem):
    return pltpu.CompilerParams(dimension_semantics=sem, vmem_limit_bytes=VMEM_LIMIT_BYTES)


def _rms(x, g):
    return x * lax.rsqrt(jnp.mean(x * x, axis=-1, keepdims=True) + RMS_EPS) * g


def _norm_matmul_kernel(x_ref, g_ref, w_ref, o_ref, h_ref):
    @pl.when(pl.program_id(1) == 0)
    def _():
        h_ref[...] = _rms(x_ref[...], g_ref[...]).astype(BF16)

    o_ref[...] = jnp.dot(h_ref[...], w_ref[...], preferred_element_type=F32)


def _norm_matmul(x, g, w, *, tm, tn):
    t, k = x.shape
    n = w.shape[1]
    return pl.pallas_call(
        _norm_matmul_kernel,
        grid=(t // tm, n // tn),
        in_specs=[
            pl.BlockSpec((tm, k), lambda i, j: (i, 0)),
            pl.BlockSpec((1, k), lambda i, j: (0, 0)),
            pl.BlockSpec((k, tn), lambda i, j: (0, j)),
        ],
        out_specs=pl.BlockSpec((tm, tn), lambda i, j: (i, j)),
        out_shape=jax.ShapeDtypeStruct((t, n), F32),
        scratch_shapes=[pltpu.VMEM((tm, k), BF16)],
        compiler_params=_cparams(("parallel", "arbitrary")),
        name="norm_matmul",
    )(x, g, w)


def _group_a_kernel(u_ref, v_ref, g_ref, b_ref, w_ref, bias_ref, o_ref, vout_ref):
    u = jax.nn.gelu(u_ref[...], approximate=True)
    v = jax.nn.gelu(v_ref[...], approximate=True)
    mean = jnp.mean(v, axis=-1, keepdims=True)
    d = v - mean
    var = jnp.mean(d * d, axis=-1, keepdims=True)
    vn = d * lax.rsqrt(var + LN_EPS) * g_ref[...] + b_ref[...]
    vout_ref[...] = vn
    vb = vn.astype(BF16)
    for h in range(A_HEADS):
        hs = slice(h * A_HEAD_DIM, (h + 1) * A_HEAD_DIM)
        mixed = jnp.dot(w_ref[h], vb[:, hs], preferred_element_type=F32) + bias_ref[:, hs]
        o_ref[:, hs] = (u[:, hs] * mixed).astype(BF16)


def _group_a(proj, ln_g, ln_b, w_sp, bias_full):
    t = proj.shape[0]
    return pl.pallas_call(
        _group_a_kernel,
        grid=(t // CHUNK,),
        in_specs=[
            pl.BlockSpec((CHUNK, A_WIDTH), lambda i: (i, COL_A // A_WIDTH)),
            pl.BlockSpec((CHUNK, A_WIDTH), lambda i: (i, COL_A // A_WIDTH + 1)),
            pl.BlockSpec((1, A_WIDTH), lambda i: (0, 0)),
            pl.BlockSpec((1, A_WIDTH), lambda i: (0, 0)),
            pl.BlockSpec((A_HEADS, CHUNK, CHUNK), lambda i: (0, 0, 0)),
            pl.BlockSpec((CHUNK, A_WIDTH), lambda i: (0, 0)),
        ],
        out_specs=[
            pl.BlockSpec((CHUNK, A_WIDTH), lambda i: (i, 0)),
            pl.BlockSpec((CHUNK, A_WIDTH), lambda i: (i, 0)),
        ],
        out_shape=[
            jax.ShapeDtypeStruct((t, A_WIDTH), BF16),
            jax.ShapeDtypeStruct((t, A_WIDTH), F32),
        ],
        compiler_params=_cparams(("parallel",)),
        name="group_a",
    )(proj, proj, ln_g, ln_b, w_sp, bias_full)


def _attn_kernel(q_ref, k_ref, v_ref, o_ref):
    q = q_ref[...]
    scale = C_HEAD_DIM ** -0.5
    for h in range(C_HEADS):
        hs = slice(h * C_HEAD_DIM, (h + 1) * C_HEAD_DIM)
        s = lax.dot_general(q[:, hs].astype(BF16), k_ref[:, hs].astype(BF16),
                            (((1,), (1,)), ((), ())), preferred_element_type=F32) * scale
        e = jnp.exp(s - jnp.max(s, axis=-1, keepdims=True))
        p = e / jnp.sum(e, axis=-1, keepdims=True)
        o = jnp.dot(p.astype(BF16), v_ref[:, hs].astype(BF16), preferred_element_type=F32)
        o_ref[:, hs] = o.astype(BF16)


def _attention(proj, mem_k, mem_v, *, batch, seq):
    tq = min(seq, 512)
    n_q = seq // tq
    return pl.pallas_call(
        _attn_kernel,
        grid=(batch, n_q),
        in_specs=[
            pl.BlockSpec((tq, C_WIDTH), lambda b, i: (b * n_q + i, COL_Q // C_WIDTH)),
            pl.BlockSpec((None, MEM_LEN, C_WIDTH), lambda b, i: (b, 0, 0)),
            pl.BlockSpec((None, MEM_LEN, C_WIDTH), lambda b, i: (b, 0, 0)),
        ],
        out_specs=pl.BlockSpec((tq, C_WIDTH), lambda b, i: (b * n_q + i, 0)),
        out_shape=jax.ShapeDtypeStruct((batch * seq, C_WIDTH), BF16),
        compiler_params=_cparams(("parallel", "arbitrary")),
        name="mem_attention",
    )(proj, mem_k, mem_v)


_V_MU_R, _V_MU_K, _V_MU_V, _V_W0, _V_A0, _V_KK, _V_KA, _V_RK, _V_LNG, _V_LNB = range(10)


_NN = (((1,), (0,)), ((), ()))
_NT = (((1,), (1,)), ((), ()))
_TN = (((0,), (0,)), ((), ()))
HEAD_PAIRS = B_HEADS // 2


def _split(x):
    hi = x.astype(BF16)
    return hi, (x - hi.astype(F32)).astype(BF16)


def _b(x):
    return x.astype(BF16)


def _bdot(a, b, dims=_NN):
    return lax.dot_general(a, b, dims, preferred_element_type=F32)


def _rwkv_kernel(pr_ref, pk_ref, pv_ref, pl_ref, sr_ref, sk_ref, sv_ref, sl_ref, wkv_ref,
                 vec_ref, mul_ref, w2_ref, a2_ref, g2_ref, seg_ref,
                 o_ref, so_ref, cr_ref, ck_ref, cv_ref, cl_ref, sbd_ref, *, groups, chunk, whole_seq):
    rows = groups * chunk
    first = pl.program_id(1) == 0
    last = pl.program_id(1) == pl.num_programs(1) - 1
    hd = B_HEAD_DIM

    @pl.when(first)
    def _():
        zero = jnp.zeros((hd, hd), F32)
        for g in range(groups):
            for q in range(HEAD_PAIRS):
                top = jnp.concatenate([wkv_ref[g, 2 * q], zero], axis=1)
                bot = jnp.concatenate([zero, wkv_ref[g, 2 * q + 1]], axis=1)
                sbd_ref[g, q] = jnp.concatenate([top, bot], axis=0)

    row_id = lax.broadcasted_iota(jnp.int32, (rows, 1), 0)

    def prev_rows(x, s_ref, carry_ref):
        width = x.shape[1]
        if whole_seq:
            start = jnp.broadcast_to(s_ref[...], (groups, chunk, width)).reshape(rows, width)
            return jnp.where((row_id & (chunk - 1)) == 0, start, pltpu.roll(x, 1, 0))

        @pl.when(first)
        def _():
            carry_ref[...] = jnp.broadcast_to(s_ref[0], (SUBLANES, width))

        ext = jnp.concatenate([carry_ref[...], x], axis=0)
        prev = pltpu.roll(ext, 1, 0)[SUBLANES:]
        carry_ref[...] = x[rows - SUBLANES:]
        return prev

    def shifted(p_ref, s_ref, carry_ref, mu):
        x = p_ref[...]
        return x + (prev_rows(x, s_ref, carry_ref) - x) * mu

    vec = lambda i: vec_ref[i:i + 1, :]
    r = shifted(pr_ref, sr_ref, cr_ref, vec(_V_MU_R))
    k = shifted(pk_ref, sk_ref, ck_ref, vec(_V_MU_K))
    v = shifted(pv_ref, sv_ref, cv_ref, vec(_V_MU_V))
    lo = shifted(pl_ref, sl_ref, cl_ref, mul_ref[0:1, :])

    seg_ones = seg_ref[...]

    def seg_sum(x):
        slabs = jnp.concatenate([x[:, q * LANES:(q + 1) * LANES] for q in range(HEAD_PAIRS)], axis=0)
        hi, lo = _split(slabs)
        s = (jnp.dot(hi, seg_ones, preferred_element_type=F32)
             + jnp.dot(lo, seg_ones, preferred_element_type=F32))
        return jnp.concatenate([s[q * rows:(q + 1) * rows] for q in range(HEAD_PAIRS)], axis=1)

    dw = jnp.dot(jnp.tanh(lo).astype(BF16), w2_ref[...], preferred_element_type=F32)
    da = jnp.dot(lo.astype(BF16), a2_ref[...], preferred_element_type=F32)
    gate = jnp.dot(jax.nn.sigmoid(lo).astype(BF16), g2_ref[...], preferred_element_type=F32)
    w_log = -jax.nn.softplus(-(vec(_V_W0) + dw)) - DECAY_OFFSET
    log_decay = -jnp.exp(w_log)
    a = jax.nn.sigmoid(vec(_V_A0) + da)
    kk = k * vec(_V_KK)
    kk = kk / jnp.maximum(jnp.sqrt(seg_sum(kk * kk)), 1e-12)
    k = k * (1.0 + (a - 1.0) * vec(_V_KA))
    bonus = seg_sum(r * k * vec(_V_RK)) * v

    col_id = lax.broadcasted_iota(jnp.int32, (1, rows), 1)
    chunk_bits = int(math.log2(chunk))
    incl = ((row_id >> chunk_bits) == (col_id >> chunk_bits)) & (col_id <= row_id)
    incl_b = incl.astype(BF16)
    ld_hi, ld_mid = _split(log_decay)
    ld_lo = (log_decay - ld_hi.astype(F32) - ld_mid.astype(F32)).astype(BF16)
    cum = (jnp.dot(incl_b, ld_hi, preferred_element_type=F32)
           + jnp.dot(incl_b, ld_mid, preferred_element_type=F32)
           + jnp.dot(incl_b, ld_lo, preferred_element_type=F32))
    p_incl = jnp.exp(cum)
    p_inv = jnp.exp(-cum)
    a_t = -kk * jnp.exp(cum - log_decay)
    b_t = kk * a * p_inv
    k_t = k * p_inv
    r_t = r * p_incl

    pr = 2 * rows
    row2 = lax.broadcasted_iota(jnp.int32, (pr, 1), 0)
    col2 = lax.broadcasted_iota(jnp.int32, (1, pr), 1)
    t2, s2 = row2 & (rows - 1), col2 & (rows - 1)
    same2 = ((row2 >> chunk_bits) == (col2 >> chunk_bits))
    incl2 = same2 & (s2 <= t2)
    strict2 = same2 & (s2 < t2)
    eye2 = (row2 == col2).astype(F32)
    left = lax.broadcasted_iota(jnp.int32, (rows, LANES), 1) < hd

    def bd(x):
        zero = jnp.zeros_like(x)
        return jnp.concatenate([jnp.where(left, x, zero), jnp.where(left, zero, x)], axis=0)

    def group_rows(mats, g):
        starts = [hh * rows + g * chunk for hh in (0, 1)]
        return jnp.concatenate([m[i:i + chunk] for m in mats for i in starts], axis=0)

    n_sq = chunk_bits - 1
    pairs = range(HEAD_PAIRS)
    lanes_of = [slice(q * LANES, (q + 1) * LANES) for q in pairs]
    bds = [[bd(t[:, ls]) for t in (a_t, r_t, b_t, k_t, v)] for ls in lanes_of]
    ar_s = [_b(jnp.concatenate([m[0], m[1]], axis=0)) for m in bds]
    bk_s = [_b(jnp.concatenate([m[2], m[3]], axis=0)) for m in bds]
    v_s = [_b(m[4]) for m in bds]
    ms = [_bdot(ar_s[q], bk_s[q], _NT) for q in pairs]
    a_ab = [jnp.where(strict2, m[:pr, :pr], 0.0) for m in ms]
    a_ak = [jnp.where(strict2, m[:pr, pr:], 0.0) for m in ms]
    col4 = lax.broadcasted_iota(jnp.int32, (1, 2 * pr), 1)
    incl4 = ((row2 >> chunk_bits) == ((col4 & (pr - 1)) >> chunk_bits)) & ((col4 & (rows - 1)) <= t2)
    a_r = [jnp.where(incl4, m[pr:], 0.0) for m in ms]
    inv = [eye2 + n for n in a_ab]
    pw_s = [_b(n) for n in a_ab]
    for _ in range(n_sq):
        pw_s = [_b(_bdot(p, p)) for p in pw_s]
        inv = [i + _bdot(p, _b(i)) for i, p in zip(inv, pw_s)]
    if groups == 1:
        xy = [_bdot(ar_s[q], _b(sbd_ref[0, q]), _NT) for q in pairs]
        x0, y0 = [m[:pr] for m in xy], [m[pr:] for m in xy]
    else:
        x0, y0 = [], []
        for q in pairs:
            x_parts, y_parts = [None] * (2 * groups), [None] * (2 * groups)
            for g in range(groups):
                xy = _bdot(_b(group_rows(bds[q][:2], g)), _b(sbd_ref[g, q]), _NT)
                for hh in (0, 1):
                    x_parts[hh * groups + g] = xy[hh * chunk:(hh + 1) * chunk]
                    y_parts[hh * groups + g] = xy[(2 + hh) * chunk:(3 + hh) * chunk]
            x0.append(jnp.concatenate(x_parts, axis=0))
            y0.append(jnp.concatenate(y_parts, axis=0))
    rhs = [_b(x0[q] + _bdot(_b(a_ak[q]), v_s[q])) for q in pairs]
    u = [_bdot(_b(inv[q]), rhs[q]) for q in pairs]
    uv_s = [jnp.concatenate([_b(u[q]), v_s[q]], axis=0) for q in pairs]
    y2 = [y0[q] + _bdot(_b(a_r[q]), uv_s[q]) for q in pairs]
    y = jnp.concatenate([m[:rows] + m[rows:] for m in y2], axis=1)
    for q in pairs:
        if groups == 1:
            ds = _bdot(uv_s[q], bk_s[q], _TN)
            sbd_ref[0, q] = (sbd_ref[0, q] + ds) * p_incl[rows - 1:rows, lanes_of[q]]
        else:
            for g in range(groups):
                ds = _bdot(_b(group_rows((u[q], bds[q][4]), g)),
                           _b(group_rows((bds[q][2], bds[q][3]), g)), _TN)
                end = (g + 1) * chunk - 1
                sbd_ref[g, q] = (sbd_ref[g, q] + ds) * p_incl[end:end + 1, lanes_of[q]]

    @pl.when(last)
    def _():
        for g in range(groups):
            for q in range(HEAD_PAIRS):
                blk = sbd_ref[g, q]
                so_ref[g, 2 * q] = blk[:hd, :hd]
                so_ref[g, 2 * q + 1] = blk[hd:, hd:]

    inv_n = 1.0 / B_HEAD_DIM
    d = y - seg_sum(y) * inv_n
    var = seg_sum(d * d) * inv_n
    yn = d * lax.rsqrt(var + GN_EPS) * vec(_V_LNG) + vec(_V_LNB)
    o_ref[...] = ((yn + bonus) * gate).astype(BF16)


def _rwkv(proj, shift_parts, wkv_prev, lw, *, batch, seq):
    whole_seq = seq <= WKV_CHUNK
    if whole_seq:
        assert seq == SUBLANES, "whole-sequence blocks rely on one sublane tile per sequence"
        chunk, groups = seq, WKV_CHUNK // seq
    else:
        chunk, groups = WKV_CHUNK, 1
    rows = groups * chunk
    n_chunks = seq // chunk
    row_blk = lambda bi, c: bi * n_chunks + c
    in_specs = [
        pl.BlockSpec((rows, B_WIDTH), lambda bi, c: (row_blk(bi, c), COL_R // B_WIDTH)),
        pl.BlockSpec((rows, B_WIDTH), lambda bi, c: (row_blk(bi, c), COL_R // B_WIDTH + 1)),
        pl.BlockSpec((rows, B_WIDTH), lambda bi, c: (row_blk(bi, c), COL_R // B_WIDTH + 2)),
        pl.BlockSpec((rows, LORA_PAD), lambda bi, c: (row_blk(bi, c), COL_L // LORA_PAD)),
        pl.BlockSpec((groups, 1, B_WIDTH), lambda bi, c: (bi, 0, 0)),
        pl.BlockSpec((groups, 1, B_WIDTH), lambda bi, c: (bi, 0, 0)),
        pl.BlockSpec((groups, 1, B_WIDTH), lambda bi, c: (bi, 0, 0)),
        pl.BlockSpec((groups, 1, LORA_PAD), lambda bi, c: (bi, 0, 0)),
        pl.BlockSpec((groups, B_HEADS, B_HEAD_DIM, B_HEAD_DIM), lambda bi, c: (bi, 0, 0, 0)),
        pl.BlockSpec((16, B_WIDTH), lambda bi, c: (0, 0)),
        pl.BlockSpec((SUBLANES, LORA_PAD), lambda bi, c: (0, 0)),
        pl.BlockSpec((LORA_PAD, B_WIDTH), lambda bi, c: (0, 0)),
        pl.BlockSpec((LORA_PAD, B_WIDTH), lambda bi, c: (0, 0)),
        pl.BlockSpec((LORA_PAD, B_WIDTH), lambda bi, c: (0, 0)),
        pl.BlockSpec((LANES, LANES), lambda bi, c: (0, 0)),
    ]
    kern = functools.partial(_rwkv_kernel, groups=groups, chunk=chunk, whole_seq=whole_seq)
    return pl.pallas_call(
        kern,
        grid=(batch // groups, n_chunks),
        in_specs=in_specs,
        out_specs=[
            pl.BlockSpec((rows, B_WIDTH), lambda bi, c: (row_blk(bi, c), 0)),
            pl.BlockSpec((groups, B_HEADS, B_HEAD_DIM, B_HEAD_DIM), lambda bi, c: (bi, 0, 0, 0)),
        ],
        out_shape=[
            jax.ShapeDtypeStruct((batch * seq, B_WIDTH), BF16),
            jax.ShapeDtypeStruct((batch, B_HEADS, B_HEAD_DIM, B_HEAD_DIM), F32),
        ],
        scratch_shapes=[
            pltpu.VMEM((SUBLANES, B_WIDTH), F32),
            pltpu.VMEM((SUBLANES, B_WIDTH), F32),
            pltpu.VMEM((SUBLANES, B_WIDTH), F32),
            pltpu.VMEM((SUBLANES, LORA_PAD), F32),
            pltpu.VMEM((groups, HEAD_PAIRS, LANES, LANES), F32),
        ],
        compiler_params=_cparams(("parallel", "arbitrary")),
        name="rwkv7",
    )(proj, proj, proj, proj, *shift_parts, wkv_prev,
      lw["rk_vecs"], lw["rk_mu_l"], lw["rk_w2"], lw["rk_a2"], lw["rk_g2"], lw["seg_ones"])


def _out_proj_kernel(a_ref, b_ref, c_ref, w_ref, x_ref, gpost_ref, gffn_ref, x1_ref, h_ref):
    acc = jnp.dot(a_ref[...], w_ref[0:A_WIDTH, :], preferred_element_type=F32)
    acc += jnp.dot(b_ref[...], w_ref[A_WIDTH:A_WIDTH + B_WIDTH, :], preferred_element_type=F32)
    acc += jnp.dot(c_ref[...], w_ref[A_WIDTH + B_WIDTH:, :], preferred_element_type=F32)
    x1 = x_ref[...] + _rms(acc, gpost_ref[...])
    x1_ref[...] = x1
    h_ref[...] = _rms(x1, gffn_ref[...]).astype(BF16)


def _out_proj(a_out, b_out, c_out, w_out, x, g_post, g_ffn, *, tm=256):
    t = x.shape[0]
    row = lambda i: (i, 0)
    fixed = lambda i: (0, 0)
    return pl.pallas_call(
        _out_proj_kernel,
        grid=(t // tm,),
        in_specs=[
            pl.BlockSpec((tm, A_WIDTH), row),
            pl.BlockSpec((tm, B_WIDTH), row),
            pl.BlockSpec((tm, C_WIDTH), row),
            pl.BlockSpec((D_MODEL, D_MODEL), fixed),
            pl.BlockSpec((tm, D_MODEL), row),
            pl.BlockSpec((1, D_MODEL), fixed),
            pl.BlockSpec((1, D_MODEL), fixed),
        ],
        out_specs=[pl.BlockSpec((tm, D_MODEL), row), pl.BlockSpec((tm, D_MODEL), row)],
        out_shape=[jax.ShapeDtypeStruct((t, D_MODEL), F32), jax.ShapeDtypeStruct((t, D_MODEL), BF16)],
        compiler_params=_cparams(("parallel",)),
        name="out_proj",
    )(a_out, b_out, c_out, w_out, x, g_post, g_ffn)


def _ffn_up_kernel(h_ref, wg_ref, wv_ref, cwg_ref, cwv_ref, cbg_ref, cbv_ref, pg_ref, pv_ref,
                   act_ref, ng_ref, nv_ref, cg_ref, cv_ref, *, tm, seq):
    h = h_ref[...]
    tn = wg_ref.shape[1]
    row_id = lax.broadcasted_iota(jnp.int32, (tm, 1), 0)

    def conv_half(w_ref, cw_ref, cb_ref, p_ref, n_ref, carry_ref):
        up = jnp.dot(h, w_ref[...], preferred_element_type=F32)
        if seq <= tm:
            n_seq = tm // seq
            prev = p_ref[...]
            e0 = jnp.broadcast_to(prev[:, 0:1, :], (n_seq, seq, tn)).reshape(tm, tn)
            e1 = jnp.broadcast_to(prev[:, 1:2, :], (n_seq, seq, tn)).reshape(tm, tn)
            tau = row_id & (seq - 1)
            m1 = jnp.where(tau == 0, e1, pltpu.roll(up, 1, 0))
            m2 = jnp.where(tau == 0, e0, jnp.where(tau == 1, e1, pltpu.roll(up, 2, 0)))
            n_ref[...] = up.reshape(n_seq, seq, tn)[:, seq - 2:, :]
        else:
            @pl.when(pl.program_id(1) % (seq // tm) == 0)
            def _():
                carry_ref[...] = jnp.concatenate(
                    [jnp.zeros((SUBLANES - 2, tn), F32), p_ref[0]], axis=0)

            ext = jnp.concatenate([carry_ref[...], up], axis=0)
            m1 = pltpu.roll(ext, 1, 0)[SUBLANES:]
            m2 = pltpu.roll(ext, 2, 0)[SUBLANES:]
            carry_ref[...] = up[tm - SUBLANES:]
            n_ref[0] = up[tm - 2:]
        return cb_ref[...] + m2 * cw_ref[0:1, :] + m1 * cw_ref[1:2, :] + up * cw_ref[2:3, :]

    gate = conv_half(wg_ref, cwg_ref, cbg_ref, pg_ref, ng_ref, cg_ref)
    val = conv_half(wv_ref, cwv_ref, cbv_ref, pv_ref, nv_ref, cv_ref)
    act_ref[...] = (jax.nn.gelu(gate, approximate=True) * val).astype(BF16)


def _ffn_up(h, w_up, conv_w, conv_b, conv_prev, *, batch, seq):
    t = h.shape[0]
    if seq <= SUBLANES:
        assert seq == SUBLANES, "whole-sequence tiles rely on one sublane tile per sequence"
        tm = t
        state_blk = (batch, 2, FFN_TN)
        state_idx = lambda half: (lambda j, i: (0, 0, j + half * FFN_N_TILES))
    else:
        tm = 512
        tiles_per_seq = seq // tm
        state_blk = (1, 2, FFN_TN)
        state_idx = lambda half: (lambda j, i: (i // tiles_per_seq, 0, j + half * FFN_N_TILES))
    col = lambda half: (lambda j, i: (0, j + half * FFN_N_TILES))
    kern = functools.partial(_ffn_up_kernel, tm=tm, seq=seq)
    state_out = jax.ShapeDtypeStruct((batch, 2, D_FF), F32)
    return pl.pallas_call(
        kern,
        grid=(FFN_N_TILES, t // tm),
        in_specs=[
            pl.BlockSpec((tm, D_MODEL), lambda j, i: (i, 0)),
            pl.BlockSpec((D_MODEL, FFN_TN), col(0)),
            pl.BlockSpec((D_MODEL, FFN_TN), col(1)),
            pl.BlockSpec((3, FFN_TN), col(0)),
            pl.BlockSpec((3, FFN_TN), col(1)),
            pl.BlockSpec((1, FFN_TN), col(0)),
            pl.BlockSpec((1, FFN_TN), col(1)),
            pl.BlockSpec(state_blk, state_idx(0)),
            pl.BlockSpec(state_blk, state_idx(1)),
        ],
        out_specs=[
            pl.BlockSpec((tm, FFN_TN), lambda j, i: (i, j)),
            pl.BlockSpec(state_blk, state_idx(0)),
            pl.BlockSpec(state_blk, state_idx(0)),
        ],
        out_shape=[jax.ShapeDtypeStruct((t, D_FF), BF16), state_out, state_out],
        scratch_shapes=[pltpu.VMEM((SUBLANES, FFN_TN), F32), pltpu.VMEM((SUBLANES, FFN_TN), F32)],
        compiler_params=_cparams(("parallel", "arbitrary")),
        name="ffn_up_conv",
    )(h, w_up, w_up, conv_w, conv_w, conv_b, conv_b, conv_prev, conv_prev)


def _ffn_down_kernel(a_ref, w_ref, x_ref, g_ref, o_ref, acc_ref):
    kk = pl.program_id(1)

    @pl.when(kk == 0)
    def _():
        acc_ref[...] = jnp.zeros_like(acc_ref)

    acc_ref[...] += jnp.dot(a_ref[...], w_ref[...], preferred_element_type=F32)

    @pl.when(kk == pl.num_programs(1) - 1)
    def _():
        o_ref[...] = x_ref[...] + _rms(acc_ref[...], g_ref[...])


def _ffn_down(act, w_down, x1, g_post, *, tm=512, tk=D_FF // 4):
    t = x1.shape[0]
    return pl.pallas_call(
        _ffn_down_kernel,
        grid=(t // tm, D_FF // tk),
        in_specs=[
            pl.BlockSpec((tm, tk), lambda i, k: (i, k)),
            pl.BlockSpec((tk, D_MODEL), lambda i, k: (k, 0)),
            pl.BlockSpec((tm, D_MODEL), lambda i, k: (i, 0)),
            pl.BlockSpec((1, D_MODEL), lambda i, k: (0, 0)),
        ],
        out_specs=pl.BlockSpec((tm, D_MODEL), lambda i, k: (i, 0)),
        out_shape=jax.ShapeDtypeStruct((t, D_MODEL), F32),
        scratch_shapes=[pltpu.VMEM((tm, D_MODEL), F32)],
        compiler_params=_cparams(("parallel", "arbitrary")),
        name="ffn_down",
    )(act, w_down, x1, g_post)


def _pad_rows(w, row0, total):
    return jnp.zeros((total, w.shape[1]), w.dtype).at[row0:row0 + w.shape[0]].set(w)


def _prepare_layer(p, l):
    w_in = p["w_in"][l]
    b0 = 2 * A_WIDTH
    q0 = b0 + B_PROJ
    w_in_p = jnp.concatenate([
        w_in[:, :b0 + 3 * B_WIDTH],
        w_in[:, q0:],
        w_in[:, b0 + 3 * B_WIDTH:q0],
        jnp.zeros((D_MODEL, IN_COLS_PAD - COL_L - 3 * LORA), w_in.dtype),
    ], axis=1).astype(BF16)
    mu = p["rk_mu"][l]
    vec_rows = [mu[:B_WIDTH], mu[B_WIDTH:2 * B_WIDTH], mu[2 * B_WIDTH:3 * B_WIDTH], p["rk_w0"][l],
                p["rk_a0"][l], p["rk_kk"][l], p["rk_ka"][l], p["rk_rk"][l].reshape(B_WIDTH),
                p["rk_lnx_g"][l], p["rk_lnx_b"][l]]
    vecs = jnp.zeros((16, B_WIDTH), F32).at[:len(vec_rows)].set(jnp.stack(vec_rows))
    mu_l = jnp.zeros((SUBLANES, LORA_PAD), F32).at[0, :3 * LORA].set(mu[3 * B_WIDTH:])
    head_of = jnp.arange(LANES) // B_HEAD_DIM
    seg_ones = (head_of[:, None] == head_of[None, :]).astype(BF16)
    tril = jnp.tril(jnp.ones((CHUNK, CHUNK), bool))
    ws = jnp.where(tril[None], p["gm_ws"][l], 0.0)
    bs = p["gm_bs"][l]
    return {
        "g_mix_pre": p["norm_mix_pre"][l][None], "g_mix_post": p["norm_mix_post"][l][None],
        "g_ffn_pre": p["norm_ffn_pre"][l][None], "g_ffn_post": p["norm_ffn_post"][l][None],
        "g_mem": p["norm_mem"][l][None],
        "w_in": w_in_p,
        "w_out": p["w_out"][l].astype(BF16),
        "w_mkv": jnp.concatenate([p["w_mem_k"][l], p["w_mem_v"][l]], axis=1).astype(BF16),
        "gm_ln_g": p["gm_ln_g"][l][None], "gm_ln_b": p["gm_ln_b"][l][None],
        "gm_ws": ws, "gm_bs": bs,
        "rk_vecs": vecs, "rk_mu_l": mu_l,
        "rk_w2": _pad_rows(p["rk_w2"][l], 0, LORA_PAD).astype(BF16),
        "rk_a2": _pad_rows(p["rk_a2"][l], LORA, LORA_PAD).astype(BF16),
        "rk_g2": _pad_rows(p["rk_g2"][l], 2 * LORA, LORA_PAD).astype(BF16),
        "seg_ones": seg_ones,
        "w_up": p["ffn_w_up"][l].astype(BF16),
        "conv_w": p["ffn_conv_w"][l], "conv_b": p["ffn_conv_b"][l][None],
        "w_down": p["ffn_w_down"][l].astype(BF16),
    }


def _spatial_weights(lw, seq):
    ws, bs = lw["gm_ws"], lw["gm_bs"]
    if seq >= CHUNK:
        w_blk, b_rows = ws, bs
    else:
        reps = CHUNK // seq
        eye = jnp.eye(reps, dtype=ws.dtype)
        w_blk = jnp.einsum("ab,hts->hatbs", eye, ws[:, :seq, :seq]).reshape(A_HEADS, CHUNK, CHUNK)
        b_rows = jnp.tile(bs[:, :seq], (1, reps))
    bias_full = jnp.repeat(b_rows.T, A_HEAD_DIM, axis=1)
    return w_blk.astype(BF16), bias_full


def _decoder_layer(x, mem_k, mem_v, shift_prev, wkv_prev, conv_prev, lw):
    batch, seq, _ = x.shape
    x2 = x.reshape(batch * seq, D_MODEL)
    proj = _norm_matmul(x2, lw["g_mix_pre"], lw["w_in"], tm=512, tn=PROJ_TN)
    w_sp, bias_full = _spatial_weights(lw, seq)
    a_out, a_v = _group_a(proj, lw["gm_ln_g"], lw["gm_ln_b"], w_sp, bias_full)
    shift_parts = [
        shift_prev[:, None, :B_WIDTH], shift_prev[:, None, B_WIDTH:2 * B_WIDTH],
        shift_prev[:, None, 2 * B_WIDTH:3 * B_WIDTH],
        jnp.pad(shift_prev[:, None, 3 * B_WIDTH:], ((0, 0), (0, 0), (0, LORA_PAD - 3 * LORA))),
    ]
    b_out, wkv_new = _rwkv(proj, shift_parts, wkv_prev, lw, batch=batch, seq=seq)
    c_out = _attention(proj, mem_k.reshape(batch, MEM_LEN, C_WIDTH), mem_v.reshape(batch, MEM_LEN, C_WIDTH),
                       batch=batch, seq=seq)
    x1, h = _out_proj(a_out, b_out, c_out, lw["w_out"], x2, lw["g_mix_post"], lw["g_ffn_pre"])
    act, conv_g, conv_v = _ffn_up(h, lw["w_up"], lw["conv_w"], lw["conv_b"], conv_prev, batch=batch, seq=seq)
    y = _ffn_down(act, lw["w_down"], x1, lw["g_ffn_post"])

    chunk_start = ((seq - 1) // CHUNK) * CHUNK
    chunk_v = a_v.reshape(batch, seq, A_HEADS, A_HEAD_DIM)[:, chunk_start:]
    last = proj.reshape(batch, seq, IN_COLS_PAD)[:, -1]
    shift_new = jnp.concatenate([last[:, COL_R:COL_R + 3 * B_WIDTH], last[:, COL_L:COL_L + 3 * LORA]], axis=1)
    conv_new = jnp.concatenate([conv_g, conv_v], axis=-1)
    return y.reshape(batch, seq, D_MODEL), chunk_v, shift_new, wkv_new, conv_new


def kernel(x_prompt, x_sample, mem_prompt, cache_mem_k, cache_mem_v, state_shift, state_wkv, state_conv,
           norm_mix_pre, norm_mix_post, norm_ffn_pre, norm_ffn_post, norm_mem, w_in, w_out, w_mem_k, w_mem_v,
           gm_ln_g, gm_ln_b, gm_ws, gm_bs, rk_mu, rk_w0, rk_w2, rk_a0, rk_a2, rk_g2, rk_kk, rk_ka, rk_rk,
           rk_lnx_g, rk_lnx_b, ffn_w_up, ffn_conv_w, ffn_conv_b, ffn_w_down):
    params = dict(
        norm_mix_pre=norm_mix_pre, norm_mix_post=norm_mix_post, norm_ffn_pre=norm_ffn_pre,
        norm_ffn_post=norm_ffn_post, norm_mem=norm_mem, w_in=w_in, w_out=w_out, w_mem_k=w_mem_k,
        w_mem_v=w_mem_v, gm_ln_g=gm_ln_g, gm_ln_b=gm_ln_b, gm_ws=gm_ws, gm_bs=gm_bs, rk_mu=rk_mu,
        rk_w0=rk_w0, rk_w2=rk_w2, rk_a0=rk_a0, rk_a2=rk_a2, rk_g2=rk_g2, rk_kk=rk_kk, rk_ka=rk_ka,
        rk_rk=rk_rk, rk_lnx_g=rk_lnx_g, rk_lnx_b=rk_lnx_b, ffn_w_up=ffn_w_up, ffn_conv_w=ffn_conv_w,
        ffn_conv_b=ffn_conv_b, ffn_w_down=ffn_w_down)
    depth = w_in.shape[0]
    bp = x_prompt.shape[0]
    y_p, y_s = x_prompt, x_sample
    outs = [[] for _ in range(10)]
    for l in range(depth):
        lw = _prepare_layer(params, l)
        mem2 = mem_prompt.reshape(bp * MEM_LEN, D_MODEL)
        mkv = _norm_matmul(mem2, lw["g_mem"], lw["w_mkv"], tm=512, tn=2 * C_WIDTH)
        mk = mkv[:, :C_WIDTH].reshape(bp, MEM_LEN, C_HEADS, C_HEAD_DIM)
        mv = mkv[:, C_WIDTH:].reshape(bp, MEM_LEN, C_HEADS, C_HEAD_DIM)
        zero_shift = jnp.zeros((bp, B_PROJ), x_prompt.dtype)
        zero_wkv = jnp.zeros((bp, B_HEADS, B_HEAD_DIM, B_HEAD_DIM), F32)
        zero_conv = jnp.zeros((bp, 2, 2 * D_FF), x_prompt.dtype)
        y_p, cv, sh, wkv, conv = _decoder_layer(y_p, mk, mv, zero_shift, zero_wkv, zero_conv, lw)
        for lst, val in zip(outs[:6], (mk, mv, cv, sh, wkv, conv)):
            lst.append(val)
        y_s, cv, sh, wkv, conv = _decoder_layer(y_s, cache_mem_k[l], cache_mem_v[l], state_shift[l],
                                                state_wkv[l], state_conv[l], lw)
        for lst, val in zip(outs[6:], (cv, sh, wkv, conv)):
            lst.append(val)
    return (y_p, y_s) + tuple(jnp.stack(o) for o in outs)
```

```python
import functools
import math

import jax
import jax.numpy as jnp
from jax import lax
from jax.experimental import pallas as pl
from jax.experimental.pallas import tpu as pltpu

D_MODEL = 2048
MEM_LEN = 256
CHUNK = 128
A_HEADS, A_HEAD_DIM = 4, 128
A_WIDTH = A_HEADS * A_HEAD_DIM
B_HEADS, B_HEAD_DIM = 16, 64
B_WIDTH = B_HEADS * B_HEAD_DIM
LORA = 64
B_PROJ = 3 * B_WIDTH + 3 * LORA
C_HEADS, C_HEAD_DIM = 4, 128
C_WIDTH = C_HEADS * C_HEAD_DIM
D_FF = 5632
RMS_EPS = 1e-6
LN_EPS = 1e-5
GN_EPS = 64e-5
DECAY_OFFSET = 0.5

LANES = 128
SUBLANES = 8
VMEM_LIMIT_BYTES = 56 * 1024 * 1024

LORA_PAD = 256
COL_A = 0
COL_R = 2 * A_WIDTH
COL_Q = COL_R + 3 * B_WIDTH
COL_L = COL_Q + C_WIDTH
IN_COLS_PAD = 5120
PROJ_TN = 1024

WKV_CHUNK = 64
FFN_N_TILES = 2
FFN_TN = D_FF // FFN_N_TILES
FFN_SUB = 256

F32 = jnp.float32
BF16 = jnp.bfloat16


def _cparams(sem):
    return pltpu.CompilerParams(dimension_semantics=sem, vmem_limit_bytes=VMEM_LIMIT_BYTES)


def _rms(x, g):
    return x * lax.rsqrt(jnp.mean(x * x, axis=-1, keepdims=True) + RMS_EPS) * g


def _norm_matmul_kernel(x_ref, g_ref, w_ref, o_ref, h_ref, *, w_is_nk):
    @pl.when(pl.program_id(1) == 0)
    def _():
        h_ref[...] = _rms(x_ref[...], g_ref[...]).astype(BF16)

    dims = (((1,), (1,)), ((), ())) if w_is_nk else (((1,), (0,)), ((), ()))
    o_ref[...] = lax.dot_general(h_ref[...], w_ref[...], dims, preferred_element_type=F32)


def _norm_matmul(x, g, w, *, tm, tn, w_is_nk=False):
    t, k = x.shape
    n = w.shape[0] if w_is_nk else w.shape[1]
    w_spec = (pl.BlockSpec((tn, k), lambda i, j: (j, 0)) if w_is_nk
              else pl.BlockSpec((k, tn), lambda i, j: (0, j)))
    return pl.pallas_call(
        functools.partial(_norm_matmul_kernel, w_is_nk=w_is_nk),
        grid=(t // tm, n // tn),
        in_specs=[
            pl.BlockSpec((tm, k), lambda i, j: (i, 0)),
            pl.BlockSpec((1, k), lambda i, j: (0, 0)),
            w_spec,
        ],
        out_specs=pl.BlockSpec((tm, tn), lambda i, j: (i, j)),
        out_shape=jax.ShapeDtypeStruct((t, n), F32),
        scratch_shapes=[pltpu.VMEM((tm, k), BF16)],
        compiler_params=_cparams(("parallel", "arbitrary")),
        name="norm_matmul",
    )(x, g, w)


def _group_a_kernel(u_ref, v_ref, g_ref, b_ref, w_ref, bias_ref, o_ref, vout_ref):
    u = jax.nn.gelu(u_ref[...], approximate=True)
    v = jax.nn.gelu(v_ref[...], approximate=True)
    mean = jnp.mean(v, axis=-1, keepdims=True)
    d = v - mean
    var = jnp.mean(d * d, axis=-1, keepdims=True)
    vn = d * lax.rsqrt(var + LN_EPS) * g_ref[...] + b_ref[...]
    vout_ref[...] = vn
    vb = vn.astype(BF16)
    for h in range(A_HEADS):
        hs = slice(h * A_HEAD_DIM, (h + 1) * A_HEAD_DIM)
        mixed = jnp.dot(w_ref[h], vb[:, hs], preferred_element_type=F32) + bias_ref[:, hs]
        o_ref[:, hs] = (u[:, hs] * mixed).astype(BF16)


def _group_a(proj, ln_g, ln_b, w_sp, bias_full):
    t = proj.shape[0]
    return pl.pallas_call(
        _group_a_kernel,
        grid=(t // CHUNK,),
        in_specs=[
            pl.BlockSpec((CHUNK, A_WIDTH), lambda i: (i, COL_A // A_WIDTH)),
            pl.BlockSpec((CHUNK, A_WIDTH), lambda i: (i, COL_A // A_WIDTH + 1)),
            pl.BlockSpec((1, A_WIDTH), lambda i: (0, 0)),
            pl.BlockSpec((1, A_WIDTH), lambda i: (0, 0)),
            pl.BlockSpec((A_HEADS, CHUNK, CHUNK), lambda i: (0, 0, 0)),
            pl.BlockSpec((CHUNK, A_WIDTH), lambda i: (0, 0)),
        ],
        out_specs=[
            pl.BlockSpec((CHUNK, A_WIDTH), lambda i: (i, 0)),
            pl.BlockSpec((CHUNK, A_WIDTH), lambda i: (i, 0)),
        ],
        out_shape=[
            jax.ShapeDtypeStruct((t, A_WIDTH), BF16),
            jax.ShapeDtypeStruct((t, A_WIDTH), F32),
        ],
        compiler_params=_cparams(("parallel",)),
        name="group_a",
    )(proj, proj, ln_g, ln_b, w_sp, bias_full)


def _softmax_rows(s):
    e = jnp.exp(s - jnp.max(s, axis=-1, keepdims=True))
    return e / jnp.sum(e, axis=-1, keepdims=True)


def _attn_kernel(q_ref, k_ref, v_ref, o_ref, *, n_b, tq, head_major_rows):
    scale = C_HEAD_DIM ** -0.5
    heads = range(C_HEADS)
    lanes = lambda h: slice(h * C_HEAD_DIM, (h + 1) * C_HEAD_DIM)
    if not head_major_rows:
        chains = [(g, h) for g in range(n_b) for h in heads]
        rows = lambda g: slice(g * tq, (g + 1) * tq)
        s = [lax.dot_general(q_ref[rows(g), lanes(h)].astype(BF16), k_ref[g, :, lanes(h)].astype(BF16), _NT,
                             preferred_element_type=F32) * scale for g, h in chains]
        p = [_softmax_rows(m).astype(BF16) for m in s]
        o = [jnp.dot(m, v_ref[g, :, lanes(h)].astype(BF16), preferred_element_type=F32)
             for m, (g, h) in zip(p, chains)]
        for m, (g, h) in zip(o, chains):
            o_ref[rows(g), lanes(h)] = m.astype(BF16)
        return
    row = lax.broadcasted_iota(jnp.int32, (C_HEADS * tq, 1), 0)
    col = lax.broadcasted_iota(jnp.int32, (1, C_HEADS * MEM_LEN), 1)
    own = (col & (C_HEADS - 1)) == (row >> int(math.log2(tq)))
    q = [q_ref[g * tq:(g + 1) * tq, :] for g in range(n_b)]
    qs = [jnp.concatenate([m[:, lanes(h)] for h in heads], axis=0).astype(BF16) for m in q]
    s = [lax.dot_general(m, k_ref[g].astype(BF16), _NT, preferred_element_type=F32) * scale
         for g, m in enumerate(qs)]
    p = [_softmax_rows(jnp.where(own, m, -1e30)).astype(BF16) for m in s]
    o = [jnp.dot(m, v_ref[g].astype(BF16), preferred_element_type=F32) for g, m in enumerate(p)]
    for g, m in enumerate(o):
        for h in heads:
            o_ref[g * tq:(g + 1) * tq, lanes(h)] = m[h * tq:(h + 1) * tq].astype(BF16)


def _attention(proj, mem_k, mem_v, *, batch, seq):
    head_major_rows = mem_k.shape[-1] == C_HEAD_DIM
    if head_major_rows:
        tq, n_b = seq, 8
    else:
        tq, n_b = 512, 1
    n_q = seq // tq
    mem_blk = (n_b,) + mem_k.shape[1:]
    return pl.pallas_call(
        functools.partial(_attn_kernel, n_b=n_b, tq=tq, head_major_rows=head_major_rows),
        grid=(batch // n_b, n_q),
        in_specs=[
            pl.BlockSpec((n_b * tq, C_WIDTH), lambda b, i: (b * n_q + i, COL_Q // C_WIDTH)),
            pl.BlockSpec(mem_blk, lambda b, i: (b, 0, 0)),
            pl.BlockSpec(mem_blk, lambda b, i: (b, 0, 0)),
        ],
        out_specs=pl.BlockSpec((n_b * tq, C_WIDTH), lambda b, i: (b * n_q + i, 0)),
        out_shape=jax.ShapeDtypeStruct((batch * seq, C_WIDTH), BF16),
        compiler_params=_cparams(("parallel", "arbitrary")),
        name="mem_attention",
    )(proj, mem_k, mem_v)


_V_MU_R, _V_MU_K, _V_MU_V, _V_W0, _V_A0, _V_KK, _V_KA, _V_RK, _V_LNG, _V_LNB = range(10)


_NN = (((1,), (0,)), ((), ()))
_NT = (((1,), (1,)), ((), ()))
_TN = (((0,), (0,)), ((), ()))
HEAD_PAIRS = B_HEADS // 2


def _split(x):
    hi = x.astype(BF16)
    return hi, (x - hi.astype(F32)).astype(BF16)


def _b(x):
    return x.astype(BF16)


def _bdot(a, b, dims=_NN):
    return lax.dot_general(a, b, dims, preferred_element_type=F32)


def _rwkv_kernel(pr_ref, pk_ref, pv_ref, pl_ref, sr_ref, sk_ref, sv_ref, sl_ref, wkv_ref,
                 vec_ref, mul_ref, w2_ref, a2_ref, g2_ref, seg_ref,
                 o_ref, so_ref, cr_ref, ck_ref, cv_ref, cl_ref, sbd_ref, *, groups, chunk, whole_seq):
    rows = groups * chunk
    first = pl.program_id(1) == 0
    last = pl.program_id(1) == pl.num_programs(1) - 1
    hd = B_HEAD_DIM

    @pl.when(first)
    def _():
        zero = jnp.zeros((hd, hd), F32)
        for g in range(groups):
            for q in range(HEAD_PAIRS):
                top = jnp.concatenate([wkv_ref[g, 2 * q], zero], axis=1)
                bot = jnp.concatenate([zero, wkv_ref[g, 2 * q + 1]], axis=1)
                sbd_ref[g, q] = jnp.concatenate([top, bot], axis=0)

    row_id = lax.broadcasted_iota(jnp.int32, (rows, 1), 0)

    def prev_rows(x, s_ref, carry_ref):
        width = x.shape[1]
        if whole_seq:
            start = jnp.broadcast_to(s_ref[...], (groups, chunk, width)).reshape(rows, width)
            return jnp.where((row_id & (chunk - 1)) == 0, start, pltpu.roll(x, 1, 0))

        @pl.when(first)
        def _():
            carry_ref[...] = jnp.broadcast_to(s_ref[0], (SUBLANES, width))

        ext = jnp.concatenate([carry_ref[...], x], axis=0)
        prev = pltpu.roll(ext, 1, 0)[SUBLANES:]
        carry_ref[...] = x[rows - SUBLANES:]
        return prev

    def shifted(p_ref, s_ref, carry_ref, mu):
        x = p_ref[...]
        return x + (prev_rows(x, s_ref, carry_ref) - x) * mu

    vec = lambda i: vec_ref[i:i + 1, :]
    r = shifted(pr_ref, sr_ref, cr_ref, vec(_V_MU_R))
    k = shifted(pk_ref, sk_ref, ck_ref, vec(_V_MU_K))
    v = shifted(pv_ref, sv_ref, cv_ref, vec(_V_MU_V))
    lo = shifted(pl_ref, sl_ref, cl_ref, mul_ref[0:1, :])

    seg_ones = seg_ref[...]

    def seg_sum(x):
        slabs = jnp.concatenate([x[:, q * LANES:(q + 1) * LANES] for q in range(HEAD_PAIRS)], axis=0)
        hi, lo = _split(slabs)
        s = (jnp.dot(hi, seg_ones, preferred_element_type=F32)
             + jnp.dot(lo, seg_ones, preferred_element_type=F32))
        return jnp.concatenate([s[q * rows:(q + 1) * rows] for q in range(HEAD_PAIRS)], axis=1)

    dw = jnp.dot(jnp.tanh(lo).astype(BF16), w2_ref[...], preferred_element_type=F32)
    da = jnp.dot(lo.astype(BF16), a2_ref[...], preferred_element_type=F32)
    gate = jnp.dot(jax.nn.sigmoid(lo).astype(BF16), g2_ref[...], preferred_element_type=F32)
    w_log = -jax.nn.softplus(-(vec(_V_W0) + dw)) - DECAY_OFFSET
    log_decay = -jnp.exp(w_log)
    a = jax.nn.sigmoid(vec(_V_A0) + da)
    kk = k * vec(_V_KK)
    kk = kk / jnp.maximum(jnp.sqrt(seg_sum(kk * kk)), 1e-12)
    k = k * (1.0 + (a - 1.0) * vec(_V_KA))
    bonus = seg_sum(r * k * vec(_V_RK)) * v

    col_id = lax.broadcasted_iota(jnp.int32, (1, rows), 1)
    chunk_bits = int(math.log2(chunk))
    incl = ((row_id >> chunk_bits) == (col_id >> chunk_bits)) & (col_id <= row_id)
    incl_b = incl.astype(BF16)
    ld_hi, ld_mid = _split(log_decay)
    ld_lo = (log_decay - ld_hi.astype(F32) - ld_mid.astype(F32)).astype(BF16)
    cum = (jnp.dot(incl_b, ld_hi, preferred_element_type=F32)
           + jnp.dot(incl_b, ld_mid, preferred_element_type=F32)
           + jnp.dot(incl_b, ld_lo, preferred_element_type=F32))
    p_incl = jnp.exp(cum)
    p_inv = jnp.exp(-cum)
    a_t = -kk * jnp.exp(cum - log_decay)
    b_t = kk * a * p_inv
    k_t = k * p_inv
    r_t = r * p_incl

    pr = 2 * rows
    row2 = lax.broadcasted_iota(jnp.int32, (pr, 1), 0)
    col2 = lax.broadcasted_iota(jnp.int32, (1, pr), 1)
    t2, s2 = row2 & (rows - 1), col2 & (rows - 1)
    same2 = ((row2 >> chunk_bits) == (col2 >> chunk_bits))
    incl2 = same2 & (s2 <= t2)
    strict2 = same2 & (s2 < t2)
    eye2 = (row2 == col2).astype(F32)
    left = lax.broadcasted_iota(jnp.int32, (rows, LANES), 1) < hd

    def bd(x):
        zero = jnp.zeros_like(x)
        return jnp.concatenate([jnp.where(left, x, zero), jnp.where(left, zero, x)], axis=0)

    def group_rows(mats, g):
        starts = [hh * rows + g * chunk for hh in (0, 1)]
        return jnp.concatenate([m[i:i + chunk] for m in mats for i in starts], axis=0)

    n_sq = chunk_bits - 1
    pairs = range(HEAD_PAIRS)
    lanes_of = [slice(q * LANES, (q + 1) * LANES) for q in pairs]
    bds = [[bd(t[:, ls]) for t in (a_t, r_t, b_t, k_t, v)] for ls in lanes_of]
    ar_s = [_b(jnp.concatenate([m[0], m[1]], axis=0)) for m in bds]
    bk_s = [_b(jnp.concatenate([m[2], m[3]], axis=0)) for m in bds]
    v_s = [_b(m[4]) for m in bds]
    ms = [_bdot(ar_s[q], bk_s[q], _NT) for q in pairs]
    a_ab = [jnp.where(strict2, m[:pr, :pr], 0.0) for m in ms]
    a_ak = [jnp.where(strict2, m[:pr, pr:], 0.0) for m in ms]
    col4 = lax.broadcasted_iota(jnp.int32, (1, 2 * pr), 1)
    incl4 = ((row2 >> chunk_bits) == ((col4 & (pr - 1)) >> chunk_bits)) & ((col4 & (rows - 1)) <= t2)
    a_r = [jnp.where(incl4, m[pr:], 0.0) for m in ms]
    inv = [eye2 + n for n in a_ab]
    pw_s = [_b(n) for n in a_ab]
    for _ in range(n_sq):
        pw_s = [_b(_bdot(p, p)) for p in pw_s]
        inv = [i + _bdot(p, _b(i)) for i, p in zip(inv, pw_s)]
    if groups == 1:
        xy = [_bdot(ar_s[q], _b(sbd_ref[0, q]), _NT) for q in pairs]
        x0, y0 = [m[:pr] for m in xy], [m[pr:] for m in xy]
    else:
        x0, y0 = [], []
        for q in pairs:
            x_parts, y_parts = [None] * (2 * groups), [None] * (2 * groups)
            for g in range(groups):
                xy = _bdot(_b(group_rows(bds[q][:2], g)), _b(sbd_ref[g, q]), _NT)
                for hh in (0, 1):
                    x_parts[hh * groups + g] = xy[hh * chunk:(hh + 1) * chunk]
                    y_parts[hh * groups + g] = xy[(2 + hh) * chunk:(3 + hh) * chunk]
            x0.append(jnp.concatenate(x_parts, axis=0))
            y0.append(jnp.concatenate(y_parts, axis=0))
    rhs = [_b(x0[q] + _bdot(_b(a_ak[q]), v_s[q])) for q in pairs]
    u = [_bdot(_b(inv[q]), rhs[q]) for q in pairs]
    uv_s = [jnp.concatenate([_b(u[q]), v_s[q]], axis=0) for q in pairs]
    y2 = [y0[q] + _bdot(_b(a_r[q]), uv_s[q]) for q in pairs]
    y = jnp.concatenate([m[:rows] + m[rows:] for m in y2], axis=1)
    for q in pairs:
        if groups == 1:
            ds = _bdot(uv_s[q], bk_s[q], _TN)
            sbd_ref[0, q] = (sbd_ref[0, q] + ds) * p_incl[rows - 1:rows, lanes_of[q]]
        else:
            for g in range(groups):
                ds = _bdot(_b(group_rows((u[q], bds[q][4]), g)),
                           _b(group_rows((bds[q][2], bds[q][3]), g)), _TN)
                end = (g + 1) * chunk - 1
                sbd_ref[g, q] = (sbd_ref[g, q] + ds) * p_incl[end:end + 1, lanes_of[q]]

    @pl.when(last)
    def _():
        for g in range(groups):
            for q in range(HEAD_PAIRS):
                blk = sbd_ref[g, q]
                so_ref[g, 2 * q] = blk[:hd, :hd]
                so_ref[g, 2 * q + 1] = blk[hd:, hd:]

    inv_n = 1.0 / B_HEAD_DIM
    d = y - seg_sum(y) * inv_n
    var = seg_sum(d * d) * inv_n
    yn = d * lax.rsqrt(var + GN_EPS) * vec(_V_LNG) + vec(_V_LNB)
    o_ref[...] = ((yn + bonus) * gate).astype(BF16)


def _rwkv(proj, shift_parts, wkv_prev, lw, *, batch, seq):
    whole_seq = seq <= WKV_CHUNK
    if whole_seq:
        assert seq == SUBLANES, "whole-sequence blocks rely on one sublane tile per sequence"
        chunk, groups = seq, WKV_CHUNK // seq
    else:
        chunk, groups = WKV_CHUNK, 1
    rows = groups * chunk
    n_chunks = seq // chunk
    row_blk = lambda bi, c: bi * n_chunks + c
    in_specs = [
        pl.BlockSpec((rows, B_WIDTH), lambda bi, c: (row_blk(bi, c), COL_R // B_WIDTH)),
        pl.BlockSpec((rows, B_WIDTH), lambda bi, c: (row_blk(bi, c), COL_R // B_WIDTH + 1)),
        pl.BlockSpec((rows, B_WIDTH), lambda bi, c: (row_blk(bi, c), COL_R // B_WIDTH + 2)),
        pl.BlockSpec((rows, LORA_PAD), lambda bi, c: (row_blk(bi, c), COL_L // LORA_PAD)),
        pl.BlockSpec((groups, 1, B_WIDTH), lambda bi, c: (bi, 0, 0)),
        pl.BlockSpec((groups, 1, B_WIDTH), lambda bi, c: (bi, 0, 0)),
        pl.BlockSpec((groups, 1, B_WIDTH), lambda bi, c: (bi, 0, 0)),
        pl.BlockSpec((groups, 1, LORA_PAD), lambda bi, c: (bi, 0, 0)),
        pl.BlockSpec((groups, B_HEADS, B_HEAD_DIM, B_HEAD_DIM), lambda bi, c: (bi, 0, 0, 0)),
        pl.BlockSpec((16, B_WIDTH), lambda bi, c: (0, 0)),
        pl.BlockSpec((SUBLANES, LORA_PAD), lambda bi, c: (0, 0)),
        pl.BlockSpec((LORA_PAD, B_WIDTH), lambda bi, c: (0, 0)),
        pl.BlockSpec((LORA_PAD, B_WIDTH), lambda bi, c: (0, 0)),
        pl.BlockSpec((LORA_PAD, B_WIDTH), lambda bi, c: (0, 0)),
        pl.BlockSpec((LANES, LANES), lambda bi, c: (0, 0)),
    ]
    kern = functools.partial(_rwkv_kernel, groups=groups, chunk=chunk, whole_seq=whole_seq)
    return pl.pallas_call(
        kern,
        grid=(batch // groups, n_chunks),
        in_specs=in_specs,
        out_specs=[
            pl.BlockSpec((rows, B_WIDTH), lambda bi, c: (row_blk(bi, c), 0)),
            pl.BlockSpec((groups, B_HEADS, B_HEAD_DIM, B_HEAD_DIM), lambda bi, c: (bi, 0, 0, 0)),
        ],
        out_shape=[
            jax.ShapeDtypeStruct((batch * seq, B_WIDTH), BF16),
            jax.ShapeDtypeStruct((batch, B_HEADS, B_HEAD_DIM, B_HEAD_DIM), F32),
        ],
        scratch_shapes=[
            pltpu.VMEM((SUBLANES, B_WIDTH), F32),
            pltpu.VMEM((SUBLANES, B_WIDTH), F32),
            pltpu.VMEM((SUBLANES, B_WIDTH), F32),
            pltpu.VMEM((SUBLANES, LORA_PAD), F32),
            pltpu.VMEM((groups, HEAD_PAIRS, LANES, LANES), F32),
        ],
        compiler_params=_cparams(("parallel", "arbitrary")),
        name="rwkv7",
    )(proj, proj, proj, proj, *shift_parts, wkv_prev,
      lw["rk_vecs"], lw["rk_mu_l"], lw["rk_w2"], lw["rk_a2"], lw["rk_g2"], lw["seg_ones"])


def _out_proj_kernel(a_ref, b_ref, c_ref, w_ref, x_ref, gpost_ref, gffn_ref, x1_ref, h_ref):
    acc = jnp.dot(a_ref[...], w_ref[0:A_WIDTH, :], preferred_element_type=F32)
    acc += jnp.dot(b_ref[...], w_ref[A_WIDTH:A_WIDTH + B_WIDTH, :], preferred_element_type=F32)
    acc += jnp.dot(c_ref[...], w_ref[A_WIDTH + B_WIDTH:, :], preferred_element_type=F32)
    x1 = x_ref[...] + _rms(acc, gpost_ref[...])
    x1_ref[...] = x1
    h_ref[...] = _rms(x1, gffn_ref[...]).astype(BF16)


def _out_proj(a_out, b_out, c_out, w_out, x, g_post, g_ffn, *, tm=256):
    t = x.shape[0]
    row = lambda i: (i, 0)
    fixed = lambda i: (0, 0)
    return pl.pallas_call(
        _out_proj_kernel,
        grid=(t // tm,),
        in_specs=[
            pl.BlockSpec((tm, A_WIDTH), row),
            pl.BlockSpec((tm, B_WIDTH), row),
            pl.BlockSpec((tm, C_WIDTH), row),
            pl.BlockSpec((D_MODEL, D_MODEL), fixed),
            pl.BlockSpec((tm, D_MODEL), row),
            pl.BlockSpec((1, D_MODEL), fixed),
            pl.BlockSpec((1, D_MODEL), fixed),
        ],
        out_specs=[pl.BlockSpec((tm, D_MODEL), row), pl.BlockSpec((tm, D_MODEL), row)],
        out_shape=[jax.ShapeDtypeStruct((t, D_MODEL), F32), jax.ShapeDtypeStruct((t, D_MODEL), BF16)],
        compiler_params=_cparams(("parallel",)),
        name="out_proj",
    )(a_out, b_out, c_out, w_out, x, g_post, g_ffn)


def _ffn_up_kernel(h_ref, wg_ref, wv_ref, cwg_ref, cwv_ref, cbg_ref, cbv_ref, pg_ref, pv_ref,
                   act_ref, ng_ref, nv_ref, cg_ref, cv_ref, *, tm, seq):
    h = h_ref[...]
    tn = wg_ref.shape[1]
    row_id = lax.broadcasted_iota(jnp.int32, (tm, 1), 0)
    whole = seq <= tm
    if not whole:
        @pl.when(pl.program_id(1) % (seq // tm) == 0)
        def _():
            for carry_ref, p_ref in ((cg_ref, pg_ref), (cv_ref, pv_ref)):
                carry_ref[...] = jnp.concatenate([jnp.zeros((SUBLANES - 2, tn), F32), p_ref[0]], axis=0)

    def conv(up, cs, cw_ref, cb_ref, p_ref, n_ref, carry_ref):
        width = up.shape[1]
        if whole:
            n_seq = tm // seq
            prev = p_ref[:, :, cs]
            e0 = jnp.broadcast_to(prev[:, 0:1, :], (n_seq, seq, width)).reshape(tm, width)
            e1 = jnp.broadcast_to(prev[:, 1:2, :], (n_seq, seq, width)).reshape(tm, width)
            tau = row_id & (seq - 1)
            m1 = jnp.where(tau == 0, e1, pltpu.roll(up, 1, 0))
            m2 = jnp.where(tau == 0, e0, jnp.where(tau == 1, e1, pltpu.roll(up, 2, 0)))
            n_ref[:, :, cs] = up.reshape(n_seq, seq, width)[:, seq - 2:, :]
        else:
            carry = carry_ref[:, cs]
            c1, c2 = carry[SUBLANES - 1:], carry[SUBLANES - 2:SUBLANES - 1]
            m1 = jnp.where(row_id == 0, c1, pltpu.roll(up, 1, 0))
            m2 = jnp.where(row_id == 0, c2, jnp.where(row_id == 1, c1, pltpu.roll(up, 2, 0)))
            carry_ref[:, cs] = up[tm - SUBLANES:]
            n_ref[0, :, cs] = up[tm - 2:]
        return cb_ref[:, cs] + m2 * cw_ref[0:1, cs] + m1 * cw_ref[1:2, cs] + up * cw_ref[2:3, cs]

    subs = [slice(s * FFN_SUB, (s + 1) * FFN_SUB) for s in range(tn // FFN_SUB)]
    dots = lambda cs: (jnp.dot(h, wg_ref[:, cs], preferred_element_type=F32),
                       jnp.dot(h, wv_ref[:, cs], preferred_element_type=F32))
    pending = dots(subs[0])
    for s, cs in enumerate(subs):
        up_g, up_v = pending
        if s + 1 < len(subs):
            pending = dots(subs[s + 1])
        gate = conv(up_g, cs, cwg_ref, cbg_ref, pg_ref, ng_ref, cg_ref)
        val = conv(up_v, cs, cwv_ref, cbv_ref, pv_ref, nv_ref, cv_ref)
        act_ref[:, cs] = (jax.nn.gelu(gate, approximate=True) * val).astype(BF16)


def _ffn_up(h, w_up, conv_w, conv_b, conv_prev, *, batch, seq):
    t = h.shape[0]
    if seq <= SUBLANES:
        assert seq == SUBLANES, "whole-sequence tiles rely on one sublane tile per sequence"
        tm, tn = t, 2 * FFN_SUB
        n_tiles = D_FF // tn
        state_blk = (batch, 2, tn)
        state_idx = lambda half: (lambda j, i: (0, 0, j + half * n_tiles))
        w_mode = {}
    else:
        tm, tn, n_tiles = 512, FFN_TN, FFN_N_TILES
        tiles_per_seq = seq // tm
        state_blk = (1, 2, tn)
        state_idx = lambda half: (lambda j, i: (i // tiles_per_seq, 0, j + half * n_tiles))
        w_mode = dict(pipeline_mode=pl.Buffered(1))
    col = lambda half: (lambda j, i: (0, j + half * n_tiles))
    kern = functools.partial(_ffn_up_kernel, tm=tm, seq=seq)
    state_out = jax.ShapeDtypeStruct((batch, 2, D_FF), F32)
    return pl.pallas_call(
        kern,
        grid=(n_tiles, t // tm),
        in_specs=[
            pl.BlockSpec((tm, D_MODEL), lambda j, i: (i, 0)),
            pl.BlockSpec((D_MODEL, tn), col(0), **w_mode),
            pl.BlockSpec((D_MODEL, tn), col(1), **w_mode),
            pl.BlockSpec((3, tn), col(0)),
            pl.BlockSpec((3, tn), col(1)),
            pl.BlockSpec((1, tn), col(0)),
            pl.BlockSpec((1, tn), col(1)),
            pl.BlockSpec(state_blk, state_idx(0)),
            pl.BlockSpec(state_blk, state_idx(1)),
        ],
        out_specs=[
            pl.BlockSpec((tm, tn), lambda j, i: (i, j)),
            pl.BlockSpec(state_blk, state_idx(0)),
            pl.BlockSpec(state_blk, state_idx(0)),
        ],
        out_shape=[jax.ShapeDtypeStruct((t, D_FF), BF16), state_out, state_out],
        scratch_shapes=[pltpu.VMEM((SUBLANES, tn), F32), pltpu.VMEM((SUBLANES, tn), F32)],
        compiler_params=_cparams(("parallel", "arbitrary")),
        name="ffn_up_conv",
    )(h, w_up, w_up, conv_w, conv_w, conv_b, conv_b, conv_prev, conv_prev)


def _ffn_down_kernel(a_ref, w_ref, x_ref, g_ref, o_ref, acc_ref):
    kk = pl.program_id(1)

    @pl.when(kk == 0)
    def _():
        acc_ref[...] = jnp.zeros_like(acc_ref)

    acc_ref[...] += jnp.dot(a_ref[...], w_ref[...], preferred_element_type=F32)

    @pl.when(kk == pl.num_programs(1) - 1)
    def _():
        o_ref[...] = x_ref[...] + _rms(acc_ref[...], g_ref[...])


def _ffn_down(act, w_down, x1, g_post, *, tm=512, tk=D_FF // 4):
    t = x1.shape[0]
    return pl.pallas_call(
        _ffn_down_kernel,
        grid=(t // tm, D_FF // tk),
        in_specs=[
            pl.BlockSpec((tm, tk), lambda i, k: (i, k)),
            pl.BlockSpec((tk, D_MODEL), lambda i, k: (k, 0)),
            pl.BlockSpec((tm, D_MODEL), lambda i, k: (i, 0)),
            pl.BlockSpec((1, D_MODEL), lambda i, k: (0, 0)),
        ],
        out_specs=pl.BlockSpec((tm, D_MODEL), lambda i, k: (i, 0)),
        out_shape=jax.ShapeDtypeStruct((t, D_MODEL), F32),
        scratch_shapes=[pltpu.VMEM((tm, D_MODEL), F32)],
        compiler_params=_cparams(("parallel", "arbitrary")),
        name="ffn_down",
    )(act, w_down, x1, g_post)


def _pad_rows(w, row0, total):
    return jnp.zeros((total, w.shape[1]), w.dtype).at[row0:row0 + w.shape[0]].set(w)


def _prepare_layer(p, l):
    w_in_t = p["w_in"][l].T
    b0 = 2 * A_WIDTH
    q0 = b0 + B_PROJ
    w_in_p = jnp.concatenate([
        w_in_t[:b0 + 3 * B_WIDTH],
        w_in_t[q0:],
        w_in_t[b0 + 3 * B_WIDTH:q0],
        jnp.zeros((IN_COLS_PAD - COL_L - 3 * LORA, D_MODEL), w_in_t.dtype),
    ], axis=0).astype(BF16)
    mu = p["rk_mu"][l]
    vec_rows = [mu[:B_WIDTH], mu[B_WIDTH:2 * B_WIDTH], mu[2 * B_WIDTH:3 * B_WIDTH], p["rk_w0"][l],
                p["rk_a0"][l], p["rk_kk"][l], p["rk_ka"][l], p["rk_rk"][l].reshape(B_WIDTH),
                p["rk_lnx_g"][l], p["rk_lnx_b"][l]]
    vecs = jnp.zeros((16, B_WIDTH), F32).at[:len(vec_rows)].set(jnp.stack(vec_rows))
    mu_l = jnp.zeros((SUBLANES, LORA_PAD), F32).at[0, :3 * LORA].set(mu[3 * B_WIDTH:])
    head_of = jnp.arange(LANES) // B_HEAD_DIM
    seg_ones = (head_of[:, None] == head_of[None, :]).astype(BF16)
    tril = jnp.tril(jnp.ones((CHUNK, CHUNK), bool))
    ws = jnp.where(tril[None], p["gm_ws"][l], 0.0)
    bs = p["gm_bs"][l]
    return {
        "g_mix_pre": p["norm_mix_pre"][l][None], "g_mix_post": p["norm_mix_post"][l][None],
        "g_ffn_pre": p["norm_ffn_pre"][l][None], "g_ffn_post": p["norm_ffn_post"][l][None],
        "g_mem": p["norm_mem"][l][None],
        "w_in": w_in_p,
        "w_out": p["w_out"][l].astype(BF16),
        "w_mkv": jnp.concatenate([p["w_mem_k"][l], p["w_mem_v"][l]], axis=1).astype(BF16),
        "gm_ln_g": p["gm_ln_g"][l][None], "gm_ln_b": p["gm_ln_b"][l][None],
        "gm_ws": ws, "gm_bs": bs,
        "rk_vecs": vecs, "rk_mu_l": mu_l,
        "rk_w2": _pad_rows(p["rk_w2"][l], 0, LORA_PAD).astype(BF16),
        "rk_a2": _pad_rows(p["rk_a2"][l], LORA, LORA_PAD).astype(BF16),
        "rk_g2": _pad_rows(p["rk_g2"][l], 2 * LORA, LORA_PAD).astype(BF16),
        "seg_ones": seg_ones,
        "w_up": p["ffn_w_up"][l].astype(BF16),
        "conv_w": p["ffn_conv_w"][l], "conv_b": p["ffn_conv_b"][l][None],
        "w_down": p["ffn_w_down"][l].astype(BF16),
    }


def _spatial_weights(lw, seq):
    ws, bs = lw["gm_ws"], lw["gm_bs"]
    if seq >= CHUNK:
        w_blk, b_rows = ws, bs
    else:
        reps = CHUNK // seq
        eye = jnp.eye(reps, dtype=ws.dtype)
        w_blk = jnp.einsum("ab,hts->hatbs", eye, ws[:, :seq, :seq]).reshape(A_HEADS, CHUNK, CHUNK)
        b_rows = jnp.tile(bs[:, :seq], (1, reps))
    bias_full = jnp.repeat(b_rows.T, A_HEAD_DIM, axis=1)
    return w_blk.astype(BF16), bias_full


def _decoder_layer(x, mem_k, mem_v, shift_prev, wkv_prev, conv_prev, lw):
    batch, seq, _ = x.shape
    x2 = x.reshape(batch * seq, D_MODEL)
    proj = _norm_matmul(x2, lw["g_mix_pre"], lw["w_in"], tm=1024, tn=PROJ_TN, w_is_nk=True)
    w_sp, bias_full = _spatial_weights(lw, seq)
    a_out, a_v = _group_a(proj, lw["gm_ln_g"], lw["gm_ln_b"], w_sp, bias_full)
    shift_parts = [
        shift_prev[:, None, :B_WIDTH], shift_prev[:, None, B_WIDTH:2 * B_WIDTH],
        shift_prev[:, None, 2 * B_WIDTH:3 * B_WIDTH],
        jnp.pad(shift_prev[:, None, 3 * B_WIDTH:], ((0, 0), (0, 0), (0, LORA_PAD - 3 * LORA))),
    ]
    b_out, wkv_new = _rwkv(proj, shift_parts, wkv_prev, lw, batch=batch, seq=seq)
    c_out = _attention(proj, mem_k, mem_v, batch=batch, seq=seq)
    x1, h = _out_proj(a_out, b_out, c_out, lw["w_out"], x2, lw["g_mix_post"], lw["g_ffn_pre"])
    act, conv_g, conv_v = _ffn_up(h, lw["w_up"], lw["conv_w"], lw["conv_b"], conv_prev, batch=batch, seq=seq)
    y = _ffn_down(act, lw["w_down"], x1, lw["g_ffn_post"])

    chunk_start = ((seq - 1) // CHUNK) * CHUNK
    chunk_v = a_v.reshape(batch, seq, A_HEADS, A_HEAD_DIM)[:, chunk_start:]
    last = proj.reshape(batch, seq, IN_COLS_PAD)[:, -1]
    shift_new = jnp.concatenate([last[:, COL_R:COL_R + 3 * B_WIDTH], last[:, COL_L:COL_L + 3 * LORA]], axis=1)
    conv_new = jnp.concatenate([conv_g, conv_v], axis=-1)
    return y.reshape(batch, seq, D_MODEL), chunk_v, shift_new, wkv_new, conv_new


def kernel(x_prompt, x_sample, mem_prompt, cache_mem_k, cache_mem_v, state_shift, state_wkv, state_conv,
           norm_mix_pre, norm_mix_post, norm_ffn_pre, norm_ffn_post, norm_mem, w_in, w_out, w_mem_k, w_mem_v,
           gm_ln_g, gm_ln_b, gm_ws, gm_bs, rk_mu, rk_w0, rk_w2, rk_a0, rk_a2, rk_g2, rk_kk, rk_ka, rk_rk,
           rk_lnx_g, rk_lnx_b, ffn_w_up, ffn_conv_w, ffn_conv_b, ffn_w_down):
    params = dict(
        norm_mix_pre=norm_mix_pre, norm_mix_post=norm_mix_post, norm_ffn_pre=norm_ffn_pre,
        norm_ffn_post=norm_ffn_post, norm_mem=norm_mem, w_in=w_in, w_out=w_out, w_mem_k=w_mem_k,
        w_mem_v=w_mem_v, gm_ln_g=gm_ln_g, gm_ln_b=gm_ln_b, gm_ws=gm_ws, gm_bs=gm_bs, rk_mu=rk_mu,
        rk_w0=rk_w0, rk_w2=rk_w2, rk_a0=rk_a0, rk_a2=rk_a2, rk_g2=rk_g2, rk_kk=rk_kk, rk_ka=rk_ka,
        rk_rk=rk_rk, rk_lnx_g=rk_lnx_g, rk_lnx_b=rk_lnx_b, ffn_w_up=ffn_w_up, ffn_conv_w=ffn_conv_w,
        ffn_conv_b=ffn_conv_b, ffn_w_down=ffn_w_down)
    depth = w_in.shape[0]
    bp = x_prompt.shape[0]
    y_p, y_s = x_prompt, x_sample
    outs = [[] for _ in range(10)]
    for l in range(depth):
        lw = _prepare_layer(params, l)
        mem2 = mem_prompt.reshape(bp * MEM_LEN, D_MODEL)
        mkv = _norm_matmul(mem2, lw["g_mem"], lw["w_mkv"], tm=512, tn=2 * C_WIDTH)
        mk = mkv[:, :C_WIDTH].reshape(bp, MEM_LEN, C_WIDTH)
        mv = mkv[:, C_WIDTH:].reshape(bp, MEM_LEN, C_WIDTH)
        zero_shift = jnp.zeros((bp, B_PROJ), x_prompt.dtype)
        zero_wkv = jnp.zeros((bp, B_HEADS, B_HEAD_DIM, B_HEAD_DIM), F32)
        zero_conv = jnp.zeros((bp, 2, 2 * D_FF), x_prompt.dtype)
        y_p, cv, sh, wkv, conv = _decoder_layer(y_p, mk, mv, zero_shift, zero_wkv, zero_conv, lw)
        mem_shape = (bp, MEM_LEN, C_HEADS, C_HEAD_DIM)
        for lst, val in zip(outs[:6], (mk.reshape(mem_shape), mv.reshape(mem_shape), cv, sh, wkv, conv)):
            lst.append(val)
        n_s = x_sample.shape[0]
        cache_k = cache_mem_k[l].reshape(n_s, MEM_LEN * C_HEADS, C_HEAD_DIM)
        cache_v = cache_mem_v[l].reshape(n_s, MEM_LEN * C_HEADS, C_HEAD_DIM)
        y_s, cv, sh, wkv, conv = _decoder_layer(y_s, cache_k, cache_v, state_shift[l],
                                                state_wkv[l], state_conv[l], lw)
        for lst, val in zip(outs[6:], (cv, sh, wkv, conv)):
            lst.append(val)
    return (y_p, y_s) + tuple(jnp.stack(o) for o in outs)
```

```python
import functools
import math

import jax
import jax.numpy as jnp
from jax import lax
from jax.experimental import pallas as pl
from jax.experimental.pallas import tpu as pltpu

D_MODEL = 2048
MEM_LEN = 256
CHUNK = 128
A_HEADS, A_HEAD_DIM = 4, 128
A_WIDTH = A_HEADS * A_HEAD_DIM
B_HEADS, B_HEAD_DIM = 16, 64
B_WIDTH = B_HEADS * B_HEAD_DIM
LORA = 64
B_PROJ = 3 * B_WIDTH + 3 * LORA
C_HEADS, C_HEAD_DIM = 4, 128
C_WIDTH = C_HEADS * C_HEAD_DIM
D_FF = 5632
RMS_EPS = 1e-6
LN_EPS = 1e-5
GN_EPS = 64e-5
DECAY_OFFSET = 0.5

LANES = 128
SUBLANES = 8
VMEM_LIMIT_BYTES = 56 * 1024 * 1024

LORA_PAD = 256
COL_A = 0
COL_R = 2 * A_WIDTH
COL_Q = COL_R + 3 * B_WIDTH
COL_L = COL_Q + C_WIDTH
IN_COLS_PAD = 5120
PROJ_TN = 1024

WKV_CHUNK = 64
FFN_N_TILES = 2
FFN_TN = D_FF // FFN_N_TILES
FFN_SUB = 256

F32 = jnp.float32
BF16 = jnp.bfloat16


def _cparams(sem):
    return pltpu.CompilerParams(dimension_semantics=sem, vmem_limit_bytes=VMEM_LIMIT_BYTES)


def _rms(x, g):
    return x * lax.rsqrt(jnp.mean(x * x, axis=-1, keepdims=True) + RMS_EPS) * g


def _norm_matmul_kernel(x_ref, g_ref, w_ref, o_ref, h_ref, *, w_is_nk):
    @pl.when(pl.program_id(1) == 0)
    def _():
        h_ref[...] = _rms(x_ref[...], g_ref[...]).astype(BF16)

    dims = (((1,), (1,)), ((), ())) if w_is_nk else (((1,), (0,)), ((), ()))
    o_ref[...] = lax.dot_general(h_ref[...], w_ref[...], dims, preferred_element_type=F32)


def _norm_matmul(x, g, w, *, tm, tn, w_is_nk=False):
    t, k = x.shape
    n = w.shape[0] if w_is_nk else w.shape[1]
    w_spec = (pl.BlockSpec((tn, k), lambda i, j: (j, 0)) if w_is_nk
              else pl.BlockSpec((k, tn), lambda i, j: (0, j)))
    return pl.pallas_call(
        functools.partial(_norm_matmul_kernel, w_is_nk=w_is_nk),
        grid=(t // tm, n // tn),
        in_specs=[
            pl.BlockSpec((tm, k), lambda i, j: (i, 0)),
            pl.BlockSpec((1, k), lambda i, j: (0, 0)),
            w_spec,
        ],
        out_specs=pl.BlockSpec((tm, tn), lambda i, j: (i, j)),
        out_shape=jax.ShapeDtypeStruct((t, n), F32),
        scratch_shapes=[pltpu.VMEM((tm, k), BF16)],
        compiler_params=_cparams(("parallel", "arbitrary")),
        name="norm_matmul",
    )(x, g, w)


def _group_a_kernel(u_ref, v_ref, g_ref, b_ref, w_ref, bias_ref, o_ref, vout_ref):
    u = jax.nn.gelu(u_ref[...], approximate=True)
    v = jax.nn.gelu(v_ref[...], approximate=True)
    mean = jnp.mean(v, axis=-1, keepdims=True)
    d = v - mean
    var = jnp.mean(d * d, axis=-1, keepdims=True)
    vn = d * lax.rsqrt(var + LN_EPS) * g_ref[...] + b_ref[...]
    vout_ref[...] = vn
    vb = vn.astype(BF16)
    for c in range(u.shape[0] // CHUNK):
        rs = slice(c * CHUNK, (c + 1) * CHUNK)
        for h in range(A_HEADS):
            hs = slice(h * A_HEAD_DIM, (h + 1) * A_HEAD_DIM)
            mixed = jnp.dot(w_ref[h], vb[rs, hs], preferred_element_type=F32) + bias_ref[:, hs]
            o_ref[rs, hs] = (u[rs, hs] * mixed).astype(BF16)


def _group_a(proj, ln_g, ln_b, w_sp, bias_full, *, rows=4 * CHUNK):
    t = proj.shape[0]
    return pl.pallas_call(
        _group_a_kernel,
        grid=(t // rows,),
        in_specs=[
            pl.BlockSpec((rows, A_WIDTH), lambda i: (i, COL_A // A_WIDTH)),
            pl.BlockSpec((rows, A_WIDTH), lambda i: (i, COL_A // A_WIDTH + 1)),
            pl.BlockSpec((1, A_WIDTH), lambda i: (0, 0)),
            pl.BlockSpec((1, A_WIDTH), lambda i: (0, 0)),
            pl.BlockSpec((A_HEADS, CHUNK, CHUNK), lambda i: (0, 0, 0)),
            pl.BlockSpec((CHUNK, A_WIDTH), lambda i: (0, 0)),
        ],
        out_specs=[
            pl.BlockSpec((rows, A_WIDTH), lambda i: (i, 0)),
            pl.BlockSpec((rows, A_WIDTH), lambda i: (i, 0)),
        ],
        out_shape=[
            jax.ShapeDtypeStruct((t, A_WIDTH), BF16),
            jax.ShapeDtypeStruct((t, A_WIDTH), F32),
        ],
        compiler_params=_cparams(("parallel",)),
        name="group_a",
    )(proj, proj, ln_g, ln_b, w_sp, bias_full)


def _softmax_rows(s):
    e = jnp.exp(s - jnp.max(s, axis=-1, keepdims=True))
    return e / jnp.sum(e, axis=-1, keepdims=True)


def _attn_kernel(q_ref, k_ref, v_ref, o_ref, *, n_b, tq, head_major_rows):
    scale = C_HEAD_DIM ** -0.5
    heads = range(C_HEADS)
    lanes = lambda h: slice(h * C_HEAD_DIM, (h + 1) * C_HEAD_DIM)
    if not head_major_rows:
        chains = [(g, h) for g in range(n_b) for h in heads]
        rows = lambda g: slice(g * tq, (g + 1) * tq)
        s = [lax.dot_general(q_ref[rows(g), lanes(h)].astype(BF16), k_ref[g, :, lanes(h)].astype(BF16), _NT,
                             preferred_element_type=F32) * scale for g, h in chains]
        p = [_softmax_rows(m).astype(BF16) for m in s]
        o = [jnp.dot(m, v_ref[g, :, lanes(h)].astype(BF16), preferred_element_type=F32)
             for m, (g, h) in zip(p, chains)]
        for m, (g, h) in zip(o, chains):
            o_ref[rows(g), lanes(h)] = m.astype(BF16)
        return
    row = lax.broadcasted_iota(jnp.int32, (C_HEADS * tq, 1), 0)
    col = lax.broadcasted_iota(jnp.int32, (1, C_HEADS * MEM_LEN), 1)
    own = (col & (C_HEADS - 1)) == (row >> int(math.log2(tq)))
    q = [q_ref[g * tq:(g + 1) * tq, :] for g in range(n_b)]
    qs = [jnp.concatenate([m[:, lanes(h)] for h in heads], axis=0).astype(BF16) for m in q]
    s = [lax.dot_general(m, k_ref[g].astype(BF16), _NT, preferred_element_type=F32) * scale
         for g, m in enumerate(qs)]
    p = [_softmax_rows(jnp.where(own, m, -1e30)).astype(BF16) for m in s]
    o = [jnp.dot(m, v_ref[g].astype(BF16), preferred_element_type=F32) for g, m in enumerate(p)]
    for g, m in enumerate(o):
        for h in heads:
            o_ref[g * tq:(g + 1) * tq, lanes(h)] = m[h * tq:(h + 1) * tq].astype(BF16)


def _attention(proj, mem_k, mem_v, *, batch, seq):
    head_major_rows = mem_k.shape[-1] == C_HEAD_DIM
    if head_major_rows:
        tq, n_b = seq, 8
    else:
        tq, n_b = 512, 1
    n_q = seq // tq
    mem_blk = (n_b,) + mem_k.shape[1:]
    return pl.pallas_call(
        functools.partial(_attn_kernel, n_b=n_b, tq=tq, head_major_rows=head_major_rows),
        grid=(batch // n_b, n_q),
        in_specs=[
            pl.BlockSpec((n_b * tq, C_WIDTH), lambda b, i: (b * n_q + i, COL_Q // C_WIDTH)),
            pl.BlockSpec(mem_blk, lambda b, i: (b, 0, 0)),
            pl.BlockSpec(mem_blk, lambda b, i: (b, 0, 0)),
        ],
        out_specs=pl.BlockSpec((n_b * tq, C_WIDTH), lambda b, i: (b * n_q + i, 0)),
        out_shape=jax.ShapeDtypeStruct((batch * seq, C_WIDTH), BF16),
        compiler_params=_cparams(("parallel", "arbitrary")),
        name="mem_attention",
    )(proj, mem_k, mem_v)


_V_MU_R, _V_MU_K, _V_MU_V, _V_W0, _V_A0, _V_KK, _V_KA, _V_RK, _V_LNG, _V_LNB = range(10)


_NN = (((1,), (0,)), ((), ()))
_NT = (((1,), (1,)), ((), ()))
_TN = (((0,), (0,)), ((), ()))
HEAD_PAIRS = B_HEADS // 2


def _split(x):
    hi = x.astype(BF16)
    return hi, (x - hi.astype(F32)).astype(BF16)


def _b(x):
    return x.astype(BF16)


def _bdot(a, b, dims=_NN):
    return lax.dot_general(a, b, dims, preferred_element_type=F32)


def _rwkv_kernel(pr_ref, pk_ref, pv_ref, pl_ref, sr_ref, sk_ref, sv_ref, sl_ref, wkv_ref,
                 vec_ref, mul_ref, w2_ref, a2_ref, g2_ref, seg_ref,
                 o_ref, so_ref, cr_ref, ck_ref, cv_ref, cl_ref, sbd_ref, *, groups, chunk, whole_seq):
    rows = groups * chunk
    first = pl.program_id(1) == 0
    last = pl.program_id(1) == pl.num_programs(1) - 1
    hd = B_HEAD_DIM

    @pl.when(first)
    def _():
        zero = jnp.zeros((hd, hd), F32)
        for g in range(groups):
            for q in range(HEAD_PAIRS):
                top = jnp.concatenate([wkv_ref[g, 2 * q], zero], axis=1)
                bot = jnp.concatenate([zero, wkv_ref[g, 2 * q + 1]], axis=1)
                sbd_ref[g, q] = jnp.concatenate([top, bot], axis=0)

    row_id = lax.broadcasted_iota(jnp.int32, (rows, 1), 0)

    def prev_rows(x, s_ref, carry_ref):
        width = x.shape[1]
        if whole_seq:
            start = jnp.broadcast_to(s_ref[...], (groups, chunk, width)).reshape(rows, width)
            return jnp.where((row_id & (chunk - 1)) == 0, start, pltpu.roll(x, 1, 0))

        @pl.when(first)
        def _():
            carry_ref[...] = jnp.broadcast_to(s_ref[0], (SUBLANES, width))

        ext = jnp.concatenate([carry_ref[...], x], axis=0)
        prev = pltpu.roll(ext, 1, 0)[SUBLANES:]
        carry_ref[...] = x[rows - SUBLANES:]
        return prev

    def shifted(p_ref, s_ref, carry_ref, mu):
        x = p_ref[...]
        return x + (prev_rows(x, s_ref, carry_ref) - x) * mu

    vec = lambda i: vec_ref[i:i + 1, :]
    r = shifted(pr_ref, sr_ref, cr_ref, vec(_V_MU_R))
    k = shifted(pk_ref, sk_ref, ck_ref, vec(_V_MU_K))
    v = shifted(pv_ref, sv_ref, cv_ref, vec(_V_MU_V))
    lo = shifted(pl_ref, sl_ref, cl_ref, mul_ref[0:1, :])

    seg_ones = seg_ref[...]

    def seg_sum(x):
        slabs = jnp.concatenate([x[:, q * LANES:(q + 1) * LANES] for q in range(HEAD_PAIRS)], axis=0)
        hi, lo = _split(slabs)
        s = (jnp.dot(hi, seg_ones, preferred_element_type=F32)
             + jnp.dot(lo, seg_ones, preferred_element_type=F32))
        return jnp.concatenate([s[q * rows:(q + 1) * rows] for q in range(HEAD_PAIRS)], axis=1)

    dw = jnp.dot(jnp.tanh(lo).astype(BF16), w2_ref[...], preferred_element_type=F32)
    da = jnp.dot(lo.astype(BF16), a2_ref[...], preferred_element_type=F32)
    gate = jnp.dot(jax.nn.sigmoid(lo).astype(BF16), g2_ref[...], preferred_element_type=F32)
    w_log = -jax.nn.softplus(-(vec(_V_W0) + dw)) - DECAY_OFFSET
    log_decay = -jnp.exp(w_log)
    a = jax.nn.sigmoid(vec(_V_A0) + da)
    kk = k * vec(_V_KK)
    kk = kk / jnp.maximum(jnp.sqrt(seg_sum(kk * kk)), 1e-12)
    k = k * (1.0 + (a - 1.0) * vec(_V_KA))
    bonus = seg_sum(r * k * vec(_V_RK)) * v

    col_id = lax.broadcasted_iota(jnp.int32, (1, rows), 1)
    chunk_bits = int(math.log2(chunk))
    incl = ((row_id >> chunk_bits) == (col_id >> chunk_bits)) & (col_id <= row_id)
    incl_b = incl.astype(BF16)
    ld_hi, ld_mid = _split(log_decay)
    ld_lo = (log_decay - ld_hi.astype(F32) - ld_mid.astype(F32)).astype(BF16)
    cum = (jnp.dot(incl_b, ld_hi, preferred_element_type=F32)
           + jnp.dot(incl_b, ld_mid, preferred_element_type=F32)
           + jnp.dot(incl_b, ld_lo, preferred_element_type=F32))
    p_incl = jnp.exp(cum)
    p_inv = jnp.exp(-cum)
    a_t = -kk * jnp.exp(cum - log_decay)
    b_t = kk * a * p_inv
    k_t = k * p_inv
    r_t = r * p_incl

    pr = 2 * rows
    row2 = lax.broadcasted_iota(jnp.int32, (pr, 1), 0)
    col2 = lax.broadcasted_iota(jnp.int32, (1, pr), 1)
    t2, s2 = row2 & (rows - 1), col2 & (rows - 1)
    same2 = ((row2 >> chunk_bits) == (col2 >> chunk_bits))
    incl2 = same2 & (s2 <= t2)
    strict2 = same2 & (s2 < t2)
    eye2 = (row2 == col2).astype(F32)
    left = lax.broadcasted_iota(jnp.int32, (rows, LANES), 1) < hd

    def bd(x):
        zero = jnp.zeros_like(x)
        return jnp.concatenate([jnp.where(left, x, zero), jnp.where(left, zero, x)], axis=0)

    def group_rows(mats, g):
        starts = [hh * rows + g * chunk for hh in (0, 1)]
        return jnp.concatenate([m[i:i + chunk] for m in mats for i in starts], axis=0)

    n_sq = chunk_bits - 1
    pairs = range(HEAD_PAIRS)
    lanes_of = [slice(q * LANES, (q + 1) * LANES) for q in pairs]
    bds = [[bd(t[:, ls]) for t in (a_t, r_t, b_t, k_t, v)] for ls in lanes_of]
    ar_s = [_b(jnp.concatenate([m[0], m[1]], axis=0)) for m in bds]
    bk_s = [_b(jnp.concatenate([m[2], m[3]], axis=0)) for m in bds]
    v_s = [_b(m[4]) for m in bds]
    ms = [_bdot(ar_s[q], bk_s[q], _NT) for q in pairs]
    a_ab = [jnp.where(strict2, m[:pr, :pr], 0.0) for m in ms]
    a_ak = [jnp.where(strict2, m[:pr, pr:], 0.0) for m in ms]
    col4 = lax.broadcasted_iota(jnp.int32, (1, 2 * pr), 1)
    incl4 = ((row2 >> chunk_bits) == ((col4 & (pr - 1)) >> chunk_bits)) & ((col4 & (rows - 1)) <= t2)
    a_r = [jnp.where(incl4, m[pr:], 0.0) for m in ms]
    inv = [eye2 + n for n in a_ab]
    pw_s = [_b(n) for n in a_ab]
    for _ in range(n_sq):
        pw_s = [_b(_bdot(p, p)) for p in pw_s]
        inv = [i + _bdot(p, _b(i)) for i, p in zip(inv, pw_s)]
    if groups == 1:
        xy = [_bdot(ar_s[q], _b(sbd_ref[0, q]), _NT) for q in pairs]
        x0, y0 = [m[:pr] for m in xy], [m[pr:] for m in xy]
    else:
        x0, y0 = [], []
        for q in pairs:
            x_parts, y_parts = [None] * (2 * groups), [None] * (2 * groups)
            for g in range(groups):
                xy = _bdot(_b(group_rows(bds[q][:2], g)), _b(sbd_ref[g, q]), _NT)
                for hh in (0, 1):
                    x_parts[hh * groups + g] = xy[hh * chunk:(hh + 1) * chunk]
                    y_parts[hh * groups + g] = xy[(2 + hh) * chunk:(3 + hh) * chunk]
            x0.append(jnp.concatenate(x_parts, axis=0))
            y0.append(jnp.concatenate(y_parts, axis=0))
    rhs = [_b(x0[q] + _bdot(_b(a_ak[q]), v_s[q])) for q in pairs]
    u = [_bdot(_b(inv[q]), rhs[q]) for q in pairs]
    uv_s = [jnp.concatenate([_b(u[q]), v_s[q]], axis=0) for q in pairs]
    y2 = [y0[q] + _bdot(_b(a_r[q]), uv_s[q]) for q in pairs]
    y = jnp.concatenate([m[:rows] + m[rows:] for m in y2], axis=1)
    for q in pairs:
        if groups == 1:
            ds = _bdot(uv_s[q], bk_s[q], _TN)
            sbd_ref[0, q] = (sbd_ref[0, q] + ds) * p_incl[rows - 1:rows, lanes_of[q]]
        else:
            for g in range(groups):
                ds = _bdot(_b(group_rows((u[q], bds[q][4]), g)),
                           _b(group_rows((bds[q][2], bds[q][3]), g)), _TN)
                end = (g + 1) * chunk - 1
                sbd_ref[g, q] = (sbd_ref[g, q] + ds) * p_incl[end:end + 1, lanes_of[q]]

    @pl.when(last)
    def _():
        for g in range(groups):
            for q in range(HEAD_PAIRS):
                blk = sbd_ref[g, q]
                so_ref[g, 2 * q] = blk[:hd, :hd]
                so_ref[g, 2 * q + 1] = blk[hd:, hd:]

    inv_n = 1.0 / B_HEAD_DIM
    d = y - seg_sum(y) * inv_n
    var = seg_sum(d * d) * inv_n
    yn = d * lax.rsqrt(var + GN_EPS) * vec(_V_LNG) + vec(_V_LNB)
    o_ref[...] = ((yn + bonus) * gate).astype(BF16)


def _rwkv(proj, shift_parts, wkv_prev, lw, *, batch, seq):
    whole_seq = seq <= WKV_CHUNK
    if whole_seq:
        assert seq == SUBLANES, "whole-sequence blocks rely on one sublane tile per sequence"
        chunk, groups = seq, WKV_CHUNK // seq
    else:
        chunk, groups = WKV_CHUNK, 1
    rows = groups * chunk
    n_chunks = seq // chunk
    row_blk = lambda bi, c: bi * n_chunks + c
    in_specs = [
        pl.BlockSpec((rows, B_WIDTH), lambda bi, c: (row_blk(bi, c), COL_R // B_WIDTH)),
        pl.BlockSpec((rows, B_WIDTH), lambda bi, c: (row_blk(bi, c), COL_R // B_WIDTH + 1)),
        pl.BlockSpec((rows, B_WIDTH), lambda bi, c: (row_blk(bi, c), COL_R // B_WIDTH + 2)),
        pl.BlockSpec((rows, LORA_PAD), lambda bi, c: (row_blk(bi, c), COL_L // LORA_PAD)),
        pl.BlockSpec((groups, 1, B_WIDTH), lambda bi, c: (bi, 0, 0)),
        pl.BlockSpec((groups, 1, B_WIDTH), lambda bi, c: (bi, 0, 0)),
        pl.BlockSpec((groups, 1, B_WIDTH), lambda bi, c: (bi, 0, 0)),
        pl.BlockSpec((groups, 1, LORA_PAD), lambda bi, c: (bi, 0, 0)),
        pl.BlockSpec((groups, B_HEADS, B_HEAD_DIM, B_HEAD_DIM), lambda bi, c: (bi, 0, 0, 0)),
        pl.BlockSpec((16, B_WIDTH), lambda bi, c: (0, 0)),
        pl.BlockSpec((SUBLANES, LORA_PAD), lambda bi, c: (0, 0)),
        pl.BlockSpec((LORA_PAD, B_WIDTH), lambda bi, c: (0, 0)),
        pl.BlockSpec((LORA_PAD, B_WIDTH), lambda bi, c: (0, 0)),
        pl.BlockSpec((LORA_PAD, B_WIDTH), lambda bi, c: (0, 0)),
        pl.BlockSpec((LANES, LANES), lambda bi, c: (0, 0)),
    ]
    kern = functools.partial(_rwkv_kernel, groups=groups, chunk=chunk, whole_seq=whole_seq)
    return pl.pallas_call(
        kern,
        grid=(batch // groups, n_chunks),
        in_specs=in_specs,
        out_specs=[
            pl.BlockSpec((rows, B_WIDTH), lambda bi, c: (row_blk(bi, c), 0)),
            pl.BlockSpec((groups, B_HEADS, B_HEAD_DIM, B_HEAD_DIM), lambda bi, c: (bi, 0, 0, 0)),
        ],
        out_shape=[
            jax.ShapeDtypeStruct((batch * seq, B_WIDTH), BF16),
            jax.ShapeDtypeStruct((batch, B_HEADS, B_HEAD_DIM, B_HEAD_DIM), F32),
        ],
        scratch_shapes=[
            pltpu.VMEM((SUBLANES, B_WIDTH), F32),
            pltpu.VMEM((SUBLANES, B_WIDTH), F32),
            pltpu.VMEM((SUBLANES, B_WIDTH), F32),
            pltpu.VMEM((SUBLANES, LORA_PAD), F32),
            pltpu.VMEM((groups, HEAD_PAIRS, LANES, LANES), F32),
        ],
        compiler_params=_cparams(("parallel", "arbitrary")),
        name="rwkv7",
    )(proj, proj, proj, proj, *shift_parts, wkv_prev,
      lw["rk_vecs"], lw["rk_mu_l"], lw["rk_w2"], lw["rk_a2"], lw["rk_g2"], lw["seg_ones"])


def _out_proj_kernel(a_ref, b_ref, c_ref, w_ref, x_ref, gpost_ref, gffn_ref, x1_ref, h_ref):
    n_sub = 4
    sub = x_ref.shape[0] // n_sub
    halves = [slice(s * sub, (s + 1) * sub) for s in range(n_sub)]

    def project(rs):
        acc = jnp.dot(a_ref[rs, :], w_ref[0:A_WIDTH, :], preferred_element_type=F32)
        acc += jnp.dot(b_ref[rs, :], w_ref[A_WIDTH:A_WIDTH + B_WIDTH, :], preferred_element_type=F32)
        return acc + jnp.dot(c_ref[rs, :], w_ref[A_WIDTH + B_WIDTH:, :], preferred_element_type=F32)

    accs = [project(rs) for rs in halves]
    for rs, acc in zip(halves, accs):
        x1 = x_ref[rs, :] + _rms(acc, gpost_ref[...])
        x1_ref[rs, :] = x1
        h_ref[rs, :] = _rms(x1, gffn_ref[...]).astype(BF16)


def _out_proj(a_out, b_out, c_out, w_out, x, g_post, g_ffn, *, tm=512):
    t = x.shape[0]
    row = lambda i: (i, 0)
    fixed = lambda i: (0, 0)
    return pl.pallas_call(
        _out_proj_kernel,
        grid=(t // tm,),
        in_specs=[
            pl.BlockSpec((tm, A_WIDTH), row),
            pl.BlockSpec((tm, B_WIDTH), row),
            pl.BlockSpec((tm, C_WIDTH), row),
            pl.BlockSpec((D_MODEL, D_MODEL), fixed, pipeline_mode=pl.Buffered(1)),
            pl.BlockSpec((tm, D_MODEL), row),
            pl.BlockSpec((1, D_MODEL), fixed),
            pl.BlockSpec((1, D_MODEL), fixed),
        ],
        out_specs=[pl.BlockSpec((tm, D_MODEL), row), pl.BlockSpec((tm, D_MODEL), row)],
        out_shape=[jax.ShapeDtypeStruct((t, D_MODEL), F32), jax.ShapeDtypeStruct((t, D_MODEL), BF16)],
        compiler_params=_cparams(("parallel",)),
        name="out_proj",
    )(a_out, b_out, c_out, w_out, x, g_post, g_ffn)


def _ffn_up_kernel(h_ref, wg_ref, wv_ref, cwg_ref, cwv_ref, cbg_ref, cbv_ref, pg_ref, pv_ref,
                   act_ref, ng_ref, nv_ref, cg_ref, cv_ref, *, tm, seq):
    h = h_ref[...]
    tn = wg_ref.shape[1]
    row_id = lax.broadcasted_iota(jnp.int32, (tm, 1), 0)
    whole = seq <= tm
    if not whole:
        @pl.when(pl.program_id(1) % (seq // tm) == 0)
        def _():
            for carry_ref, p_ref in ((cg_ref, pg_ref), (cv_ref, pv_ref)):
                carry_ref[...] = jnp.concatenate([jnp.zeros((SUBLANES - 2, tn), F32), p_ref[0]], axis=0)

    def conv(up, cs, cw_ref, cb_ref, p_ref, n_ref, carry_ref):
        width = up.shape[1]
        if whole:
            n_seq = tm // seq
            prev = p_ref[:, :, cs]
            e0 = jnp.broadcast_to(prev[:, 0:1, :], (n_seq, seq, width)).reshape(tm, width)
            e1 = jnp.broadcast_to(prev[:, 1:2, :], (n_seq, seq, width)).reshape(tm, width)
            tau = row_id & (seq - 1)
            m1 = jnp.where(tau == 0, e1, pltpu.roll(up, 1, 0))
            m2 = jnp.where(tau == 0, e0, jnp.where(tau == 1, e1, pltpu.roll(up, 2, 0)))
            n_ref[:, :, cs] = up.reshape(n_seq, seq, width)[:, seq - 2:, :]
        else:
            carry = carry_ref[:, cs]
            c1, c2 = carry[SUBLANES - 1:], carry[SUBLANES - 2:SUBLANES - 1]
            m1 = jnp.where(row_id == 0, c1, pltpu.roll(up, 1, 0))
            m2 = jnp.where(row_id == 0, c2, jnp.where(row_id == 1, c1, pltpu.roll(up, 2, 0)))
            carry_ref[:, cs] = up[tm - SUBLANES:]
            n_ref[0, :, cs] = up[tm - 2:]
        return cb_ref[:, cs] + m2 * cw_ref[0:1, cs] + m1 * cw_ref[1:2, cs] + up * cw_ref[2:3, cs]

    subs = [slice(s * FFN_SUB, (s + 1) * FFN_SUB) for s in range(tn // FFN_SUB)]
    dots = lambda cs: (jnp.dot(h, wg_ref[:, cs], preferred_element_type=F32),
                       jnp.dot(h, wv_ref[:, cs], preferred_element_type=F32))
    pending = dots(subs[0])
    for s, cs in enumerate(subs):
        up_g, up_v = pending
        if s + 1 < len(subs):
            pending = dots(subs[s + 1])
        gate = conv(up_g, cs, cwg_ref, cbg_ref, pg_ref, ng_ref, cg_ref)
        val = conv(up_v, cs, cwv_ref, cbv_ref, pv_ref, nv_ref, cv_ref)
        act_ref[:, cs] = (jax.nn.gelu(gate, approximate=True) * val).astype(BF16)


def _ffn_up(h, w_up, conv_w, conv_b, conv_prev, *, batch, seq):
    t = h.shape[0]
    if seq <= SUBLANES:
        assert seq == SUBLANES, "whole-sequence tiles rely on one sublane tile per sequence"
        tm, tn = t, 2 * FFN_SUB
        n_tiles = D_FF // tn
        state_blk = (batch, 2, tn)
        state_idx = lambda half: (lambda j, i: (0, 0, j + half * n_tiles))
        w_mode = {}
    else:
        tm, tn, n_tiles = 512, FFN_TN, FFN_N_TILES
        tiles_per_seq = seq // tm
        state_blk = (1, 2, tn)
        state_idx = lambda half: (lambda j, i: (i // tiles_per_seq, 0, j + half * n_tiles))
        w_mode = dict(pipeline_mode=pl.Buffered(1))
    col = lambda half: (lambda j, i: (0, j + half * n_tiles))
    kern = functools.partial(_ffn_up_kernel, tm=tm, seq=seq)
    state_out = jax.ShapeDtypeStruct((batch, 2, D_FF), F32)
    return pl.pallas_call(
        kern,
        grid=(n_tiles, t // tm),
        in_specs=[
            pl.BlockSpec((tm, D_MODEL), lambda j, i: (i, 0)),
            pl.BlockSpec((D_MODEL, tn), col(0), **w_mode),
            pl.BlockSpec((D_MODEL, tn), col(1), **w_mode),
            pl.BlockSpec((3, tn), col(0)),
            pl.BlockSpec((3, tn), col(1)),
            pl.BlockSpec((1, tn), col(0)),
            pl.BlockSpec((1, tn), col(1)),
            pl.BlockSpec(state_blk, state_idx(0)),
            pl.BlockSpec(state_blk, state_idx(1)),
        ],
        out_specs=[
            pl.BlockSpec((tm, tn), lambda j, i: (i, j)),
            pl.BlockSpec(state_blk, state_idx(0)),
            pl.BlockSpec(state_blk, state_idx(0)),
        ],
        out_shape=[jax.ShapeDtypeStruct((t, D_FF), BF16), state_out, state_out],
        scratch_shapes=[pltpu.VMEM((SUBLANES, tn), F32), pltpu.VMEM((SUBLANES, tn), F32)],
        compiler_params=_cparams(("parallel", "arbitrary")),
        name="ffn_up_conv",
    )(h, w_up, w_up, conv_w, conv_w, conv_b, conv_b, conv_prev, conv_prev)


def _ffn_down_kernel(a_ref, w_ref, x_ref, g_ref, o_ref):
    acc = jnp.dot(a_ref[...], w_ref[...], preferred_element_type=F32)
    o_ref[...] = x_ref[...] + _rms(acc, g_ref[...])


def _ffn_down(act, w_down, x1, g_post, *, tm=256):
    t = x1.shape[0]
    return pl.pallas_call(
        _ffn_down_kernel,
        grid=(t // tm,),
        in_specs=[
            pl.BlockSpec((tm, D_FF), lambda i: (i, 0)),
            pl.BlockSpec((D_FF, D_MODEL), lambda i: (0, 0), pipeline_mode=pl.Buffered(1)),
            pl.BlockSpec((tm, D_MODEL), lambda i: (i, 0)),
            pl.BlockSpec((1, D_MODEL), lambda i: (0, 0)),
        ],
        out_specs=pl.BlockSpec((tm, D_MODEL), lambda i: (i, 0)),
        out_shape=jax.ShapeDtypeStruct((t, D_MODEL), F32),
        compiler_params=_cparams(("parallel",)),
        name="ffn_down",
    )(act, w_down, x1, g_post)


def _pad_rows(w, row0, total):
    return jnp.zeros((total, w.shape[1]), w.dtype).at[row0:row0 + w.shape[0]].set(w)


def _prepare_layer(p, l):
    w_in_t = p["w_in"][l].T
    b0 = 2 * A_WIDTH
    q0 = b0 + B_PROJ
    w_in_p = jnp.concatenate([
        w_in_t[:b0 + 3 * B_WIDTH],
        w_in_t[q0:],
        w_in_t[b0 + 3 * B_WIDTH:q0],
        jnp.zeros((IN_COLS_PAD - COL_L - 3 * LORA, D_MODEL), w_in_t.dtype),
    ], axis=0).astype(BF16)
    mu = p["rk_mu"][l]
    vec_rows = [mu[:B_WIDTH], mu[B_WIDTH:2 * B_WIDTH], mu[2 * B_WIDTH:3 * B_WIDTH], p["rk_w0"][l],
                p["rk_a0"][l], p["rk_kk"][l], p["rk_ka"][l], p["rk_rk"][l].reshape(B_WIDTH),
                p["rk_lnx_g"][l], p["rk_lnx_b"][l]]
    vecs = jnp.zeros((16, B_WIDTH), F32).at[:len(vec_rows)].set(jnp.stack(vec_rows))
    mu_l = jnp.zeros((SUBLANES, LORA_PAD), F32).at[0, :3 * LORA].set(mu[3 * B_WIDTH:])
    head_of = jnp.arange(LANES) // B_HEAD_DIM
    seg_ones = (head_of[:, None] == head_of[None, :]).astype(BF16)
    tril = jnp.tril(jnp.ones((CHUNK, CHUNK), bool))
    ws = jnp.where(tril[None], p["gm_ws"][l], 0.0)
    bs = p["gm_bs"][l]
    return {
        "g_mix_pre": p["norm_mix_pre"][l][None], "g_mix_post": p["norm_mix_post"][l][None],
        "g_ffn_pre": p["norm_ffn_pre"][l][None], "g_ffn_post": p["norm_ffn_post"][l][None],
        "g_mem": p["norm_mem"][l][None],
        "w_in": w_in_p,
        "w_out": p["w_out"][l].astype(BF16),
        "w_mkv": jnp.concatenate([p["w_mem_k"][l], p["w_mem_v"][l]], axis=1).astype(BF16),
        "gm_ln_g": p["gm_ln_g"][l][None], "gm_ln_b": p["gm_ln_b"][l][None],
        "gm_ws": ws, "gm_bs": bs,
        "rk_vecs": vecs, "rk_mu_l": mu_l,
        "rk_w2": _pad_rows(p["rk_w2"][l], 0, LORA_PAD).astype(BF16),
        "rk_a2": _pad_rows(p["rk_a2"][l], LORA, LORA_PAD).astype(BF16),
        "rk_g2": _pad_rows(p["rk_g2"][l], 2 * LORA, LORA_PAD).astype(BF16),
        "seg_ones": seg_ones,
        "w_up": p["ffn_w_up"][l].astype(BF16),
        "conv_w": p["ffn_conv_w"][l], "conv_b": p["ffn_conv_b"][l][None],
        "w_down": p["ffn_w_down"][l].astype(BF16),
    }


def _spatial_weights(lw, seq):
    ws, bs = lw["gm_ws"], lw["gm_bs"]
    if seq >= CHUNK:
        w_blk, b_rows = ws, bs
    else:
        reps = CHUNK // seq
        eye = jnp.eye(reps, dtype=ws.dtype)
        w_blk = jnp.einsum("ab,hts->hatbs", eye, ws[:, :seq, :seq]).reshape(A_HEADS, CHUNK, CHUNK)
        b_rows = jnp.tile(bs[:, :seq], (1, reps))
    bias_full = jnp.repeat(b_rows.T, A_HEAD_DIM, axis=1)
    return w_blk.astype(BF16), bias_full


def _decoder_layer(x, mem_k, mem_v, shift_prev, wkv_prev, conv_prev, lw):
    batch, seq, _ = x.shape
    x2 = x.reshape(batch * seq, D_MODEL)
    proj = _norm_matmul(x2, lw["g_mix_pre"], lw["w_in"], tm=1024, tn=PROJ_TN, w_is_nk=True)
    w_sp, bias_full = _spatial_weights(lw, seq)
    a_out, a_v = _group_a(proj, lw["gm_ln_g"], lw["gm_ln_b"], w_sp, bias_full)
    shift_parts = [
        shift_prev[:, None, :B_WIDTH], shift_prev[:, None, B_WIDTH:2 * B_WIDTH],
        shift_prev[:, None, 2 * B_WIDTH:3 * B_WIDTH],
        jnp.pad(shift_prev[:, None, 3 * B_WIDTH:], ((0, 0), (0, 0), (0, LORA_PAD - 3 * LORA))),
    ]
    b_out, wkv_new = _rwkv(proj, shift_parts, wkv_prev, lw, batch=batch, seq=seq)
    c_out = _attention(proj, mem_k, mem_v, batch=batch, seq=seq)
    x1, h = _out_proj(a_out, b_out, c_out, lw["w_out"], x2, lw["g_mix_post"], lw["g_ffn_pre"])
    act, conv_g, conv_v = _ffn_up(h, lw["w_up"], lw["conv_w"], lw["conv_b"], conv_prev, batch=batch, seq=seq)
    y = _ffn_down(act, lw["w_down"], x1, lw["g_ffn_post"])

    chunk_start = ((seq - 1) // CHUNK) * CHUNK
    chunk_v = a_v.reshape(batch, seq, A_WIDTH)[:, chunk_start:].reshape(batch, -1, A_HEADS, A_HEAD_DIM)
    last = proj.reshape(batch, seq, IN_COLS_PAD)[:, -1]
    shift_new = jnp.concatenate([last[:, COL_R:COL_R + 3 * B_WIDTH], last[:, COL_L:COL_L + 3 * LORA]], axis=1)
    conv_new = jnp.concatenate([conv_g, conv_v], axis=-1)
    return y.reshape(batch, seq, D_MODEL), chunk_v, shift_new, wkv_new, conv_new


def kernel(x_prompt, x_sample, mem_prompt, cache_mem_k, cache_mem_v, state_shift, state_wkv, state_conv,
           norm_mix_pre, norm_mix_post, norm_ffn_pre, norm_ffn_post, norm_mem, w_in, w_out, w_mem_k, w_mem_v,
           gm_ln_g, gm_ln_b, gm_ws, gm_bs, rk_mu, rk_w0, rk_w2, rk_a0, rk_a2, rk_g2, rk_kk, rk_ka, rk_rk,
           rk_lnx_g, rk_lnx_b, ffn_w_up, ffn_conv_w, ffn_conv_b, ffn_w_down):
    params = dict(
        norm_mix_pre=norm_mix_pre, norm_mix_post=norm_mix_post, norm_ffn_pre=norm_ffn_pre,
        norm_ffn_post=norm_ffn_post, norm_mem=norm_mem, w_in=w_in, w_out=w_out, w_mem_k=w_mem_k,
        w_mem_v=w_mem_v, gm_ln_g=gm_ln_g, gm_ln_b=gm_ln_b, gm_ws=gm_ws, gm_bs=gm_bs, rk_mu=rk_mu,
        rk_w0=rk_w0, rk_w2=rk_w2, rk_a0=rk_a0, rk_a2=rk_a2, rk_g2=rk_g2, rk_kk=rk_kk, rk_ka=rk_ka,
        rk_rk=rk_rk, rk_lnx_g=rk_lnx_g, rk_lnx_b=rk_lnx_b, ffn_w_up=ffn_w_up, ffn_conv_w=ffn_conv_w,
        ffn_conv_b=ffn_conv_b, ffn_w_down=ffn_w_down)
    depth = w_in.shape[0]
    bp = x_prompt.shape[0]
    y_p, y_s = x_prompt, x_sample
    outs = [[] for _ in range(10)]
    for l in range(depth):
        lw = _prepare_layer(params, l)
        mem2 = mem_prompt.reshape(bp * MEM_LEN, D_MODEL)
        mkv = _norm_matmul(mem2, lw["g_mem"], lw["w_mkv"], tm=512, tn=2 * C_WIDTH)
        mk = mkv[:, :C_WIDTH].reshape(bp, MEM_LEN, C_WIDTH)
        mv = mkv[:, C_WIDTH:].reshape(bp, MEM_LEN, C_WIDTH)
        zero_shift = jnp.zeros((bp, B_PROJ), x_prompt.dtype)
        zero_wkv = jnp.zeros((bp, B_HEADS, B_HEAD_DIM, B_HEAD_DIM), F32)
        zero_conv = jnp.zeros((bp, 2, 2 * D_FF), x_prompt.dtype)
        y_p, cv, sh, wkv, conv = _decoder_layer(y_p, mk, mv, zero_shift, zero_wkv, zero_conv, lw)
        mem_shape = (bp, MEM_LEN, C_HEADS, C_HEAD_DIM)
        for lst, val in zip(outs[:6], (mk.reshape(mem_shape), mv.reshape(mem_shape), cv, sh, wkv, conv)):
            lst.append(val)
        n_s = x_sample.shape[0]
        cache_k = cache_mem_k[l].reshape(n_s, MEM_LEN * C_HEADS, C_HEAD_DIM)
        cache_v = cache_mem_v[l].reshape(n_s, MEM_LEN * C_HEADS, C_HEAD_DIM)
        y_s, cv, sh, wkv, conv = _decoder_layer(y_s, cache_k, cache_v, state_shift[l],
                                                state_wkv[l], state_conv[l], lw)
        for lst, val in zip(outs[6:], (cv, sh, wkv, conv)):
            lst.append(val)
    return (y_p, y_s) + tuple(jnp.stack(o) for o in outs)
```

```python
import functools
import math

import jax
import jax.numpy as jnp
from jax import lax
from jax.experimental import pallas as pl
from jax.experimental.pallas import tpu as pltpu

D_MODEL = 2048
MEM_LEN = 256
CHUNK = 128
A_HEADS, A_HEAD_DIM = 4, 128
A_WIDTH = A_HEADS * A_HEAD_DIM
B_HEADS, B_HEAD_DIM = 16, 64
B_WIDTH = B_HEADS * B_HEAD_DIM
LORA = 64
B_PROJ = 3 * B_WIDTH + 3 * LORA
C_HEADS, C_HEAD_DIM = 4, 128
C_WIDTH = C_HEADS * C_HEAD_DIM
D_FF = 5632
RMS_EPS = 1e-6
LN_EPS = 1e-5
GN_EPS = 64e-5
DECAY_OFFSET = 0.5

LANES = 128
SUBLANES = 8
VMEM_LIMIT_BYTES = 56 * 1024 * 1024

LORA_PAD = 256
COL_A = 0
COL_R = 2 * A_WIDTH
COL_Q = COL_R + 3 * B_WIDTH
COL_L = COL_Q + C_WIDTH
IN_COLS_PAD = 5120
PROJ_TN = 1024

WKV_CHUNK = 64
FFN_N_TILES = 2
FFN_TN = D_FF // FFN_N_TILES
FFN_SUB = 256

F32 = jnp.float32
BF16 = jnp.bfloat16


def _cparams(sem):
    return pltpu.CompilerParams(dimension_semantics=sem, vmem_limit_bytes=VMEM_LIMIT_BYTES)


def _rms(x, g):
    return x * lax.rsqrt(jnp.mean(x * x, axis=-1, keepdims=True) + RMS_EPS) * g


def _norm_matmul_kernel(x_ref, g_ref, w_ref, o_ref, h_ref, *, w_is_nk):
    @pl.when(pl.program_id(1) == 0)
    def _():
        h_ref[...] = _rms(x_ref[...], g_ref[...]).astype(BF16)

    dims = (((1,), (1,)), ((), ())) if w_is_nk else (((1,), (0,)), ((), ()))
    o_ref[...] = lax.dot_general(h_ref[...], w_ref[...], dims, preferred_element_type=F32)


def _norm_matmul(x, g, w, *, tm, tn, w_is_nk=False):
    t, k = x.shape
    n = w.shape[0] if w_is_nk else w.shape[1]
    w_spec = (pl.BlockSpec((tn, k), lambda i, j: (j, 0)) if w_is_nk
              else pl.BlockSpec((k, tn), lambda i, j: (0, j)))
    return pl.pallas_call(
        functools.partial(_norm_matmul_kernel, w_is_nk=w_is_nk),
        grid=(t // tm, n // tn),
        in_specs=[
            pl.BlockSpec((tm, k), lambda i, j: (i, 0)),
            pl.BlockSpec((1, k), lambda i, j: (0, 0)),
            w_spec,
        ],
        out_specs=pl.BlockSpec((tm, tn), lambda i, j: (i, j)),
        out_shape=jax.ShapeDtypeStruct((t, n), F32),
        scratch_shapes=[pltpu.VMEM((tm, k), BF16)],
        compiler_params=_cparams(("parallel", "arbitrary")),
        name="norm_matmul",
    )(x, g, w)


def _group_a_kernel(u_ref, v_ref, g_ref, b_ref, w_ref, bias_ref, o_ref, vout_ref):
    u = jax.nn.gelu(u_ref[...], approximate=True)
    v = jax.nn.gelu(v_ref[...], approximate=True)
    mean = jnp.mean(v, axis=-1, keepdims=True)
    d = v - mean
    var = jnp.mean(d * d, axis=-1, keepdims=True)
    vn = d * lax.rsqrt(var + LN_EPS) * g_ref[...] + b_ref[...]
    vout_ref[...] = vn
    vb = vn.astype(BF16)
    for c in range(u.shape[0] // CHUNK):
        rs = slice(c * CHUNK, (c + 1) * CHUNK)
        for h in range(A_HEADS):
            hs = slice(h * A_HEAD_DIM, (h + 1) * A_HEAD_DIM)
            mixed = jnp.dot(w_ref[h], vb[rs, hs], preferred_element_type=F32) + bias_ref[:, hs]
            o_ref[rs, hs] = (u[rs, hs] * mixed).astype(BF16)


def _group_a(proj, ln_g, ln_b, w_sp, bias_full, *, rows=4 * CHUNK):
    t = proj.shape[0]
    return pl.pallas_call(
        _group_a_kernel,
        grid=(t // rows,),
        in_specs=[
            pl.BlockSpec((rows, A_WIDTH), lambda i: (i, COL_A // A_WIDTH)),
            pl.BlockSpec((rows, A_WIDTH), lambda i: (i, COL_A // A_WIDTH + 1)),
            pl.BlockSpec((1, A_WIDTH), lambda i: (0, 0)),
            pl.BlockSpec((1, A_WIDTH), lambda i: (0, 0)),
            pl.BlockSpec((A_HEADS, CHUNK, CHUNK), lambda i: (0, 0, 0)),
            pl.BlockSpec((CHUNK, A_WIDTH), lambda i: (0, 0)),
        ],
        out_specs=[
            pl.BlockSpec((rows, A_WIDTH), lambda i: (i, 0)),
            pl.BlockSpec((rows, A_WIDTH), lambda i: (i, 0)),
        ],
        out_shape=[
            jax.ShapeDtypeStruct((t, A_WIDTH), BF16),
            jax.ShapeDtypeStruct((t, A_WIDTH), F32),
        ],
        compiler_params=_cparams(("parallel",)),
        name="group_a",
    )(proj, proj, ln_g, ln_b, w_sp, bias_full)


def _softmax_rows(s):
    e = jnp.exp(s - jnp.max(s, axis=-1, keepdims=True))
    return e / jnp.sum(e, axis=-1, keepdims=True)


def _attn_kernel(q_ref, k_ref, v_ref, o_ref, *, n_b, tq, head_major_rows):
    scale = C_HEAD_DIM ** -0.5
    heads = range(C_HEADS)
    lanes = lambda h: slice(h * C_HEAD_DIM, (h + 1) * C_HEAD_DIM)
    if not head_major_rows:
        chains = [(g, h) for g in range(n_b) for h in heads]
        rows = lambda g: slice(g * tq, (g + 1) * tq)
        s = [lax.dot_general(q_ref[rows(g), lanes(h)].astype(BF16), k_ref[g, :, lanes(h)].astype(BF16), _NT,
                             preferred_element_type=F32) * scale for g, h in chains]
        p = [_softmax_rows(m).astype(BF16) for m in s]
        o = [jnp.dot(m, v_ref[g, :, lanes(h)].astype(BF16), preferred_element_type=F32)
             for m, (g, h) in zip(p, chains)]
        for m, (g, h) in zip(o, chains):
            o_ref[rows(g), lanes(h)] = m.astype(BF16)
        return
    row = lax.broadcasted_iota(jnp.int32, (C_HEADS * tq, 1), 0)
    col = lax.broadcasted_iota(jnp.int32, (1, C_HEADS * MEM_LEN), 1)
    own = (col & (C_HEADS - 1)) == (row >> int(math.log2(tq)))
    q = [q_ref[g * tq:(g + 1) * tq, :] for g in range(n_b)]
    qs = [jnp.concatenate([m[:, lanes(h)] for h in heads], axis=0).astype(BF16) for m in q]
    s = [lax.dot_general(m, k_ref[g].astype(BF16), _NT, preferred_element_type=F32) * scale
         for g, m in enumerate(qs)]
    p = [_softmax_rows(jnp.where(own, m, -1e30)).astype(BF16) for m in s]
    o = [jnp.dot(m, v_ref[g].astype(BF16), preferred_element_type=F32) for g, m in enumerate(p)]
    for g, m in enumerate(o):
        for h in heads:
            o_ref[g * tq:(g + 1) * tq, lanes(h)] = m[h * tq:(h + 1) * tq].astype(BF16)


def _attention(proj, mem_k, mem_v, *, batch, seq):
    head_major_rows = mem_k.shape[-1] == C_HEAD_DIM
    if head_major_rows:
        tq, n_b = seq, 8
    else:
        tq, n_b = 512, 1
    n_q = seq // tq
    mem_blk = (n_b,) + mem_k.shape[1:]
    return pl.pallas_call(
        functools.partial(_attn_kernel, n_b=n_b, tq=tq, head_major_rows=head_major_rows),
        grid=(batch // n_b, n_q),
        in_specs=[
            pl.BlockSpec((n_b * tq, C_WIDTH), lambda b, i: (b * n_q + i, COL_Q // C_WIDTH)),
            pl.BlockSpec(mem_blk, lambda b, i: (b, 0, 0)),
            pl.BlockSpec(mem_blk, lambda b, i: (b, 0, 0)),
        ],
        out_specs=pl.BlockSpec((n_b * tq, C_WIDTH), lambda b, i: (b * n_q + i, 0)),
        out_shape=jax.ShapeDtypeStruct((batch * seq, C_WIDTH), BF16),
        compiler_params=_cparams(("parallel", "arbitrary")),
        name="mem_attention",
    )(proj, mem_k, mem_v)


_V_MU_R, _V_MU_K, _V_MU_V, _V_W0, _V_A0, _V_KK, _V_KA, _V_RK, _V_LNG, _V_LNB = range(10)


_NN = (((1,), (0,)), ((), ()))
_NT = (((1,), (1,)), ((), ()))
_TN = (((0,), (0,)), ((), ()))
HEAD_PAIRS = B_HEADS // 2


def _split(x):
    hi = x.astype(BF16)
    return hi, (x - hi.astype(F32)).astype(BF16)


def _b(x):
    return x.astype(BF16)


def _bdot(a, b, dims=_NN):
    return lax.dot_general(a, b, dims, preferred_element_type=F32)


def _rwkv_kernel(pr_ref, pk_ref, pv_ref, pl_ref, sr_ref, sk_ref, sv_ref, sl_ref, wkv_ref,
                 vec_ref, mul_ref, w2_ref, a2_ref, g2_ref, seg_ref,
                 o_ref, so_ref, cr_ref, ck_ref, cv_ref, cl_ref, sbd_ref, *, n_blk, groups, chunk, whole_seq):
    rows = groups * chunk
    first = pl.program_id(1) == 0
    last = pl.program_id(1) == pl.num_programs(1) - 1
    hd = B_HEAD_DIM
    n_seq = n_blk * groups

    @pl.when(first)
    def _():
        zero = jnp.zeros((hd, hd), F32)
        for g in range(n_seq):
            for q in range(HEAD_PAIRS):
                top = jnp.concatenate([wkv_ref[g, 2 * q], zero], axis=1)
                bot = jnp.concatenate([zero, wkv_ref[g, 2 * q + 1]], axis=1)
                sbd_ref[g, q] = jnp.concatenate([top, bot], axis=0)
        if not whole_seq:
            for s_ref, carry_ref in ((sr_ref, cr_ref), (sk_ref, ck_ref), (sv_ref, cv_ref), (sl_ref, cl_ref)):
                for blk in range(n_blk):
                    carry_ref[blk] = jnp.broadcast_to(s_ref[blk], (SUBLANES, s_ref.shape[-1]))

    row_id = lax.broadcasted_iota(jnp.int32, (rows, 1), 0)
    col_id = lax.broadcasted_iota(jnp.int32, (1, rows), 1)
    chunk_bits = int(math.log2(chunk))
    incl_b = (((row_id >> chunk_bits) == (col_id >> chunk_bits)) & (col_id <= row_id)).astype(BF16)
    vec = lambda i: vec_ref[i:i + 1, :]
    seg_ones = seg_ref[...]

    def seg_sum(x):
        slabs = jnp.concatenate([x[:, q * LANES:(q + 1) * LANES] for q in range(HEAD_PAIRS)], axis=0)
        hi, lo = _split(slabs)
        s = (jnp.dot(hi, seg_ones, preferred_element_type=F32)
             + jnp.dot(lo, seg_ones, preferred_element_type=F32))
        return jnp.concatenate([s[q * rows:(q + 1) * rows] for q in range(HEAD_PAIRS)], axis=1)

    def prologue(blk, out):
        seqs = slice(blk * groups, (blk + 1) * groups)

        def prev_rows(x, s_ref, carry_ref):
            width = x.shape[1]
            if whole_seq:
                start = jnp.broadcast_to(s_ref[seqs], (groups, chunk, width)).reshape(rows, width)
                return jnp.where((row_id & (chunk - 1)) == 0, start, pltpu.roll(x, 1, 0))

            ext = jnp.concatenate([carry_ref[blk], x], axis=0)
            prev = pltpu.roll(ext, 1, 0)[SUBLANES:]
            carry_ref[blk] = x[rows - SUBLANES:]
            return prev

        def shifted(p_ref, s_ref, carry_ref, mu):
            x = p_ref[blk]
            return x + (prev_rows(x, s_ref, carry_ref) - x) * mu

        lo = shifted(pl_ref, sl_ref, cl_ref, mul_ref[0:1, :])
        dw = jnp.dot(jnp.tanh(lo).astype(BF16), w2_ref[...], preferred_element_type=F32)
        da = jnp.dot(lo.astype(BF16), a2_ref[...], preferred_element_type=F32)
        gate = jnp.dot(jax.nn.sigmoid(lo).astype(BF16), g2_ref[...], preferred_element_type=F32)
        yield
        w_log = -jax.nn.softplus(-(vec(_V_W0) + dw)) - DECAY_OFFSET
        log_decay = -jnp.exp(w_log)
        ld_hi, ld_mid = _split(log_decay)
        ld_lo = (log_decay - ld_hi.astype(F32) - ld_mid.astype(F32)).astype(BF16)
        cum = (jnp.dot(incl_b, ld_hi, preferred_element_type=F32)
               + jnp.dot(incl_b, ld_mid, preferred_element_type=F32)
               + jnp.dot(incl_b, ld_lo, preferred_element_type=F32))
        yield
        a = jax.nn.sigmoid(vec(_V_A0) + da)
        k = shifted(pk_ref, sk_ref, ck_ref, vec(_V_MU_K))
        kk = k * vec(_V_KK)
        kk = kk / jnp.maximum(jnp.sqrt(seg_sum(kk * kk)), 1e-12)
        yield
        p_incl = jnp.exp(cum)
        p_inv = jnp.exp(-cum)
        a_t = -kk * jnp.exp(cum - log_decay)
        b_t = kk * a * p_inv
        yield
        k = k * (1.0 + (a - 1.0) * vec(_V_KA))
        k_t = k * p_inv
        r = shifted(pr_ref, sr_ref, cr_ref, vec(_V_MU_R))
        r_t = r * p_incl
        yield
        v = shifted(pv_ref, sv_ref, cv_ref, vec(_V_MU_V))
        bonus = seg_sum(r * k * vec(_V_RK)) * v
        out.update(a_t=a_t, b_t=b_t, k_t=k_t, r_t=r_t, v=v, p_incl=p_incl, bonus=bonus, gate=gate)

    pr = 2 * rows
    row2 = lax.broadcasted_iota(jnp.int32, (pr, 1), 0)
    col2 = lax.broadcasted_iota(jnp.int32, (1, pr), 1)
    t2, s2 = row2 & (rows - 1), col2 & (rows - 1)
    same2 = ((row2 >> chunk_bits) == (col2 >> chunk_bits))
    incl2 = same2 & (s2 <= t2)
    strict2 = same2 & (s2 < t2)
    eye2 = (row2 == col2).astype(F32)
    left = lax.broadcasted_iota(jnp.int32, (rows, LANES), 1) < hd

    def bd(x):
        zero = jnp.zeros_like(x)
        return jnp.concatenate([jnp.where(left, x, zero), jnp.where(left, zero, x)], axis=0)

    def group_rows(mats, g):
        starts = [hh * rows + g * chunk for hh in (0, 1)]
        return jnp.concatenate([m[i:i + chunk] for m in mats for i in starts], axis=0)

    n_sq = chunk_bits - 1
    col4 = lax.broadcasted_iota(jnp.int32, (1, 2 * pr), 1)
    incl4 = ((row2 >> chunk_bits) == ((col4 & (pr - 1)) >> chunk_bits)) & ((col4 & (rows - 1)) <= t2)
    pairs = range(HEAD_PAIRS)
    lanes_of = [slice(q * LANES, (q + 1) * LANES) for q in pairs]

    def recurrence(blk, pro):
        s0 = blk * groups
        p_incl = pro["p_incl"]
        bds = [[bd(pro[name][:, ls]) for name in ("a_t", "r_t", "b_t", "k_t", "v")] for ls in lanes_of]
        ar_s = [_b(jnp.concatenate([m[0], m[1]], axis=0)) for m in bds]
        bk_s = [_b(jnp.concatenate([m[2], m[3]], axis=0)) for m in bds]
        v_s = [_b(m[4]) for m in bds]
        ms = [_bdot(ar_s[q], bk_s[q], _NT) for q in pairs]
        yield
        a_ab = [jnp.where(strict2, m[:pr, :pr], 0.0) for m in ms]
        a_ak = [jnp.where(strict2, m[:pr, pr:], 0.0) for m in ms]
        a_r = [jnp.where(incl4, m[pr:], 0.0) for m in ms]
        inv = [eye2 + n for n in a_ab]
        pw_s = [_b(n) for n in a_ab]
        for _ in range(n_sq):
            pw_s = [_b(_bdot(p, p)) for p in pw_s]
            inv = [i + _bdot(p, _b(i)) for i, p in zip(inv, pw_s)]
            yield
        if groups == 1:
            xy = [_bdot(ar_s[q], _b(sbd_ref[s0, q]), _NT) for q in pairs]
            x0, y0 = [m[:pr] for m in xy], [m[pr:] for m in xy]
        else:
            x0, y0 = [], []
            for q in pairs:
                x_parts, y_parts = [None] * (2 * groups), [None] * (2 * groups)
                for g in range(groups):
                    xy = _bdot(_b(group_rows(bds[q][:2], g)), _b(sbd_ref[s0 + g, q]), _NT)
                    for hh in (0, 1):
                        x_parts[hh * groups + g] = xy[hh * chunk:(hh + 1) * chunk]
                        y_parts[hh * groups + g] = xy[(2 + hh) * chunk:(3 + hh) * chunk]
                x0.append(jnp.concatenate(x_parts, axis=0))
                y0.append(jnp.concatenate(y_parts, axis=0))
        yield
        rhs = [_b(x0[q] + _bdot(_b(a_ak[q]), v_s[q])) for q in pairs]
        u = [_bdot(_b(inv[q]), rhs[q]) for q in pairs]
        yield
        uv_s = [jnp.concatenate([_b(u[q]), v_s[q]], axis=0) for q in pairs]
        y2 = [y0[q] + _bdot(_b(a_r[q]), uv_s[q]) for q in pairs]
        pro["y"] = jnp.concatenate([m[:rows] + m[rows:] for m in y2], axis=1)
        yield
        for q in pairs:
            if groups == 1:
                ds = _bdot(uv_s[q], bk_s[q], _TN)
                sbd_ref[s0, q] = (sbd_ref[s0, q] + ds) * p_incl[rows - 1:rows, lanes_of[q]]
            else:
                for g in range(groups):
                    ds = _bdot(_b(group_rows((u[q], bds[q][4]), g)),
                               _b(group_rows((bds[q][2], bds[q][3]), g)), _TN)
                    end = (g + 1) * chunk - 1
                    sbd_ref[s0 + g, q] = (sbd_ref[s0 + g, q] + ds) * p_incl[end:end + 1, lanes_of[q]]

    def finish(blk, pro):
        inv_n = 1.0 / B_HEAD_DIM
        y = pro["y"]
        d = y - seg_sum(y) * inv_n
        yield
        var = seg_sum(d * d) * inv_n
        yield
        yn = d * lax.rsqrt(var + GN_EPS) * vec(_V_LNG) + vec(_V_LNB)
        o_ref[blk] = ((yn + pro["bonus"]) * pro["gate"]).astype(BF16)

    def interleave(*gens):
        live = list(gens)
        while live:
            for gen in list(live):
                if next(gen, StopIteration) is StopIteration:
                    live.remove(gen)

    pros = [dict() for _ in range(n_blk)]
    interleave(prologue(0, pros[0]))
    for blk in range(n_blk):
        work = [recurrence(blk, pros[blk])]
        if blk + 1 < n_blk:
            work.append(prologue(blk + 1, pros[blk + 1]))
        if blk > 0:
            work.append(finish(blk - 1, pros[blk - 1]))
        interleave(*work)
    interleave(finish(n_blk - 1, pros[n_blk - 1]))

    @pl.when(last)
    def _():
        for g in range(n_seq):
            for q in range(HEAD_PAIRS):
                sq = sbd_ref[g, q]
                so_ref[g, 2 * q] = sq[:hd, :hd]
                so_ref[g, 2 * q + 1] = sq[hd:, hd:]


def _rwkv(proj, shift_parts, wkv_prev, lw, *, batch, seq):
    whole_seq = seq <= WKV_CHUNK
    if whole_seq:
        assert seq == SUBLANES, "whole-sequence blocks rely on one sublane tile per sequence"
        chunk, groups = seq, WKV_CHUNK // seq
        n_blk, n_chunks = 2, 1
        n_outer = batch // (groups * n_blk)
        proj3 = proj.reshape(batch // groups, WKV_CHUNK, IN_COLS_PAD)
    else:
        chunk, groups = WKV_CHUNK, 1
        n_blk, n_outer, n_chunks = batch, 1, seq // WKV_CHUNK
        proj3 = proj.reshape(batch, seq, IN_COLS_PAD)
    n_seq = n_blk * groups
    act = lambda col_blk, width: pl.BlockSpec((n_blk, WKV_CHUNK, width), lambda o, c: (o, c, col_blk))
    per_seq = lambda *tail: pl.BlockSpec((n_seq,) + tail, lambda o, c: (o,) + (0,) * len(tail))
    fixed = lambda *shape: pl.BlockSpec(shape, lambda o, c: (0,) * len(shape))
    state = (B_HEADS, B_HEAD_DIM, B_HEAD_DIM)
    kern = functools.partial(_rwkv_kernel, n_blk=n_blk, groups=groups, chunk=chunk, whole_seq=whole_seq)
    b_out, wkv_new = pl.pallas_call(
        kern,
        grid=(n_outer, n_chunks),
        in_specs=[
            act(COL_R // B_WIDTH, B_WIDTH), act(COL_R // B_WIDTH + 1, B_WIDTH),
            act(COL_R // B_WIDTH + 2, B_WIDTH), act(COL_L // LORA_PAD, LORA_PAD),
            per_seq(1, B_WIDTH), per_seq(1, B_WIDTH), per_seq(1, B_WIDTH), per_seq(1, LORA_PAD),
            per_seq(*state),
            fixed(16, B_WIDTH), fixed(SUBLANES, LORA_PAD),
            fixed(LORA_PAD, B_WIDTH), fixed(LORA_PAD, B_WIDTH), fixed(LORA_PAD, B_WIDTH),
            fixed(LANES, LANES),
        ],
        out_specs=[act(0, B_WIDTH), per_seq(*state)],
        out_shape=[
            jax.ShapeDtypeStruct(proj3.shape[:2] + (B_WIDTH,), BF16),
            jax.ShapeDtypeStruct((batch,) + state, F32),
        ],
        scratch_shapes=[
            pltpu.VMEM((n_blk, SUBLANES, B_WIDTH), F32),
            pltpu.VMEM((n_blk, SUBLANES, B_WIDTH), F32),
            pltpu.VMEM((n_blk, SUBLANES, B_WIDTH), F32),
            pltpu.VMEM((n_blk, SUBLANES, LORA_PAD), F32),
            pltpu.VMEM((n_seq, HEAD_PAIRS, LANES, LANES), F32),
        ],
        compiler_params=_cparams(("parallel", "arbitrary")),
        name="rwkv7",
    )(proj3, proj3, proj3, proj3, *shift_parts, wkv_prev,
      lw["rk_vecs"], lw["rk_mu_l"], lw["rk_w2"], lw["rk_a2"], lw["rk_g2"], lw["seg_ones"])
    return b_out.reshape(batch * seq, B_WIDTH), wkv_new


def _out_proj_kernel(a_ref, b_ref, c_ref, w_ref, x_ref, gpost_ref, gffn_ref, x1_ref, h_ref):
    n_sub = 4
    sub = x_ref.shape[0] // n_sub
    halves = [slice(s * sub, (s + 1) * sub) for s in range(n_sub)]

    def project(rs):
        acc = jnp.dot(a_ref[rs, :], w_ref[0:A_WIDTH, :], preferred_element_type=F32)
        acc += jnp.dot(b_ref[rs, :], w_ref[A_WIDTH:A_WIDTH + B_WIDTH, :], preferred_element_type=F32)
        return acc + jnp.dot(c_ref[rs, :], w_ref[A_WIDTH + B_WIDTH:, :], preferred_element_type=F32)

    accs = [project(rs) for rs in halves]
    for rs, acc in zip(halves, accs):
        x1 = x_ref[rs, :] + _rms(acc, gpost_ref[...])
        x1_ref[rs, :] = x1
        h_ref[rs, :] = _rms(x1, gffn_ref[...]).astype(BF16)


def _out_proj(a_out, b_out, c_out, w_out, x, g_post, g_ffn, *, tm=512):
    t = x.shape[0]
    row = lambda i: (i, 0)
    fixed = lambda i: (0, 0)
    return pl.pallas_call(
        _out_proj_kernel,
        grid=(t // tm,),
        in_specs=[
            pl.BlockSpec((tm, A_WIDTH), row),
            pl.BlockSpec((tm, B_WIDTH), row),
            pl.BlockSpec((tm, C_WIDTH), row),
            pl.BlockSpec((D_MODEL, D_MODEL), fixed, pipeline_mode=pl.Buffered(1)),
            pl.BlockSpec((tm, D_MODEL), row),
            pl.BlockSpec((1, D_MODEL), fixed),
            pl.BlockSpec((1, D_MODEL), fixed),
        ],
        out_specs=[pl.BlockSpec((tm, D_MODEL), row), pl.BlockSpec((tm, D_MODEL), row)],
        out_shape=[jax.ShapeDtypeStruct((t, D_MODEL), F32), jax.ShapeDtypeStruct((t, D_MODEL), BF16)],
        compiler_params=_cparams(("parallel",)),
        name="out_proj",
    )(a_out, b_out, c_out, w_out, x, g_post, g_ffn)


def _ffn_up_kernel(h_ref, wg_ref, wv_ref, cwg_ref, cwv_ref, cbg_ref, cbv_ref, pg_ref, pv_ref,
                   act_ref, ng_ref, nv_ref, cg_ref, cv_ref, *, tm, seq):
    h = h_ref[...]
    tn = wg_ref.shape[1]
    row_id = lax.broadcasted_iota(jnp.int32, (tm, 1), 0)
    whole = seq <= tm
    if not whole:
        @pl.when(pl.program_id(1) % (seq // tm) == 0)
        def _():
            for carry_ref, p_ref in ((cg_ref, pg_ref), (cv_ref, pv_ref)):
                carry_ref[...] = jnp.concatenate([jnp.zeros((SUBLANES - 2, tn), F32), p_ref[0]], axis=0)

    def conv(up, cs, cw_ref, cb_ref, p_ref, n_ref, carry_ref):
        width = up.shape[1]
        if whole:
            n_seq = tm // seq
            prev = p_ref[:, :, cs]
            e0 = jnp.broadcast_to(prev[:, 0:1, :], (n_seq, seq, width)).reshape(tm, width)
            e1 = jnp.broadcast_to(prev[:, 1:2, :], (n_seq, seq, width)).reshape(tm, width)
            tau = row_id & (seq - 1)
            m1 = jnp.where(tau == 0, e1, pltpu.roll(up, 1, 0))
            m2 = jnp.where(tau == 0, e0, jnp.where(tau == 1, e1, pltpu.roll(up, 2, 0)))
            n_ref[:, :, cs] = up.reshape(n_seq, seq, width)[:, seq - 2:, :]
        else:
            carry = carry_ref[:, cs]
            c1, c2 = carry[SUBLANES - 1:], carry[SUBLANES - 2:SUBLANES - 1]
            m1 = jnp.where(row_id == 0, c1, pltpu.roll(up, 1, 0))
            m2 = jnp.where(row_id == 0, c2, jnp.where(row_id == 1, c1, pltpu.roll(up, 2, 0)))
            carry_ref[:, cs] = up[tm - SUBLANES:]
            n_ref[0, :, cs] = up[tm - 2:]
        return cb_ref[:, cs] + m2 * cw_ref[0:1, cs] + m1 * cw_ref[1:2, cs] + up * cw_ref[2:3, cs]

    subs = [slice(s * FFN_SUB, (s + 1) * FFN_SUB) for s in range(tn // FFN_SUB)]
    dots = lambda cs: (jnp.dot(h, wg_ref[:, cs], preferred_element_type=F32),
                       jnp.dot(h, wv_ref[:, cs], preferred_element_type=F32))
    pending = dots(subs[0])
    for s, cs in enumerate(subs):
        up_g, up_v = pending
        if s + 1 < len(subs):
            pending = dots(subs[s + 1])
        gate = conv(up_g, cs, cwg_ref, cbg_ref, pg_ref, ng_ref, cg_ref)
        val = conv(up_v, cs, cwv_ref, cbv_ref, pv_ref, nv_ref, cv_ref)
        act_ref[:, cs] = (jax.nn.gelu(gate, approximate=True) * val).astype(BF16)


def _ffn_up(h, w_up, conv_w, conv_b, conv_prev, *, batch, seq):
    t = h.shape[0]
    if seq <= SUBLANES:
        assert seq == SUBLANES, "whole-sequence tiles rely on one sublane tile per sequence"
        tm, tn = t, 2 * FFN_SUB
        n_tiles = D_FF // tn
        state_blk = (batch, 2, tn)
        state_idx = lambda half: (lambda j, i: (0, 0, j + half * n_tiles))
        w_mode = {}
    else:
        tm, tn, n_tiles = 512, FFN_TN, FFN_N_TILES
        tiles_per_seq = seq // tm
        state_blk = (1, 2, tn)
        state_idx = lambda half: (lambda j, i: (i // tiles_per_seq, 0, j + half * n_tiles))
        w_mode = dict(pipeline_mode=pl.Buffered(1))
    col = lambda half: (lambda j, i: (0, j + half * n_tiles))
    kern = functools.partial(_ffn_up_kernel, tm=tm, seq=seq)
    state_out = jax.ShapeDtypeStruct((batch, 2, D_FF), F32)
    return pl.pallas_call(
        kern,
        grid=(n_tiles, t // tm),
        in_specs=[
            pl.BlockSpec((tm, D_MODEL), lambda j, i: (i, 0)),
            pl.BlockSpec((D_MODEL, tn), col(0), **w_mode),
            pl.BlockSpec((D_MODEL, tn), col(1), **w_mode),
            pl.BlockSpec((3, tn), col(0)),
            pl.BlockSpec((3, tn), col(1)),
            pl.BlockSpec((1, tn), col(0)),
            pl.BlockSpec((1, tn), col(1)),
            pl.BlockSpec(state_blk, state_idx(0)),
            pl.BlockSpec(state_blk, state_idx(1)),
        ],
        out_specs=[
            pl.BlockSpec((tm, tn), lambda j, i: (i, j)),
            pl.BlockSpec(state_blk, state_idx(0)),
            pl.BlockSpec(state_blk, state_idx(0)),
        ],
        out_shape=[jax.ShapeDtypeStruct((t, D_FF), BF16), state_out, state_out],
        scratch_shapes=[pltpu.VMEM((SUBLANES, tn), F32), pltpu.VMEM((SUBLANES, tn), F32)],
        compiler_params=_cparams(("parallel", "arbitrary")),
        name="ffn_up_conv",
    )(h, w_up, w_up, conv_w, conv_w, conv_b, conv_b, conv_prev, conv_prev)


def _ffn_down_kernel(a_ref, w_ref, x_ref, g_ref, o_ref):
    acc = jnp.dot(a_ref[...], w_ref[...], preferred_element_type=F32)
    o_ref[...] = x_ref[...] + _rms(acc, g_ref[...])


def _ffn_down(act, w_down, x1, g_post, *, tm=256):
    t = x1.shape[0]
    return pl.pallas_call(
        _ffn_down_kernel,
        grid=(t // tm,),
        in_specs=[
            pl.BlockSpec((tm, D_FF), lambda i: (i, 0)),
            pl.BlockSpec((D_FF, D_MODEL), lambda i: (0, 0), pipeline_mode=pl.Buffered(1)),
            pl.BlockSpec((tm, D_MODEL), lambda i: (i, 0)),
            pl.BlockSpec((1, D_MODEL), lambda i: (0, 0)),
        ],
        out_specs=pl.BlockSpec((tm, D_MODEL), lambda i: (i, 0)),
        out_shape=jax.ShapeDtypeStruct((t, D_MODEL), F32),
        compiler_params=_cparams(("parallel",)),
        name="ffn_down",
    )(act, w_down, x1, g_post)


def _pad_rows(w, row0, total):
    return jnp.zeros((total, w.shape[1]), w.dtype).at[row0:row0 + w.shape[0]].set(w)


def _prepare_layer(p, l):
    w_in_t = p["w_in"][l].T
    b0 = 2 * A_WIDTH
    q0 = b0 + B_PROJ
    w_in_p = jnp.concatenate([
        w_in_t[:b0 + 3 * B_WIDTH],
        w_in_t[q0:],
        w_in_t[b0 + 3 * B_WIDTH:q0],
        jnp.zeros((IN_COLS_PAD - COL_L - 3 * LORA, D_MODEL), w_in_t.dtype),
    ], axis=0).astype(BF16)
    mu = p["rk_mu"][l]
    vec_rows = [mu[:B_WIDTH], mu[B_WIDTH:2 * B_WIDTH], mu[2 * B_WIDTH:3 * B_WIDTH], p["rk_w0"][l],
                p["rk_a0"][l], p["rk_kk"][l], p["rk_ka"][l], p["rk_rk"][l].reshape(B_WIDTH),
                p["rk_lnx_g"][l], p["rk_lnx_b"][l]]
    vecs = jnp.zeros((16, B_WIDTH), F32).at[:len(vec_rows)].set(jnp.stack(vec_rows))
    mu_l = jnp.zeros((SUBLANES, LORA_PAD), F32).at[0, :3 * LORA].set(mu[3 * B_WIDTH:])
    head_of = jnp.arange(LANES) // B_HEAD_DIM
    seg_ones = (head_of[:, None] == head_of[None, :]).astype(BF16)
    tril = jnp.tril(jnp.ones((CHUNK, CHUNK), bool))
    ws = jnp.where(tril[None], p["gm_ws"][l], 0.0)
    bs = p["gm_bs"][l]
    return {
        "g_mix_pre": p["norm_mix_pre"][l][None], "g_mix_post": p["norm_mix_post"][l][None],
        "g_ffn_pre": p["norm_ffn_pre"][l][None], "g_ffn_post": p["norm_ffn_post"][l][None],
        "g_mem": p["norm_mem"][l][None],
        "w_in": w_in_p,
        "w_out": p["w_out"][l].astype(BF16),
        "w_mkv": jnp.concatenate([p["w_mem_k"][l], p["w_mem_v"][l]], axis=1).astype(BF16),
        "gm_ln_g": p["gm_ln_g"][l][None], "gm_ln_b": p["gm_ln_b"][l][None],
        "gm_ws": ws, "gm_bs": bs,
        "rk_vecs": vecs, "rk_mu_l": mu_l,
        "rk_w2": _pad_rows(p["rk_w2"][l], 0, LORA_PAD).astype(BF16),
        "rk_a2": _pad_rows(p["rk_a2"][l], LORA, LORA_PAD).astype(BF16),
        "rk_g2": _pad_rows(p["rk_g2"][l], 2 * LORA, LORA_PAD).astype(BF16),
        "seg_ones": seg_ones,
        "w_up": p["ffn_w_up"][l].astype(BF16),
        "conv_w": p["ffn_conv_w"][l], "conv_b": p["ffn_conv_b"][l][None],
        "w_down": p["ffn_w_down"][l].astype(BF16),
    }


def _spatial_weights(lw, seq):
    ws, bs = lw["gm_ws"], lw["gm_bs"]
    if seq >= CHUNK:
        w_blk, b_rows = ws, bs
    else:
        reps = CHUNK // seq
        eye = jnp.eye(reps, dtype=ws.dtype)
        w_blk = jnp.einsum("ab,hts->hatbs", eye, ws[:, :seq, :seq]).reshape(A_HEADS, CHUNK, CHUNK)
        b_rows = jnp.tile(bs[:, :seq], (1, reps))
    bias_full = jnp.repeat(b_rows.T, A_HEAD_DIM, axis=1)
    return w_blk.astype(BF16), bias_full


def _decoder_layer(x, mem_k, mem_v, shift_prev, wkv_prev, conv_prev, lw):
    batch, seq, _ = x.shape
    x2 = x.reshape(batch * seq, D_MODEL)
    proj = _norm_matmul(x2, lw["g_mix_pre"], lw["w_in"], tm=1024, tn=PROJ_TN, w_is_nk=True)
    w_sp, bias_full = _spatial_weights(lw, seq)
    a_out, a_v = _group_a(proj, lw["gm_ln_g"], lw["gm_ln_b"], w_sp, bias_full)
    shift_parts = [
        shift_prev[:, None, :B_WIDTH], shift_prev[:, None, B_WIDTH:2 * B_WIDTH],
        shift_prev[:, None, 2 * B_WIDTH:3 * B_WIDTH],
        jnp.pad(shift_prev[:, None, 3 * B_WIDTH:], ((0, 0), (0, 0), (0, LORA_PAD - 3 * LORA))),
    ]
    b_out, wkv_new = _rwkv(proj, shift_parts, wkv_prev, lw, batch=batch, seq=seq)
    c_out = _attention(proj, mem_k, mem_v, batch=batch, seq=seq)
    x1, h = _out_proj(a_out, b_out, c_out, lw["w_out"], x2, lw["g_mix_post"], lw["g_ffn_pre"])
    act, conv_g, conv_v = _ffn_up(h, lw["w_up"], lw["conv_w"], lw["conv_b"], conv_prev, batch=batch, seq=seq)
    y = _ffn_down(act, lw["w_down"], x1, lw["g_ffn_post"])

    chunk_start = ((seq - 1) // CHUNK) * CHUNK
    chunk_v = a_v.reshape(batch, seq, A_WIDTH)[:, chunk_start:].reshape(batch, -1, A_HEADS, A_HEAD_DIM)
    last = proj.reshape(batch, seq, IN_COLS_PAD)[:, -1]
    shift_new = jnp.concatenate([last[:, COL_R:COL_R + 3 * B_WIDTH], last[:, COL_L:COL_L + 3 * LORA]], axis=1)
    conv_new = jnp.concatenate([conv_g, conv_v], axis=-1)
    return y.reshape(batch, seq, D_MODEL), chunk_v, shift_new, wkv_new, conv_new


def kernel(x_prompt, x_sample, mem_prompt, cache_mem_k, cache_mem_v, state_shift, state_wkv, state_conv,
           norm_mix_pre, norm_mix_post, norm_ffn_pre, norm_ffn_post, norm_mem, w_in, w_out, w_mem_k, w_mem_v,
           gm_ln_g, gm_ln_b, gm_ws, gm_bs, rk_mu, rk_w0, rk_w2, rk_a0, rk_a2, rk_g2, rk_kk, rk_ka, rk_rk,
           rk_lnx_g, rk_lnx_b, ffn_w_up, ffn_conv_w, ffn_conv_b, ffn_w_down):
    params = dict(
        norm_mix_pre=norm_mix_pre, norm_mix_post=norm_mix_post, norm_ffn_pre=norm_ffn_pre,
        norm_ffn_post=norm_ffn_post, norm_mem=norm_mem, w_in=w_in, w_out=w_out, w_mem_k=w_mem_k,
        w_mem_v=w_mem_v, gm_ln_g=gm_ln_g, gm_ln_b=gm_ln_b, gm_ws=gm_ws, gm_bs=gm_bs, rk_mu=rk_mu,
        rk_w0=rk_w0, rk_w2=rk_w2, rk_a0=rk_a0, rk_a2=rk_a2, rk_g2=rk_g2, rk_kk=rk_kk, rk_ka=rk_ka,
        rk_rk=rk_rk, rk_lnx_g=rk_lnx_g, rk_lnx_b=rk_lnx_b, ffn_w_up=ffn_w_up, ffn_conv_w=ffn_conv_w,
        ffn_conv_b=ffn_conv_b, ffn_w_down=ffn_w_down)
    depth = w_in.shape[0]
    bp = x_prompt.shape[0]
    y_p, y_s = x_prompt, x_sample
    outs = [[] for _ in range(10)]
    for l in range(depth):
        lw = _prepare_layer(params, l)
        mem2 = mem_prompt.reshape(bp * MEM_LEN, D_MODEL)
        mkv = _norm_matmul(mem2, lw["g_mem"], lw["w_mkv"], tm=512, tn=2 * C_WIDTH)
        mk = mkv[:, :C_WIDTH].reshape(bp, MEM_LEN, C_WIDTH)
        mv = mkv[:, C_WIDTH:].reshape(bp, MEM_LEN, C_WIDTH)
        zero_shift = jnp.zeros((bp, B_PROJ), x_prompt.dtype)
        zero_wkv = jnp.zeros((bp, B_HEADS, B_HEAD_DIM, B_HEAD_DIM), F32)
        zero_conv = jnp.zeros((bp, 2, 2 * D_FF), x_prompt.dtype)
        y_p, cv, sh, wkv, conv = _decoder_layer(y_p, mk, mv, zero_shift, zero_wkv, zero_conv, lw)
        mem_shape = (bp, MEM_LEN, C_HEADS, C_HEAD_DIM)
        for lst, val in zip(outs[:6], (mk.reshape(mem_shape), mv.reshape(mem_shape), cv, sh, wkv, conv)):
            lst.append(val)
        n_s = x_sample.shape[0]
        cache_k = cache_mem_k[l].reshape(n_s, MEM_LEN * C_HEADS, C_HEAD_DIM)
        cache_v = cache_mem_v[l].reshape(n_s, MEM_LEN * C_HEADS, C_HEAD_DIM)
        y_s, cv, sh, wkv, conv = _decoder_layer(y_s, cache_k, cache_v, state_shift[l],
                                                state_wkv[l], state_conv[l], lw)
        for lst, val in zip(outs[6:], (cv, sh, wkv, conv)):
            lst.append(val)
    return (y_p, y_s) + tuple(jnp.stack(o) for o in outs)
```

```python
import functools
import math

import jax
import jax.numpy as jnp
from jax import lax
from jax.experimental import pallas as pl
from jax.experimental.pallas import tpu as pltpu

D_MODEL = 2048
MEM_LEN = 256
CHUNK = 128
A_HEADS, A_HEAD_DIM = 4, 128
A_WIDTH = A_HEADS * A_HEAD_DIM
B_HEADS, B_HEAD_DIM = 16, 64
B_WIDTH = B_HEADS * B_HEAD_DIM
LORA = 64
B_PROJ = 3 * B_WIDTH + 3 * LORA
C_HEADS, C_HEAD_DIM = 4, 128
C_WIDTH = C_HEADS * C_HEAD_DIM
D_FF = 5632
RMS_EPS = 1e-6
LN_EPS = 1e-5
GN_EPS = 64e-5
DECAY_OFFSET = 0.5

LANES = 128
SUBLANES = 8
VMEM_LIMIT_BYTES = 56 * 1024 * 1024

LORA_PAD = 256
COL_A = 0
COL_R = 2 * A_WIDTH
COL_Q = COL_R + 3 * B_WIDTH
COL_L = COL_Q + C_WIDTH
IN_COLS_PAD = 5120
PROJ_TN = 1024

WKV_CHUNK = 64
FFN_N_TILES = 2
FFN_TN = D_FF // FFN_N_TILES
FFN_SUB = 256

F32 = jnp.float32
BF16 = jnp.bfloat16


def _cparams(sem):
    return pltpu.CompilerParams(dimension_semantics=sem, vmem_limit_bytes=VMEM_LIMIT_BYTES)


def _rms(x, g):
    return x * lax.rsqrt(jnp.mean(x * x, axis=-1, keepdims=True) + RMS_EPS) * g


def _norm_matmul_kernel(x_ref, g_ref, w_ref, o_ref, h_ref, *, w_is_nk):
    @pl.when(pl.program_id(1) == 0)
    def _():
        h_ref[...] = _rms(x_ref[...], g_ref[...]).astype(BF16)

    dims = (((1,), (1,)), ((), ())) if w_is_nk else (((1,), (0,)), ((), ()))
    o_ref[...] = lax.dot_general(h_ref[...], w_ref[...], dims, preferred_element_type=F32)


def _norm_matmul(x, g, w, *, tm, tn, w_is_nk=False):
    t, k = x.shape
    n = w.shape[0] if w_is_nk else w.shape[1]
    w_spec = (pl.BlockSpec((tn, k), lambda i, j: (j, 0)) if w_is_nk
              else pl.BlockSpec((k, tn), lambda i, j: (0, j)))
    return pl.pallas_call(
        functools.partial(_norm_matmul_kernel, w_is_nk=w_is_nk),
        grid=(t // tm, n // tn),
        in_specs=[
            pl.BlockSpec((tm, k), lambda i, j: (i, 0)),
            pl.BlockSpec((1, k), lambda i, j: (0, 0)),
            w_spec,
        ],
        out_specs=pl.BlockSpec((tm, tn), lambda i, j: (i, j)),
        out_shape=jax.ShapeDtypeStruct((t, n), F32),
        scratch_shapes=[pltpu.VMEM((tm, k), BF16)],
        compiler_params=_cparams(("parallel", "arbitrary")),
        name="norm_matmul",
    )(x, g, w)


def _group_a_kernel(u_ref, v_ref, g_ref, b_ref, w_ref, bias_ref, o_ref, vout_ref):
    u = jax.nn.gelu(u_ref[...], approximate=True)
    v = jax.nn.gelu(v_ref[...], approximate=True)
    mean = jnp.mean(v, axis=-1, keepdims=True)
    d = v - mean
    var = jnp.mean(d * d, axis=-1, keepdims=True)
    vn = d * lax.rsqrt(var + LN_EPS) * g_ref[...] + b_ref[...]
    vout_ref[...] = vn
    vb = vn.astype(BF16)
    for c in range(u.shape[0] // CHUNK):
        rs = slice(c * CHUNK, (c + 1) * CHUNK)
        for h in range(A_HEADS):
            hs = slice(h * A_HEAD_DIM, (h + 1) * A_HEAD_DIM)
            mixed = jnp.dot(w_ref[h], vb[rs, hs], preferred_element_type=F32) + bias_ref[:, hs]
            o_ref[rs, hs] = (u[rs, hs] * mixed).astype(BF16)


def _group_a(proj, ln_g, ln_b, w_sp, bias_full, *, rows=4 * CHUNK):
    t = proj.shape[0]
    return pl.pallas_call(
        _group_a_kernel,
        grid=(t // rows,),
        in_specs=[
            pl.BlockSpec((rows, A_WIDTH), lambda i: (i, COL_A // A_WIDTH)),
            pl.BlockSpec((rows, A_WIDTH), lambda i: (i, COL_A // A_WIDTH + 1)),
            pl.BlockSpec((1, A_WIDTH), lambda i: (0, 0)),
            pl.BlockSpec((1, A_WIDTH), lambda i: (0, 0)),
            pl.BlockSpec((A_HEADS, CHUNK, CHUNK), lambda i: (0, 0, 0)),
            pl.BlockSpec((CHUNK, A_WIDTH), lambda i: (0, 0)),
        ],
        out_specs=[
            pl.BlockSpec((rows, A_WIDTH), lambda i: (i, 0)),
            pl.BlockSpec((rows, A_WIDTH), lambda i: (i, 0)),
        ],
        out_shape=[
            jax.ShapeDtypeStruct((t, A_WIDTH), BF16),
            jax.ShapeDtypeStruct((t, A_WIDTH), F32),
        ],
        compiler_params=_cparams(("parallel",)),
        name="group_a",
    )(proj, proj, ln_g, ln_b, w_sp, bias_full)


def _softmax_rows(s):
    e = jnp.exp(s - jnp.max(s, axis=-1, keepdims=True))
    return e / jnp.sum(e, axis=-1, keepdims=True)


def _attn_kernel(q_ref, k_ref, v_ref, o_ref, *, n_b, tq, head_major_rows):
    scale = C_HEAD_DIM ** -0.5
    heads = range(C_HEADS)
    lanes = lambda h: slice(h * C_HEAD_DIM, (h + 1) * C_HEAD_DIM)
    if not head_major_rows:
        chains = [(g, h) for g in range(n_b) for h in heads]
        rows = lambda g: slice(g * tq, (g + 1) * tq)
        s = [lax.dot_general(q_ref[rows(g), lanes(h)].astype(BF16), k_ref[g, :, lanes(h)].astype(BF16), _NT,
                             preferred_element_type=F32) * scale for g, h in chains]
        p = [_softmax_rows(m).astype(BF16) for m in s]
        o = [jnp.dot(m, v_ref[g, :, lanes(h)].astype(BF16), preferred_element_type=F32)
             for m, (g, h) in zip(p, chains)]
        for m, (g, h) in zip(o, chains):
            o_ref[rows(g), lanes(h)] = m.astype(BF16)
        return
    row = lax.broadcasted_iota(jnp.int32, (C_HEADS * tq, 1), 0)
    col = lax.broadcasted_iota(jnp.int32, (1, C_HEADS * MEM_LEN), 1)
    own = (col & (C_HEADS - 1)) == (row >> int(math.log2(tq)))
    q = [q_ref[g * tq:(g + 1) * tq, :] for g in range(n_b)]
    qs = [jnp.concatenate([m[:, lanes(h)] for h in heads], axis=0).astype(BF16) for m in q]
    s = [lax.dot_general(m, k_ref[g].astype(BF16), _NT, preferred_element_type=F32) * scale
         for g, m in enumerate(qs)]
    p = [_softmax_rows(jnp.where(own, m, -1e30)).astype(BF16) for m in s]
    o = [jnp.dot(m, v_ref[g].astype(BF16), preferred_element_type=F32) for g, m in enumerate(p)]
    for g, m in enumerate(o):
        for h in heads:
            o_ref[g * tq:(g + 1) * tq, lanes(h)] = m[h * tq:(h + 1) * tq].astype(BF16)


def _attention(proj, mem_k, mem_v, *, batch, seq):
    head_major_rows = mem_k.shape[-1] == C_HEAD_DIM
    if head_major_rows:
        tq, n_b = seq, 8
    else:
        tq, n_b = 512, 1
    n_q = seq // tq
    mem_blk = (n_b,) + mem_k.shape[1:]
    return pl.pallas_call(
        functools.partial(_attn_kernel, n_b=n_b, tq=tq, head_major_rows=head_major_rows),
        grid=(batch // n_b, n_q),
        in_specs=[
            pl.BlockSpec((n_b * tq, C_WIDTH), lambda b, i: (b * n_q + i, COL_Q // C_WIDTH)),
            pl.BlockSpec(mem_blk, lambda b, i: (b, 0, 0)),
            pl.BlockSpec(mem_blk, lambda b, i: (b, 0, 0)),
        ],
        out_specs=pl.BlockSpec((n_b * tq, C_WIDTH), lambda b, i: (b * n_q + i, 0)),
        out_shape=jax.ShapeDtypeStruct((batch * seq, C_WIDTH), BF16),
        compiler_params=_cparams(("parallel", "arbitrary")),
        name="mem_attention",
    )(proj, mem_k, mem_v)


_V_MU_R, _V_MU_K, _V_MU_V, _V_W0, _V_A0, _V_KK, _V_KA, _V_RK, _V_LNG, _V_LNB = range(10)


_NN = (((1,), (0,)), ((), ()))
_NT = (((1,), (1,)), ((), ()))
_TN = (((0,), (0,)), ((), ()))
HEAD_PAIRS = B_HEADS // 2


def _split(x):
    hi = x.astype(BF16)
    return hi, (x - hi.astype(F32)).astype(BF16)


def _b(x):
    return x.astype(BF16)


def _bdot(a, b, dims=_NN):
    return lax.dot_general(a, b, dims, preferred_element_type=F32)


def _rwkv_kernel(pr_ref, pk_ref, pv_ref, pl_ref, sr_ref, sk_ref, sv_ref, sl_ref, wkv_ref,
                 vec_ref, mul_ref, w2_ref, a2_ref, g2_ref, seg_ref,
                 o_ref, so_ref, cr_ref, ck_ref, cv_ref, cl_ref, sbd_ref, *, n_blk, groups, chunk, whole_seq):
    rows = groups * chunk
    first = pl.program_id(1) == 0
    last = pl.program_id(1) == pl.num_programs(1) - 1
    hd = B_HEAD_DIM
    n_seq = n_blk * groups

    @pl.when(first)
    def _():
        zero = jnp.zeros((hd, hd), F32)
        for g in range(n_seq):
            for q in range(HEAD_PAIRS):
                top = jnp.concatenate([wkv_ref[g, 2 * q], zero], axis=1)
                bot = jnp.concatenate([zero, wkv_ref[g, 2 * q + 1]], axis=1)
                sbd_ref[g, q] = jnp.concatenate([top, bot], axis=0)
        if not whole_seq:
            for s_ref, carry_ref in ((sr_ref, cr_ref), (sk_ref, ck_ref), (sv_ref, cv_ref), (sl_ref, cl_ref)):
                for blk in range(n_blk):
                    carry_ref[blk] = jnp.broadcast_to(s_ref[blk], (SUBLANES, s_ref.shape[-1]))

    row_id = lax.broadcasted_iota(jnp.int32, (rows, 1), 0)
    col_id = lax.broadcasted_iota(jnp.int32, (1, rows), 1)
    chunk_bits = int(math.log2(chunk))
    incl_b = (((row_id >> chunk_bits) == (col_id >> chunk_bits)) & (col_id <= row_id)).astype(BF16)
    vec = lambda i: vec_ref[i:i + 1, :]
    seg_ones = seg_ref[...]

    def seg_sum(x):
        slabs = jnp.concatenate([x[:, q * LANES:(q + 1) * LANES] for q in range(HEAD_PAIRS)], axis=0)
        hi, lo = _split(slabs)
        s = (jnp.dot(hi, seg_ones, preferred_element_type=F32)
             + jnp.dot(lo, seg_ones, preferred_element_type=F32))
        return jnp.concatenate([s[q * rows:(q + 1) * rows] for q in range(HEAD_PAIRS)], axis=1)

    def prologue(blk, out):
        seqs = slice(blk * groups, (blk + 1) * groups)

        def prev_rows(x, s_ref, carry_ref):
            width = x.shape[1]
            if whole_seq:
                start = jnp.broadcast_to(s_ref[seqs], (groups, chunk, width)).reshape(rows, width)
                return jnp.where((row_id & (chunk - 1)) == 0, start, pltpu.roll(x, 1, 0))

            ext = jnp.concatenate([carry_ref[blk], x], axis=0)
            prev = pltpu.roll(ext, 1, 0)[SUBLANES:]
            carry_ref[blk] = x[rows - SUBLANES:]
            return prev

        def shifted(p_ref, s_ref, carry_ref, mu):
            x = p_ref[blk]
            return x + (prev_rows(x, s_ref, carry_ref) - x) * mu

        lo = shifted(pl_ref, sl_ref, cl_ref, mul_ref[0:1, :])
        dw = jnp.dot(jnp.tanh(lo).astype(BF16), w2_ref[...], preferred_element_type=F32)
        da = jnp.dot(lo.astype(BF16), a2_ref[...], preferred_element_type=F32)
        gate = jnp.dot(jax.nn.sigmoid(lo).astype(BF16), g2_ref[...], preferred_element_type=F32)
        yield
        w_log = -jax.nn.softplus(-(vec(_V_W0) + dw)) - DECAY_OFFSET
        log_decay = -jnp.exp(w_log)
        ld_hi, ld_mid = _split(log_decay)
        ld_lo = (log_decay - ld_hi.astype(F32) - ld_mid.astype(F32)).astype(BF16)
        cum = (jnp.dot(incl_b, ld_hi, preferred_element_type=F32)
               + jnp.dot(incl_b, ld_mid, preferred_element_type=F32)
               + jnp.dot(incl_b, ld_lo, preferred_element_type=F32))
        yield
        a = jax.nn.sigmoid(vec(_V_A0) + da)
        k = shifted(pk_ref, sk_ref, ck_ref, vec(_V_MU_K))
        kk = k * vec(_V_KK)
        kk = kk / jnp.maximum(jnp.sqrt(seg_sum(kk * kk)), 1e-12)
        yield
        p_incl = jnp.exp(cum)
        p_inv = jnp.exp(-cum)
        a_t = -kk * jnp.exp(cum - log_decay)
        b_t = kk * a * p_inv
        yield
        k = k * (1.0 + (a - 1.0) * vec(_V_KA))
        k_t = k * p_inv
        r = shifted(pr_ref, sr_ref, cr_ref, vec(_V_MU_R))
        r_t = r * p_incl
        yield
        v = shifted(pv_ref, sv_ref, cv_ref, vec(_V_MU_V))
        bonus = seg_sum(r * k * vec(_V_RK)) * v
        out.update(a_t=a_t, b_t=b_t, k_t=k_t, r_t=r_t, v=v, p_incl=p_incl, bonus=bonus, gate=gate)

    pr = 2 * rows
    row2 = lax.broadcasted_iota(jnp.int32, (pr, 1), 0)
    col2 = lax.broadcasted_iota(jnp.int32, (1, pr), 1)
    t2, s2 = row2 & (rows - 1), col2 & (rows - 1)
    same2 = ((row2 >> chunk_bits) == (col2 >> chunk_bits))
    incl2 = same2 & (s2 <= t2)
    strict2 = same2 & (s2 < t2)
    eye2 = (row2 == col2).astype(F32)
    left = lax.broadcasted_iota(jnp.int32, (rows, LANES), 1) < hd

    def bd(x):
        zero = jnp.zeros_like(x)
        return jnp.concatenate([jnp.where(left, x, zero), jnp.where(left, zero, x)], axis=0)

    def group_rows(mats, g):
        starts = [hh * rows + g * chunk for hh in (0, 1)]
        return jnp.concatenate([m[i:i + chunk] for m in mats for i in starts], axis=0)

    n_sq = chunk_bits - 1
    col4 = lax.broadcasted_iota(jnp.int32, (1, 2 * pr), 1)
    incl4 = ((row2 >> chunk_bits) == ((col4 & (pr - 1)) >> chunk_bits)) & ((col4 & (rows - 1)) <= t2)
    pairs = range(HEAD_PAIRS)
    lanes_of = [slice(q * LANES, (q + 1) * LANES) for q in pairs]

    def recurrence(blk, pro):
        s0 = blk * groups
        p_incl = pro["p_incl"]
        bds = [[bd(pro[name][:, ls]) for name in ("a_t", "r_t", "b_t", "k_t", "v")] for ls in lanes_of]
        ar_s = [_b(jnp.concatenate([m[0], m[1]], axis=0)) for m in bds]
        bk_s = [_b(jnp.concatenate([m[2], m[3]], axis=0)) for m in bds]
        v_s = [_b(m[4]) for m in bds]
        ms = [_bdot(ar_s[q], bk_s[q], _NT) for q in pairs]
        yield
        a_ab = [jnp.where(strict2, m[:pr, :pr], 0.0) for m in ms]
        a_ak = [jnp.where(strict2, m[:pr, pr:], 0.0) for m in ms]
        a_r = [jnp.where(incl4, m[pr:], 0.0) for m in ms]
        inv = [eye2 + n for n in a_ab]
        pw_s = [_b(n) for n in a_ab]
        for _ in range(n_sq):
            pw_s = [_b(_bdot(p, p)) for p in pw_s]
            inv = [i + _bdot(p, _b(i)) for i, p in zip(inv, pw_s)]
            yield
        if groups == 1:
            xy = [_bdot(ar_s[q], _b(sbd_ref[s0, q]), _NT) for q in pairs]
            x0, y0 = [m[:pr] for m in xy], [m[pr:] for m in xy]
        else:
            x0, y0 = [], []
            for q in pairs:
                x_parts, y_parts = [None] * (2 * groups), [None] * (2 * groups)
                for g in range(groups):
                    xy = _bdot(_b(group_rows(bds[q][:2], g)), _b(sbd_ref[s0 + g, q]), _NT)
                    for hh in (0, 1):
                        x_parts[hh * groups + g] = xy[hh * chunk:(hh + 1) * chunk]
                        y_parts[hh * groups + g] = xy[(2 + hh) * chunk:(3 + hh) * chunk]
                x0.append(jnp.concatenate(x_parts, axis=0))
                y0.append(jnp.concatenate(y_parts, axis=0))
        yield
        rhs = [_b(x0[q] + _bdot(_b(a_ak[q]), v_s[q])) for q in pairs]
        u = [_bdot(_b(inv[q]), rhs[q]) for q in pairs]
        yield
        uv_s = [jnp.concatenate([_b(u[q]), v_s[q]], axis=0) for q in pairs]
        y2 = [y0[q] + _bdot(_b(a_r[q]), uv_s[q]) for q in pairs]
        pro["y"] = jnp.concatenate([m[:rows] + m[rows:] for m in y2], axis=1)
        yield
        for q in pairs:
            if groups == 1:
                ds = _bdot(uv_s[q], bk_s[q], _TN)
                sbd_ref[s0, q] = (sbd_ref[s0, q] + ds) * p_incl[rows - 1:rows, lanes_of[q]]
            else:
                for g in range(groups):
                    ds = _bdot(_b(group_rows((u[q], bds[q][4]), g)),
                               _b(group_rows((bds[q][2], bds[q][3]), g)), _TN)
                    end = (g + 1) * chunk - 1
                    sbd_ref[s0 + g, q] = (sbd_ref[s0 + g, q] + ds) * p_incl[end:end + 1, lanes_of[q]]

    def finish(blk, pro):
        inv_n = 1.0 / B_HEAD_DIM
        y = pro["y"]
        d = y - seg_sum(y) * inv_n
        yield
        var = seg_sum(d * d) * inv_n
        yield
        yn = d * lax.rsqrt(var + GN_EPS) * vec(_V_LNG) + vec(_V_LNB)
        o_ref[blk] = ((yn + pro["bonus"]) * pro["gate"]).astype(BF16)

    def interleave(*gens):
        live = list(gens)
        while live:
            for gen in list(live):
                if next(gen, StopIteration) is StopIteration:
                    live.remove(gen)

    pros = [dict() for _ in range(n_blk)]
    interleave(prologue(0, pros[0]))
    for blk in range(n_blk):
        work = [recurrence(blk, pros[blk])]
        if blk + 1 < n_blk:
            work.append(prologue(blk + 1, pros[blk + 1]))
        if blk > 0:
            work.append(finish(blk - 1, pros[blk - 1]))
        interleave(*work)
    interleave(finish(n_blk - 1, pros[n_blk - 1]))

    @pl.when(last)
    def _():
        for g in range(n_seq):
            for q in range(HEAD_PAIRS):
                sq = sbd_ref[g, q]
                so_ref[g, 2 * q] = sq[:hd, :hd]
                so_ref[g, 2 * q + 1] = sq[hd:, hd:]


def _rwkv(proj, shift_parts, wkv_prev, lw, *, batch, seq):
    whole_seq = seq <= WKV_CHUNK
    if whole_seq:
        assert seq == SUBLANES, "whole-sequence blocks rely on one sublane tile per sequence"
        chunk, groups = seq, WKV_CHUNK // seq
        n_blk, n_chunks = 2, 1
        n_outer = batch // (groups * n_blk)
        proj3 = proj.reshape(batch // groups, WKV_CHUNK, IN_COLS_PAD)
    else:
        chunk, groups = WKV_CHUNK, 1
        n_blk, n_outer, n_chunks = batch, 1, seq // WKV_CHUNK
        proj3 = proj.reshape(batch, seq, IN_COLS_PAD)
    n_seq = n_blk * groups
    act = lambda col_blk, width: pl.BlockSpec((n_blk, WKV_CHUNK, width), lambda o, c: (o, c, col_blk))
    per_seq = lambda *tail: pl.BlockSpec((n_seq,) + tail, lambda o, c: (o,) + (0,) * len(tail))
    fixed = lambda *shape: pl.BlockSpec(shape, lambda o, c: (0,) * len(shape))
    state = (B_HEADS, B_HEAD_DIM, B_HEAD_DIM)
    kern = functools.partial(_rwkv_kernel, n_blk=n_blk, groups=groups, chunk=chunk, whole_seq=whole_seq)
    b_out, wkv_new = pl.pallas_call(
        kern,
        grid=(n_outer, n_chunks),
        in_specs=[
            act(COL_R // B_WIDTH, B_WIDTH), act(COL_R // B_WIDTH + 1, B_WIDTH),
            act(COL_R // B_WIDTH + 2, B_WIDTH), act(COL_L // LORA_PAD, LORA_PAD),
            per_seq(1, B_WIDTH), per_seq(1, B_WIDTH), per_seq(1, B_WIDTH), per_seq(1, LORA_PAD),
            per_seq(*state),
            fixed(16, B_WIDTH), fixed(SUBLANES, LORA_PAD),
            fixed(LORA_PAD, B_WIDTH), fixed(LORA_PAD, B_WIDTH), fixed(LORA_PAD, B_WIDTH),
            fixed(LANES, LANES),
        ],
        out_specs=[act(0, B_WIDTH), per_seq(*state)],
        out_shape=[
            jax.ShapeDtypeStruct(proj3.shape[:2] + (B_WIDTH,), BF16),
            jax.ShapeDtypeStruct((batch,) + state, F32),
        ],
        scratch_shapes=[
            pltpu.VMEM((n_blk, SUBLANES, B_WIDTH), F32),
            pltpu.VMEM((n_blk, SUBLANES, B_WIDTH), F32),
            pltpu.VMEM((n_blk, SUBLANES, B_WIDTH), F32),
            pltpu.VMEM((n_blk, SUBLANES, LORA_PAD), F32),
            pltpu.VMEM((n_seq, HEAD_PAIRS, LANES, LANES), F32),
        ],
        compiler_params=_cparams(("parallel", "arbitrary")),
        name="rwkv7",
    )(proj3, proj3, proj3, proj3, *shift_parts, wkv_prev,
      lw["rk_vecs"], lw["rk_mu_l"], lw["rk_w2"], lw["rk_a2"], lw["rk_g2"], lw["seg_ones"])
    return b_out.reshape(batch * seq, B_WIDTH), wkv_new


def _out_proj_kernel(a_ref, b_ref, c_ref, w_ref, x_ref, gpost_ref, gffn_ref, x1_ref, h_ref):
    n_sub = 4
    sub = x_ref.shape[0] // n_sub
    halves = [slice(s * sub, (s + 1) * sub) for s in range(n_sub)]

    def project(rs):
        acc = jnp.dot(a_ref[rs, :], w_ref[0:A_WIDTH, :], preferred_element_type=F32)
        acc += jnp.dot(b_ref[rs, :], w_ref[A_WIDTH:A_WIDTH + B_WIDTH, :], preferred_element_type=F32)
        return acc + jnp.dot(c_ref[rs, :], w_ref[A_WIDTH + B_WIDTH:, :], preferred_element_type=F32)

    accs = [project(rs) for rs in halves]
    for rs, acc in zip(halves, accs):
        x1 = x_ref[rs, :] + _rms(acc, gpost_ref[...])
        x1_ref[rs, :] = x1
        h_ref[rs, :] = _rms(x1, gffn_ref[...]).astype(BF16)


def _out_proj(a_out, b_out, c_out, w_out, x, g_post, g_ffn, *, tm=512):
    t = x.shape[0]
    row = lambda i: (i, 0)
    fixed = lambda i: (0, 0)
    return pl.pallas_call(
        _out_proj_kernel,
        grid=(t // tm,),
        in_specs=[
            pl.BlockSpec((tm, A_WIDTH), row),
            pl.BlockSpec((tm, B_WIDTH), row),
            pl.BlockSpec((tm, C_WIDTH), row),
            pl.BlockSpec((D_MODEL, D_MODEL), fixed, pipeline_mode=pl.Buffered(1)),
            pl.BlockSpec((tm, D_MODEL), row),
            pl.BlockSpec((1, D_MODEL), fixed),
            pl.BlockSpec((1, D_MODEL), fixed),
        ],
        out_specs=[pl.BlockSpec((tm, D_MODEL), row), pl.BlockSpec((tm, D_MODEL), row)],
        out_shape=[jax.ShapeDtypeStruct((t, D_MODEL), F32), jax.ShapeDtypeStruct((t, D_MODEL), BF16)],
        compiler_params=_cparams(("parallel",)),
        name="out_proj",
    )(a_out, b_out, c_out, w_out, x, g_post, g_ffn)


def _ffn_up_kernel(h_ref, w_ref, cwg_ref, cwv_ref, cbg_ref, cbv_ref, pg_ref, pv_ref,
                   act_ref, ng_ref, nv_ref, *, tm, seq):
    h = h_ref[...]
    tn = cwg_ref.shape[1]
    n_seq = tm // seq
    tau = lax.broadcasted_iota(jnp.int32, (tm, 1), 0) & (seq - 1)

    def conv(up, cs, cw_ref, cb_ref, p_ref, n_ref):
        width = up.shape[1]
        prev = p_ref[:, :, cs]
        e0 = jnp.broadcast_to(prev[:, 0:1, :], (n_seq, seq, width)).reshape(tm, width)
        e1 = jnp.broadcast_to(prev[:, 1:2, :], (n_seq, seq, width)).reshape(tm, width)
        m1 = jnp.where(tau == 0, e1, pltpu.roll(up, 1, 0))
        m2 = jnp.where(tau == 0, e0, jnp.where(tau == 1, e1, pltpu.roll(up, 2, 0)))
        n_ref[:, :, cs] = up.reshape(n_seq, seq, width)[:, seq - 2:, :]
        return cb_ref[:, cs] + m2 * cw_ref[0:1, cs] + m1 * cw_ref[1:2, cs] + up * cw_ref[2:3, cs]

    n_sub = tn // FFN_SUB
    packed = lambda s: slice(2 * s * FFN_SUB, 2 * (s + 1) * FFN_SUB)
    ups = [jnp.dot(h, w_ref[:, packed(s)], preferred_element_type=F32) for s in range(n_sub)]
    for s, up in enumerate(ups):
        cs = slice(s * FFN_SUB, (s + 1) * FFN_SUB)
        gate = conv(up[:, :FFN_SUB], cs, cwg_ref, cbg_ref, pg_ref, ng_ref)
        val = conv(up[:, FFN_SUB:], cs, cwv_ref, cbv_ref, pv_ref, nv_ref)
        act_ref[:, cs] = (jax.nn.gelu(gate, approximate=True) * val).astype(BF16)


def _ffn_up_skew_kernel(h_ref, w_ref, cwg_ref, cwv_ref, cbg_ref, cbv_ref, pg_ref, pv_ref,
                        act_ref, ng_ref, nv_ref, cg_ref, cv_ref, u0_ref, u1_ref,
                        *, tm, tiles_per_seq):
    i = pl.program_id(1)
    tn = cwg_ref.shape[1]
    n_sub = tn // FFN_SUB
    packed = lambda s: slice(2 * s * FFN_SUB, 2 * (s + 1) * FFN_SUB)

    @pl.when(i == 0)
    def _():
        for ref in (u1_ref, cg_ref, cv_ref):
            ref[...] = jnp.zeros_like(ref)

    @pl.when(i % tiles_per_seq == 1 % tiles_per_seq)
    def _():
        for carry_ref, p_ref in ((cg_ref, pg_ref), (cv_ref, pv_ref)):
            carry_ref[...] = jnp.concatenate([jnp.zeros((SUBLANES - 2, tn), F32), p_ref[0]], axis=0)

    piece = 64

    def conv(u_ref, r0, us, cs, cw_ref, cb_ref, carry_ref):
        if r0 == 0:
            ext = jnp.concatenate([carry_ref[:, cs], u_ref[0:piece, us]], axis=0)
        else:
            ext = u_ref[r0 - SUBLANES:r0 + piece, us]
        m1 = pltpu.roll(ext, 1, 0)[SUBLANES:]
        m2 = pltpu.roll(ext, 2, 0)[SUBLANES:]
        return (cb_ref[:, cs] + m2 * cw_ref[0:1, cs] + m1 * cw_ref[1:2, cs]
                + ext[SUBLANES:] * cw_ref[2:3, cs])

    n_k = D_MODEL // FFN_SUB
    chunks_per_piece = n_k // (tm // piece)

    def step(new_ref, old_ref):
        for s in range(n_sub):
            cs = slice(s * FFN_SUB, (s + 1) * FFN_SUB)
            gs = slice(2 * s * FFN_SUB, (2 * s + 1) * FFN_SUB)
            vs = slice((2 * s + 1) * FFN_SUB, (2 * s + 2) * FFN_SUB)
            acc = None
            for kc in range(n_k):
                ks = slice(kc * FFN_SUB, (kc + 1) * FFN_SUB)
                part = jnp.dot(h_ref[:, ks], w_ref[ks, packed(s)], preferred_element_type=F32)
                acc = part if acc is None else acc + part
                if (kc + 1) % chunks_per_piece == 0:
                    r0 = (kc // chunks_per_piece) * piece
                    gate = conv(old_ref, r0, gs, cs, cwg_ref, cbg_ref, cg_ref)
                    val = conv(old_ref, r0, vs, cs, cwv_ref, cbv_ref, cv_ref)
                    act_ref[r0:r0 + piece, cs] = (jax.nn.gelu(gate, approximate=True) * val).astype(BF16)
            new_ref[:, packed(s)] = acc
            for us, carry_ref, n_ref in ((gs, cg_ref, ng_ref), (vs, cv_ref, nv_ref)):
                carry_ref[:, cs] = old_ref[tm - SUBLANES:, us]
                n_ref[0, :, cs] = old_ref[tm - 2:, us]

    @pl.when(i % 2 == 0)
    def _():
        step(u0_ref, u1_ref)

    @pl.when(i % 2 == 1)
    def _():
        step(u1_ref, u0_ref)


def _ffn_up_skew(h, w_up, conv_w, conv_b, conv_prev, *, batch, seq, tm=256):
    t = h.shape[0]
    tn, n_tiles = FFN_TN, FFN_N_TILES
    n_m = t // tm
    tiles_per_seq = seq // tm
    prev_tile = lambda i: jnp.maximum(i - 1, 0)
    col = lambda half: (lambda j, i: (0, j + half * n_tiles))
    state_blk = (1, 2, tn)
    state_idx = lambda half: (lambda j, i: (prev_tile(i) // tiles_per_seq, 0, j + half * n_tiles))
    state_out = jax.ShapeDtypeStruct((batch, 2, D_FF), F32)
    up_buf = pltpu.VMEM((tm, 2 * tn), F32)
    return pl.pallas_call(
        functools.partial(_ffn_up_skew_kernel, tm=tm, tiles_per_seq=tiles_per_seq),
        grid=(n_tiles, n_m + 1),
        in_specs=[
            pl.BlockSpec((tm, D_MODEL), lambda j, i: (jnp.minimum(i, n_m - 1), 0)),
            pl.BlockSpec((D_MODEL, 2 * tn), lambda j, i: (0, j), pipeline_mode=pl.Buffered(1)),
            pl.BlockSpec((3, tn), col(0)),
            pl.BlockSpec((3, tn), col(1)),
            pl.BlockSpec((1, tn), col(0)),
            pl.BlockSpec((1, tn), col(1)),
            pl.BlockSpec(state_blk, state_idx(0)),
            pl.BlockSpec(state_blk, state_idx(1)),
        ],
        out_specs=[
            pl.BlockSpec((tm, tn), lambda j, i: (prev_tile(i), j)),
            pl.BlockSpec(state_blk, state_idx(0)),
            pl.BlockSpec(state_blk, state_idx(0)),
        ],
        out_shape=[jax.ShapeDtypeStruct((t, D_FF), BF16), state_out, state_out],
        scratch_shapes=[pltpu.VMEM((SUBLANES, tn), F32), pltpu.VMEM((SUBLANES, tn), F32), up_buf, up_buf],
        compiler_params=_cparams(("parallel", "arbitrary")),
        name="ffn_up_conv",
    )(h, w_up, conv_w, conv_w, conv_b, conv_b, conv_prev, conv_prev)


def _ffn_up(h, w_up, conv_w, conv_b, conv_prev, *, batch, seq):
    t = h.shape[0]
    if seq > SUBLANES:
        return _ffn_up_skew(h, w_up, conv_w, conv_b, conv_prev, batch=batch, seq=seq)
    assert seq == SUBLANES, "whole-sequence tiles rely on one sublane tile per sequence"
    tm, tn = t, 2 * FFN_SUB
    n_tiles = D_FF // tn
    state_blk = (batch, 2, tn)
    state_idx = lambda half: (lambda j, i: (0, 0, j + half * n_tiles))
    col = lambda half: (lambda j, i: (0, j + half * n_tiles))
    kern = functools.partial(_ffn_up_kernel, tm=tm, seq=seq)
    state_out = jax.ShapeDtypeStruct((batch, 2, D_FF), F32)
    return pl.pallas_call(
        kern,
        grid=(n_tiles, t // tm),
        in_specs=[
            pl.BlockSpec((tm, D_MODEL), lambda j, i: (i, 0)),
            pl.BlockSpec((D_MODEL, 2 * tn), lambda j, i: (0, j)),
            pl.BlockSpec((3, tn), col(0)),
            pl.BlockSpec((3, tn), col(1)),
            pl.BlockSpec((1, tn), col(0)),
            pl.BlockSpec((1, tn), col(1)),
            pl.BlockSpec(state_blk, state_idx(0)),
            pl.BlockSpec(state_blk, state_idx(1)),
        ],
        out_specs=[
            pl.BlockSpec((tm, tn), lambda j, i: (i, j)),
            pl.BlockSpec(state_blk, state_idx(0)),
            pl.BlockSpec(state_blk, state_idx(0)),
        ],
        out_shape=[jax.ShapeDtypeStruct((t, D_FF), BF16), state_out, state_out],
        compiler_params=_cparams(("parallel", "arbitrary")),
        name="ffn_up_conv",
    )(h, w_up, conv_w, conv_w, conv_b, conv_b, conv_prev, conv_prev)


def _ffn_down_kernel(a_ref, w_ref, x_ref, g_ref, o_ref):
    acc = jnp.dot(a_ref[...], w_ref[...], preferred_element_type=F32)
    o_ref[...] = x_ref[...] + _rms(acc, g_ref[...])


def _ffn_down(act, w_down, x1, g_post, *, tm=256):
    t = x1.shape[0]
    return pl.pallas_call(
        _ffn_down_kernel,
        grid=(t // tm,),
        in_specs=[
            pl.BlockSpec((tm, D_FF), lambda i: (i, 0)),
            pl.BlockSpec((D_FF, D_MODEL), lambda i: (0, 0), pipeline_mode=pl.Buffered(1)),
            pl.BlockSpec((tm, D_MODEL), lambda i: (i, 0)),
            pl.BlockSpec((1, D_MODEL), lambda i: (0, 0)),
        ],
        out_specs=pl.BlockSpec((tm, D_MODEL), lambda i: (i, 0)),
        out_shape=jax.ShapeDtypeStruct((t, D_MODEL), F32),
        compiler_params=_cparams(("parallel",)),
        name="ffn_down",
    )(act, w_down, x1, g_post)


def _pad_rows(w, row0, total):
    return jnp.zeros((total, w.shape[1]), w.dtype).at[row0:row0 + w.shape[0]].set(w)


def _prepare_layer(p, l):
    w_in_t = p["w_in"][l].T
    b0 = 2 * A_WIDTH
    q0 = b0 + B_PROJ
    w_in_p = jnp.concatenate([
        w_in_t[:b0 + 3 * B_WIDTH],
        w_in_t[q0:],
        w_in_t[b0 + 3 * B_WIDTH:q0],
        jnp.zeros((IN_COLS_PAD - COL_L - 3 * LORA, D_MODEL), w_in_t.dtype),
    ], axis=0).astype(BF16)
    mu = p["rk_mu"][l]
    vec_rows = [mu[:B_WIDTH], mu[B_WIDTH:2 * B_WIDTH], mu[2 * B_WIDTH:3 * B_WIDTH], p["rk_w0"][l],
                p["rk_a0"][l], p["rk_kk"][l], p["rk_ka"][l], p["rk_rk"][l].reshape(B_WIDTH),
                p["rk_lnx_g"][l], p["rk_lnx_b"][l]]
    vecs = jnp.zeros((16, B_WIDTH), F32).at[:len(vec_rows)].set(jnp.stack(vec_rows))
    mu_l = jnp.zeros((SUBLANES, LORA_PAD), F32).at[0, :3 * LORA].set(mu[3 * B_WIDTH:])
    head_of = jnp.arange(LANES) // B_HEAD_DIM
    seg_ones = (head_of[:, None] == head_of[None, :]).astype(BF16)
    tril = jnp.tril(jnp.ones((CHUNK, CHUNK), bool))
    ws = jnp.where(tril[None], p["gm_ws"][l], 0.0)
    bs = p["gm_bs"][l]
    return {
        "g_mix_pre": p["norm_mix_pre"][l][None], "g_mix_post": p["norm_mix_post"][l][None],
        "g_ffn_pre": p["norm_ffn_pre"][l][None], "g_ffn_post": p["norm_ffn_post"][l][None],
        "g_mem": p["norm_mem"][l][None],
        "w_in": w_in_p,
        "w_out": p["w_out"][l].astype(BF16),
        "w_mkv": jnp.concatenate([p["w_mem_k"][l], p["w_mem_v"][l]], axis=1).astype(BF16),
        "gm_ln_g": p["gm_ln_g"][l][None], "gm_ln_b": p["gm_ln_b"][l][None],
        "gm_ws": ws, "gm_bs": bs,
        "rk_vecs": vecs, "rk_mu_l": mu_l,
        "rk_w2": _pad_rows(p["rk_w2"][l], 0, LORA_PAD).astype(BF16),
        "rk_a2": _pad_rows(p["rk_a2"][l], LORA, LORA_PAD).astype(BF16),
        "rk_g2": _pad_rows(p["rk_g2"][l], 2 * LORA, LORA_PAD).astype(BF16),
        "seg_ones": seg_ones,
        "w_up": jnp.stack([p["ffn_w_up"][l][:, :D_FF].reshape(D_MODEL, D_FF // FFN_SUB, FFN_SUB),
                           p["ffn_w_up"][l][:, D_FF:].reshape(D_MODEL, D_FF // FFN_SUB, FFN_SUB)],
                          axis=2).reshape(D_MODEL, 2 * D_FF).astype(BF16),
        "conv_w": p["ffn_conv_w"][l], "conv_b": p["ffn_conv_b"][l][None],
        "w_down": p["ffn_w_down"][l].astype(BF16),
    }


def _spatial_weights(lw, seq):
    ws, bs = lw["gm_ws"], lw["gm_bs"]
    if seq >= CHUNK:
        w_blk, b_rows = ws, bs
    else:
        reps = CHUNK // seq
        eye = jnp.eye(reps, dtype=ws.dtype)
        w_blk = jnp.einsum("ab,hts->hatbs", eye, ws[:, :seq, :seq]).reshape(A_HEADS, CHUNK, CHUNK)
        b_rows = jnp.tile(bs[:, :seq], (1, reps))
    bias_full = jnp.repeat(b_rows.T, A_HEAD_DIM, axis=1)
    return w_blk.astype(BF16), bias_full


def _decoder_layer(x, mem_k, mem_v, shift_prev, wkv_prev, conv_prev, lw):
    batch, seq, _ = x.shape
    x2 = x.reshape(batch * seq, D_MODEL)
    proj = _norm_matmul(x2, lw["g_mix_pre"], lw["w_in"], tm=1024, tn=PROJ_TN, w_is_nk=True)
    w_sp, bias_full = _spatial_weights(lw, seq)
    a_out, a_v = _group_a(proj, lw["gm_ln_g"], lw["gm_ln_b"], w_sp, bias_full)
    shift_parts = [
        shift_prev[:, None, :B_WIDTH], shift_prev[:, None, B_WIDTH:2 * B_WIDTH],
        shift_prev[:, None, 2 * B_WIDTH:3 * B_WIDTH],
        jnp.pad(shift_prev[:, None, 3 * B_WIDTH:], ((0, 0), (0, 0), (0, LORA_PAD - 3 * LORA))),
    ]
    b_out, wkv_new = _rwkv(proj, shift_parts, wkv_prev, lw, batch=batch, seq=seq)
    c_out = _attention(proj, mem_k, mem_v, batch=batch, seq=seq)
    x1, h = _out_proj(a_out, b_out, c_out, lw["w_out"], x2, lw["g_mix_post"], lw["g_ffn_pre"])
    act, conv_g, conv_v = _ffn_up(h, lw["w_up"], lw["conv_w"], lw["conv_b"], conv_prev, batch=batch, seq=seq)
    y = _ffn_down(act, lw["w_down"], x1, lw["g_ffn_post"])

    chunk_start = ((seq - 1) // CHUNK) * CHUNK
    chunk_v = a_v.reshape(batch, seq, A_WIDTH)[:, chunk_start:].reshape(batch, -1, A_HEADS, A_HEAD_DIM)
    last = proj.reshape(batch, seq, IN_COLS_PAD)[:, -1]
    shift_new = jnp.concatenate([last[:, COL_R:COL_R + 3 * B_WIDTH], last[:, COL_L:COL_L + 3 * LORA]], axis=1)
    conv_new = jnp.concatenate([conv_g, conv_v], axis=-1)
    return y.reshape(batch, seq, D_MODEL), chunk_v, shift_new, wkv_new, conv_new


def kernel(x_prompt, x_sample, mem_prompt, cache_mem_k, cache_mem_v, state_shift, state_wkv, state_conv,
           norm_mix_pre, norm_mix_post, norm_ffn_pre, norm_ffn_post, norm_mem, w_in, w_out, w_mem_k, w_mem_v,
           gm_ln_g, gm_ln_b, gm_ws, gm_bs, rk_mu, rk_w0, rk_w2, rk_a0, rk_a2, rk_g2, rk_kk, rk_ka, rk_rk,
           rk_lnx_g, rk_lnx_b, ffn_w_up, ffn_conv_w, ffn_conv_b, ffn_w_down):
    params = dict(
        norm_mix_pre=norm_mix_pre, norm_mix_post=norm_mix_post, norm_ffn_pre=norm_ffn_pre,
        norm_ffn_post=norm_ffn_post, norm_mem=norm_mem, w_in=w_in, w_out=w_out, w_mem_k=w_mem_k,
        w_mem_v=w_mem_v, gm_ln_g=gm_ln_g, gm_ln_b=gm_ln_b, gm_ws=gm_ws, gm_bs=gm_bs, rk_mu=rk_mu,
        rk_w0=rk_w0, rk_w2=rk_w2, rk_a0=rk_a0, rk_a2=rk_a2, rk_g2=rk_g2, rk_kk=rk_kk, rk_ka=rk_ka,
        rk_rk=rk_rk, rk_lnx_g=rk_lnx_g, rk_lnx_b=rk_lnx_b, ffn_w_up=ffn_w_up, ffn_conv_w=ffn_conv_w,
        ffn_conv_b=ffn_conv_b, ffn_w_down=ffn_w_down)
    depth = w_in.shape[0]
    bp = x_prompt.shape[0]
    y_p, y_s = x_prompt, x_sample
    outs = [[] for _ in range(10)]
    for l in range(depth):
        lw = _prepare_layer(params, l)
        mem2 = mem_prompt.reshape(bp * MEM_LEN, D_MODEL)
        mkv = _norm_matmul(mem2, lw["g_mem"], lw["w_mkv"], tm=512, tn=2 * C_WIDTH)
        mk = mkv[:, :C_WIDTH].reshape(bp, MEM_LEN, C_WIDTH)
        mv = mkv[:, C_WIDTH:].reshape(bp, MEM_LEN, C_WIDTH)
        zero_shift = jnp.zeros((bp, B_PROJ), x_prompt.dtype)
        zero_wkv = jnp.zeros((bp, B_HEADS, B_HEAD_DIM, B_HEAD_DIM), F32)
        zero_conv = jnp.zeros((bp, 2, 2 * D_FF), x_prompt.dtype)
        y_p, cv, sh, wkv, conv = _decoder_layer(y_p, mk, mv, zero_shift, zero_wkv, zero_conv, lw)
        mem_shape = (bp, MEM_LEN, C_HEADS, C_HEAD_DIM)
        for lst, val in zip(outs[:6], (mk.reshape(mem_shape), mv.reshape(mem_shape), cv, sh, wkv, conv)):
            lst.append(val)
        n_s = x_sample.shape[0]
        cache_k = cache_mem_k[l].reshape(n_s, MEM_LEN * C_HEADS, C_HEAD_DIM)
        cache_v = cache_mem_v[l].reshape(n_s, MEM_LEN * C_HEADS, C_HEAD_DIM)
        y_s, cv, sh, wkv, conv = _decoder_layer(y_s, cache_k, cache_v, state_shift[l],
                                                state_wkv[l], state_conv[l], lw)
        for lst, val in zip(outs[6:], (cv, sh, wkv, conv)):
            lst.append(val)
    return (y_p, y_s) + tuple(jnp.stack(o) for o in outs)
```

```python
import functools
import math

import jax
import jax.numpy as jnp
from jax import lax
from jax.experimental import pallas as pl
from jax.experimental.pallas import tpu as pltpu

D_MODEL = 2048
MEM_LEN = 256
CHUNK = 128
A_HEADS, A_HEAD_DIM = 4, 128
A_WIDTH = A_HEADS * A_HEAD_DIM
B_HEADS, B_HEAD_DIM = 16, 64
B_WIDTH = B_HEADS * B_HEAD_DIM
LORA = 64
B_PROJ = 3 * B_WIDTH + 3 * LORA
C_HEADS, C_HEAD_DIM = 4, 128
C_WIDTH = C_HEADS * C_HEAD_DIM
D_FF = 5632
RMS_EPS = 1e-6
LN_EPS = 1e-5
GN_EPS = 64e-5
DECAY_OFFSET = 0.5

LANES = 128
SUBLANES = 8
VMEM_LIMIT_BYTES = 56 * 1024 * 1024

LORA_PAD = 256
COL_A = 0
COL_R = 2 * A_WIDTH
COL_Q = COL_R + 3 * B_WIDTH
COL_L = COL_Q + C_WIDTH
IN_COLS_PAD = 5120
PROJ_TN = 1024

WKV_CHUNK = 64
FFN_N_TILES = 2
FFN_TN = D_FF // FFN_N_TILES
FFN_SUB = 256

F32 = jnp.float32
BF16 = jnp.bfloat16


def _cparams(sem):
    return pltpu.CompilerParams(dimension_semantics=sem, vmem_limit_bytes=VMEM_LIMIT_BYTES)


def _rms(x, g):
    return x * lax.rsqrt(jnp.mean(x * x, axis=-1, keepdims=True) + RMS_EPS) * g


def _norm_matmul_kernel(x_ref, g_ref, w_ref, o_ref, h_ref, *, w_is_nk):
    @pl.when(pl.program_id(1) == 0)
    def _():
        h_ref[...] = _rms(x_ref[...], g_ref[...]).astype(BF16)

    dims = (((1,), (1,)), ((), ())) if w_is_nk else (((1,), (0,)), ((), ()))
    o_ref[...] = lax.dot_general(h_ref[...], w_ref[...], dims, preferred_element_type=F32)


def _norm_matmul(x, g, w, *, tm, tn, w_is_nk=False):
    t, k = x.shape
    n = w.shape[0] if w_is_nk else w.shape[1]
    w_spec = (pl.BlockSpec((tn, k), lambda i, j: (j, 0)) if w_is_nk
              else pl.BlockSpec((k, tn), lambda i, j: (0, j)))
    return pl.pallas_call(
        functools.partial(_norm_matmul_kernel, w_is_nk=w_is_nk),
        grid=(t // tm, n // tn),
        in_specs=[
            pl.BlockSpec((tm, k), lambda i, j: (i, 0)),
            pl.BlockSpec((1, k), lambda i, j: (0, 0)),
            w_spec,
        ],
        out_specs=pl.BlockSpec((tm, tn), lambda i, j: (i, j)),
        out_shape=jax.ShapeDtypeStruct((t, n), F32),
        scratch_shapes=[pltpu.VMEM((tm, k), BF16)],
        compiler_params=_cparams(("parallel", "arbitrary")),
        name="norm_matmul",
    )(x, g, w)


def _group_a_kernel(u_ref, v_ref, g_ref, b_ref, w_ref, bias_ref, o_ref, vout_ref):
    u = jax.nn.gelu(u_ref[...], approximate=True)
    v = jax.nn.gelu(v_ref[...], approximate=True)
    mean = jnp.mean(v, axis=-1, keepdims=True)
    d = v - mean
    var = jnp.mean(d * d, axis=-1, keepdims=True)
    vn = d * lax.rsqrt(var + LN_EPS) * g_ref[...] + b_ref[...]
    vout_ref[...] = vn
    vb = vn.astype(BF16)
    for c in range(u.shape[0] // CHUNK):
        rs = slice(c * CHUNK, (c + 1) * CHUNK)
        for h in range(A_HEADS):
            hs = slice(h * A_HEAD_DIM, (h + 1) * A_HEAD_DIM)
            mixed = jnp.dot(w_ref[h], vb[rs, hs], preferred_element_type=F32) + bias_ref[:, hs]
            o_ref[rs, hs] = (u[rs, hs] * mixed).astype(BF16)


def _group_a(proj, ln_g, ln_b, w_sp, bias_full, *, rows=4 * CHUNK):
    t = proj.shape[0]
    return pl.pallas_call(
        _group_a_kernel,
        grid=(t // rows,),
        in_specs=[
            pl.BlockSpec((rows, A_WIDTH), lambda i: (i, COL_A // A_WIDTH)),
            pl.BlockSpec((rows, A_WIDTH), lambda i: (i, COL_A // A_WIDTH + 1)),
            pl.BlockSpec((1, A_WIDTH), lambda i: (0, 0)),
            pl.BlockSpec((1, A_WIDTH), lambda i: (0, 0)),
            pl.BlockSpec((A_HEADS, CHUNK, CHUNK), lambda i: (0, 0, 0)),
            pl.BlockSpec((CHUNK, A_WIDTH), lambda i: (0, 0)),
        ],
        out_specs=[
            pl.BlockSpec((rows, A_WIDTH), lambda i: (i, 0)),
            pl.BlockSpec((rows, A_WIDTH), lambda i: (i, 0)),
        ],
        out_shape=[
            jax.ShapeDtypeStruct((t, A_WIDTH), BF16),
            jax.ShapeDtypeStruct((t, A_WIDTH), F32),
        ],
        compiler_params=_cparams(("parallel",)),
        name="group_a",
    )(proj, proj, ln_g, ln_b, w_sp, bias_full)


def _softmax_rows(s):
    e = jnp.exp(s - jnp.max(s, axis=-1, keepdims=True))
    return e / jnp.sum(e, axis=-1, keepdims=True)


def _attn_kernel(q_ref, k_ref, v_ref, o_ref, *, n_b, tq, head_major_rows):
    scale = C_HEAD_DIM ** -0.5
    heads = range(C_HEADS)
    lanes = lambda h: slice(h * C_HEAD_DIM, (h + 1) * C_HEAD_DIM)
    if not head_major_rows:
        chains = [(g, h) for g in range(n_b) for h in heads]
        rows = lambda g: slice(g * tq, (g + 1) * tq)
        s = [lax.dot_general(q_ref[rows(g), lanes(h)].astype(BF16), k_ref[g, :, lanes(h)].astype(BF16), _NT,
                             preferred_element_type=F32) * scale for g, h in chains]
        p = [_softmax_rows(m).astype(BF16) for m in s]
        o = [jnp.dot(m, v_ref[g, :, lanes(h)].astype(BF16), preferred_element_type=F32)
             for m, (g, h) in zip(p, chains)]
        for m, (g, h) in zip(o, chains):
            o_ref[rows(g), lanes(h)] = m.astype(BF16)
        return
    row = lax.broadcasted_iota(jnp.int32, (C_HEADS * tq, 1), 0)
    col = lax.broadcasted_iota(jnp.int32, (1, C_HEADS * MEM_LEN), 1)
    own = (col & (C_HEADS - 1)) == (row >> int(math.log2(tq)))
    q = [q_ref[g * tq:(g + 1) * tq, :] for g in range(n_b)]
    qs = [jnp.concatenate([m[:, lanes(h)] for h in heads], axis=0).astype(BF16) for m in q]
    s = [lax.dot_general(m, k_ref[g].astype(BF16), _NT, preferred_element_type=F32) * scale
         for g, m in enumerate(qs)]
    p = [_softmax_rows(jnp.where(own, m, -1e30)).astype(BF16) for m in s]
    o = [jnp.dot(m, v_ref[g].astype(BF16), preferred_element_type=F32) for g, m in enumerate(p)]
    for g, m in enumerate(o):
        for h in heads:
            o_ref[g * tq:(g + 1) * tq, lanes(h)] = m[h * tq:(h + 1) * tq].astype(BF16)


def _attention(proj, mem_k, mem_v, *, batch, seq):
    head_major_rows = mem_k.shape[-1] == C_HEAD_DIM
    if head_major_rows:
        tq, n_b = seq, 8
    else:
        tq, n_b = 512, 1
    n_q = seq // tq
    mem_blk = (n_b,) + mem_k.shape[1:]
    return pl.pallas_call(
        functools.partial(_attn_kernel, n_b=n_b, tq=tq, head_major_rows=head_major_rows),
        grid=(batch // n_b, n_q),
        in_specs=[
            pl.BlockSpec((n_b * tq, C_WIDTH), lambda b, i: (b * n_q + i, COL_Q // C_WIDTH)),
            pl.BlockSpec(mem_blk, lambda b, i: (b, 0, 0)),
            pl.BlockSpec(mem_blk, lambda b, i: (b, 0, 0)),
        ],
        out_specs=pl.BlockSpec((n_b * tq, C_WIDTH), lambda b, i: (b * n_q + i, 0)),
        out_shape=jax.ShapeDtypeStruct((batch * seq, C_WIDTH), BF16),
        compiler_params=_cparams(("parallel", "arbitrary")),
        name="mem_attention",
    )(proj, mem_k, mem_v)


_V_MU_R, _V_MU_K, _V_MU_V, _V_W0, _V_A0, _V_KK, _V_KA, _V_RK, _V_LNG, _V_LNB = range(10)


_NN = (((1,), (0,)), ((), ()))
_NT = (((1,), (1,)), ((), ()))
_TN = (((0,), (0,)), ((), ()))
HEAD_PAIRS = B_HEADS // 2


def _split(x):
    hi = x.astype(BF16)
    return hi, (x - hi.astype(F32)).astype(BF16)


def _b(x):
    return x.astype(BF16)


def _bdot(a, b, dims=_NN):
    return lax.dot_general(a, b, dims, preferred_element_type=F32)


def _rwkv_kernel(pr_ref, pk_ref, pv_ref, pl_ref, sr_ref, sk_ref, sv_ref, sl_ref, wkv_ref,
                 vec_ref, mul_ref, w2_ref, a2_ref, g2_ref, seg_ref,
                 o_ref, so_ref, cr_ref, ck_ref, cv_ref, cl_ref, sbd_ref, *, n_blk, groups, chunk, whole_seq):
    rows = groups * chunk
    first = pl.program_id(1) == 0
    last = pl.program_id(1) == pl.num_programs(1) - 1
    hd = B_HEAD_DIM
    n_seq = n_blk * groups

    @pl.when(first)
    def _():
        zero = jnp.zeros((hd, hd), F32)
        for g in range(n_seq):
            for q in range(HEAD_PAIRS):
                top = jnp.concatenate([wkv_ref[g, 2 * q], zero], axis=1)
                bot = jnp.concatenate([zero, wkv_ref[g, 2 * q + 1]], axis=1)
                sbd_ref[g, q] = jnp.concatenate([top, bot], axis=0)
        if not whole_seq:
            for s_ref, carry_ref in ((sr_ref, cr_ref), (sk_ref, ck_ref), (sv_ref, cv_ref), (sl_ref, cl_ref)):
                for blk in range(n_blk):
                    carry_ref[blk] = jnp.broadcast_to(s_ref[blk], (SUBLANES, s_ref.shape[-1]))

    row_id = lax.broadcasted_iota(jnp.int32, (rows, 1), 0)
    col_id = lax.broadcasted_iota(jnp.int32, (1, rows), 1)
    chunk_bits = int(math.log2(chunk))
    incl_b = (((row_id >> chunk_bits) == (col_id >> chunk_bits)) & (col_id <= row_id)).astype(BF16)
    vec = lambda i: vec_ref[i:i + 1, :]
    seg_ones = seg_ref[...]

    def seg_sum(x):
        slabs = jnp.concatenate([x[:, q * LANES:(q + 1) * LANES] for q in range(HEAD_PAIRS)], axis=0)
        hi, lo = _split(slabs)
        s = (jnp.dot(hi, seg_ones, preferred_element_type=F32)
             + jnp.dot(lo, seg_ones, preferred_element_type=F32))
        return jnp.concatenate([s[q * rows:(q + 1) * rows] for q in range(HEAD_PAIRS)], axis=1)

    def prologue(blk, out):
        seqs = slice(blk * groups, (blk + 1) * groups)

        def prev_rows(x, s_ref, carry_ref):
            width = x.shape[1]
            if whole_seq:
                start = jnp.broadcast_to(s_ref[seqs], (groups, chunk, width)).reshape(rows, width)
                return jnp.where((row_id & (chunk - 1)) == 0, start, pltpu.roll(x, 1, 0))

            ext = jnp.concatenate([carry_ref[blk], x], axis=0)
            prev = pltpu.roll(ext, 1, 0)[SUBLANES:]
            carry_ref[blk] = x[rows - SUBLANES:]
            return prev

        def shifted(p_ref, s_ref, carry_ref, mu):
            x = p_ref[blk]
            return x + (prev_rows(x, s_ref, carry_ref) - x) * mu

        lo = shifted(pl_ref, sl_ref, cl_ref, mul_ref[0:1, :])
        dw = jnp.dot(jnp.tanh(lo).astype(BF16), w2_ref[...], preferred_element_type=F32)
        da = jnp.dot(lo.astype(BF16), a2_ref[...], preferred_element_type=F32)
        gate = jnp.dot(jax.nn.sigmoid(lo).astype(BF16), g2_ref[...], preferred_element_type=F32)
        yield
        w_log = -jax.nn.softplus(-(vec(_V_W0) + dw)) - DECAY_OFFSET
        log_decay = -jnp.exp(w_log)
        ld_hi, ld_mid = _split(log_decay)
        ld_lo = (log_decay - ld_hi.astype(F32) - ld_mid.astype(F32)).astype(BF16)
        cum = (jnp.dot(incl_b, ld_hi, preferred_element_type=F32)
               + jnp.dot(incl_b, ld_mid, preferred_element_type=F32)
               + jnp.dot(incl_b, ld_lo, preferred_element_type=F32))
        yield
        a = jax.nn.sigmoid(vec(_V_A0) + da)
        k = shifted(pk_ref, sk_ref, ck_ref, vec(_V_MU_K))
        kk = k * vec(_V_KK)
        kk = kk / jnp.maximum(jnp.sqrt(seg_sum(kk * kk)), 1e-12)
        yield
        p_incl = jnp.exp(cum)
        p_inv = jnp.exp(-cum)
        a_t = -kk * jnp.exp(cum - log_decay)
        b_t = kk * a * p_inv
        yield
        k = k * (1.0 + (a - 1.0) * vec(_V_KA))
        k_t = k * p_inv
        r = shifted(pr_ref, sr_ref, cr_ref, vec(_V_MU_R))
        r_t = r * p_incl
        yield
        v = shifted(pv_ref, sv_ref, cv_ref, vec(_V_MU_V))
        bonus = seg_sum(r * k * vec(_V_RK)) * v
        out.update(a_t=a_t, b_t=b_t, k_t=k_t, r_t=r_t, v=v, p_incl=p_incl, bonus=bonus, gate=gate)

    pr = 2 * rows
    row2 = lax.broadcasted_iota(jnp.int32, (pr, 1), 0)
    col2 = lax.broadcasted_iota(jnp.int32, (1, pr), 1)
    t2, s2 = row2 & (rows - 1), col2 & (rows - 1)
    same2 = ((row2 >> chunk_bits) == (col2 >> chunk_bits))
    incl2 = same2 & (s2 <= t2)
    strict2 = same2 & (s2 < t2)
    eye2 = (row2 == col2).astype(F32)
    left = lax.broadcasted_iota(jnp.int32, (rows, LANES), 1) < hd

    def bd(x):
        zero = jnp.zeros_like(x)
        return jnp.concatenate([jnp.where(left, x, zero), jnp.where(left, zero, x)], axis=0)

    def group_rows(mats, g):
        starts = [hh * rows + g * chunk for hh in (0, 1)]
        return jnp.concatenate([m[i:i + chunk] for m in mats for i in starts], axis=0)

    n_sq = chunk_bits - 1
    col4 = lax.broadcasted_iota(jnp.int32, (1, 2 * pr), 1)
    incl4 = ((row2 >> chunk_bits) == ((col4 & (pr - 1)) >> chunk_bits)) & ((col4 & (rows - 1)) <= t2)
    pairs = range(HEAD_PAIRS)
    lanes_of = [slice(q * LANES, (q + 1) * LANES) for q in pairs]

    def recurrence(blk, pro):
        s0 = blk * groups
        p_incl = pro["p_incl"]
        bds = [[bd(pro[name][:, ls]) for name in ("a_t", "r_t", "b_t", "k_t", "v")] for ls in lanes_of]
        ar_s = [_b(jnp.concatenate([m[0], m[1]], axis=0)) for m in bds]
        bk_s = [_b(jnp.concatenate([m[2], m[3]], axis=0)) for m in bds]
        v_s = [_b(m[4]) for m in bds]
        if groups == 1:
            ms = [_bdot(ar_s[q], jnp.concatenate([bk_s[q], _b(sbd_ref[s0, q])], axis=0), _NT) for q in pairs]
        else:
            ms = [_bdot(ar_s[q], bk_s[q], _NT) for q in pairs]
        yield
        a_ab = [jnp.where(strict2, m[:pr, :pr], 0.0) for m in ms]
        a_ak = [jnp.where(strict2, m[:pr, pr:2 * pr], 0.0) for m in ms]
        a_r = [jnp.where(incl4, m[pr:, :2 * pr], 0.0) for m in ms]
        inv = [eye2 + n for n in a_ab]
        pw_s = [_b(_bdot(_b(n), _b(n))) for n in a_ab]
        yield
        for _ in range(1, n_sq):
            both = [_bdot(p, jnp.concatenate([p, _b(i)], axis=1)) for p, i in zip(pw_s, inv)]
            inv = [i + m[:, pr:] for i, m in zip(inv, both)]
            pw_s = [_b(m[:, :pr]) for m in both]
            yield
        inv = [i + _bdot(p, _b(i)) for i, p in zip(inv, pw_s)]
        yield
        if groups == 1:
            x0, y0 = [m[:pr, 2 * pr:] for m in ms], [m[pr:, 2 * pr:] for m in ms]
        else:
            x0, y0 = [], []
            for q in pairs:
                x_parts, y_parts = [None] * (2 * groups), [None] * (2 * groups)
                for g in range(groups):
                    xy = _bdot(_b(group_rows(bds[q][:2], g)), _b(sbd_ref[s0 + g, q]), _NT)
                    for hh in (0, 1):
                        x_parts[hh * groups + g] = xy[hh * chunk:(hh + 1) * chunk]
                        y_parts[hh * groups + g] = xy[(2 + hh) * chunk:(3 + hh) * chunk]
                x0.append(jnp.concatenate(x_parts, axis=0))
                y0.append(jnp.concatenate(y_parts, axis=0))
        yield
        rhs = [_b(x0[q] + _bdot(_b(a_ak[q]), v_s[q])) for q in pairs]
        u = [_bdot(_b(inv[q]), rhs[q]) for q in pairs]
        yield
        uv_s = [jnp.concatenate([_b(u[q]), v_s[q]], axis=0) for q in pairs]
        y2 = [y0[q] + _bdot(_b(a_r[q]), uv_s[q]) for q in pairs]
        pro["y"] = jnp.concatenate([m[:rows] + m[rows:] for m in y2], axis=1)
        yield
        for q in pairs:
            if groups == 1:
                ds = _bdot(uv_s[q], bk_s[q], _TN)
                sbd_ref[s0, q] = (sbd_ref[s0, q] + ds) * p_incl[rows - 1:rows, lanes_of[q]]
            else:
                for g in range(groups):
                    ds = _bdot(_b(group_rows((u[q], bds[q][4]), g)),
                               _b(group_rows((bds[q][2], bds[q][3]), g)), _TN)
                    end = (g + 1) * chunk - 1
                    sbd_ref[s0 + g, q] = (sbd_ref[s0 + g, q] + ds) * p_incl[end:end + 1, lanes_of[q]]

    def finish(blk, pro):
        inv_n = 1.0 / B_HEAD_DIM
        y = pro["y"]
        d = y - seg_sum(y) * inv_n
        yield
        var = seg_sum(d * d) * inv_n
        yield
        yn = d * lax.rsqrt(var + GN_EPS) * vec(_V_LNG) + vec(_V_LNB)
        o_ref[blk] = ((yn + pro["bonus"]) * pro["gate"]).astype(BF16)

    def interleave(*gens):
        live = list(gens)
        while live:
            for gen in list(live):
                if next(gen, StopIteration) is StopIteration:
                    live.remove(gen)

    pros = [dict() for _ in range(n_blk)]
    interleave(prologue(0, pros[0]))
    for blk in range(n_blk):
        work = [recurrence(blk, pros[blk])]
        if blk + 1 < n_blk:
            work.append(prologue(blk + 1, pros[blk + 1]))
        if blk > 0:
            work.append(finish(blk - 1, pros[blk - 1]))
        interleave(*work)
    interleave(finish(n_blk - 1, pros[n_blk - 1]))

    @pl.when(last)
    def _():
        for g in range(n_seq):
            for q in range(HEAD_PAIRS):
                sq = sbd_ref[g, q]
                so_ref[g, 2 * q] = sq[:hd, :hd]
                so_ref[g, 2 * q + 1] = sq[hd:, hd:]


def _rwkv(proj, shift_parts, wkv_prev, lw, *, batch, seq):
    whole_seq = seq <= WKV_CHUNK
    if whole_seq:
        assert seq == SUBLANES, "whole-sequence blocks rely on one sublane tile per sequence"
        chunk, groups = seq, WKV_CHUNK // seq
        n_blk, n_chunks = 2, 1
        n_outer = batch // (groups * n_blk)
        proj3 = proj.reshape(batch // groups, WKV_CHUNK, IN_COLS_PAD)
    else:
        chunk, groups = WKV_CHUNK, 1
        n_blk, n_outer, n_chunks = batch, 1, seq // WKV_CHUNK
        proj3 = proj.reshape(batch, seq, IN_COLS_PAD)
    n_seq = n_blk * groups
    act = lambda col_blk, width: pl.BlockSpec((n_blk, WKV_CHUNK, width), lambda o, c: (o, c, col_blk))
    per_seq = lambda *tail: pl.BlockSpec((n_seq,) + tail, lambda o, c: (o,) + (0,) * len(tail))
    fixed = lambda *shape: pl.BlockSpec(shape, lambda o, c: (0,) * len(shape))
    state = (B_HEADS, B_HEAD_DIM, B_HEAD_DIM)
    kern = functools.partial(_rwkv_kernel, n_blk=n_blk, groups=groups, chunk=chunk, whole_seq=whole_seq)
    b_out, wkv_new = pl.pallas_call(
        kern,
        grid=(n_outer, n_chunks),
        in_specs=[
            act(COL_R // B_WIDTH, B_WIDTH), act(COL_R // B_WIDTH + 1, B_WIDTH),
            act(COL_R // B_WIDTH + 2, B_WIDTH), act(COL_L // LORA_PAD, LORA_PAD),
            per_seq(1, B_WIDTH), per_seq(1, B_WIDTH), per_seq(1, B_WIDTH), per_seq(1, LORA_PAD),
            per_seq(*state),
            fixed(16, B_WIDTH), fixed(SUBLANES, LORA_PAD),
            fixed(LORA_PAD, B_WIDTH), fixed(LORA_PAD, B_WIDTH), fixed(LORA_PAD, B_WIDTH),
            fixed(LANES, LANES),
        ],
        out_specs=[act(0, B_WIDTH), per_seq(*state)],
        out_shape=[
            jax.ShapeDtypeStruct(proj3.shape[:2] + (B_WIDTH,), BF16),
            jax.ShapeDtypeStruct((batch,) + state, F32),
        ],
        scratch_shapes=[
            pltpu.VMEM((n_blk, SUBLANES, B_WIDTH), F32),
            pltpu.VMEM((n_blk, SUBLANES, B_WIDTH), F32),
            pltpu.VMEM((n_blk, SUBLANES, B_WIDTH), F32),
            pltpu.VMEM((n_blk, SUBLANES, LORA_PAD), F32),
            pltpu.VMEM((n_seq, HEAD_PAIRS, LANES, LANES), F32),
        ],
        compiler_params=_cparams(("parallel", "arbitrary")),
        name="rwkv7",
    )(proj3, proj3, proj3, proj3, *shift_parts, wkv_prev,
      lw["rk_vecs"], lw["rk_mu_l"], lw["rk_w2"], lw["rk_a2"], lw["rk_g2"], lw["seg_ones"])
    return b_out.reshape(batch * seq, B_WIDTH), wkv_new


def _out_proj_kernel(a_ref, b_ref, c_ref, w_ref, x_ref, gpost_ref, gffn_ref, x1_ref, h_ref):
    n_sub = 4
    sub = x_ref.shape[0] // n_sub
    halves = [slice(s * sub, (s + 1) * sub) for s in range(n_sub)]

    def project(rs):
        acc = jnp.dot(a_ref[rs, :], w_ref[0:A_WIDTH, :], preferred_element_type=F32)
        acc += jnp.dot(b_ref[rs, :], w_ref[A_WIDTH:A_WIDTH + B_WIDTH, :], preferred_element_type=F32)
        return acc + jnp.dot(c_ref[rs, :], w_ref[A_WIDTH + B_WIDTH:, :], preferred_element_type=F32)

    accs = [project(rs) for rs in halves]
    for rs, acc in zip(halves, accs):
        x1 = x_ref[rs, :] + _rms(acc, gpost_ref[...])
        x1_ref[rs, :] = x1
        h_ref[rs, :] = _rms(x1, gffn_ref[...]).astype(BF16)


def _out_proj(a_out, b_out, c_out, w_out, x, g_post, g_ffn, *, tm=512):
    t = x.shape[0]
    row = lambda i: (i, 0)
    fixed = lambda i: (0, 0)
    return pl.pallas_call(
        _out_proj_kernel,
        grid=(t // tm,),
        in_specs=[
            pl.BlockSpec((tm, A_WIDTH), row),
            pl.BlockSpec((tm, B_WIDTH), row),
            pl.BlockSpec((tm, C_WIDTH), row),
            pl.BlockSpec((D_MODEL, D_MODEL), fixed, pipeline_mode=pl.Buffered(1)),
            pl.BlockSpec((tm, D_MODEL), row),
            pl.BlockSpec((1, D_MODEL), fixed),
            pl.BlockSpec((1, D_MODEL), fixed),
        ],
        out_specs=[pl.BlockSpec((tm, D_MODEL), row), pl.BlockSpec((tm, D_MODEL), row)],
        out_shape=[jax.ShapeDtypeStruct((t, D_MODEL), F32), jax.ShapeDtypeStruct((t, D_MODEL), BF16)],
        compiler_params=_cparams(("parallel",)),
        name="out_proj",
    )(a_out, b_out, c_out, w_out, x, g_post, g_ffn)


def _ffn_up_kernel(h_ref, w_ref, cwg_ref, cwv_ref, cbg_ref, cbv_ref, pg_ref, pv_ref,
                   act_ref, ng_ref, nv_ref, *, tm, seq):
    h = h_ref[...]
    tn = cwg_ref.shape[1]
    n_seq = tm // seq
    tau = lax.broadcasted_iota(jnp.int32, (tm, 1), 0) & (seq - 1)

    def conv(up, cs, cw_ref, cb_ref, p_ref, n_ref):
        width = up.shape[1]
        prev = p_ref[:, :, cs]
        e0 = jnp.broadcast_to(prev[:, 0:1, :], (n_seq, seq, width)).reshape(tm, width)
        e1 = jnp.broadcast_to(prev[:, 1:2, :], (n_seq, seq, width)).reshape(tm, width)
        m1 = jnp.where(tau == 0, e1, pltpu.roll(up, 1, 0))
        m2 = jnp.where(tau == 0, e0, jnp.where(tau == 1, e1, pltpu.roll(up, 2, 0)))
        n_ref[:, :, cs] = up.reshape(n_seq, seq, width)[:, seq - 2:, :]
        return cb_ref[:, cs] + m2 * cw_ref[0:1, cs] + m1 * cw_ref[1:2, cs] + up * cw_ref[2:3, cs]

    n_sub = tn // FFN_SUB
    packed = lambda s: slice(2 * s * FFN_SUB, 2 * (s + 1) * FFN_SUB)
    ups = [jnp.dot(h, w_ref[:, packed(s)], preferred_element_type=F32) for s in range(n_sub)]
    for s, up in enumerate(ups):
        cs = slice(s * FFN_SUB, (s + 1) * FFN_SUB)
        gate = conv(up[:, :FFN_SUB], cs, cwg_ref, cbg_ref, pg_ref, ng_ref)
        val = conv(up[:, FFN_SUB:], cs, cwv_ref, cbv_ref, pv_ref, nv_ref)
        act_ref[:, cs] = (jax.nn.gelu(gate, approximate=True) * val).astype(BF16)


def _ffn_up_skew_kernel(h_ref, w_ref, cwg_ref, cwv_ref, cbg_ref, cbv_ref, pg_ref, pv_ref,
                        act_ref, ng_ref, nv_ref, cg_ref, cv_ref, u0_ref, u1_ref,
                        *, tm, tiles_per_seq):
    i = pl.program_id(1)
    tn = cwg_ref.shape[1]
    n_sub = tn // FFN_SUB
    packed = lambda s: slice(2 * s * FFN_SUB, 2 * (s + 1) * FFN_SUB)

    @pl.when(i == 0)
    def _():
        for ref in (u1_ref, cg_ref, cv_ref):
            ref[...] = jnp.zeros_like(ref)

    @pl.when(i % tiles_per_seq == 1 % tiles_per_seq)
    def _():
        for carry_ref, p_ref in ((cg_ref, pg_ref), (cv_ref, pv_ref)):
            carry_ref[...] = jnp.concatenate([jnp.zeros((SUBLANES - 2, tn), F32), p_ref[0]], axis=0)

    piece = 64

    def conv(u_ref, r0, us, cs, cw_ref, cb_ref, carry_ref):
        if r0 == 0:
            ext = jnp.concatenate([carry_ref[:, cs], u_ref[0:piece, us]], axis=0)
        else:
            ext = u_ref[r0 - SUBLANES:r0 + piece, us]
        m1 = pltpu.roll(ext, 1, 0)[SUBLANES:]
        m2 = pltpu.roll(ext, 2, 0)[SUBLANES:]
        return (cb_ref[:, cs] + m2 * cw_ref[0:1, cs] + m1 * cw_ref[1:2, cs]
                + ext[SUBLANES:] * cw_ref[2:3, cs])

    n_k = D_MODEL // FFN_SUB
    chunks_per_piece = n_k // (tm // piece)

    def step(new_ref, old_ref):
        for s in range(n_sub):
            cs = slice(s * FFN_SUB, (s + 1) * FFN_SUB)
            gs = slice(2 * s * FFN_SUB, (2 * s + 1) * FFN_SUB)
            vs = slice((2 * s + 1) * FFN_SUB, (2 * s + 2) * FFN_SUB)
            acc = None
            for kc in range(n_k):
                ks = slice(kc * FFN_SUB, (kc + 1) * FFN_SUB)
                part = jnp.dot(h_ref[:, ks], w_ref[ks, packed(s)], preferred_element_type=F32)
                acc = part if acc is None else acc + part
                if (kc + 1) % chunks_per_piece == 0:
                    r0 = (kc // chunks_per_piece) * piece
                    gate = conv(old_ref, r0, gs, cs, cwg_ref, cbg_ref, cg_ref)
                    val = conv(old_ref, r0, vs, cs, cwv_ref, cbv_ref, cv_ref)
                    act_ref[r0:r0 + piece, cs] = (jax.nn.gelu(gate, approximate=True) * val).astype(BF16)
            new_ref[:, packed(s)] = acc
            for us, carry_ref, n_ref in ((gs, cg_ref, ng_ref), (vs, cv_ref, nv_ref)):
                carry_ref[:, cs] = old_ref[tm - SUBLANES:, us]
                n_ref[0, :, cs] = old_ref[tm - 2:, us]

    @pl.when(i % 2 == 0)
    def _():
        step(u0_ref, u1_ref)

    @pl.when(i % 2 == 1)
    def _():
        step(u1_ref, u0_ref)


def _ffn_up_skew(h, w_up, conv_w, conv_b, conv_prev, *, batch, seq, tm=256):
    t = h.shape[0]
    tn, n_tiles = FFN_TN, FFN_N_TILES
    n_m = t // tm
    tiles_per_seq = seq // tm
    prev_tile = lambda i: jnp.maximum(i - 1, 0)
    col = lambda half: (lambda j, i: (0, j + half * n_tiles))
    state_blk = (1, 2, tn)
    state_idx = lambda half: (lambda j, i: (prev_tile(i) // tiles_per_seq, 0, j + half * n_tiles))
    state_out = jax.ShapeDtypeStruct((batch, 2, D_FF), F32)
    up_buf = pltpu.VMEM((tm, 2 * tn), F32)
    return pl.pallas_call(
        functools.partial(_ffn_up_skew_kernel, tm=tm, tiles_per_seq=tiles_per_seq),
        grid=(n_tiles, n_m + 1),
        in_specs=[
            pl.BlockSpec((tm, D_MODEL), lambda j, i: (jnp.minimum(i, n_m - 1), 0)),
            pl.BlockSpec((D_MODEL, 2 * tn), lambda j, i: (0, j), pipeline_mode=pl.Buffered(1)),
            pl.BlockSpec((3, tn), col(0)),
            pl.BlockSpec((3, tn), col(1)),
            pl.BlockSpec((1, tn), col(0)),
            pl.BlockSpec((1, tn), col(1)),
            pl.BlockSpec(state_blk, state_idx(0)),
            pl.BlockSpec(state_blk, state_idx(1)),
        ],
        out_specs=[
            pl.BlockSpec((tm, tn), lambda j, i: (prev_tile(i), j)),
            pl.BlockSpec(state_blk, state_idx(0)),
            pl.BlockSpec(state_blk, state_idx(0)),
        ],
        out_shape=[jax.ShapeDtypeStruct((t, D_FF), BF16), state_out, state_out],
        scratch_shapes=[pltpu.VMEM((SUBLANES, tn), F32), pltpu.VMEM((SUBLANES, tn), F32), up_buf, up_buf],
        compiler_params=_cparams(("parallel", "arbitrary")),
        name="ffn_up_conv",
    )(h, w_up, conv_w, conv_w, conv_b, conv_b, conv_prev, conv_prev)


def _ffn_up(h, w_up, conv_w, conv_b, conv_prev, *, batch, seq):
    t = h.shape[0]
    if seq > SUBLANES:
        return _ffn_up_skew(h, w_up, conv_w, conv_b, conv_prev, batch=batch, seq=seq)
    assert seq == SUBLANES, "whole-sequence tiles rely on one sublane tile per sequence"
    tm, tn = t, 2 * FFN_SUB
    n_tiles = D_FF // tn
    state_blk = (batch, 2, tn)
    state_idx = lambda half: (lambda j, i: (0, 0, j + half * n_tiles))
    col = lambda half: (lambda j, i: (0, j + half * n_tiles))
    kern = functools.partial(_ffn_up_kernel, tm=tm, seq=seq)
    state_out = jax.ShapeDtypeStruct((batch, 2, D_FF), F32)
    return pl.pallas_call(
        kern,
        grid=(n_tiles, t // tm),
        in_specs=[
            pl.BlockSpec((tm, D_MODEL), lambda j, i: (i, 0)),
            pl.BlockSpec((D_MODEL, 2 * tn), lambda j, i: (0, j)),
            pl.BlockSpec((3, tn), col(0)),
            pl.BlockSpec((3, tn), col(1)),
            pl.BlockSpec((1, tn), col(0)),
            pl.BlockSpec((1, tn), col(1)),
            pl.BlockSpec(state_blk, state_idx(0)),
            pl.BlockSpec(state_blk, state_idx(1)),
        ],
        out_specs=[
            pl.BlockSpec((tm, tn), lambda j, i: (i, j)),
            pl.BlockSpec(state_blk, state_idx(0)),
            pl.BlockSpec(state_blk, state_idx(0)),
        ],
        out_shape=[jax.ShapeDtypeStruct((t, D_FF), BF16), state_out, state_out],
        compiler_params=_cparams(("parallel", "arbitrary")),
        name="ffn_up_conv",
    )(h, w_up, conv_w, conv_w, conv_b, conv_b, conv_prev, conv_prev)


def _ffn_down_kernel(a_ref, w_ref, x_ref, g_ref, o_ref):
    acc = jnp.dot(a_ref[...], w_ref[...], preferred_element_type=F32)
    o_ref[...] = x_ref[...] + _rms(acc, g_ref[...])


def _ffn_down(act, w_down, x1, g_post, *, tm=256):
    t = x1.shape[0]
    return pl.pallas_call(
        _ffn_down_kernel,
        grid=(t // tm,),
        in_specs=[
            pl.BlockSpec((tm, D_FF), lambda i: (i, 0)),
            pl.BlockSpec((D_FF, D_MODEL), lambda i: (0, 0), pipeline_mode=pl.Buffered(1)),
            pl.BlockSpec((tm, D_MODEL), lambda i: (i, 0)),
            pl.BlockSpec((1, D_MODEL), lambda i: (0, 0)),
        ],
        out_specs=pl.BlockSpec((tm, D_MODEL), lambda i: (i, 0)),
        out_shape=jax.ShapeDtypeStruct((t, D_MODEL), F32),
        compiler_params=_cparams(("parallel",)),
        name="ffn_down",
    )(act, w_down, x1, g_post)


def _pad_rows(w, row0, total):
    return jnp.zeros((total, w.shape[1]), w.dtype).at[row0:row0 + w.shape[0]].set(w)


def _prepare_layer(p, l):
    w_in_t = p["w_in"][l].T
    b0 = 2 * A_WIDTH
    q0 = b0 + B_PROJ
    w_in_p = jnp.concatenate([
        w_in_t[:b0 + 3 * B_WIDTH],
        w_in_t[q0:],
        w_in_t[b0 + 3 * B_WIDTH:q0],
        jnp.zeros((IN_COLS_PAD - COL_L - 3 * LORA, D_MODEL), w_in_t.dtype),
    ], axis=0).astype(BF16)
    mu = p["rk_mu"][l]
    vec_rows = [mu[:B_WIDTH], mu[B_WIDTH:2 * B_WIDTH], mu[2 * B_WIDTH:3 * B_WIDTH], p["rk_w0"][l],
                p["rk_a0"][l], p["rk_kk"][l], p["rk_ka"][l], p["rk_rk"][l].reshape(B_WIDTH),
                p["rk_lnx_g"][l], p["rk_lnx_b"][l]]
    vecs = jnp.zeros((16, B_WIDTH), F32).at[:len(vec_rows)].set(jnp.stack(vec_rows))
    mu_l = jnp.zeros((SUBLANES, LORA_PAD), F32).at[0, :3 * LORA].set(mu[3 * B_WIDTH:])
    head_of = jnp.arange(LANES) // B_HEAD_DIM
    seg_ones = (head_of[:, None] == head_of[None, :]).astype(BF16)
    tril = jnp.tril(jnp.ones((CHUNK, CHUNK), bool))
    ws = jnp.where(tril[None], p["gm_ws"][l], 0.0)
    bs = p["gm_bs"][l]
    return {
        "g_mix_pre": p["norm_mix_pre"][l][None], "g_mix_post": p["norm_mix_post"][l][None],
        "g_ffn_pre": p["norm_ffn_pre"][l][None], "g_ffn_post": p["norm_ffn_post"][l][None],
        "g_mem": p["norm_mem"][l][None],
        "w_in": w_in_p,
        "w_out": p["w_out"][l].astype(BF16),
        "w_mkv": jnp.concatenate([p["w_mem_k"][l], p["w_mem_v"][l]], axis=1).astype(BF16),
        "gm_ln_g": p["gm_ln_g"][l][None], "gm_ln_b": p["gm_ln_b"][l][None],
        "gm_ws": ws, "gm_bs": bs,
        "rk_vecs": vecs, "rk_mu_l": mu_l,
        "rk_w2": _pad_rows(p["rk_w2"][l], 0, LORA_PAD).astype(BF16),
        "rk_a2": _pad_rows(p["rk_a2"][l], LORA, LORA_PAD).astype(BF16),
        "rk_g2": _pad_rows(p["rk_g2"][l], 2 * LORA, LORA_PAD).astype(BF16),
        "seg_ones": seg_ones,
        "w_up": jnp.concatenate(
            [p["ffn_w_up"][l][:, half * D_FF + s * FFN_SUB:half * D_FF + (s + 1) * FFN_SUB].astype(BF16)
             for s in range(D_FF // FFN_SUB) for half in (0, 1)], axis=1),
        "conv_w": p["ffn_conv_w"][l], "conv_b": p["ffn_conv_b"][l][None],
        "w_down": p["ffn_w_down"][l].astype(BF16),
    }


def _spatial_weights(lw, seq):
    ws, bs = lw["gm_ws"], lw["gm_bs"]
    if seq >= CHUNK:
        w_blk, b_rows = ws, bs
    else:
        reps = CHUNK // seq
        eye = jnp.eye(reps, dtype=ws.dtype)
        w_blk = jnp.einsum("ab,hts->hatbs", eye, ws[:, :seq, :seq]).reshape(A_HEADS, CHUNK, CHUNK)
        b_rows = jnp.tile(bs[:, :seq], (1, reps))
    bias_full = jnp.repeat(b_rows.T, A_HEAD_DIM, axis=1)
    return w_blk.astype(BF16), bias_full


def _decoder_layer(x, mem_k, mem_v, shift_prev, wkv_prev, conv_prev, lw):
    batch, seq, _ = x.shape
    x2 = x.reshape(batch * seq, D_MODEL)
    proj = _norm_matmul(x2, lw["g_mix_pre"], lw["w_in"], tm=1024, tn=PROJ_TN, w_is_nk=True)
    w_sp, bias_full = _spatial_weights(lw, seq)
    a_out, a_v = _group_a(proj, lw["gm_ln_g"], lw["gm_ln_b"], w_sp, bias_full)
    shift_parts = [
        shift_prev[:, None, :B_WIDTH], shift_prev[:, None, B_WIDTH:2 * B_WIDTH],
        shift_prev[:, None, 2 * B_WIDTH:3 * B_WIDTH],
        jnp.pad(shift_prev[:, None, 3 * B_WIDTH:], ((0, 0), (0, 0), (0, LORA_PAD - 3 * LORA))),
    ]
    b_out, wkv_new = _rwkv(proj, shift_parts, wkv_prev, lw, batch=batch, seq=seq)
    c_out = _attention(proj, mem_k, mem_v, batch=batch, seq=seq)
    x1, h = _out_proj(a_out, b_out, c_out, lw["w_out"], x2, lw["g_mix_post"], lw["g_ffn_pre"])
    act, conv_g, conv_v = _ffn_up(h, lw["w_up"], lw["conv_w"], lw["conv_b"], conv_prev, batch=batch, seq=seq)
    y = _ffn_down(act, lw["w_down"], x1, lw["g_ffn_post"])

    chunk_start = ((seq - 1) // CHUNK) * CHUNK
    chunk_v = a_v.reshape(batch, seq, A_WIDTH)[:, chunk_start:].reshape(batch, -1, A_HEADS, A_HEAD_DIM)
    last = proj.reshape(batch, seq, IN_COLS_PAD)[:, -1]
    shift_new = jnp.concatenate([last[:, COL_R:COL_R + 3 * B_WIDTH], last[:, COL_L:COL_L + 3 * LORA]], axis=1)
    conv_new = jnp.concatenate([conv_g, conv_v], axis=-1)
    return y.reshape(batch, seq, D_MODEL), chunk_v, shift_new, wkv_new, conv_new


def kernel(x_prompt, x_sample, mem_prompt, cache_mem_k, cache_mem_v, state_shift, state_wkv, state_conv,
           norm_mix_pre, norm_mix_post, norm_ffn_pre, norm_ffn_post, norm_mem, w_in, w_out, w_mem_k, w_mem_v,
           gm_ln_g, gm_ln_b, gm_ws, gm_bs, rk_mu, rk_w0, rk_w2, rk_a0, rk_a2, rk_g2, rk_kk, rk_ka, rk_rk,
           rk_lnx_g, rk_lnx_b, ffn_w_up, ffn_conv_w, ffn_conv_b, ffn_w_down):
    params = dict(
        norm_mix_pre=norm_mix_pre, norm_mix_post=norm_mix_post, norm_ffn_pre=norm_ffn_pre,
        norm_ffn_post=norm_ffn_post, norm_mem=norm_mem, w_in=w_in, w_out=w_out, w_mem_k=w_mem_k,
        w_mem_v=w_mem_v, gm_ln_g=gm_ln_g, gm_ln_b=gm_ln_b, gm_ws=gm_ws, gm_bs=gm_bs, rk_mu=rk_mu,
        rk_w0=rk_w0, rk_w2=rk_w2, rk_a0=rk_a0, rk_a2=rk_a2, rk_g2=rk_g2, rk_kk=rk_kk, rk_ka=rk_ka,
        rk_rk=rk_rk, rk_lnx_g=rk_lnx_g, rk_lnx_b=rk_lnx_b, ffn_w_up=ffn_w_up, ffn_conv_w=ffn_conv_w,
        ffn_conv_b=ffn_conv_b, ffn_w_down=ffn_w_down)
    depth = w_in.shape[0]
    bp = x_prompt.shape[0]
    y_p, y_s = x_prompt, x_sample
    outs = [[] for _ in range(10)]
    for l in range(depth):
        lw = _prepare_layer(params, l)
        mem2 = mem_prompt.reshape(bp * MEM_LEN, D_MODEL)
        mkv = _norm_matmul(mem2, lw["g_mem"], lw["w_mkv"], tm=512, tn=2 * C_WIDTH)
        mk = mkv[:, :C_WIDTH].reshape(bp, MEM_LEN, C_WIDTH)
        mv = mkv[:, C_WIDTH:].reshape(bp, MEM_LEN, C_WIDTH)
        zero_shift = jnp.zeros((bp, B_PROJ), x_prompt.dtype)
        zero_wkv = jnp.zeros((bp, B_HEADS, B_HEAD_DIM, B_HEAD_DIM), F32)
        zero_conv = jnp.zeros((bp, 2, 2 * D_FF), x_prompt.dtype)
        y_p, cv, sh, wkv, conv = _decoder_layer(y_p, mk, mv, zero_shift, zero_wkv, zero_conv, lw)
        mem_shape = (bp, MEM_LEN, C_HEADS, C_HEAD_DIM)
        for lst, val in zip(outs[:6], (mk.reshape(mem_shape), mv.reshape(mem_shape), cv, sh, wkv, conv)):
            lst.append(val)
        n_s = x_sample.shape[0]
        cache_k = cache_mem_k[l].reshape(n_s, MEM_LEN * C_HEADS, C_HEAD_DIM)
        cache_v = cache_mem_v[l].reshape(n_s, MEM_LEN * C_HEADS, C_HEAD_DIM)
        y_s, cv, sh, wkv, conv = _decoder_layer(y_s, cache_k, cache_v, state_shift[l],
                                                state_wkv[l], state_conv[l], lw)
        for lst, val in zip(outs[6:], (cv, sh, wkv, conv)):
            lst.append(val)
    return (y_p, y_s) + tuple(jnp.stack(o) for o in outs)
```

```python
import functools
import math

import jax
import jax.numpy as jnp
from jax import lax
from jax.experimental import pallas as pl
from jax.experimental.pallas import tpu as pltpu

D_MODEL = 2048
MEM_LEN = 256
CHUNK = 128
A_HEADS, A_HEAD_DIM = 4, 128
A_WIDTH = A_HEADS * A_HEAD_DIM
B_HEADS, B_HEAD_DIM = 16, 64
B_WIDTH = B_HEADS * B_HEAD_DIM
LORA = 64
B_PROJ = 3 * B_WIDTH + 3 * LORA
C_HEADS, C_HEAD_DIM = 4, 128
C_WIDTH = C_HEADS * C_HEAD_DIM
D_FF = 5632
RMS_EPS = 1e-6
LN_EPS = 1e-5
GN_EPS = 64e-5
DECAY_OFFSET = 0.5

LANES = 128
SUBLANES = 8
VMEM_LIMIT_BYTES = 56 * 1024 * 1024

LORA_PAD = 256
COL_A = 0
COL_R = 2 * A_WIDTH
COL_Q = COL_R + 3 * B_WIDTH
COL_L = COL_Q + C_WIDTH
IN_COLS_PAD = 5120
PROJ_TN = 1024

WKV_CHUNK = 64
FFN_N_TILES = 2
FFN_TN = D_FF // FFN_N_TILES
FFN_SUB = 256

F32 = jnp.float32
BF16 = jnp.bfloat16


def _cparams(sem):
    return pltpu.CompilerParams(dimension_semantics=sem, vmem_limit_bytes=VMEM_LIMIT_BYTES)


def _rms(x, g):
    return x * lax.rsqrt(jnp.mean(x * x, axis=-1, keepdims=True) + RMS_EPS) * g


def _norm_matmul_kernel(x_ref, g_ref, w_ref, o_ref, h_ref, *, w_is_nk):
    @pl.when(pl.program_id(1) == 0)
    def _():
        h_ref[...] = _rms(x_ref[...], g_ref[...]).astype(BF16)

    dims = (((1,), (1,)), ((), ())) if w_is_nk else (((1,), (0,)), ((), ()))
    o_ref[...] = lax.dot_general(h_ref[...], w_ref[...], dims, preferred_element_type=F32)


def _norm_matmul(x, g, w, *, tm, tn, w_is_nk=False):
    t, k = x.shape
    n = w.shape[0] if w_is_nk else w.shape[1]
    w_spec = (pl.BlockSpec((tn, k), lambda i, j: (j, 0)) if w_is_nk
              else pl.BlockSpec((k, tn), lambda i, j: (0, j)))
    return pl.pallas_call(
        functools.partial(_norm_matmul_kernel, w_is_nk=w_is_nk),
        grid=(t // tm, n // tn),
        in_specs=[
            pl.BlockSpec((tm, k), lambda i, j: (i, 0)),
            pl.BlockSpec((1, k), lambda i, j: (0, 0)),
            w_spec,
        ],
        out_specs=pl.BlockSpec((tm, tn), lambda i, j: (i, j)),
        out_shape=jax.ShapeDtypeStruct((t, n), F32),
        scratch_shapes=[pltpu.VMEM((tm, k), BF16)],
        compiler_params=_cparams(("parallel", "arbitrary")),
        name="norm_matmul",
    )(x, g, w)


def _group_a_kernel(u_ref, v_ref, g_ref, b_ref, w_ref, bias_ref, o_ref, vout_ref):
    u = jax.nn.gelu(u_ref[...], approximate=True)
    v = jax.nn.gelu(v_ref[...], approximate=True)
    mean = jnp.mean(v, axis=-1, keepdims=True)
    d = v - mean
    var = jnp.mean(d * d, axis=-1, keepdims=True)
    vn = d * lax.rsqrt(var + LN_EPS) * g_ref[...] + b_ref[...]
    vout_ref[...] = vn
    vb = vn.astype(BF16)
    for c in range(u.shape[0] // CHUNK):
        rs = slice(c * CHUNK, (c + 1) * CHUNK)
        for h in range(A_HEADS):
            hs = slice(h * A_HEAD_DIM, (h + 1) * A_HEAD_DIM)
            mixed = jnp.dot(w_ref[h], vb[rs, hs], preferred_element_type=F32) + bias_ref[:, hs]
            o_ref[rs, hs] = (u[rs, hs] * mixed).astype(BF16)


def _group_a(proj, ln_g, ln_b, w_sp, bias_full, *, rows=4 * CHUNK):
    t = proj.shape[0]
    return pl.pallas_call(
        _group_a_kernel,
        grid=(t // rows,),
        in_specs=[
            pl.BlockSpec((rows, A_WIDTH), lambda i: (i, COL_A // A_WIDTH)),
            pl.BlockSpec((rows, A_WIDTH), lambda i: (i, COL_A // A_WIDTH + 1)),
            pl.BlockSpec((1, A_WIDTH), lambda i: (0, 0)),
            pl.BlockSpec((1, A_WIDTH), lambda i: (0, 0)),
            pl.BlockSpec((A_HEADS, CHUNK, CHUNK), lambda i: (0, 0, 0)),
            pl.BlockSpec((CHUNK, A_WIDTH), lambda i: (0, 0)),
        ],
        out_specs=[
            pl.BlockSpec((rows, A_WIDTH), lambda i: (i, 0)),
            pl.BlockSpec((rows, A_WIDTH), lambda i: (i, 0)),
        ],
        out_shape=[
            jax.ShapeDtypeStruct((t, A_WIDTH), BF16),
            jax.ShapeDtypeStruct((t, A_WIDTH), F32),
        ],
        compiler_params=_cparams(("parallel",)),
        name="group_a",
    )(proj, proj, ln_g, ln_b, w_sp, bias_full)


def _softmax_rows(s):
    e = jnp.exp(s - jnp.max(s, axis=-1, keepdims=True))
    return e / jnp.sum(e, axis=-1, keepdims=True)


def _attn_kernel(q_ref, k_ref, v_ref, o_ref, *, n_b, tq, head_major_rows):
    scale = C_HEAD_DIM ** -0.5
    heads = range(C_HEADS)
    lanes = lambda h: slice(h * C_HEAD_DIM, (h + 1) * C_HEAD_DIM)
    if not head_major_rows:
        chains = [(g, h) for g in range(n_b) for h in heads]
        rows = lambda g: slice(g * tq, (g + 1) * tq)
        s = [lax.dot_general(q_ref[rows(g), lanes(h)].astype(BF16), k_ref[g, :, lanes(h)].astype(BF16), _NT,
                             preferred_element_type=F32) * scale for g, h in chains]
        p = [_softmax_rows(m).astype(BF16) for m in s]
        o = [jnp.dot(m, v_ref[g, :, lanes(h)].astype(BF16), preferred_element_type=F32)
             for m, (g, h) in zip(p, chains)]
        for m, (g, h) in zip(o, chains):
            o_ref[rows(g), lanes(h)] = m.astype(BF16)
        return
    row = lax.broadcasted_iota(jnp.int32, (C_HEADS * tq, 1), 0)
    col = lax.broadcasted_iota(jnp.int32, (1, C_HEADS * MEM_LEN), 1)
    own = (col & (C_HEADS - 1)) == (row >> int(math.log2(tq)))
    q = [q_ref[g * tq:(g + 1) * tq, :] for g in range(n_b)]
    qs = [jnp.concatenate([m[:, lanes(h)] for h in heads], axis=0).astype(BF16) for m in q]
    s = [lax.dot_general(m, k_ref[g].astype(BF16), _NT, preferred_element_type=F32) * scale
         for g, m in enumerate(qs)]
    p = [_softmax_rows(jnp.where(own, m, -1e30)).astype(BF16) for m in s]
    o = [jnp.dot(m, v_ref[g].astype(BF16), preferred_element_type=F32) for g, m in enumerate(p)]
    for g, m in enumerate(o):
        for h in heads:
            o_ref[g * tq:(g + 1) * tq, lanes(h)] = m[h * tq:(h + 1) * tq].astype(BF16)


def _attention(proj, mem_k, mem_v, *, batch, seq):
    head_major_rows = mem_k.shape[-1] == C_HEAD_DIM
    if head_major_rows:
        tq, n_b = seq, 8
    else:
        tq, n_b = 512, 1
    n_q = seq // tq
    mem_blk = (n_b,) + mem_k.shape[1:]
    return pl.pallas_call(
        functools.partial(_attn_kernel, n_b=n_b, tq=tq, head_major_rows=head_major_rows),
        grid=(batch // n_b, n_q),
        in_specs=[
            pl.BlockSpec((n_b * tq, C_WIDTH), lambda b, i: (b * n_q + i, COL_Q // C_WIDTH)),
            pl.BlockSpec(mem_blk, lambda b, i: (b, 0, 0)),
            pl.BlockSpec(mem_blk, lambda b, i: (b, 0, 0)),
        ],
        out_specs=pl.BlockSpec((n_b * tq, C_WIDTH), lambda b, i: (b * n_q + i, 0)),
        out_shape=jax.ShapeDtypeStruct((batch * seq, C_WIDTH), BF16),
        compiler_params=_cparams(("parallel", "arbitrary")),
        name="mem_attention",
    )(proj, mem_k, mem_v)


_V_MU_R, _V_MU_K, _V_MU_V, _V_W0, _V_A0, _V_KK, _V_KA, _V_RK, _V_LNG, _V_LNB = range(10)


_NN = (((1,), (0,)), ((), ()))
_NT = (((1,), (1,)), ((), ()))
_TN = (((0,), (0,)), ((), ()))
HEAD_PAIRS = B_HEADS // 2


def _split(x):
    hi = x.astype(BF16)
    return hi, (x - hi.astype(F32)).astype(BF16)


def _b(x):
    return x.astype(BF16)


def _bdot(a, b, dims=_NN):
    return lax.dot_general(a, b, dims, preferred_element_type=F32)


def _rwkv_kernel(pr_ref, pk_ref, pv_ref, pl_ref, sr_ref, sk_ref, sv_ref, sl_ref, wkv_ref,
                 vec_ref, mul_ref, w2_ref, a2_ref, g2_ref, seg_ref,
                 o_ref, so_ref, cr_ref, ck_ref, cv_ref, cl_ref, sbd_ref, *, n_blk, groups, chunk, whole_seq):
    rows = groups * chunk
    first = pl.program_id(1) == 0
    last = pl.program_id(1) == pl.num_programs(1) - 1
    hd = B_HEAD_DIM
    n_seq = n_blk * groups

    @pl.when(first)
    def _():
        zero = jnp.zeros((hd, hd), F32)
        for g in range(n_seq):
            for q in range(HEAD_PAIRS):
                top = jnp.concatenate([wkv_ref[g, 2 * q], zero], axis=1)
                bot = jnp.concatenate([zero, wkv_ref[g, 2 * q + 1]], axis=1)
                sbd_ref[g, q] = jnp.concatenate([top, bot], axis=0)
        if not whole_seq:
            for s_ref, carry_ref in ((sr_ref, cr_ref), (sk_ref, ck_ref), (sv_ref, cv_ref), (sl_ref, cl_ref)):
                for blk in range(n_blk):
                    carry_ref[blk] = jnp.broadcast_to(s_ref[blk], (SUBLANES, s_ref.shape[-1]))

    row_id = lax.broadcasted_iota(jnp.int32, (rows, 1), 0)
    col_id = lax.broadcasted_iota(jnp.int32, (1, rows), 1)
    chunk_bits = int(math.log2(chunk))
    incl_b = (((row_id >> chunk_bits) == (col_id >> chunk_bits)) & (col_id <= row_id)).astype(BF16)
    vec = lambda i: vec_ref[i:i + 1, :]
    seg_ones = seg_ref[...]

    def seg_sum(x):
        slabs = jnp.concatenate([x[:, q * LANES:(q + 1) * LANES] for q in range(HEAD_PAIRS)], axis=0)
        hi, lo = _split(slabs)
        s = (jnp.dot(hi, seg_ones, preferred_element_type=F32)
             + jnp.dot(lo, seg_ones, preferred_element_type=F32))
        return jnp.concatenate([s[q * rows:(q + 1) * rows] for q in range(HEAD_PAIRS)], axis=1)

    def prologue(blk, out):
        seqs = slice(blk * groups, (blk + 1) * groups)

        def prev_rows(x, s_ref, carry_ref):
            width = x.shape[1]
            if whole_seq:
                start = jnp.broadcast_to(s_ref[seqs], (groups, chunk, width)).reshape(rows, width)
                return jnp.where((row_id & (chunk - 1)) == 0, start, pltpu.roll(x, 1, 0))

            ext = jnp.concatenate([carry_ref[blk], x], axis=0)
            prev = pltpu.roll(ext, 1, 0)[SUBLANES:]
            carry_ref[blk] = x[rows - SUBLANES:]
            return prev

        def shifted(p_ref, s_ref, carry_ref, mu):
            x = p_ref[blk]
            return x + (prev_rows(x, s_ref, carry_ref) - x) * mu

        lo = shifted(pl_ref, sl_ref, cl_ref, mul_ref[0:1, :])
        dw = jnp.dot(jnp.tanh(lo).astype(BF16), w2_ref[...], preferred_element_type=F32)
        da = jnp.dot(lo.astype(BF16), a2_ref[...], preferred_element_type=F32)
        gate = jnp.dot(jax.nn.sigmoid(lo).astype(BF16), g2_ref[...], preferred_element_type=F32)
        yield
        w_log = -jax.nn.softplus(-(vec(_V_W0) + dw)) - DECAY_OFFSET
        log_decay = -jnp.exp(w_log)
        ld_hi, ld_mid = _split(log_decay)
        ld_lo = (log_decay - ld_hi.astype(F32) - ld_mid.astype(F32)).astype(BF16)
        cum = (jnp.dot(incl_b, ld_hi, preferred_element_type=F32)
               + jnp.dot(incl_b, ld_mid, preferred_element_type=F32)
               + jnp.dot(incl_b, ld_lo, preferred_element_type=F32))
        yield
        a = jax.nn.sigmoid(vec(_V_A0) + da)
        k = shifted(pk_ref, sk_ref, ck_ref, vec(_V_MU_K))
        kk = k * vec(_V_KK)
        kk = kk / jnp.maximum(jnp.sqrt(seg_sum(kk * kk)), 1e-12)
        yield
        p_incl = jnp.exp(cum)
        p_inv = jnp.exp(-cum)
        a_t = -kk * jnp.exp(cum - log_decay)
        b_t = kk * a * p_inv
        yield
        k = k * (1.0 + (a - 1.0) * vec(_V_KA))
        k_t = k * p_inv
        r = shifted(pr_ref, sr_ref, cr_ref, vec(_V_MU_R))
        r_t = r * p_incl
        yield
        v = shifted(pv_ref, sv_ref, cv_ref, vec(_V_MU_V))
        bonus = seg_sum(r * k * vec(_V_RK)) * v
        out.update(a_t=a_t, b_t=b_t, k_t=k_t, r_t=r_t, v=v, p_incl=p_incl, bonus=bonus, gate=gate)

    pr = 2 * rows
    row2 = lax.broadcasted_iota(jnp.int32, (pr, 1), 0)
    col2 = lax.broadcasted_iota(jnp.int32, (1, pr), 1)
    t2, s2 = row2 & (rows - 1), col2 & (rows - 1)
    same2 = ((row2 >> chunk_bits) == (col2 >> chunk_bits))
    incl2 = same2 & (s2 <= t2)
    strict2 = same2 & (s2 < t2)
    eye2 = (row2 == col2).astype(F32)
    left = lax.broadcasted_iota(jnp.int32, (rows, LANES), 1) < hd

    def bd(x):
        zero = jnp.zeros_like(x)
        return jnp.concatenate([jnp.where(left, x, zero), jnp.where(left, zero, x)], axis=0)

    def group_rows(mats, g):
        starts = [hh * rows + g * chunk for hh in (0, 1)]
        return jnp.concatenate([m[i:i + chunk] for m in mats for i in starts], axis=0)

    n_sq = chunk_bits - 1
    col4 = lax.broadcasted_iota(jnp.int32, (1, 2 * pr), 1)
    incl4 = ((row2 >> chunk_bits) == ((col4 & (pr - 1)) >> chunk_bits)) & ((col4 & (rows - 1)) <= t2)
    pairs = range(HEAD_PAIRS)
    lanes_of = [slice(q * LANES, (q + 1) * LANES) for q in pairs]

    def recurrence(blk, pro):
        s0 = blk * groups
        p_incl = pro["p_incl"]
        bds = [[bd(pro[name][:, ls]) for name in ("a_t", "r_t", "b_t", "k_t", "v")] for ls in lanes_of]
        ar_s = [_b(jnp.concatenate([m[0], m[1]], axis=0)) for m in bds]
        bk_s = [_b(jnp.concatenate([m[2], m[3]], axis=0)) for m in bds]
        v_s = [_b(m[4]) for m in bds]
        if groups == 1:
            ms = [_bdot(ar_s[q], jnp.concatenate([bk_s[q], _b(sbd_ref[s0, q])], axis=0), _NT) for q in pairs]
        else:
            ms = [_bdot(ar_s[q], bk_s[q], _NT) for q in pairs]
        yield
        a_ab = [jnp.where(strict2, m[:pr, :pr], 0.0) for m in ms]
        a_ak = [jnp.where(strict2, m[:pr, pr:2 * pr], 0.0) for m in ms]
        a_r = [jnp.where(incl4, m[pr:, :2 * pr], 0.0) for m in ms]
        inv = [eye2 + n for n in a_ab]
        pw_s = [_b(_bdot(_b(n), _b(n))) for n in a_ab]
        yield
        for _ in range(1, n_sq):
            both = [_bdot(p, jnp.concatenate([p, _b(i)], axis=1)) for p, i in zip(pw_s, inv)]
            inv = [i + m[:, pr:] for i, m in zip(inv, both)]
            pw_s = [_b(m[:, :pr]) for m in both]
            yield
        inv = [i + _bdot(p, _b(i)) for i, p in zip(inv, pw_s)]
        yield
        if groups == 1:
            x0, y0 = [m[:pr, 2 * pr:] for m in ms], [m[pr:, 2 * pr:] for m in ms]
        else:
            x0, y0 = [], []
            for q in pairs:
                x_parts, y_parts = [None] * (2 * groups), [None] * (2 * groups)
                for g in range(groups):
                    xy = _bdot(_b(group_rows(bds[q][:2], g)), _b(sbd_ref[s0 + g, q]), _NT)
                    for hh in (0, 1):
                        x_parts[hh * groups + g] = xy[hh * chunk:(hh + 1) * chunk]
                        y_parts[hh * groups + g] = xy[(2 + hh) * chunk:(3 + hh) * chunk]
                x0.append(jnp.concatenate(x_parts, axis=0))
                y0.append(jnp.concatenate(y_parts, axis=0))
        yield
        rhs = [_b(x0[q] + _bdot(_b(a_ak[q]), v_s[q])) for q in pairs]
        u = [_bdot(_b(inv[q]), rhs[q]) for q in pairs]
        yield
        uv_s = [jnp.concatenate([_b(u[q]), v_s[q]], axis=0) for q in pairs]
        y2 = [y0[q] + _bdot(_b(a_r[q]), uv_s[q]) for q in pairs]
        pro["y"] = jnp.concatenate([m[:rows] + m[rows:] for m in y2], axis=1)
        yield
        for q in pairs:
            if groups == 1:
                ds = _bdot(uv_s[q], bk_s[q], _TN)
                sbd_ref[s0, q] = (sbd_ref[s0, q] + ds) * p_incl[rows - 1:rows, lanes_of[q]]
            else:
                for g in range(groups):
                    ds = _bdot(_b(group_rows((u[q], bds[q][4]), g)),
                               _b(group_rows((bds[q][2], bds[q][3]), g)), _TN)
                    end = (g + 1) * chunk - 1
                    sbd_ref[s0 + g, q] = (sbd_ref[s0 + g, q] + ds) * p_incl[end:end + 1, lanes_of[q]]

    def finish(blk, pro):
        inv_n = 1.0 / B_HEAD_DIM
        y = pro["y"]
        d = y - seg_sum(y) * inv_n
        yield
        var = seg_sum(d * d) * inv_n
        yield
        yn = d * lax.rsqrt(var + GN_EPS) * vec(_V_LNG) + vec(_V_LNB)
        o_ref[blk] = ((yn + pro["bonus"]) * pro["gate"]).astype(BF16)

    def interleave(*gens):
        live = list(gens)
        while live:
            for gen in list(live):
                if next(gen, StopIteration) is StopIteration:
                    live.remove(gen)

    pros = [dict() for _ in range(n_blk)]
    interleave(prologue(0, pros[0]))
    for blk in range(n_blk):
        work = [recurrence(blk, pros[blk])]
        if blk + 1 < n_blk:
            work.append(prologue(blk + 1, pros[blk + 1]))
        if blk > 0:
            work.append(finish(blk - 1, pros[blk - 1]))
        interleave(*work)
    interleave(finish(n_blk - 1, pros[n_blk - 1]))

    @pl.when(last)
    def _():
        for g in range(n_seq):
            for q in range(HEAD_PAIRS):
                sq = sbd_ref[g, q]
                so_ref[g, 2 * q] = sq[:hd, :hd]
                so_ref[g, 2 * q + 1] = sq[hd:, hd:]


def _rwkv(proj, shift_parts, wkv_prev, lw, *, batch, seq):
    whole_seq = seq <= WKV_CHUNK
    if whole_seq:
        assert seq == SUBLANES, "whole-sequence blocks rely on one sublane tile per sequence"
        chunk, groups = seq, WKV_CHUNK // seq
        n_blk, n_chunks = 2, 1
        n_outer = batch // (groups * n_blk)
        proj3 = proj.reshape(batch // groups, WKV_CHUNK, IN_COLS_PAD)
    else:
        chunk, groups = WKV_CHUNK, 1
        n_blk, n_outer, n_chunks = batch, 1, seq // WKV_CHUNK
        proj3 = proj.reshape(batch, seq, IN_COLS_PAD)
    n_seq = n_blk * groups
    act = lambda col_blk, width: pl.BlockSpec((n_blk, WKV_CHUNK, width), lambda o, c: (o, c, col_blk))
    per_seq = lambda *tail: pl.BlockSpec((n_seq,) + tail, lambda o, c: (o,) + (0,) * len(tail))
    fixed = lambda *shape: pl.BlockSpec(shape, lambda o, c: (0,) * len(shape))
    state = (B_HEADS, B_HEAD_DIM, B_HEAD_DIM)
    kern = functools.partial(_rwkv_kernel, n_blk=n_blk, groups=groups, chunk=chunk, whole_seq=whole_seq)
    b_out, wkv_new = pl.pallas_call(
        kern,
        grid=(n_outer, n_chunks),
        in_specs=[
            act(COL_R // B_WIDTH, B_WIDTH), act(COL_R // B_WIDTH + 1, B_WIDTH),
            act(COL_R // B_WIDTH + 2, B_WIDTH), act(COL_L // LORA_PAD, LORA_PAD),
            per_seq(1, B_WIDTH), per_seq(1, B_WIDTH), per_seq(1, B_WIDTH), per_seq(1, LORA_PAD),
            per_seq(*state),
            fixed(16, B_WIDTH), fixed(SUBLANES, LORA_PAD),
            fixed(LORA_PAD, B_WIDTH), fixed(LORA_PAD, B_WIDTH), fixed(LORA_PAD, B_WIDTH),
            fixed(LANES, LANES),
        ],
        out_specs=[act(0, B_WIDTH), per_seq(*state)],
        out_shape=[
            jax.ShapeDtypeStruct(proj3.shape[:2] + (B_WIDTH,), BF16),
            jax.ShapeDtypeStruct((batch,) + state, F32),
        ],
        scratch_shapes=[
            pltpu.VMEM((n_blk, SUBLANES, B_WIDTH), F32),
            pltpu.VMEM((n_blk, SUBLANES, B_WIDTH), F32),
            pltpu.VMEM((n_blk, SUBLANES, B_WIDTH), F32),
            pltpu.VMEM((n_blk, SUBLANES, LORA_PAD), F32),
            pltpu.VMEM((n_seq, HEAD_PAIRS, LANES, LANES), F32),
        ],
        compiler_params=_cparams(("parallel", "arbitrary")),
        name="rwkv7",
    )(proj3, proj3, proj3, proj3, *shift_parts, wkv_prev,
      lw["rk_vecs"], lw["rk_mu_l"], lw["rk_w2"], lw["rk_a2"], lw["rk_g2"], lw["seg_ones"])
    return b_out.reshape(batch * seq, B_WIDTH), wkv_new


def _out_proj_kernel(a_ref, b_ref, c_ref, w_ref, x_ref, gpost_ref, gffn_ref, x1_ref, h_ref):
    n_sub = 4
    sub = x_ref.shape[0] // n_sub
    halves = [slice(s * sub, (s + 1) * sub) for s in range(n_sub)]

    def project(rs):
        acc = jnp.dot(a_ref[rs, :], w_ref[0:A_WIDTH, :], preferred_element_type=F32)
        acc += jnp.dot(b_ref[rs, :], w_ref[A_WIDTH:A_WIDTH + B_WIDTH, :], preferred_element_type=F32)
        return acc + jnp.dot(c_ref[rs, :], w_ref[A_WIDTH + B_WIDTH:, :], preferred_element_type=F32)

    accs = [project(rs) for rs in halves]
    for rs, acc in zip(halves, accs):
        x1 = x_ref[rs, :] + _rms(acc, gpost_ref[...])
        x1_ref[rs, :] = x1
        h_ref[rs, :] = _rms(x1, gffn_ref[...]).astype(BF16)


def _out_proj(a_out, b_out, c_out, w_out, x, g_post, g_ffn, *, tm=512):
    t = x.shape[0]
    row = lambda i: (i, 0)
    fixed = lambda i: (0, 0)
    return pl.pallas_call(
        _out_proj_kernel,
        grid=(t // tm,),
        in_specs=[
            pl.BlockSpec((tm, A_WIDTH), row),
            pl.BlockSpec((tm, B_WIDTH), row),
            pl.BlockSpec((tm, C_WIDTH), row),
            pl.BlockSpec((D_MODEL, D_MODEL), fixed, pipeline_mode=pl.Buffered(1)),
            pl.BlockSpec((tm, D_MODEL), row),
            pl.BlockSpec((1, D_MODEL), fixed),
            pl.BlockSpec((1, D_MODEL), fixed),
        ],
        out_specs=[pl.BlockSpec((tm, D_MODEL), row), pl.BlockSpec((tm, D_MODEL), row)],
        out_shape=[jax.ShapeDtypeStruct((t, D_MODEL), F32), jax.ShapeDtypeStruct((t, D_MODEL), BF16)],
        compiler_params=_cparams(("parallel",)),
        name="out_proj",
    )(a_out, b_out, c_out, w_out, x, g_post, g_ffn)


def _chunked_dots(h_ref, w_refs, cs, between):
    accs = [None] * len(w_refs)
    for kc in range(h_ref.shape[1] // FFN_SUB):
        ks = slice(kc * FFN_SUB, (kc + 1) * FFN_SUB)
        for n, w_ref in enumerate(w_refs):
            part = jnp.dot(h_ref[:, ks], w_ref[ks, cs], preferred_element_type=F32)
            accs[n] = part if accs[n] is None else accs[n] + part
        between(kc)
    return accs


def _ffn_up_kernel(h_ref, wg_ref, wv_ref, cwg_ref, cwv_ref, cbg_ref, cbv_ref, pg_ref, pv_ref,
                   act_ref, ng_ref, nv_ref, *, tm, seq):
    h = h_ref[...]
    tn = cwg_ref.shape[1]
    n_seq = tm // seq
    tau = lax.broadcasted_iota(jnp.int32, (tm, 1), 0) & (seq - 1)

    def conv(up, cs, cw_ref, cb_ref, p_ref, n_ref):
        width = up.shape[1]
        prev = p_ref[:, :, cs]
        e0 = jnp.broadcast_to(prev[:, 0:1, :], (n_seq, seq, width)).reshape(tm, width)
        e1 = jnp.broadcast_to(prev[:, 1:2, :], (n_seq, seq, width)).reshape(tm, width)
        m1 = jnp.where(tau == 0, e1, pltpu.roll(up, 1, 0))
        m2 = jnp.where(tau == 0, e0, jnp.where(tau == 1, e1, pltpu.roll(up, 2, 0)))
        n_ref[:, :, cs] = up.reshape(n_seq, seq, width)[:, seq - 2:, :]
        return cb_ref[:, cs] + m2 * cw_ref[0:1, cs] + m1 * cw_ref[1:2, cs] + up * cw_ref[2:3, cs]

    subs = [slice(s * FFN_SUB, (s + 1) * FFN_SUB) for s in range(tn // FFN_SUB)]
    ups = [(jnp.dot(h, wg_ref[:, cs], preferred_element_type=F32),
            jnp.dot(h, wv_ref[:, cs], preferred_element_type=F32)) for cs in subs]
    for cs, (up_g, up_v) in zip(subs, ups):
        gate = conv(up_g, cs, cwg_ref, cbg_ref, pg_ref, ng_ref)
        val = conv(up_v, cs, cwv_ref, cbv_ref, pv_ref, nv_ref)
        act_ref[:, cs] = (jax.nn.gelu(gate, approximate=True) * val).astype(BF16)


def _ffn_up_skew_kernel(h_ref, wg_ref, wv_ref, cwg_ref, cwv_ref, cbg_ref, cbv_ref, pg_ref, pv_ref,
                        act_ref, ng_ref, nv_ref, cg_ref, cv_ref, u0_ref, u1_ref,
                        *, tm, tiles_per_seq):
    i = pl.program_id(1)
    tn = cwg_ref.shape[1]
    n_sub = tn // FFN_SUB

    @pl.when(i == 0)
    def _():
        for ref in (u1_ref, cg_ref, cv_ref):
            ref[...] = jnp.zeros_like(ref)

    @pl.when(i % tiles_per_seq == 1 % tiles_per_seq)
    def _():
        for carry_ref, p_ref in ((cg_ref, pg_ref), (cv_ref, pv_ref)):
            carry_ref[...] = jnp.concatenate([jnp.zeros((SUBLANES - 2, tn), F32), p_ref[0]], axis=0)

    piece = 64

    def conv(u_ref, r0, us, cs, cw_ref, cb_ref, carry_ref):
        if r0 == 0:
            ext = jnp.concatenate([carry_ref[:, cs], u_ref[0:piece, us]], axis=0)
        else:
            ext = u_ref[r0 - SUBLANES:r0 + piece, us]
        m1 = pltpu.roll(ext, 1, 0)[SUBLANES:]
        m2 = pltpu.roll(ext, 2, 0)[SUBLANES:]
        return (cb_ref[:, cs] + m2 * cw_ref[0:1, cs] + m1 * cw_ref[1:2, cs]
                + ext[SUBLANES:] * cw_ref[2:3, cs])

    n_k = D_MODEL // FFN_SUB
    chunks_per_piece = n_k // (tm // piece)

    def step(new_ref, old_ref):
        for s in range(n_sub):
            cs = slice(s * FFN_SUB, (s + 1) * FFN_SUB)
            gs, vs = cs, slice(tn + s * FFN_SUB, tn + (s + 1) * FFN_SUB)

            def conv_piece(kc):
                if (kc + 1) % chunks_per_piece == 0:
                    r0 = (kc // chunks_per_piece) * piece
                    gate = conv(old_ref, r0, gs, cs, cwg_ref, cbg_ref, cg_ref)
                    val = conv(old_ref, r0, vs, cs, cwv_ref, cbv_ref, cv_ref)
                    act_ref[r0:r0 + piece, cs] = (jax.nn.gelu(gate, approximate=True) * val).astype(BF16)

            new_ref[:, gs], new_ref[:, vs] = _chunked_dots(h_ref, (wg_ref, wv_ref), cs, conv_piece)
            for us, carry_ref, n_ref in ((gs, cg_ref, ng_ref), (vs, cv_ref, nv_ref)):
                carry_ref[:, cs] = old_ref[tm - SUBLANES:, us]
                n_ref[0, :, cs] = old_ref[tm - 2:, us]

    @pl.when(i % 2 == 0)
    def _():
        step(u0_ref, u1_ref)

    @pl.when(i % 2 == 1)
    def _():
        step(u1_ref, u0_ref)


def _ffn_up_skew(h, w_up, conv_w, conv_b, conv_prev, *, batch, seq, tm=256):
    t = h.shape[0]
    tn, n_tiles = FFN_TN, FFN_N_TILES
    n_m = t // tm
    tiles_per_seq = seq // tm
    prev_tile = lambda i: jnp.maximum(i - 1, 0)
    col = lambda half: (lambda j, i: (0, j + half * n_tiles))
    state_blk = (1, 2, tn)
    state_idx = lambda half: (lambda j, i: (prev_tile(i) // tiles_per_seq, 0, j + half * n_tiles))
    state_out = jax.ShapeDtypeStruct((batch, 2, D_FF), F32)
    up_buf = pltpu.VMEM((tm, 2 * tn), F32)
    return pl.pallas_call(
        functools.partial(_ffn_up_skew_kernel, tm=tm, tiles_per_seq=tiles_per_seq),
        grid=(n_tiles, n_m + 1),
        in_specs=[
            pl.BlockSpec((tm, D_MODEL), lambda j, i: (jnp.minimum(i, n_m - 1), 0)),
            pl.BlockSpec((D_MODEL, tn), col(0), pipeline_mode=pl.Buffered(1)),
            pl.BlockSpec((D_MODEL, tn), col(1), pipeline_mode=pl.Buffered(1)),
            pl.BlockSpec((3, tn), col(0)),
            pl.BlockSpec((3, tn), col(1)),
            pl.BlockSpec((1, tn), col(0)),
            pl.BlockSpec((1, tn), col(1)),
            pl.BlockSpec(state_blk, state_idx(0)),
            pl.BlockSpec(state_blk, state_idx(1)),
        ],
        out_specs=[
            pl.BlockSpec((tm, tn), lambda j, i: (prev_tile(i), j)),
            pl.BlockSpec(state_blk, state_idx(0)),
            pl.BlockSpec(state_blk, state_idx(0)),
        ],
        out_shape=[jax.ShapeDtypeStruct((t, D_FF), BF16), state_out, state_out],
        scratch_shapes=[pltpu.VMEM((SUBLANES, tn), F32), pltpu.VMEM((SUBLANES, tn), F32), up_buf, up_buf],
        compiler_params=_cparams(("parallel", "arbitrary")),
        name="ffn_up_conv",
    )(h, w_up, w_up, conv_w, conv_w, conv_b, conv_b, conv_prev, conv_prev)


def _ffn_up(h, w_up, conv_w, conv_b, conv_prev, *, batch, seq):
    t = h.shape[0]
    if seq > SUBLANES:
        return _ffn_up_skew(h, w_up, conv_w, conv_b, conv_prev, batch=batch, seq=seq)
    assert seq == SUBLANES, "whole-sequence tiles rely on one sublane tile per sequence"
    tm, tn = t, 2 * FFN_SUB
    n_tiles = D_FF // tn
    state_blk = (batch, 2, tn)
    state_idx = lambda half: (lambda j, i: (0, 0, j + half * n_tiles))
    col = lambda half: (lambda j, i: (0, j + half * n_tiles))
    kern = functools.partial(_ffn_up_kernel, tm=tm, seq=seq)
    state_out = jax.ShapeDtypeStruct((batch, 2, D_FF), F32)
    return pl.pallas_call(
        kern,
        grid=(n_tiles, t // tm),
        in_specs=[
            pl.BlockSpec((tm, D_MODEL), lambda j, i: (i, 0)),
            pl.BlockSpec((D_MODEL, tn), col(0)),
            pl.BlockSpec((D_MODEL, tn), col(1)),
            pl.BlockSpec((3, tn), col(0)),
            pl.BlockSpec((3, tn), col(1)),
            pl.BlockSpec((1, tn), col(0)),
            pl.BlockSpec((1, tn), col(1)),
            pl.BlockSpec(state_blk, state_idx(0)),
            pl.BlockSpec(state_blk, state_idx(1)),
        ],
        out_specs=[
            pl.BlockSpec((tm, tn), lambda j, i: (i, j)),
            pl.BlockSpec(state_blk, state_idx(0)),
            pl.BlockSpec(state_blk, state_idx(0)),
        ],
        out_shape=[jax.ShapeDtypeStruct((t, D_FF), BF16), state_out, state_out],
        compiler_params=_cparams(("parallel", "arbitrary")),
        name="ffn_up_conv",
    )(h, w_up, w_up, conv_w, conv_w, conv_b, conv_b, conv_prev, conv_prev)


def _ffn_down_kernel(a_ref, w_ref, x_ref, g_ref, o_ref):
    acc = jnp.dot(a_ref[...], w_ref[...], preferred_element_type=F32)
    o_ref[...] = x_ref[...] + _rms(acc, g_ref[...])


def _ffn_down(act, w_down, x1, g_post, *, tm=256):
    t = x1.shape[0]
    return pl.pallas_call(
        _ffn_down_kernel,
        grid=(t // tm,),
        in_specs=[
            pl.BlockSpec((tm, D_FF), lambda i: (i, 0)),
            pl.BlockSpec((D_FF, D_MODEL), lambda i: (0, 0), pipeline_mode=pl.Buffered(1)),
            pl.BlockSpec((tm, D_MODEL), lambda i: (i, 0)),
            pl.BlockSpec((1, D_MODEL), lambda i: (0, 0)),
        ],
        out_specs=pl.BlockSpec((tm, D_MODEL), lambda i: (i, 0)),
        out_shape=jax.ShapeDtypeStruct((t, D_MODEL), F32),
        compiler_params=_cparams(("parallel",)),
        name="ffn_down",
    )(act, w_down, x1, g_post)


def _pad_rows(w, row0, total):
    return jnp.zeros((total, w.shape[1]), w.dtype).at[row0:row0 + w.shape[0]].set(w)


def _prepare_layer(p, l):
    w_in_t = p["w_in"][l].T
    b0 = 2 * A_WIDTH
    q0 = b0 + B_PROJ
    w_in_p = jnp.concatenate([
        w_in_t[:b0 + 3 * B_WIDTH],
        w_in_t[q0:],
        w_in_t[b0 + 3 * B_WIDTH:q0],
        jnp.zeros((IN_COLS_PAD - COL_L - 3 * LORA, D_MODEL), w_in_t.dtype),
    ], axis=0).astype(BF16)
    mu = p["rk_mu"][l]
    vec_rows = [mu[:B_WIDTH], mu[B_WIDTH:2 * B_WIDTH], mu[2 * B_WIDTH:3 * B_WIDTH], p["rk_w0"][l],
                p["rk_a0"][l], p["rk_kk"][l], p["rk_ka"][l], p["rk_rk"][l].reshape(B_WIDTH),
                p["rk_lnx_g"][l], p["rk_lnx_b"][l]]
    vecs = jnp.zeros((16, B_WIDTH), F32).at[:len(vec_rows)].set(jnp.stack(vec_rows))
    mu_l = jnp.zeros((SUBLANES, LORA_PAD), F32).at[0, :3 * LORA].set(mu[3 * B_WIDTH:])
    head_of = jnp.arange(LANES) // B_HEAD_DIM
    seg_ones = (head_of[:, None] == head_of[None, :]).astype(BF16)
    tril = jnp.tril(jnp.ones((CHUNK, CHUNK), bool))
    ws = jnp.where(tril[None], p["gm_ws"][l], 0.0)
    bs = p["gm_bs"][l]
    return {
        "g_mix_pre": p["norm_mix_pre"][l][None], "g_mix_post": p["norm_mix_post"][l][None],
        "g_ffn_pre": p["norm_ffn_pre"][l][None], "g_ffn_post": p["norm_ffn_post"][l][None],
        "g_mem": p["norm_mem"][l][None],
        "w_in": w_in_p,
        "w_out": p["w_out"][l].astype(BF16),
        "w_mkv": jnp.concatenate([p["w_mem_k"][l], p["w_mem_v"][l]], axis=1).astype(BF16),
        "gm_ln_g": p["gm_ln_g"][l][None], "gm_ln_b": p["gm_ln_b"][l][None],
        "gm_ws": ws, "gm_bs": bs,
        "rk_vecs": vecs, "rk_mu_l": mu_l,
        "rk_w2": _pad_rows(p["rk_w2"][l], 0, LORA_PAD).astype(BF16),
        "rk_a2": _pad_rows(p["rk_a2"][l], LORA, LORA_PAD).astype(BF16),
        "rk_g2": _pad_rows(p["rk_g2"][l], 2 * LORA, LORA_PAD).astype(BF16),
        "seg_ones": seg_ones,
        "w_up": p["ffn_w_up"][l].astype(BF16),
        "conv_w": p["ffn_conv_w"][l], "conv_b": p["ffn_conv_b"][l][None],
        "w_down": p["ffn_w_down"][l].astype(BF16),
    }


def _spatial_weights(lw, seq):
    ws, bs = lw["gm_ws"], lw["gm_bs"]
    if seq >= CHUNK:
        w_blk, b_rows = ws, bs
    else:
        reps = CHUNK // seq
        eye = jnp.eye(reps, dtype=ws.dtype)
        w_blk = jnp.einsum("ab,hts->hatbs", eye, ws[:, :seq, :seq]).reshape(A_HEADS, CHUNK, CHUNK)
        b_rows = jnp.tile(bs[:, :seq], (1, reps))
    bias_full = jnp.repeat(b_rows.T, A_HEAD_DIM, axis=1)
    return w_blk.astype(BF16), bias_full


def _decoder_layer(x, mem_k, mem_v, shift_prev, wkv_prev, conv_prev, lw):
    batch, seq, _ = x.shape
    x2 = x.reshape(batch * seq, D_MODEL)
    proj = _norm_matmul(x2, lw["g_mix_pre"], lw["w_in"], tm=1024, tn=PROJ_TN, w_is_nk=True)
    w_sp, bias_full = _spatial_weights(lw, seq)
    a_out, a_v = _group_a(proj, lw["gm_ln_g"], lw["gm_ln_b"], w_sp, bias_full)
    shift_parts = [
        shift_prev[:, None, :B_WIDTH], shift_prev[:, None, B_WIDTH:2 * B_WIDTH],
        shift_prev[:, None, 2 * B_WIDTH:3 * B_WIDTH],
        jnp.pad(shift_prev[:, None, 3 * B_WIDTH:], ((0, 0), (0, 0), (0, LORA_PAD - 3 * LORA))),
    ]
    b_out, wkv_new = _rwkv(proj, shift_parts, wkv_prev, lw, batch=batch, seq=seq)
    c_out = _attention(proj, mem_k, mem_v, batch=batch, seq=seq)
    x1, h = _out_proj(a_out, b_out, c_out, lw["w_out"], x2, lw["g_mix_post"], lw["g_ffn_pre"])
    act, conv_g, conv_v = _ffn_up(h, lw["w_up"], lw["conv_w"], lw["conv_b"], conv_prev, batch=batch, seq=seq)
    y = _ffn_down(act, lw["w_down"], x1, lw["g_ffn_post"])

    chunk_start = ((seq - 1) // CHUNK) * CHUNK
    chunk_v = a_v.reshape(batch, seq, A_WIDTH)[:, chunk_start:].reshape(batch, -1, A_HEADS, A_HEAD_DIM)
    last = proj.reshape(batch, seq, IN_COLS_PAD)[:, -1]
    shift_new = jnp.concatenate([last[:, COL_R:COL_R + 3 * B_WIDTH], last[:, COL_L:COL_L + 3 * LORA]], axis=1)
    conv_new = jnp.concatenate([conv_g, conv_v], axis=-1)
    return y.reshape(batch, seq, D_MODEL), chunk_v, shift_new, wkv_new, conv_new


def kernel(x_prompt, x_sample, mem_prompt, cache_mem_k, cache_mem_v, state_shift, state_wkv, state_conv,
           norm_mix_pre, norm_mix_post, norm_ffn_pre, norm_ffn_post, norm_mem, w_in, w_out, w_mem_k, w_mem_v,
           gm_ln_g, gm_ln_b, gm_ws, gm_bs, rk_mu, rk_w0, rk_w2, rk_a0, rk_a2, rk_g2, rk_kk, rk_ka, rk_rk,
           rk_lnx_g, rk_lnx_b, ffn_w_up, ffn_conv_w, ffn_conv_b, ffn_w_down):
    params = dict(
        norm_mix_pre=norm_mix_pre, norm_mix_post=norm_mix_post, norm_ffn_pre=norm_ffn_pre,
        norm_ffn_post=norm_ffn_post, norm_mem=norm_mem, w_in=w_in, w_out=w_out, w_mem_k=w_mem_k,
        w_mem_v=w_mem_v, gm_ln_g=gm_ln_g, gm_ln_b=gm_ln_b, gm_ws=gm_ws, gm_bs=gm_bs, rk_mu=rk_mu,
        rk_w0=rk_w0, rk_w2=rk_w2, rk_a0=rk_a0, rk_a2=rk_a2, rk_g2=rk_g2, rk_kk=rk_kk, rk_ka=rk_ka,
        rk_rk=rk_rk, rk_lnx_g=rk_lnx_g, rk_lnx_b=rk_lnx_b, ffn_w_up=ffn_w_up, ffn_conv_w=ffn_conv_w,
        ffn_conv_b=ffn_conv_b, ffn_w_down=ffn_w_down)
    depth = w_in.shape[0]
    bp = x_prompt.shape[0]
    y_p, y_s = x_prompt, x_sample
    outs = [[] for _ in range(10)]
    for l in range(depth):
        lw = _prepare_layer(params, l)
        mem2 = mem_prompt.reshape(bp * MEM_LEN, D_MODEL)
        mkv = _norm_matmul(mem2, lw["g_mem"], lw["w_mkv"], tm=512, tn=2 * C_WIDTH)
        mk = mkv[:, :C_WIDTH].reshape(bp, MEM_LEN, C_WIDTH)
        mv = mkv[:, C_WIDTH:].reshape(bp, MEM_LEN, C_WIDTH)
        zero_shift = jnp.zeros((bp, B_PROJ), x_prompt.dtype)
        zero_wkv = jnp.zeros((bp, B_HEADS, B_HEAD_DIM, B_HEAD_DIM), F32)
        zero_conv = jnp.zeros((bp, 2, 2 * D_FF), x_prompt.dtype)
        y_p, cv, sh, wkv, conv = _decoder_layer(y_p, mk, mv, zero_shift, zero_wkv, zero_conv, lw)
        mem_shape = (bp, MEM_LEN, C_HEADS, C_HEAD_DIM)
        for lst, val in zip(outs[:6], (mk.reshape(mem_shape), mv.reshape(mem_shape), cv, sh, wkv, conv)):
            lst.append(val)
        n_s = x_sample.shape[0]
        cache_k = cache_mem_k[l].reshape(n_s, MEM_LEN * C_HEADS, C_HEAD_DIM)
        cache_v = cache_mem_v[l].reshape(n_s, MEM_LEN * C_HEADS, C_HEAD_DIM)
        y_s, cv, sh, wkv, conv = _decoder_layer(y_s, cache_k, cache_v, state_shift[l],
                                                state_wkv[l], state_conv[l], lw)
        for lst, val in zip(outs[6:], (cv, sh, wkv, conv)):
            lst.append(val)
    return (y_p, y_s) + tuple(jnp.stack(o) for o in outs)
```

```python
import functools
import math

import jax
import jax.numpy as jnp
from jax import lax
from jax.experimental import pallas as pl
from jax.experimental.pallas import tpu as pltpu

D_MODEL = 2048
MEM_LEN = 256
CHUNK = 128
A_HEADS, A_HEAD_DIM = 4, 128
A_WIDTH = A_HEADS * A_HEAD_DIM
B_HEADS, B_HEAD_DIM = 16, 64
B_WIDTH = B_HEADS * B_HEAD_DIM
LORA = 64
B_PROJ = 3 * B_WIDTH + 3 * LORA
C_HEADS, C_HEAD_DIM = 4, 128
C_WIDTH = C_HEADS * C_HEAD_DIM
D_FF = 5632
RMS_EPS = 1e-6
LN_EPS = 1e-5
GN_EPS = 64e-5
DECAY_OFFSET = 0.5

LANES = 128
SUBLANES = 8
VMEM_LIMIT_BYTES = 56 * 1024 * 1024

LORA_PAD = 256
COL_A = 0
COL_R = 2 * A_WIDTH
MAIN_COLS = COL_R + 3 * B_WIDTH
COL_Q = 0
COL_L = C_WIDTH
TAIL_COLS = COL_L + LORA_PAD
PROJ_TN = 1024

WKV_CHUNK = 64
FFN_N_TILES = 2
FFN_TN = D_FF // FFN_N_TILES
FFN_SUB = 256

F32 = jnp.float32
BF16 = jnp.bfloat16


def _cparams(sem):
    return pltpu.CompilerParams(dimension_semantics=sem, vmem_limit_bytes=VMEM_LIMIT_BYTES)


def _rms(x, g):
    return x * lax.rsqrt(jnp.mean(x * x, axis=-1, keepdims=True) + RMS_EPS) * g


def _norm_matmul_kernel(x_ref, g_ref, w_ref, o_ref, h_ref):
    @pl.when(pl.program_id(1) == 0)
    def _():
        h_ref[...] = _rms(x_ref[...], g_ref[...]).astype(BF16)

    o_ref[...] = jnp.dot(h_ref[...], w_ref[...], preferred_element_type=F32)


def _norm_matmul(x, g, w, *, tm, tn):
    t, k = x.shape
    n = w.shape[1]
    return pl.pallas_call(
        _norm_matmul_kernel,
        grid=(t // tm, n // tn),
        in_specs=[
            pl.BlockSpec((tm, k), lambda i, j: (i, 0)),
            pl.BlockSpec((1, k), lambda i, j: (0, 0)),
            pl.BlockSpec((k, tn), lambda i, j: (0, j)),
        ],
        out_specs=pl.BlockSpec((tm, tn), lambda i, j: (i, j)),
        out_shape=jax.ShapeDtypeStruct((t, n), F32),
        scratch_shapes=[pltpu.VMEM((tm, k), BF16)],
        compiler_params=_cparams(("parallel", "arbitrary")),
        name="norm_matmul",
    )(x, g, w)


def _in_proj_kernel(x_ref, g_ref, wm_ref, wt_ref, om_ref, ot_ref, h_ref, *, n_main):
    j = pl.program_id(1)

    @pl.when(j == 0)
    def _():
        h_ref[...] = _rms(x_ref[...], g_ref[...]).astype(BF16)

    nt = (((1,), (1,)), ((), ()))

    @pl.when(j < n_main)
    def _():
        om_ref[...] = lax.dot_general(h_ref[...], wm_ref[...], nt, preferred_element_type=F32)

    @pl.when(j == n_main)
    def _():
        ot_ref[...] = lax.dot_general(h_ref[...], wt_ref[...], nt, preferred_element_type=F32)


def _in_proj(x, g, w_main, w_tail, *, tm=1024):
    t, k = x.shape
    n_main = MAIN_COLS // PROJ_TN
    main_tile = lambda j: jnp.minimum(j, n_main - 1)
    return pl.pallas_call(
        functools.partial(_in_proj_kernel, n_main=n_main),
        grid=(t // tm, n_main + 1),
        in_specs=[
            pl.BlockSpec((tm, k), lambda i, j: (i, 0)),
            pl.BlockSpec((1, k), lambda i, j: (0, 0)),
            pl.BlockSpec((PROJ_TN, k), lambda i, j: (main_tile(j), 0)),
            pl.BlockSpec((TAIL_COLS, k), lambda i, j: (0, 0)),
        ],
        out_specs=[
            pl.BlockSpec((tm, PROJ_TN), lambda i, j: (i, main_tile(j))),
            pl.BlockSpec((tm, TAIL_COLS), lambda i, j: (i, 0)),
        ],
        out_shape=[jax.ShapeDtypeStruct((t, MAIN_COLS), F32), jax.ShapeDtypeStruct((t, TAIL_COLS), F32)],
        scratch_shapes=[pltpu.VMEM((tm, k), BF16)],
        compiler_params=_cparams(("parallel", "arbitrary")),
        name="in_proj",
    )(x, g, w_main, w_tail)


def _group_a_kernel(u_ref, v_ref, g_ref, b_ref, w_ref, bias_ref, o_ref, vout_ref):
    u = jax.nn.gelu(u_ref[...], approximate=True)
    v = jax.nn.gelu(v_ref[...], approximate=True)
    mean = jnp.mean(v, axis=-1, keepdims=True)
    d = v - mean
    var = jnp.mean(d * d, axis=-1, keepdims=True)
    vn = d * lax.rsqrt(var + LN_EPS) * g_ref[...] + b_ref[...]
    vout_ref[...] = vn
    vb = vn.astype(BF16)
    for c in range(u.shape[0] // CHUNK):
        rs = slice(c * CHUNK, (c + 1) * CHUNK)
        for h in range(A_HEADS):
            hs = slice(h * A_HEAD_DIM, (h + 1) * A_HEAD_DIM)
            mixed = jnp.dot(w_ref[h], vb[rs, hs], preferred_element_type=F32) + bias_ref[:, hs]
            o_ref[rs, hs] = (u[rs, hs] * mixed).astype(BF16)


def _group_a(proj, ln_g, ln_b, w_sp, bias_full, *, rows=4 * CHUNK):
    t = proj.shape[0]
    return pl.pallas_call(
        _group_a_kernel,
        grid=(t // rows,),
        in_specs=[
            pl.BlockSpec((rows, A_WIDTH), lambda i: (i, COL_A // A_WIDTH)),
            pl.BlockSpec((rows, A_WIDTH), lambda i: (i, COL_A // A_WIDTH + 1)),
            pl.BlockSpec((1, A_WIDTH), lambda i: (0, 0)),
            pl.BlockSpec((1, A_WIDTH), lambda i: (0, 0)),
            pl.BlockSpec((A_HEADS, CHUNK, CHUNK), lambda i: (0, 0, 0)),
            pl.BlockSpec((CHUNK, A_WIDTH), lambda i: (0, 0)),
        ],
        out_specs=[
            pl.BlockSpec((rows, A_WIDTH), lambda i: (i, 0)),
            pl.BlockSpec((rows, A_WIDTH), lambda i: (i, 0)),
        ],
        out_shape=[
            jax.ShapeDtypeStruct((t, A_WIDTH), BF16),
            jax.ShapeDtypeStruct((t, A_WIDTH), F32),
        ],
        compiler_params=_cparams(("parallel",)),
        name="group_a",
    )(proj, proj, ln_g, ln_b, w_sp, bias_full)


def _softmax_rows(s):
    e = jnp.exp(s - jnp.max(s, axis=-1, keepdims=True))
    return e / jnp.sum(e, axis=-1, keepdims=True)


def _attn_kernel(q_ref, k_ref, v_ref, o_ref, *, n_b, tq, head_major_rows):
    scale = C_HEAD_DIM ** -0.5
    heads = range(C_HEADS)
    lanes = lambda h: slice(h * C_HEAD_DIM, (h + 1) * C_HEAD_DIM)
    if not head_major_rows:
        chains = [(g, h) for g in range(n_b) for h in heads]
        rows = lambda g: slice(g * tq, (g + 1) * tq)
        s = [lax.dot_general(q_ref[rows(g), lanes(h)].astype(BF16), k_ref[g, :, lanes(h)].astype(BF16), _NT,
                             preferred_element_type=F32) * scale for g, h in chains]
        p = [_softmax_rows(m).astype(BF16) for m in s]
        o = [jnp.dot(m, v_ref[g, :, lanes(h)].astype(BF16), preferred_element_type=F32)
             for m, (g, h) in zip(p, chains)]
        for m, (g, h) in zip(o, chains):
            o_ref[rows(g), lanes(h)] = m.astype(BF16)
        return
    row = lax.broadcasted_iota(jnp.int32, (C_HEADS * tq, 1), 0)
    col = lax.broadcasted_iota(jnp.int32, (1, C_HEADS * MEM_LEN), 1)
    own = (col & (C_HEADS - 1)) == (row >> int(math.log2(tq)))
    q = [q_ref[g * tq:(g + 1) * tq, :] for g in range(n_b)]
    qs = [jnp.concatenate([m[:, lanes(h)] for h in heads], axis=0).astype(BF16) for m in q]
    s = [lax.dot_general(m, k_ref[g].astype(BF16), _NT, preferred_element_type=F32) * scale
         for g, m in enumerate(qs)]
    p = [_softmax_rows(jnp.where(own, m, -1e30)).astype(BF16) for m in s]
    o = [jnp.dot(m, v_ref[g].astype(BF16), preferred_element_type=F32) for g, m in enumerate(p)]
    for g, m in enumerate(o):
        for h in heads:
            o_ref[g * tq:(g + 1) * tq, lanes(h)] = m[h * tq:(h + 1) * tq].astype(BF16)


def _attention(proj, mem_k, mem_v, *, batch, seq):
    head_major_rows = mem_k.shape[-1] == C_HEAD_DIM
    if head_major_rows:
        tq, n_b = seq, 8
    else:
        tq, n_b = 512, 1
    n_q = seq // tq
    mem_blk = (n_b,) + mem_k.shape[1:]
    return pl.pallas_call(
        functools.partial(_attn_kernel, n_b=n_b, tq=tq, head_major_rows=head_major_rows),
        grid=(batch // n_b, n_q),
        in_specs=[
            pl.BlockSpec((n_b * tq, C_WIDTH), lambda b, i: (b * n_q + i, COL_Q // C_WIDTH)),
            pl.BlockSpec(mem_blk, lambda b, i: (b, 0, 0)),
            pl.BlockSpec(mem_blk, lambda b, i: (b, 0, 0)),
        ],
        out_specs=pl.BlockSpec((n_b * tq, C_WIDTH), lambda b, i: (b * n_q + i, 0)),
        out_shape=jax.ShapeDtypeStruct((batch * seq, C_WIDTH), BF16),
        compiler_params=_cparams(("parallel", "arbitrary")),
        name="mem_attention",
    )(proj, mem_k, mem_v)


_V_MU_R, _V_MU_K, _V_MU_V, _V_W0, _V_A0, _V_KK, _V_KA, _V_RK, _V_LNG, _V_LNB = range(10)


_NN = (((1,), (0,)), ((), ()))
_NT = (((1,), (1,)), ((), ()))
_TN = (((0,), (0,)), ((), ()))
HEAD_PAIRS = B_HEADS // 2


def _split(x):
    hi = x.astype(BF16)
    return hi, (x - hi.astype(F32)).astype(BF16)


def _b(x):
    return x.astype(BF16)


def _bdot(a, b, dims=_NN):
    return lax.dot_general(a, b, dims, preferred_element_type=F32)


def _rwkv_kernel(pr_ref, pk_ref, pv_ref, pl_ref, sr_ref, sk_ref, sv_ref, sl_ref, wkv_ref,
                 vec_ref, mul_ref, w2_ref, a2_ref, g2_ref, seg_ref,
                 o_ref, so_ref, nr_ref, nk_ref, nv_ref, nl_ref,
                 cr_ref, ck_ref, cv_ref, cl_ref, sbd_ref, *, n_blk, groups, chunk, whole_seq):
    rows = groups * chunk
    first = pl.program_id(1) == 0
    last = pl.program_id(1) == pl.num_programs(1) - 1
    hd = B_HEAD_DIM
    n_seq = n_blk * groups

    @pl.when(first)
    def _():
        zero = jnp.zeros((hd, hd), F32)
        for g in range(n_seq):
            for q in range(HEAD_PAIRS):
                top = jnp.concatenate([wkv_ref[g, 2 * q], zero], axis=1)
                bot = jnp.concatenate([zero, wkv_ref[g, 2 * q + 1]], axis=1)
                sbd_ref[g, q] = jnp.concatenate([top, bot], axis=0)
        if not whole_seq:
            for s_ref, carry_ref in ((sr_ref, cr_ref), (sk_ref, ck_ref), (sv_ref, cv_ref), (sl_ref, cl_ref)):
                for blk in range(n_blk):
                    carry_ref[blk] = jnp.broadcast_to(s_ref[blk], (SUBLANES, s_ref.shape[-1]))

    row_id = lax.broadcasted_iota(jnp.int32, (rows, 1), 0)
    col_id = lax.broadcasted_iota(jnp.int32, (1, rows), 1)
    chunk_bits = int(math.log2(chunk))
    incl_b = (((row_id >> chunk_bits) == (col_id >> chunk_bits)) & (col_id <= row_id)).astype(BF16)
    vec = lambda i: vec_ref[i:i + 1, :]
    seg_ones = seg_ref[...]

    def seg_sum(x):
        slabs = jnp.concatenate([x[:, q * LANES:(q + 1) * LANES] for q in range(HEAD_PAIRS)], axis=0)
        hi, lo = _split(slabs)
        s = (jnp.dot(hi, seg_ones, preferred_element_type=F32)
             + jnp.dot(lo, seg_ones, preferred_element_type=F32))
        return jnp.concatenate([s[q * rows:(q + 1) * rows] for q in range(HEAD_PAIRS)], axis=1)

    def prologue(blk, out):
        seqs = slice(blk * groups, (blk + 1) * groups)

        def prev_rows(x, s_ref, carry_ref):
            width = x.shape[1]
            if whole_seq:
                start = jnp.broadcast_to(s_ref[seqs], (groups, chunk, width)).reshape(rows, width)
                return jnp.where((row_id & (chunk - 1)) == 0, start, pltpu.roll(x, 1, 0))

            ext = jnp.concatenate([carry_ref[blk], x], axis=0)
            prev = pltpu.roll(ext, 1, 0)[SUBLANES:]
            carry_ref[blk] = x[rows - SUBLANES:]
            return prev

        def shifted(p_ref, s_ref, carry_ref, n_ref, mu):
            x = p_ref[blk]
            if whole_seq:
                n_ref[seqs] = x.reshape(groups, chunk, x.shape[1])[:, chunk - 1:, :]
            else:
                n_ref[blk] = x[rows - 1:]
            return x + (prev_rows(x, s_ref, carry_ref) - x) * mu

        lo = shifted(pl_ref, sl_ref, cl_ref, nl_ref, mul_ref[0:1, :])
        dw = jnp.dot(jnp.tanh(lo).astype(BF16), w2_ref[...], preferred_element_type=F32)
        da = jnp.dot(lo.astype(BF16), a2_ref[...], preferred_element_type=F32)
        gate = jnp.dot(jax.nn.sigmoid(lo).astype(BF16), g2_ref[...], preferred_element_type=F32)
        yield
        w_log = -jax.nn.softplus(-(vec(_V_W0) + dw)) - DECAY_OFFSET
        log_decay = -jnp.exp(w_log)
        ld_hi, ld_mid = _split(log_decay)
        ld_lo = (log_decay - ld_hi.astype(F32) - ld_mid.astype(F32)).astype(BF16)
        cum = (jnp.dot(incl_b, ld_hi, preferred_element_type=F32)
               + jnp.dot(incl_b, ld_mid, preferred_element_type=F32)
               + jnp.dot(incl_b, ld_lo, preferred_element_type=F32))
        yield
        a = jax.nn.sigmoid(vec(_V_A0) + da)
        k = shifted(pk_ref, sk_ref, ck_ref, nk_ref, vec(_V_MU_K))
        kk = k * vec(_V_KK)
        kk = kk / jnp.maximum(jnp.sqrt(seg_sum(kk * kk)), 1e-12)
        yield
        p_incl = jnp.exp(cum)
        p_inv = jnp.exp(-cum)
        a_t = -kk * jnp.exp(cum - log_decay)
        b_t = kk * a * p_inv
        yield
        k = k * (1.0 + (a - 1.0) * vec(_V_KA))
        k_t = k * p_inv
        r = shifted(pr_ref, sr_ref, cr_ref, nr_ref, vec(_V_MU_R))
        r_t = r * p_incl
        yield
        v = shifted(pv_ref, sv_ref, cv_ref, nv_ref, vec(_V_MU_V))
        bonus = seg_sum(r * k * vec(_V_RK)) * v
        out.update(a_t=a_t, b_t=b_t, k_t=k_t, r_t=r_t, v=v, p_incl=p_incl, bonus=bonus, gate=gate)

    pr = 2 * rows
    row2 = lax.broadcasted_iota(jnp.int32, (pr, 1), 0)
    col2 = lax.broadcasted_iota(jnp.int32, (1, pr), 1)
    t2, s2 = row2 & (rows - 1), col2 & (rows - 1)
    same2 = ((row2 >> chunk_bits) == (col2 >> chunk_bits))
    incl2 = same2 & (s2 <= t2)
    strict2 = same2 & (s2 < t2)
    eye2 = (row2 == col2).astype(F32)
    left = lax.broadcasted_iota(jnp.int32, (rows, LANES), 1) < hd

    def bd(x):
        zero = jnp.zeros_like(x)
        return jnp.concatenate([jnp.where(left, x, zero), jnp.where(left, zero, x)], axis=0)

    def group_rows(mats, g):
        starts = [hh * rows + g * chunk for hh in (0, 1)]
        return jnp.concatenate([m[i:i + chunk] for m in mats for i in starts], axis=0)

    n_sq = chunk_bits - 1
    col4 = lax.broadcasted_iota(jnp.int32, (1, 2 * pr), 1)
    incl4 = ((row2 >> chunk_bits) == ((col4 & (pr - 1)) >> chunk_bits)) & ((col4 & (rows - 1)) <= t2)
    pairs = range(HEAD_PAIRS)
    lanes_of = [slice(q * LANES, (q + 1) * LANES) for q in pairs]

    def recurrence(blk, pro, qs):
        s0 = blk * groups
        p_incl = pro["p_incl"]
        pairs = range(len(qs))
        bds = [[bd(pro[name][:, lanes_of[q]]) for name in ("a_t", "r_t", "b_t", "k_t", "v")] for q in qs]
        ar_s = [_b(jnp.concatenate([m[0], m[1]], axis=0)) for m in bds]
        bk_s = [_b(jnp.concatenate([m[2], m[3]], axis=0)) for m in bds]
        v_s = [_b(m[4]) for m in bds]
        if groups == 1:
            ms = [_bdot(ar_s[q], jnp.concatenate([bk_s[q], _b(sbd_ref[s0, qs[q]])], axis=0), _NT)
                  for q in pairs]
        else:
            ms = [_bdot(ar_s[q], bk_s[q], _NT) for q in pairs]
        yield
        a_ab = [jnp.where(strict2, m[:pr, :pr], 0.0) for m in ms]
        a_ak = [jnp.where(strict2, m[:pr, pr:2 * pr], 0.0) for m in ms]
        a_r = [jnp.where(incl4, m[pr:, :2 * pr], 0.0) for m in ms]
        inv = [eye2 + n for n in a_ab]
        pw_s = [_b(_bdot(_b(n), _b(n))) for n in a_ab]
        yield
        for _ in range(1, n_sq):
            both = [_bdot(p, jnp.concatenate([p, _b(i)], axis=1)) for p, i in zip(pw_s, inv)]
            inv = [i + m[:, pr:] for i, m in zip(inv, both)]
            pw_s = [_b(m[:, :pr]) for m in both]
            yield
        inv = [i + _bdot(p, _b(i)) for i, p in zip(inv, pw_s)]
        yield
        if groups == 1:
            x0, y0 = [m[:pr, 2 * pr:] for m in ms], [m[pr:, 2 * pr:] for m in ms]
        else:
            x0, y0 = [], []
            for q in pairs:
                x_parts, y_parts = [None] * (2 * groups), [None] * (2 * groups)
                for g in range(groups):
                    xy = _bdot(_b(group_rows(bds[q][:2], g)), _b(sbd_ref[s0 + g, qs[q]]), _NT)
                    for hh in (0, 1):
                        x_parts[hh * groups + g] = xy[hh * chunk:(hh + 1) * chunk]
                        y_parts[hh * groups + g] = xy[(2 + hh) * chunk:(3 + hh) * chunk]
                x0.append(jnp.concatenate(x_parts, axis=0))
                y0.append(jnp.concatenate(y_parts, axis=0))
        yield
        rhs = [_b(x0[q] + _bdot(_b(a_ak[q]), v_s[q])) for q in pairs]
        u = [_bdot(_b(inv[q]), rhs[q]) for q in pairs]
        yield
        uv_s = [jnp.concatenate([_b(u[q]), v_s[q]], axis=0) for q in pairs]
        y2 = [y0[q] + _bdot(_b(a_r[q]), uv_s[q]) for q in pairs]
        for q in pairs:
            pro["y"][qs[q]] = y2[q][:rows] + y2[q][rows:]
        yield
        for q in pairs:
            ls = lanes_of[qs[q]]
            if groups == 1:
                ds = _bdot(uv_s[q], bk_s[q], _TN)
                sbd_ref[s0, qs[q]] = (sbd_ref[s0, qs[q]] + ds) * p_incl[rows - 1:rows, ls]
            else:
                for g in range(groups):
                    ds = _bdot(_b(group_rows((u[q], bds[q][4]), g)),
                               _b(group_rows((bds[q][2], bds[q][3]), g)), _TN)
                    end = (g + 1) * chunk - 1
                    sbd_ref[s0 + g, qs[q]] = (sbd_ref[s0 + g, qs[q]] + ds) * p_incl[end:end + 1, ls]

    def finish(blk, pro):
        inv_n = 1.0 / B_HEAD_DIM
        y = jnp.concatenate([pro["y"][q] for q in range(HEAD_PAIRS)], axis=1)
        d = y - seg_sum(y) * inv_n
        yield
        var = seg_sum(d * d) * inv_n
        yield
        yn = d * lax.rsqrt(var + GN_EPS) * vec(_V_LNG) + vec(_V_LNB)
        o_ref[blk] = ((yn + pro["bonus"]) * pro["gate"]).astype(BF16)

    def interleave(*gens):
        live = list(gens)
        while live:
            for gen in list(live):
                if next(gen, StopIteration) is StopIteration:
                    live.remove(gen)

    def chain(*gens):
        for gen in gens:
            yield from gen

    pair_groups = [list(pairs)]
    pros = [dict(y={}) for _ in range(n_blk)]
    interleave(prologue(0, pros[0]))
    for blk in range(n_blk):
        work = [chain(*(recurrence(blk, pros[blk], qs) for qs in pair_groups))]
        if blk + 1 < n_blk:
            work.append(prologue(blk + 1, pros[blk + 1]))
        if blk > 0:
            work.append(finish(blk - 1, pros[blk - 1]))
        interleave(*work)
    interleave(finish(n_blk - 1, pros[n_blk - 1]))

    @pl.when(last)
    def _():
        for g in range(n_seq):
            for q in range(HEAD_PAIRS):
                sq = sbd_ref[g, q]
                so_ref[g, 2 * q] = sq[:hd, :hd]
                so_ref[g, 2 * q + 1] = sq[hd:, hd:]


def _rwkv(proj, proj_tail, shift_parts, wkv_prev, lw, *, batch, seq):
    whole_seq = seq <= WKV_CHUNK
    if whole_seq:
        assert seq == SUBLANES, "whole-sequence blocks rely on one sublane tile per sequence"
        chunk, groups = seq, WKV_CHUNK // seq
        n_blk, n_chunks = 2, 1
        n_outer = batch // (groups * n_blk)
        lead = (batch // groups, WKV_CHUNK)
    else:
        chunk, groups = WKV_CHUNK, 1
        n_blk, n_outer, n_chunks = batch, 1, seq // WKV_CHUNK
        lead = (batch, seq)
    proj3, tail3 = proj.reshape(lead + (MAIN_COLS,)), proj_tail.reshape(lead + (TAIL_COLS,))
    n_seq = n_blk * groups
    act = lambda col_blk, width: pl.BlockSpec((n_blk, WKV_CHUNK, width), lambda o, c: (o, c, col_blk))
    per_seq = lambda *tail: pl.BlockSpec((n_seq,) + tail, lambda o, c: (o,) + (0,) * len(tail))
    fixed = lambda *shape: pl.BlockSpec(shape, lambda o, c: (0,) * len(shape))
    state = (B_HEADS, B_HEAD_DIM, B_HEAD_DIM)
    kern = functools.partial(_rwkv_kernel, n_blk=n_blk, groups=groups, chunk=chunk, whole_seq=whole_seq)
    shift_specs = [per_seq(1, B_WIDTH), per_seq(1, B_WIDTH), per_seq(1, B_WIDTH), per_seq(1, LORA_PAD)]
    b_out, wkv_new, *last_rows = pl.pallas_call(
        kern,
        grid=(n_outer, n_chunks),
        in_specs=[
            act(COL_R // B_WIDTH, B_WIDTH), act(COL_R // B_WIDTH + 1, B_WIDTH),
            act(COL_R // B_WIDTH + 2, B_WIDTH), act(COL_L // LORA_PAD, LORA_PAD),
            *shift_specs,
            per_seq(*state),
            fixed(16, B_WIDTH), fixed(SUBLANES, LORA_PAD),
            fixed(LORA_PAD, B_WIDTH), fixed(LORA_PAD, B_WIDTH), fixed(LORA_PAD, B_WIDTH),
            fixed(LANES, LANES),
        ],
        out_specs=[act(0, B_WIDTH), per_seq(*state), *shift_specs],
        out_shape=[
            jax.ShapeDtypeStruct(lead + (B_WIDTH,), BF16),
            jax.ShapeDtypeStruct((batch,) + state, F32),
            *(jax.ShapeDtypeStruct(p.shape, F32) for p in shift_parts),
        ],
        scratch_shapes=[
            pltpu.VMEM((n_blk, SUBLANES, B_WIDTH), F32),
            pltpu.VMEM((n_blk, SUBLANES, B_WIDTH), F32),
            pltpu.VMEM((n_blk, SUBLANES, B_WIDTH), F32),
            pltpu.VMEM((n_blk, SUBLANES, LORA_PAD), F32),
            pltpu.VMEM((n_seq, HEAD_PAIRS, LANES, LANES), F32),
        ],
        compiler_params=_cparams(("parallel", "arbitrary")),
        name="rwkv7",
    )(proj3, proj3, proj3, tail3, *shift_parts, wkv_prev,
      lw["rk_vecs"], lw["rk_mu_l"], lw["rk_w2"], lw["rk_a2"], lw["rk_g2"], lw["seg_ones"])
    shift_new = jnp.concatenate([p[:, 0] for p in last_rows[:3]] + [last_rows[3][:, 0, :3 * LORA]], axis=1)
    return b_out.reshape(batch * seq, B_WIDTH), wkv_new, shift_new


def _out_proj_kernel(a_ref, b_ref, c_ref, w_ref, x_ref, gpost_ref, gffn_ref, x1_ref, h_ref):
    n_sub = 4
    sub = x_ref.shape[0] // n_sub
    halves = [slice(s * sub, (s + 1) * sub) for s in range(n_sub)]

    def project(rs):
        acc = jnp.dot(a_ref[rs, :], w_ref[0:A_WIDTH, :], preferred_element_type=F32)
        acc += jnp.dot(b_ref[rs, :], w_ref[A_WIDTH:A_WIDTH + B_WIDTH, :], preferred_element_type=F32)
        return acc + jnp.dot(c_ref[rs, :], w_ref[A_WIDTH + B_WIDTH:, :], preferred_element_type=F32)

    accs = [project(rs) for rs in halves]
    for rs, acc in zip(halves, accs):
        x1 = x_ref[rs, :] + _rms(acc, gpost_ref[...])
        x1_ref[rs, :] = x1
        h_ref[rs, :] = _rms(x1, gffn_ref[...]).astype(BF16)


def _out_proj(a_out, b_out, c_out, w_out, x, g_post, g_ffn, *, tm=512):
    t = x.shape[0]
    row = lambda i: (i, 0)
    fixed = lambda i: (0, 0)
    return pl.pallas_call(
        _out_proj_kernel,
        grid=(t // tm,),
        in_specs=[
            pl.BlockSpec((tm, A_WIDTH), row),
            pl.BlockSpec((tm, B_WIDTH), row),
            pl.BlockSpec((tm, C_WIDTH), row),
            pl.BlockSpec((D_MODEL, D_MODEL), fixed, pipeline_mode=pl.Buffered(1)),
            pl.BlockSpec((tm, D_MODEL), row),
            pl.BlockSpec((1, D_MODEL), fixed),
            pl.BlockSpec((1, D_MODEL), fixed),
        ],
        out_specs=[pl.BlockSpec((tm, D_MODEL), row), pl.BlockSpec((tm, D_MODEL), row)],
        out_shape=[jax.ShapeDtypeStruct((t, D_MODEL), F32), jax.ShapeDtypeStruct((t, D_MODEL), BF16)],
        compiler_params=_cparams(("parallel",)),
        name="out_proj",
    )(a_out, b_out, c_out, w_out, x, g_post, g_ffn)


def _chunked_dots(h_ref, w_refs, cs, between):
    accs = [None] * len(w_refs)
    for kc in range(h_ref.shape[1] // FFN_SUB):
        ks = slice(kc * FFN_SUB, (kc + 1) * FFN_SUB)
        for n, w_ref in enumerate(w_refs):
            part = jnp.dot(h_ref[:, ks], w_ref[ks, cs], preferred_element_type=F32)
            accs[n] = part if accs[n] is None else accs[n] + part
        between(kc)
    return accs


def _ffn_up_kernel(h_ref, wg_ref, wv_ref, cwg_ref, cwv_ref, cbg_ref, cbv_ref, pg_ref, pv_ref,
                   act_ref, ng_ref, nv_ref, *, tm, seq):
    h = h_ref[...]
    tn = cwg_ref.shape[1]
    n_seq = tm // seq
    tau = lax.broadcasted_iota(jnp.int32, (tm, 1), 0) & (seq - 1)

    def conv(up, cs, cw_ref, cb_ref, p_ref, n_ref):
        width = up.shape[1]
        prev = p_ref[:, :, cs]
        e0 = jnp.broadcast_to(prev[:, 0:1, :], (n_seq, seq, width)).reshape(tm, width)
        e1 = jnp.broadcast_to(prev[:, 1:2, :], (n_seq, seq, width)).reshape(tm, width)
        m1 = jnp.where(tau == 0, e1, pltpu.roll(up, 1, 0))
        m2 = jnp.where(tau == 0, e0, jnp.where(tau == 1, e1, pltpu.roll(up, 2, 0)))
        n_ref[:, :, cs] = up.reshape(n_seq, seq, width)[:, seq - 2:, :]
        return cb_ref[:, cs] + m2 * cw_ref[0:1, cs] + m1 * cw_ref[1:2, cs] + up * cw_ref[2:3, cs]

    subs = [slice(s * FFN_SUB, (s + 1) * FFN_SUB) for s in range(tn // FFN_SUB)]
    ups = [(jnp.dot(h, wg_ref[:, cs], preferred_element_type=F32),
            jnp.dot(h, wv_ref[:, cs], preferred_element_type=F32)) for cs in subs]
    for cs, (up_g, up_v) in zip(subs, ups):
        gate = conv(up_g, cs, cwg_ref, cbg_ref, pg_ref, ng_ref)
        val = conv(up_v, cs, cwv_ref, cbv_ref, pv_ref, nv_ref)
        act_ref[:, cs] = (jax.nn.gelu(gate, approximate=True) * val).astype(BF16)


def _ffn_up_skew_kernel(h_ref, wg_ref, wv_ref, cwg_ref, cwv_ref, cbg_ref, cbv_ref, pg_ref, pv_ref,
                        act_ref, ng_ref, nv_ref, cg_ref, cv_ref, u0_ref, u1_ref,
                        *, tm, tiles_per_seq):
    i = pl.program_id(1)
    tn = cwg_ref.shape[1]
    n_sub = tn // FFN_SUB

    @pl.when(i == 0)
    def _():
        for ref in (u1_ref, cg_ref, cv_ref):
            ref[...] = jnp.zeros_like(ref)

    @pl.when(i % tiles_per_seq == 1 % tiles_per_seq)
    def _():
        for carry_ref, p_ref in ((cg_ref, pg_ref), (cv_ref, pv_ref)):
            carry_ref[...] = jnp.concatenate([jnp.zeros((SUBLANES - 2, tn), F32), p_ref[0]], axis=0)

    piece = 64

    def conv(u_ref, r0, us, cs, cw_ref, cb_ref, carry_ref):
        if r0 == 0:
            ext = jnp.concatenate([carry_ref[:, cs], u_ref[0:piece, us]], axis=0)
        else:
            ext = u_ref[r0 - SUBLANES:r0 + piece, us]
        m1 = pltpu.roll(ext, 1, 0)[SUBLANES:]
        m2 = pltpu.roll(ext, 2, 0)[SUBLANES:]
        return (cb_ref[:, cs] + m2 * cw_ref[0:1, cs] + m1 * cw_ref[1:2, cs]
                + ext[SUBLANES:] * cw_ref[2:3, cs])

    n_k = D_MODEL // FFN_SUB
    chunks_per_piece = n_k // (tm // piece)

    def step(new_ref, old_ref):
        for s in range(n_sub):
            cs = slice(s * FFN_SUB, (s + 1) * FFN_SUB)
            gs, vs = cs, slice(tn + s * FFN_SUB, tn + (s + 1) * FFN_SUB)

            def conv_piece(kc):
                if (kc + 1) % chunks_per_piece == 0:
                    r0 = (kc // chunks_per_piece) * piece
                    gate = conv(old_ref, r0, gs, cs, cwg_ref, cbg_ref, cg_ref)
                    val = conv(old_ref, r0, vs, cs, cwv_ref, cbv_ref, cv_ref)
                    act_ref[r0:r0 + piece, cs] = (jax.nn.gelu(gate, approximate=True) * val).astype(BF16)

            new_ref[:, gs], new_ref[:, vs] = _chunked_dots(h_ref, (wg_ref, wv_ref), cs, conv_piece)
            for us, carry_ref, n_ref in ((gs, cg_ref, ng_ref), (vs, cv_ref, nv_ref)):
                carry_ref[:, cs] = old_ref[tm - SUBLANES:, us]
                n_ref[0, :, cs] = old_ref[tm - 2:, us]

    @pl.when(i % 2 == 0)
    def _():
        step(u0_ref, u1_ref)

    @pl.when(i % 2 == 1)
    def _():
        step(u1_ref, u0_ref)


def _ffn_up_skew(h, w_up, conv_w, conv_b, conv_prev, *, batch, seq, tm=256):
    t = h.shape[0]
    tn, n_tiles = FFN_TN, FFN_N_TILES
    n_m = t // tm
    tiles_per_seq = seq // tm
    prev_tile = lambda i: jnp.maximum(i - 1, 0)
    col = lambda half: (lambda j, i: (0, j + half * n_tiles))
    state_blk = (1, 2, tn)
    state_idx = lambda half: (lambda j, i: (prev_tile(i) // tiles_per_seq, 0, j + half * n_tiles))
    state_out = jax.ShapeDtypeStruct((batch, 2, D_FF), F32)
    up_buf = pltpu.VMEM((tm, 2 * tn), F32)
    return pl.pallas_call(
        functools.partial(_ffn_up_skew_kernel, tm=tm, tiles_per_seq=tiles_per_seq),
        grid=(n_tiles, n_m + 1),
        in_specs=[
            pl.BlockSpec((tm, D_MODEL), lambda j, i: (jnp.minimum(i, n_m - 1), 0)),
            pl.BlockSpec((D_MODEL, tn), col(0), pipeline_mode=pl.Buffered(1)),
            pl.BlockSpec((D_MODEL, tn), col(1), pipeline_mode=pl.Buffered(1)),
            pl.BlockSpec((3, tn), col(0)),
            pl.BlockSpec((3, tn), col(1)),
            pl.BlockSpec((1, tn), col(0)),
            pl.BlockSpec((1, tn), col(1)),
            pl.BlockSpec(state_blk, state_idx(0)),
            pl.BlockSpec(state_blk, state_idx(1)),
        ],
        out_specs=[
            pl.BlockSpec((tm, tn), lambda j, i: (prev_tile(i), j)),
            pl.BlockSpec(state_blk, state_idx(0)),
            pl.BlockSpec(state_blk, state_idx(0)),
        ],
        out_shape=[jax.ShapeDtypeStruct((t, D_FF), BF16), state_out, state_out],
        scratch_shapes=[pltpu.VMEM((SUBLANES, tn), F32), pltpu.VMEM((SUBLANES, tn), F32), up_buf, up_buf],
        compiler_params=_cparams(("parallel", "arbitrary")),
        name="ffn_up_conv",
    )(h, w_up, w_up, conv_w, conv_w, conv_b, conv_b, conv_prev, conv_prev)


def _ffn_up(h, w_up, conv_w, conv_b, conv_prev, *, batch, seq):
    t = h.shape[0]
    if seq > SUBLANES:
        return _ffn_up_skew(h, w_up, conv_w, conv_b, conv_prev, batch=batch, seq=seq)
    assert seq == SUBLANES, "whole-sequence tiles rely on one sublane tile per sequence"
    tm, tn = t, 2 * FFN_SUB
    n_tiles = D_FF // tn
    state_blk = (batch, 2, tn)
    state_idx = lambda half: (lambda j, i: (0, 0, j + half * n_tiles))
    col = lambda half: (lambda j, i: (0, j + half * n_tiles))
    kern = functools.partial(_ffn_up_kernel, tm=tm, seq=seq)
    state_out = jax.ShapeDtypeStruct((batch, 2, D_FF), F32)
    return pl.pallas_call(
        kern,
        grid=(n_tiles, t // tm),
        in_specs=[
            pl.BlockSpec((tm, D_MODEL), lambda j, i: (i, 0)),
            pl.BlockSpec((D_MODEL, tn), col(0)),
            pl.BlockSpec((D_MODEL, tn), col(1)),
            pl.BlockSpec((3, tn), col(0)),
            pl.BlockSpec((3, tn), col(1)),
            pl.BlockSpec((1, tn), col(0)),
            pl.BlockSpec((1, tn), col(1)),
            pl.BlockSpec(state_blk, state_idx(0)),
            pl.BlockSpec(state_blk, state_idx(1)),
        ],
        out_specs=[
            pl.BlockSpec((tm, tn), lambda j, i: (i, j)),
            pl.BlockSpec(state_blk, state_idx(0)),
            pl.BlockSpec(state_blk, state_idx(0)),
        ],
        out_shape=[jax.ShapeDtypeStruct((t, D_FF), BF16), state_out, state_out],
        compiler_params=_cparams(("parallel", "arbitrary")),
        name="ffn_up_conv",
    )(h, w_up, w_up, conv_w, conv_w, conv_b, conv_b, conv_prev, conv_prev)


def _ffn_down_kernel(a_ref, w_ref, x_ref, g_ref, o_ref):
    acc = jnp.dot(a_ref[...], w_ref[...], preferred_element_type=F32)
    o_ref[...] = x_ref[...] + _rms(acc, g_ref[...])


def _ffn_down(act, w_down, x1, g_post, *, tm=256):
    t = x1.shape[0]
    return pl.pallas_call(
        _ffn_down_kernel,
        grid=(t // tm,),
        in_specs=[
            pl.BlockSpec((tm, D_FF), lambda i: (i, 0)),
            pl.BlockSpec((D_FF, D_MODEL), lambda i: (0, 0), pipeline_mode=pl.Buffered(1)),
            pl.BlockSpec((tm, D_MODEL), lambda i: (i, 0)),
            pl.BlockSpec((1, D_MODEL), lambda i: (0, 0)),
        ],
        out_specs=pl.BlockSpec((tm, D_MODEL), lambda i: (i, 0)),
        out_shape=jax.ShapeDtypeStruct((t, D_MODEL), F32),
        compiler_params=_cparams(("parallel",)),
        name="ffn_down",
    )(act, w_down, x1, g_post)


def _pad_rows(w, row0, total):
    return jnp.zeros((total, w.shape[1]), w.dtype).at[row0:row0 + w.shape[0]].set(w)


def _prepare_layer(p, l):
    w_in_t = p["w_in"][l].T.astype(BF16)
    q0 = MAIN_COLS + 3 * LORA
    w_in_tail = jnp.concatenate([
        w_in_t[q0:], w_in_t[MAIN_COLS:q0], jnp.zeros((LORA_PAD - 3 * LORA, D_MODEL), BF16)], axis=0)
    mu = p["rk_mu"][l]
    vec_rows = [mu[:B_WIDTH], mu[B_WIDTH:2 * B_WIDTH], mu[2 * B_WIDTH:3 * B_WIDTH], p["rk_w0"][l],
                p["rk_a0"][l], p["rk_kk"][l], p["rk_ka"][l], p["rk_rk"][l].reshape(B_WIDTH),
                p["rk_lnx_g"][l], p["rk_lnx_b"][l]]
    vecs = jnp.zeros((16, B_WIDTH), F32).at[:len(vec_rows)].set(jnp.stack(vec_rows))
    mu_l = jnp.zeros((SUBLANES, LORA_PAD), F32).at[0, :3 * LORA].set(mu[3 * B_WIDTH:])
    head_of = jnp.arange(LANES) // B_HEAD_DIM
    seg_ones = (head_of[:, None] == head_of[None, :]).astype(BF16)
    tril = jnp.tril(jnp.ones((CHUNK, CHUNK), bool))
    ws = jnp.where(tril[None], p["gm_ws"][l], 0.0)
    bs = p["gm_bs"][l]
    return {
        "g_mix_pre": p["norm_mix_pre"][l][None], "g_mix_post": p["norm_mix_post"][l][None],
        "g_ffn_pre": p["norm_ffn_pre"][l][None], "g_ffn_post": p["norm_ffn_post"][l][None],
        "g_mem": p["norm_mem"][l][None],
        "w_in": w_in_t, "w_in_tail": w_in_tail,
        "w_out": p["w_out"][l].astype(BF16),
        "w_mkv": jnp.concatenate([p["w_mem_k"][l], p["w_mem_v"][l]], axis=1).astype(BF16),
        "gm_ln_g": p["gm_ln_g"][l][None], "gm_ln_b": p["gm_ln_b"][l][None],
        "gm_ws": ws, "gm_bs": bs,
        "rk_vecs": vecs, "rk_mu_l": mu_l,
        "rk_w2": _pad_rows(p["rk_w2"][l], 0, LORA_PAD).astype(BF16),
        "rk_a2": _pad_rows(p["rk_a2"][l], LORA, LORA_PAD).astype(BF16),
        "rk_g2": _pad_rows(p["rk_g2"][l], 2 * LORA, LORA_PAD).astype(BF16),
        "seg_ones": seg_ones,
        "w_up": p["ffn_w_up"][l].astype(BF16),
        "conv_w": p["ffn_conv_w"][l], "conv_b": p["ffn_conv_b"][l][None],
        "w_down": p["ffn_w_down"][l].astype(BF16),
    }


def _spatial_weights(lw, seq):
    ws, bs = lw["gm_ws"], lw["gm_bs"]
    if seq >= CHUNK:
        w_blk, b_rows = ws, bs
    else:
        reps = CHUNK // seq
        eye = jnp.eye(reps, dtype=ws.dtype)
        w_blk = jnp.einsum("ab,hts->hatbs", eye, ws[:, :seq, :seq]).reshape(A_HEADS, CHUNK, CHUNK)
        b_rows = jnp.tile(bs[:, :seq], (1, reps))
    bias_full = jnp.repeat(b_rows.T, A_HEAD_DIM, axis=1)
    return w_blk.astype(BF16), bias_full


def _decoder_layer(x, mem_k, mem_v, shift_prev, wkv_prev, conv_prev, lw):
    batch, seq, _ = x.shape
    x2 = x.reshape(batch * seq, D_MODEL)
    proj, proj_tail = _in_proj(x2, lw["g_mix_pre"], lw["w_in"], lw["w_in_tail"])
    w_sp, bias_full = _spatial_weights(lw, seq)
    a_out, a_v = _group_a(proj, lw["gm_ln_g"], lw["gm_ln_b"], w_sp, bias_full)
    shift_parts = [
        shift_prev[:, None, :B_WIDTH], shift_prev[:, None, B_WIDTH:2 * B_WIDTH],
        shift_prev[:, None, 2 * B_WIDTH:3 * B_WIDTH],
        jnp.pad(shift_prev[:, None, 3 * B_WIDTH:], ((0, 0), (0, 0), (0, LORA_PAD - 3 * LORA))),
    ]
    b_out, wkv_new, shift_new = _rwkv(proj, proj_tail, shift_parts, wkv_prev, lw, batch=batch, seq=seq)
    c_out = _attention(proj_tail, mem_k, mem_v, batch=batch, seq=seq)
    x1, h = _out_proj(a_out, b_out, c_out, lw["w_out"], x2, lw["g_mix_post"], lw["g_ffn_pre"])
    act, conv_g, conv_v = _ffn_up(h, lw["w_up"], lw["conv_w"], lw["conv_b"], conv_prev, batch=batch, seq=seq)
    y = _ffn_down(act, lw["w_down"], x1, lw["g_ffn_post"])

    chunk_start = ((seq - 1) // CHUNK) * CHUNK
    chunk_v = a_v.reshape(batch, seq, A_WIDTH)[:, chunk_start:].reshape(batch, -1, A_HEADS, A_HEAD_DIM)
    conv_new = jnp.concatenate([conv_g, conv_v], axis=-1)
    return y.reshape(batch, seq, D_MODEL), chunk_v, shift_new, wkv_new, conv_new


def kernel(x_prompt, x_sample, mem_prompt, cache_mem_k, cache_mem_v, state_shift, state_wkv, state_conv,
           norm_mix_pre, norm_mix_post, norm_ffn_pre, norm_ffn_post, norm_mem, w_in, w_out, w_mem_k, w_mem_v,
           gm_ln_g, gm_ln_b, gm_ws, gm_bs, rk_mu, rk_w0, rk_w2, rk_a0, rk_a2, rk_g2, rk_kk, rk_ka, rk_rk,
           rk_lnx_g, rk_lnx_b, ffn_w_up, ffn_conv_w, ffn_conv_b, ffn_w_down):
    params = dict(
        norm_mix_pre=norm_mix_pre, norm_mix_post=norm_mix_post, norm_ffn_pre=norm_ffn_pre,
        norm_ffn_post=norm_ffn_post, norm_mem=norm_mem, w_in=w_in, w_out=w_out, w_mem_k=w_mem_k,
        w_mem_v=w_mem_v, gm_ln_g=gm_ln_g, gm_ln_b=gm_ln_b, gm_ws=gm_ws, gm_bs=gm_bs, rk_mu=rk_mu,
        rk_w0=rk_w0, rk_w2=rk_w2, rk_a0=rk_a0, rk_a2=rk_a2, rk_g2=rk_g2, rk_kk=rk_kk, rk_ka=rk_ka,
        rk_rk=rk_rk, rk_lnx_g=rk_lnx_g, rk_lnx_b=rk_lnx_b, ffn_w_up=ffn_w_up, ffn_conv_w=ffn_conv_w,
        ffn_conv_b=ffn_conv_b, ffn_w_down=ffn_w_down)
    depth = w_in.shape[0]
    bp = x_prompt.shape[0]
    y_p, y_s = x_prompt, x_sample
    outs = [[] for _ in range(10)]
    for l in range(depth):
        lw = _prepare_layer(params, l)
        mem2 = mem_prompt.reshape(bp * MEM_LEN, D_MODEL)
        mkv = _norm_matmul(mem2, lw["g_mem"], lw["w_mkv"], tm=512, tn=2 * C_WIDTH)
        mk = mkv[:, :C_WIDTH].reshape(bp, MEM_LEN, C_WIDTH)
        mv = mkv[:, C_WIDTH:].reshape(bp, MEM_LEN, C_WIDTH)
        zero_shift = jnp.zeros((bp, B_PROJ), x_prompt.dtype)
        zero_wkv = jnp.zeros((bp, B_HEADS, B_HEAD_DIM, B_HEAD_DIM), F32)
        zero_conv = jnp.zeros((bp, 2, 2 * D_FF), x_prompt.dtype)
        y_p, cv, sh, wkv, conv = _decoder_layer(y_p, mk, mv, zero_shift, zero_wkv, zero_conv, lw)
        mem_shape = (bp, MEM_LEN, C_HEADS, C_HEAD_DIM)
        for lst, val in zip(outs[:6], (mk.reshape(mem_shape), mv.reshape(mem_shape), cv, sh, wkv, conv)):
            lst.append(val)
        n_s = x_sample.shape[0]
        cache_k = cache_mem_k[l].reshape(n_s, MEM_LEN * C_HEADS, C_HEAD_DIM)
        cache_v = cache_mem_v[l].reshape(n_s, MEM_LEN * C_HEADS, C_HEAD_DIM)
        y_s, cv, sh, wkv, conv = _decoder_layer(y_s, cache_k, cache_v, state_shift[l],
                                                state_wkv[l], state_conv[l], lw)
        for lst, val in zip(outs[6:], (cv, sh, wkv, conv)):
            lst.append(val)
    return (y_p, y_s) + tuple(jnp.stack(o) for o in outs)
```

```python
import functools
import math

import jax
import jax.numpy as jnp
from jax import lax
from jax.experimental import pallas as pl
from jax.experimental.pallas import tpu as pltpu

D_MODEL = 2048
MEM_LEN = 256
CHUNK = 128
A_HEADS, A_HEAD_DIM = 4, 128
A_WIDTH = A_HEADS * A_HEAD_DIM
B_HEADS, B_HEAD_DIM = 16, 64
B_WIDTH = B_HEADS * B_HEAD_DIM
LORA = 64
B_PROJ = 3 * B_WIDTH + 3 * LORA
C_HEADS, C_HEAD_DIM = 4, 128
C_WIDTH = C_HEADS * C_HEAD_DIM
D_FF = 5632
RMS_EPS = 1e-6
LN_EPS = 1e-5
GN_EPS = 64e-5
DECAY_OFFSET = 0.5

LANES = 128
SUBLANES = 8
VMEM_LIMIT_BYTES = 56 * 1024 * 1024

LORA_PAD = 256
COL_A = 0
COL_R = 2 * A_WIDTH
MAIN_COLS = COL_R + 3 * B_WIDTH
COL_Q = 0
COL_L = C_WIDTH
TAIL_COLS = COL_L + LORA_PAD
PROJ_TN = 1024

WKV_CHUNK = 64
FFN_N_TILES = 2
FFN_TN = D_FF // FFN_N_TILES
FFN_SUB = 256

F32 = jnp.float32
BF16 = jnp.bfloat16


def _cparams(sem):
    return pltpu.CompilerParams(dimension_semantics=sem, vmem_limit_bytes=VMEM_LIMIT_BYTES)


def _rms(x, g):
    return x * lax.rsqrt(jnp.mean(x * x, axis=-1, keepdims=True) + RMS_EPS) * g


def _norm_matmul_kernel(x_ref, g_ref, w_ref, o_ref, h_ref):
    @pl.when(pl.program_id(1) == 0)
    def _():
        h_ref[...] = _rms(x_ref[...], g_ref[...]).astype(BF16)

    o_ref[...] = jnp.dot(h_ref[...], w_ref[...], preferred_element_type=F32)


def _norm_matmul(x, g, w, *, tm, tn):
    t, k = x.shape
    n = w.shape[1]
    return pl.pallas_call(
        _norm_matmul_kernel,
        grid=(t // tm, n // tn),
        in_specs=[
            pl.BlockSpec((tm, k), lambda i, j: (i, 0)),
            pl.BlockSpec((1, k), lambda i, j: (0, 0)),
            pl.BlockSpec((k, tn), lambda i, j: (0, j)),
        ],
        out_specs=pl.BlockSpec((tm, tn), lambda i, j: (i, j)),
        out_shape=jax.ShapeDtypeStruct((t, n), F32),
        scratch_shapes=[pltpu.VMEM((tm, k), BF16)],
        compiler_params=_cparams(("parallel", "arbitrary")),
        name="norm_matmul",
    )(x, g, w)


def _in_proj_kernel(x_ref, g_ref, wm_ref, wt_ref, om_ref, ot_ref, h_ref, *, n_main):
    j = pl.program_id(1)

    @pl.when(j == 0)
    def _():
        h_ref[...] = _rms(x_ref[...], g_ref[...]).astype(BF16)

    nt = (((1,), (1,)), ((), ()))

    @pl.when(j < n_main)
    def _():
        om_ref[...] = lax.dot_general(h_ref[...], wm_ref[...], nt, preferred_element_type=F32)

    @pl.when(j == n_main)
    def _():
        ot_ref[...] = lax.dot_general(h_ref[...], wt_ref[...], nt, preferred_element_type=F32)


def _in_proj(x, g, w_main, w_tail, *, tm=1024):
    t, k = x.shape
    n_main = MAIN_COLS // PROJ_TN
    main_tile = lambda j: jnp.minimum(j, n_main - 1)
    return pl.pallas_call(
        functools.partial(_in_proj_kernel, n_main=n_main),
        grid=(t // tm, n_main + 1),
        in_specs=[
            pl.BlockSpec((tm, k), lambda i, j: (i, 0)),
            pl.BlockSpec((1, k), lambda i, j: (0, 0)),
            pl.BlockSpec((PROJ_TN, k), lambda i, j: (main_tile(j), 0)),
            pl.BlockSpec((TAIL_COLS, k), lambda i, j: (0, 0)),
        ],
        out_specs=[
            pl.BlockSpec((tm, PROJ_TN), lambda i, j: (i, main_tile(j))),
            pl.BlockSpec((tm, TAIL_COLS), lambda i, j: (i, 0)),
        ],
        out_shape=[jax.ShapeDtypeStruct((t, MAIN_COLS), F32), jax.ShapeDtypeStruct((t, TAIL_COLS), F32)],
        scratch_shapes=[pltpu.VMEM((tm, k), BF16)],
        compiler_params=_cparams(("parallel", "arbitrary")),
        name="in_proj",
    )(x, g, w_main, w_tail)


def _group_a_kernel(u_ref, v_ref, g_ref, b_ref, w_ref, bias_ref, o_ref, vout_ref):
    u = jax.nn.gelu(u_ref[...], approximate=True)
    v = jax.nn.gelu(v_ref[...], approximate=True)
    mean = jnp.mean(v, axis=-1, keepdims=True)
    d = v - mean
    var = jnp.mean(d * d, axis=-1, keepdims=True)
    vn = d * lax.rsqrt(var + LN_EPS) * g_ref[...] + b_ref[...]
    vout_ref[...] = vn
    vb = vn.astype(BF16)
    for c in range(u.shape[0] // CHUNK):
        rs = slice(c * CHUNK, (c + 1) * CHUNK)
        for h in range(A_HEADS):
            hs = slice(h * A_HEAD_DIM, (h + 1) * A_HEAD_DIM)
            mixed = jnp.dot(w_ref[h], vb[rs, hs], preferred_element_type=F32) + bias_ref[:, hs]
            o_ref[rs, hs] = (u[rs, hs] * mixed).astype(BF16)


def _group_a(proj, ln_g, ln_b, w_sp, bias_full, *, rows=4 * CHUNK):
    t = proj.shape[0]
    return pl.pallas_call(
        _group_a_kernel,
        grid=(t // rows,),
        in_specs=[
            pl.BlockSpec((rows, A_WIDTH), lambda i: (i, COL_A // A_WIDTH)),
            pl.BlockSpec((rows, A_WIDTH), lambda i: (i, COL_A // A_WIDTH + 1)),
            pl.BlockSpec((1, A_WIDTH), lambda i: (0, 0)),
            pl.BlockSpec((1, A_WIDTH), lambda i: (0, 0)),
            pl.BlockSpec((A_HEADS, CHUNK, CHUNK), lambda i: (0, 0, 0)),
            pl.BlockSpec((CHUNK, A_WIDTH), lambda i: (0, 0)),
        ],
        out_specs=[
            pl.BlockSpec((rows, A_WIDTH), lambda i: (i, 0)),
            pl.BlockSpec((rows, A_WIDTH), lambda i: (i, 0)),
        ],
        out_shape=[
            jax.ShapeDtypeStruct((t, A_WIDTH), BF16),
            jax.ShapeDtypeStruct((t, A_WIDTH), F32),
        ],
        compiler_params=_cparams(("parallel",)),
        name="group_a",
    )(proj, proj, ln_g, ln_b, w_sp, bias_full)


def _softmax_rows(s):
    e = jnp.exp(s - jnp.max(s, axis=-1, keepdims=True))
    return e / jnp.sum(e, axis=-1, keepdims=True)


def _attn_kernel(q_ref, k_ref, v_ref, o_ref, *, n_b, tq, head_major_rows):
    scale = C_HEAD_DIM ** -0.5
    heads = range(C_HEADS)
    lanes = lambda h: slice(h * C_HEAD_DIM, (h + 1) * C_HEAD_DIM)
    if not head_major_rows:
        chains = [(g, h) for g in range(n_b) for h in heads]
        rows = lambda g: slice(g * tq, (g + 1) * tq)
        s = [lax.dot_general(q_ref[rows(g), lanes(h)].astype(BF16), k_ref[g, :, lanes(h)].astype(BF16), _NT,
                             preferred_element_type=F32) * scale for g, h in chains]
        p = [_softmax_rows(m).astype(BF16) for m in s]
        o = [jnp.dot(m, v_ref[g, :, lanes(h)].astype(BF16), preferred_element_type=F32)
             for m, (g, h) in zip(p, chains)]
        for m, (g, h) in zip(o, chains):
            o_ref[rows(g), lanes(h)] = m.astype(BF16)
        return
    row = lax.broadcasted_iota(jnp.int32, (C_HEADS * tq, 1), 0)
    col = lax.broadcasted_iota(jnp.int32, (1, C_HEADS * MEM_LEN), 1)
    own = (col & (C_HEADS - 1)) == (row >> int(math.log2(tq)))
    q = [q_ref[g * tq:(g + 1) * tq, :] for g in range(n_b)]
    qs = [jnp.concatenate([m[:, lanes(h)] for h in heads], axis=0).astype(BF16) for m in q]
    s = [lax.dot_general(m, k_ref[g].astype(BF16), _NT, preferred_element_type=F32) * scale
         for g, m in enumerate(qs)]
    p = [_softmax_rows(jnp.where(own, m, -1e30)).astype(BF16) for m in s]
    o = [jnp.dot(m, v_ref[g].astype(BF16), preferred_element_type=F32) for g, m in enumerate(p)]
    for g, m in enumerate(o):
        for h in heads:
            o_ref[g * tq:(g + 1) * tq, lanes(h)] = m[h * tq:(h + 1) * tq].astype(BF16)


def _attention(proj, mem_k, mem_v, *, batch, seq):
    head_major_rows = mem_k.shape[-1] == C_HEAD_DIM
    if head_major_rows:
        tq, n_b = seq, 8
    else:
        tq, n_b = 512, 1
    n_q = seq // tq
    mem_blk = (n_b,) + mem_k.shape[1:]
    return pl.pallas_call(
        functools.partial(_attn_kernel, n_b=n_b, tq=tq, head_major_rows=head_major_rows),
        grid=(batch // n_b, n_q),
        in_specs=[
            pl.BlockSpec((n_b * tq, C_WIDTH), lambda b, i: (b * n_q + i, COL_Q // C_WIDTH)),
            pl.BlockSpec(mem_blk, lambda b, i: (b, 0, 0)),
            pl.BlockSpec(mem_blk, lambda b, i: (b, 0, 0)),
        ],
        out_specs=pl.BlockSpec((n_b * tq, C_WIDTH), lambda b, i: (b * n_q + i, 0)),
        out_shape=jax.ShapeDtypeStruct((batch * seq, C_WIDTH), BF16),
        compiler_params=_cparams(("parallel", "arbitrary")),
        name="mem_attention",
    )(proj, mem_k, mem_v)


_V_MU_R, _V_MU_K, _V_MU_V, _V_W0, _V_A0, _V_KK, _V_KA, _V_RK, _V_LNG, _V_LNB = range(10)


_NN = (((1,), (0,)), ((), ()))
_NT = (((1,), (1,)), ((), ()))
_TN = (((0,), (0,)), ((), ()))
HEAD_PAIRS = B_HEADS // 2


def _split(x):
    hi = x.astype(BF16)
    return hi, (x - hi.astype(F32)).astype(BF16)


def _b(x):
    return x.astype(BF16)


def _bdot(a, b, dims=_NN):
    return lax.dot_general(a, b, dims, preferred_element_type=F32)


def _rwkv_kernel(pr_ref, pk_ref, pv_ref, pl_ref, sr_ref, sk_ref, sv_ref, sl_ref, wkv_ref,
                 vec_ref, mul_ref, w2_ref, a2_ref, g2_ref, seg_ref,
                 o_ref, so_ref, nr_ref, nk_ref, nv_ref, nl_ref,
                 cr_ref, ck_ref, cv_ref, cl_ref, sbd_ref, *, n_blk, groups, chunk, whole_seq):
    rows = groups * chunk
    first = pl.program_id(1) == 0
    last = pl.program_id(1) == pl.num_programs(1) - 1
    hd = B_HEAD_DIM
    n_seq = n_blk * groups

    @pl.when(first)
    def _():
        zero = jnp.zeros((hd, hd), F32)
        for g in range(n_seq):
            for q in range(HEAD_PAIRS):
                top = jnp.concatenate([wkv_ref[g, 2 * q], zero], axis=1)
                bot = jnp.concatenate([zero, wkv_ref[g, 2 * q + 1]], axis=1)
                sbd_ref[g, q] = jnp.concatenate([top, bot], axis=0)
        if not whole_seq:
            for s_ref, carry_ref in ((sr_ref, cr_ref), (sk_ref, ck_ref), (sv_ref, cv_ref), (sl_ref, cl_ref)):
                for blk in range(n_blk):
                    carry_ref[blk] = jnp.broadcast_to(s_ref[blk], (SUBLANES, s_ref.shape[-1]))

    row_id = lax.broadcasted_iota(jnp.int32, (rows, 1), 0)
    col_id = lax.broadcasted_iota(jnp.int32, (1, rows), 1)
    chunk_bits = int(math.log2(chunk))
    incl_b = (((row_id >> chunk_bits) == (col_id >> chunk_bits)) & (col_id <= row_id)).astype(BF16)
    vec = lambda i: vec_ref[i:i + 1, :]
    seg_ones = seg_ref[...]

    def seg_sum(x):
        slabs = jnp.concatenate([x[:, q * LANES:(q + 1) * LANES] for q in range(HEAD_PAIRS)], axis=0)
        hi, lo = _split(slabs)
        s = (jnp.dot(hi, seg_ones, preferred_element_type=F32)
             + jnp.dot(lo, seg_ones, preferred_element_type=F32))
        return jnp.concatenate([s[q * rows:(q + 1) * rows] for q in range(HEAD_PAIRS)], axis=1)

    def prologue(blk, out):
        seqs = slice(blk * groups, (blk + 1) * groups)

        def prev_rows(x, s_ref, carry_ref):
            width = x.shape[1]
            if whole_seq:
                start = jnp.broadcast_to(s_ref[seqs], (groups, chunk, width)).reshape(rows, width)
                return jnp.where((row_id & (chunk - 1)) == 0, start, pltpu.roll(x, 1, 0))

            ext = jnp.concatenate([carry_ref[blk], x], axis=0)
            prev = pltpu.roll(ext, 1, 0)[SUBLANES:]
            carry_ref[blk] = x[rows - SUBLANES:]
            return prev

        def shifted(p_ref, s_ref, carry_ref, n_ref, mu):
            x = p_ref[blk]
            if whole_seq:
                n_ref[seqs] = x.reshape(groups, chunk, x.shape[1])[:, chunk - 1:, :]
            else:
                n_ref[blk] = x[rows - 1:]
            return x + (prev_rows(x, s_ref, carry_ref) - x) * mu

        lo = shifted(pl_ref, sl_ref, cl_ref, nl_ref, mul_ref[0:1, :])
        dw = jnp.dot(jnp.tanh(lo).astype(BF16), w2_ref[...], preferred_element_type=F32)
        da = jnp.dot(lo.astype(BF16), a2_ref[...], preferred_element_type=F32)
        gate = jnp.dot(jax.nn.sigmoid(lo).astype(BF16), g2_ref[...], preferred_element_type=F32)
        yield
        w_log = -jax.nn.softplus(-(vec(_V_W0) + dw)) - DECAY_OFFSET
        log_decay = -jnp.exp(w_log)
        ld_hi, ld_mid = _split(log_decay)
        ld_lo = (log_decay - ld_hi.astype(F32) - ld_mid.astype(F32)).astype(BF16)
        cum = (jnp.dot(incl_b, ld_hi, preferred_element_type=F32)
               + jnp.dot(incl_b, ld_mid, preferred_element_type=F32)
               + jnp.dot(incl_b, ld_lo, preferred_element_type=F32))
        yield
        a = jax.nn.sigmoid(vec(_V_A0) + da)
        k = shifted(pk_ref, sk_ref, ck_ref, nk_ref, vec(_V_MU_K))
        kk = k * vec(_V_KK)
        kk = kk / jnp.maximum(jnp.sqrt(seg_sum(kk * kk)), 1e-12)
        yield
        p_incl = jnp.exp(cum)
        p_inv = jnp.exp(-cum)
        a_t = -kk * jnp.exp(cum - log_decay)
        b_t = kk * a * p_inv
        yield
        k = k * (1.0 + (a - 1.0) * vec(_V_KA))
        k_t = k * p_inv
        r = shifted(pr_ref, sr_ref, cr_ref, nr_ref, vec(_V_MU_R))
        r_t = r * p_incl
        yield
        v = shifted(pv_ref, sv_ref, cv_ref, nv_ref, vec(_V_MU_V))
        bonus = seg_sum(r * k * vec(_V_RK)) * v
        out.update(a_t=a_t, b_t=b_t, k_t=k_t, r_t=r_t, v=v, p_incl=p_incl, bonus=bonus, gate=gate)

    pr = 2 * rows
    row2 = lax.broadcasted_iota(jnp.int32, (pr, 1), 0)
    col2 = lax.broadcasted_iota(jnp.int32, (1, pr), 1)
    t2, s2 = row2 & (rows - 1), col2 & (rows - 1)
    same2 = ((row2 >> chunk_bits) == (col2 >> chunk_bits))
    incl2 = same2 & (s2 <= t2)
    strict2 = same2 & (s2 < t2)
    eye2 = (row2 == col2).astype(F32)
    left = lax.broadcasted_iota(jnp.int32, (rows, LANES), 1) < hd

    def bd(x):
        zero = jnp.zeros_like(x)
        return jnp.concatenate([jnp.where(left, x, zero), jnp.where(left, zero, x)], axis=0)

    def group_rows(mats, g):
        starts = [hh * rows + g * chunk for hh in (0, 1)]
        return jnp.concatenate([m[i:i + chunk] for m in mats for i in starts], axis=0)

    n_sq = chunk_bits - 1
    col4 = lax.broadcasted_iota(jnp.int32, (1, 2 * pr), 1)
    incl4 = ((row2 >> chunk_bits) == ((col4 & (pr - 1)) >> chunk_bits)) & ((col4 & (rows - 1)) <= t2)
    pairs = range(HEAD_PAIRS)
    lanes_of = [slice(q * LANES, (q + 1) * LANES) for q in pairs]

    def recurrence(blk, pro, qs):
        s0 = blk * groups
        p_incl = pro["p_incl"]
        pairs = range(len(qs))
        bds = [[bd(pro[name][:, lanes_of[q]]) for name in ("a_t", "r_t", "b_t", "k_t", "v")] for q in qs]
        ar_s = [_b(jnp.concatenate([m[0], m[1]], axis=0)) for m in bds]
        bk_s = [_b(jnp.concatenate([m[2], m[3]], axis=0)) for m in bds]
        v_s = [_b(m[4]) for m in bds]
        if groups == 1:
            ms = [_bdot(ar_s[q], jnp.concatenate([bk_s[q], _b(sbd_ref[s0, qs[q]])], axis=0), _NT)
                  for q in pairs]
        else:
            ms = [_bdot(ar_s[q], bk_s[q], _NT) for q in pairs]
        yield
        a_ab = [jnp.where(strict2, m[:pr, :pr], 0.0) for m in ms]
        a_ak = [jnp.where(strict2, m[:pr, pr:2 * pr], 0.0) for m in ms]
        a_r = [jnp.where(incl4, m[pr:, :2 * pr], 0.0) for m in ms]
        inv = [eye2 + n for n in a_ab]
        pw_s = [_b(_bdot(_b(n), _b(n))) for n in a_ab]
        yield
        for _ in range(1, n_sq):
            both = [_bdot(p, jnp.concatenate([p, _b(i)], axis=1)) for p, i in zip(pw_s, inv)]
            inv = [i + m[:, pr:] for i, m in zip(inv, both)]
            pw_s = [_b(m[:, :pr]) for m in both]
            yield
        inv = [i + _bdot(p, _b(i)) for i, p in zip(inv, pw_s)]
        yield
        if groups == 1:
            x0, y0 = [m[:pr, 2 * pr:] for m in ms], [m[pr:, 2 * pr:] for m in ms]
        else:
            x0, y0 = [], []
            for q in pairs:
                x_parts, y_parts = [None] * (2 * groups), [None] * (2 * groups)
                for g in range(groups):
                    xy = _bdot(_b(group_rows(bds[q][:2], g)), _b(sbd_ref[s0 + g, qs[q]]), _NT)
                    for hh in (0, 1):
                        x_parts[hh * groups + g] = xy[hh * chunk:(hh + 1) * chunk]
                        y_parts[hh * groups + g] = xy[(2 + hh) * chunk:(3 + hh) * chunk]
                x0.append(jnp.concatenate(x_parts, axis=0))
                y0.append(jnp.concatenate(y_parts, axis=0))
        yield
        rhs = [_b(x0[q] + _bdot(_b(a_ak[q]), v_s[q])) for q in pairs]
        u = [_bdot(_b(inv[q]), rhs[q]) for q in pairs]
        yield
        uv_s = [jnp.concatenate([_b(u[q]), v_s[q]], axis=0) for q in pairs]
        y2 = [y0[q] + _bdot(_b(a_r[q]), uv_s[q]) for q in pairs]
        for q in pairs:
            pro["y"][qs[q]] = y2[q][:rows] + y2[q][rows:]
        yield
        for q in pairs:
            ls = lanes_of[qs[q]]
            if groups == 1:
                ds = _bdot(uv_s[q], bk_s[q], _TN)
                sbd_ref[s0, qs[q]] = (sbd_ref[s0, qs[q]] + ds) * p_incl[rows - 1:rows, ls]
            else:
                for g in range(groups):
                    ds = _bdot(_b(group_rows((u[q], bds[q][4]), g)),
                               _b(group_rows((bds[q][2], bds[q][3]), g)), _TN)
                    end = (g + 1) * chunk - 1
                    sbd_ref[s0 + g, qs[q]] = (sbd_ref[s0 + g, qs[q]] + ds) * p_incl[end:end + 1, ls]

    def finish(blk, pro):
        inv_n = 1.0 / B_HEAD_DIM
        y = jnp.concatenate([pro["y"][q] for q in range(HEAD_PAIRS)], axis=1)
        d = y - seg_sum(y) * inv_n
        yield
        var = seg_sum(d * d) * inv_n
        yield
        yn = d * lax.rsqrt(var + GN_EPS) * vec(_V_LNG) + vec(_V_LNB)
        o_ref[blk] = ((yn + pro["bonus"]) * pro["gate"]).astype(BF16)

    def interleave(*gens):
        live = list(gens)
        while live:
            for gen in list(live):
                if next(gen, StopIteration) is StopIteration:
                    live.remove(gen)

    pros = [dict(y={}) for _ in range(n_blk)]
    interleave(prologue(0, pros[0]))
    for blk in range(n_blk):
        work = [recurrence(blk, pros[blk], list(pairs))]
        if blk + 1 < n_blk:
            work.append(prologue(blk + 1, pros[blk + 1]))
        if blk > 0:
            work.append(finish(blk - 1, pros[blk - 1]))
        interleave(*work)
    interleave(finish(n_blk - 1, pros[n_blk - 1]))

    @pl.when(last)
    def _():
        for g in range(n_seq):
            for q in range(HEAD_PAIRS):
                sq = sbd_ref[g, q]
                so_ref[g, 2 * q] = sq[:hd, :hd]
                so_ref[g, 2 * q + 1] = sq[hd:, hd:]


def _rwkv(proj, proj_tail, shift_parts, wkv_prev, lw, *, batch, seq):
    whole_seq = seq <= WKV_CHUNK
    if whole_seq:
        assert seq == SUBLANES, "whole-sequence blocks rely on one sublane tile per sequence"
        chunk, groups = seq, WKV_CHUNK // seq
        n_blk, n_chunks = 2, 1
        n_outer = batch // (groups * n_blk)
        lead = (batch // groups, WKV_CHUNK)
    else:
        chunk, groups = WKV_CHUNK, 1
        n_blk, n_outer, n_chunks = batch, 1, seq // WKV_CHUNK
        lead = (batch, seq)
    proj3, tail3 = proj.reshape(lead + (MAIN_COLS,)), proj_tail.reshape(lead + (TAIL_COLS,))
    n_seq = n_blk * groups
    act = lambda col_blk, width: pl.BlockSpec((n_blk, WKV_CHUNK, width), lambda o, c: (o, c, col_blk))
    per_seq = lambda *tail: pl.BlockSpec((n_seq,) + tail, lambda o, c: (o,) + (0,) * len(tail))
    fixed = lambda *shape: pl.BlockSpec(shape, lambda o, c: (0,) * len(shape))
    state = (B_HEADS, B_HEAD_DIM, B_HEAD_DIM)
    kern = functools.partial(_rwkv_kernel, n_blk=n_blk, groups=groups, chunk=chunk, whole_seq=whole_seq)
    shift_specs = [per_seq(1, B_WIDTH), per_seq(1, B_WIDTH), per_seq(1, B_WIDTH), per_seq(1, LORA_PAD)]
    b_out, wkv_new, *last_rows = pl.pallas_call(
        kern,
        grid=(n_outer, n_chunks),
        in_specs=[
            act(COL_R // B_WIDTH, B_WIDTH), act(COL_R // B_WIDTH + 1, B_WIDTH),
            act(COL_R // B_WIDTH + 2, B_WIDTH), act(COL_L // LORA_PAD, LORA_PAD),
            *shift_specs,
            per_seq(*state),
            fixed(16, B_WIDTH), fixed(SUBLANES, LORA_PAD),
            fixed(LORA_PAD, B_WIDTH), fixed(LORA_PAD, B_WIDTH), fixed(LORA_PAD, B_WIDTH),
            fixed(LANES, LANES),
        ],
        out_specs=[act(0, B_WIDTH), per_seq(*state), *shift_specs],
        out_shape=[
            jax.ShapeDtypeStruct(lead + (B_WIDTH,), BF16),
            jax.ShapeDtypeStruct((batch,) + state, F32),
            *(jax.ShapeDtypeStruct(p.shape, F32) for p in shift_parts),
        ],
        scratch_shapes=[
            pltpu.VMEM((n_blk, SUBLANES, B_WIDTH), F32),
            pltpu.VMEM((n_blk, SUBLANES, B_WIDTH), F32),
            pltpu.VMEM((n_blk, SUBLANES, B_WIDTH), F32),
            pltpu.VMEM((n_blk, SUBLANES, LORA_PAD), F32),
            pltpu.VMEM((n_seq, HEAD_PAIRS, LANES, LANES), F32),
        ],
        compiler_params=_cparams(("parallel", "arbitrary")),
        name="rwkv7",
    )(proj3, proj3, proj3, tail3, *shift_parts, wkv_prev,
      lw["rk_vecs"], lw["rk_mu_l"], lw["rk_w2"], lw["rk_a2"], lw["rk_g2"], lw["seg_ones"])
    shift_new = jnp.concatenate([p[:, 0] for p in last_rows[:3]] + [last_rows[3][:, 0, :3 * LORA]], axis=1)
    return b_out.reshape(batch * seq, B_WIDTH), wkv_new, shift_new


def _out_proj_kernel(a_ref, b_ref, c_ref, w_ref, x_ref, gpost_ref, gffn_ref, x1_ref, h_ref):
    n_sub = 4
    sub = x_ref.shape[0] // n_sub
    halves = [slice(s * sub, (s + 1) * sub) for s in range(n_sub)]

    def project(rs):
        acc = jnp.dot(a_ref[rs, :], w_ref[0:A_WIDTH, :], preferred_element_type=F32)
        acc += jnp.dot(b_ref[rs, :], w_ref[A_WIDTH:A_WIDTH + B_WIDTH, :], preferred_element_type=F32)
        return acc + jnp.dot(c_ref[rs, :], w_ref[A_WIDTH + B_WIDTH:, :], preferred_element_type=F32)

    accs = [project(rs) for rs in halves]
    for rs, acc in zip(halves, accs):
        x1 = x_ref[rs, :] + _rms(acc, gpost_ref[...])
        x1_ref[rs, :] = x1
        h_ref[rs, :] = _rms(x1, gffn_ref[...]).astype(BF16)


def _out_proj(a_out, b_out, c_out, w_out, x, g_post, g_ffn, *, tm=512):
    t = x.shape[0]
    row = lambda i: (i, 0)
    fixed = lambda i: (0, 0)
    return pl.pallas_call(
        _out_proj_kernel,
        grid=(t // tm,),
        in_specs=[
            pl.BlockSpec((tm, A_WIDTH), row),
            pl.BlockSpec((tm, B_WIDTH), row),
            pl.BlockSpec((tm, C_WIDTH), row),
            pl.BlockSpec((D_MODEL, D_MODEL), fixed, pipeline_mode=pl.Buffered(1)),
            pl.BlockSpec((tm, D_MODEL), row),
            pl.BlockSpec((1, D_MODEL), fixed),
            pl.BlockSpec((1, D_MODEL), fixed),
        ],
        out_specs=[pl.BlockSpec((tm, D_MODEL), row), pl.BlockSpec((tm, D_MODEL), row)],
        out_shape=[jax.ShapeDtypeStruct((t, D_MODEL), F32), jax.ShapeDtypeStruct((t, D_MODEL), BF16)],
        compiler_params=_cparams(("parallel",)),
        name="out_proj",
    )(a_out, b_out, c_out, w_out, x, g_post, g_ffn)


def _chunked_dots(h_ref, w_refs, cs, between):
    accs = [None] * len(w_refs)
    for kc in range(h_ref.shape[1] // FFN_SUB):
        ks = slice(kc * FFN_SUB, (kc + 1) * FFN_SUB)
        for n, w_ref in enumerate(w_refs):
            part = jnp.dot(h_ref[:, ks], w_ref[ks, cs], preferred_element_type=F32)
            accs[n] = part if accs[n] is None else accs[n] + part
        between(kc)
    return accs


def _ffn_up_kernel(h_ref, wg_ref, wv_ref, cwg_ref, cwv_ref, cbg_ref, cbv_ref, pg_ref, pv_ref,
                   act_ref, ng_ref, nv_ref, *, tm, seq):
    h = h_ref[...]
    tn = cwg_ref.shape[1]
    n_seq = tm // seq
    tau = lax.broadcasted_iota(jnp.int32, (tm, 1), 0) & (seq - 1)

    def conv(up, cs, cw_ref, cb_ref, p_ref, n_ref):
        width = up.shape[1]
        prev = p_ref[:, :, cs]
        e0 = jnp.broadcast_to(prev[:, 0:1, :], (n_seq, seq, width)).reshape(tm, width)
        e1 = jnp.broadcast_to(prev[:, 1:2, :], (n_seq, seq, width)).reshape(tm, width)
        m1 = jnp.where(tau == 0, e1, pltpu.roll(up, 1, 0))
        m2 = jnp.where(tau == 0, e0, jnp.where(tau == 1, e1, pltpu.roll(up, 2, 0)))
        n_ref[:, :, cs] = up.reshape(n_seq, seq, width)[:, seq - 2:, :]
        return cb_ref[:, cs] + m2 * cw_ref[0:1, cs] + m1 * cw_ref[1:2, cs] + up * cw_ref[2:3, cs]

    subs = [slice(s * FFN_SUB, (s + 1) * FFN_SUB) for s in range(tn // FFN_SUB)]
    ups = [(jnp.dot(h, wg_ref[:, cs], preferred_element_type=F32),
            jnp.dot(h, wv_ref[:, cs], preferred_element_type=F32)) for cs in subs]
    for cs, (up_g, up_v) in zip(subs, ups):
        gate = conv(up_g, cs, cwg_ref, cbg_ref, pg_ref, ng_ref)
        val = conv(up_v, cs, cwv_ref, cbv_ref, pv_ref, nv_ref)
        act_ref[:, cs] = (jax.nn.gelu(gate, approximate=True) * val).astype(BF16)


def _ffn_up_skew_kernel(h_ref, wg_ref, wv_ref, cwg_ref, cwv_ref, cbg_ref, cbv_ref, pg_ref, pv_ref,
                        act_ref, ng_ref, nv_ref, cg_ref, cv_ref, u0_ref, u1_ref,
                        *, tm, tiles_per_seq, n_row_tiles):
    i = pl.program_id(1)
    tn = cwg_ref.shape[1]
    n_sub = tn // FFN_SUB

    @pl.when(i == 0)
    def _():
        for ref in (u1_ref, cg_ref, cv_ref):
            ref[...] = jnp.zeros_like(ref)

    @pl.when(i % tiles_per_seq == 1 % tiles_per_seq)
    def _():
        for carry_ref, p_ref in ((cg_ref, pg_ref), (cv_ref, pv_ref)):
            carry_ref[...] = jnp.concatenate([jnp.zeros((SUBLANES - 2, tn), F32), p_ref[0]], axis=0)

    piece = 64

    def conv(u_ref, r0, us, cs, cw_ref, cb_ref, carry_ref):
        if r0 == 0:
            ext = jnp.concatenate([carry_ref[:, cs], u_ref[0:piece, us]], axis=0)
        else:
            ext = u_ref[r0 - SUBLANES:r0 + piece, us]
        m1 = pltpu.roll(ext, 1, 0)[SUBLANES:]
        m2 = pltpu.roll(ext, 2, 0)[SUBLANES:]
        return (cb_ref[:, cs] + m2 * cw_ref[0:1, cs] + m1 * cw_ref[1:2, cs]
                + ext[SUBLANES:] * cw_ref[2:3, cs])

    n_k = D_MODEL // FFN_SUB
    chunks_per_piece = n_k // (tm // piece)

    def step(new_ref, old_ref):
        for s in range(n_sub):
            cs = slice(s * FFN_SUB, (s + 1) * FFN_SUB)
            gs, vs = cs, slice(tn + s * FFN_SUB, tn + (s + 1) * FFN_SUB)

            def conv_piece(kc):
                if (kc + 1) % chunks_per_piece == 0:
                    r0 = (kc // chunks_per_piece) * piece
                    gate = conv(old_ref, r0, gs, cs, cwg_ref, cbg_ref, cg_ref)
                    val = conv(old_ref, r0, vs, cs, cwv_ref, cbv_ref, cv_ref)
                    act_ref[r0:r0 + piece, cs] = (jax.nn.gelu(gate, approximate=True) * val).astype(BF16)

            if new_ref is None:
                for kc in range(n_k):
                    conv_piece(kc)
            else:
                new_ref[:, gs], new_ref[:, vs] = _chunked_dots(h_ref, (wg_ref, wv_ref), cs, conv_piece)
            for us, carry_ref, n_ref in ((gs, cg_ref, ng_ref), (vs, cv_ref, nv_ref)):
                carry_ref[:, cs] = old_ref[tm - SUBLANES:, us]
                n_ref[0, :, cs] = old_ref[tm - 2:, us]

    bufs = (u0_ref, u1_ref)

    @pl.when((i % 2 == 0) & (i < n_row_tiles))
    def _():
        step(u0_ref, u1_ref)

    @pl.when((i % 2 == 1) & (i < n_row_tiles))
    def _():
        step(u1_ref, u0_ref)

    @pl.when(i == n_row_tiles)
    def _():
        step(None, bufs[(n_row_tiles - 1) % 2])


def _ffn_up_skew(h, w_up, conv_w, conv_b, conv_prev, *, batch, seq, tm=256):
    t = h.shape[0]
    tn, n_tiles = FFN_TN, FFN_N_TILES
    n_m = t // tm
    tiles_per_seq = seq // tm
    prev_tile = lambda i: jnp.maximum(i - 1, 0)
    col = lambda half: (lambda j, i: (0, j + half * n_tiles))
    state_blk = (1, 2, tn)
    state_idx = lambda half: (lambda j, i: (prev_tile(i) // tiles_per_seq, 0, j + half * n_tiles))
    state_out = jax.ShapeDtypeStruct((batch, 2, D_FF), F32)
    up_buf = pltpu.VMEM((tm, 2 * tn), F32)
    return pl.pallas_call(
        functools.partial(_ffn_up_skew_kernel, tm=tm, tiles_per_seq=tiles_per_seq, n_row_tiles=n_m),
        grid=(n_tiles, n_m + 1),
        in_specs=[
            pl.BlockSpec((tm, D_MODEL), lambda j, i: (jnp.minimum(i, n_m - 1), 0)),
            pl.BlockSpec((D_MODEL, tn), col(0), pipeline_mode=pl.Buffered(1)),
            pl.BlockSpec((D_MODEL, tn), col(1), pipeline_mode=pl.Buffered(1)),
            pl.BlockSpec((3, tn), col(0)),
            pl.BlockSpec((3, tn), col(1)),
            pl.BlockSpec((1, tn), col(0)),
            pl.BlockSpec((1, tn), col(1)),
            pl.BlockSpec(state_blk, state_idx(0)),
            pl.BlockSpec(state_blk, state_idx(1)),
        ],
        out_specs=[
            pl.BlockSpec((tm, tn), lambda j, i: (prev_tile(i), j)),
            pl.BlockSpec(state_blk, state_idx(0)),
            pl.BlockSpec(state_blk, state_idx(0)),
        ],
        out_shape=[jax.ShapeDtypeStruct((t, D_FF), BF16), state_out, state_out],
        scratch_shapes=[pltpu.VMEM((SUBLANES, tn), F32), pltpu.VMEM((SUBLANES, tn), F32), up_buf, up_buf],
        compiler_params=_cparams(("parallel", "arbitrary")),
        name="ffn_up_conv",
    )(h, w_up, w_up, conv_w, conv_w, conv_b, conv_b, conv_prev, conv_prev)


def _ffn_up(h, w_up, conv_w, conv_b, conv_prev, *, batch, seq):
    t = h.shape[0]
    if seq > SUBLANES:
        return _ffn_up_skew(h, w_up, conv_w, conv_b, conv_prev, batch=batch, seq=seq)
    assert seq == SUBLANES, "whole-sequence tiles rely on one sublane tile per sequence"
    tm, tn = t, 2 * FFN_SUB
    n_tiles = D_FF // tn
    state_blk = (batch, 2, tn)
    state_idx = lambda half: (lambda j, i: (0, 0, j + half * n_tiles))
    col = lambda half: (lambda j, i: (0, j + half * n_tiles))
    kern = functools.partial(_ffn_up_kernel, tm=tm, seq=seq)
    state_out = jax.ShapeDtypeStruct((batch, 2, D_FF), F32)
    return pl.pallas_call(
        kern,
        grid=(n_tiles, t // tm),
        in_specs=[
            pl.BlockSpec((tm, D_MODEL), lambda j, i: (i, 0)),
            pl.BlockSpec((D_MODEL, tn), col(0)),
            pl.BlockSpec((D_MODEL, tn), col(1)),
            pl.BlockSpec((3, tn), col(0)),
            pl.BlockSpec((3, tn), col(1)),
            pl.BlockSpec((1, tn), col(0)),
            pl.BlockSpec((1, tn), col(1)),
            pl.BlockSpec(state_blk, state_idx(0)),
            pl.BlockSpec(state_blk, state_idx(1)),
        ],
        out_specs=[
            pl.BlockSpec((tm, tn), lambda j, i: (i, j)),
            pl.BlockSpec(state_blk, state_idx(0)),
            pl.BlockSpec(state_blk, state_idx(0)),
        ],
        out_shape=[jax.ShapeDtypeStruct((t, D_FF), BF16), state_out, state_out],
        compiler_params=_cparams(("parallel", "arbitrary")),
        name="ffn_up_conv",
    )(h, w_up, w_up, conv_w, conv_w, conv_b, conv_b, conv_prev, conv_prev)


def _ffn_down_kernel(a_ref, w_ref, x_ref, g_ref, o_ref):
    acc = jnp.dot(a_ref[...], w_ref[...], preferred_element_type=F32)
    o_ref[...] = x_ref[...] + _rms(acc, g_ref[...])


def _ffn_down(act, w_down, x1, g_post, *, tm=256):
    t = x1.shape[0]
    return pl.pallas_call(
        _ffn_down_kernel,
        grid=(t // tm,),
        in_specs=[
            pl.BlockSpec((tm, D_FF), lambda i: (i, 0)),
            pl.BlockSpec((D_FF, D_MODEL), lambda i: (0, 0), pipeline_mode=pl.Buffered(1)),
            pl.BlockSpec((tm, D_MODEL), lambda i: (i, 0)),
            pl.BlockSpec((1, D_MODEL), lambda i: (0, 0)),
        ],
        out_specs=pl.BlockSpec((tm, D_MODEL), lambda i: (i, 0)),
        out_shape=jax.ShapeDtypeStruct((t, D_MODEL), F32),
        compiler_params=_cparams(("parallel",)),
        name="ffn_down",
    )(act, w_down, x1, g_post)


def _pad_rows(w, row0, total):
    return jnp.zeros((total, w.shape[1]), w.dtype).at[row0:row0 + w.shape[0]].set(w)


def _prepare_layer(p, l):
    w_in_t = p["w_in"][l].T.astype(BF16)
    q0 = MAIN_COLS + 3 * LORA
    w_in_tail = jnp.concatenate([
        w_in_t[q0:], w_in_t[MAIN_COLS:q0], jnp.zeros((LORA_PAD - 3 * LORA, D_MODEL), BF16)], axis=0)
    mu = p["rk_mu"][l]
    vec_rows = [mu[:B_WIDTH], mu[B_WIDTH:2 * B_WIDTH], mu[2 * B_WIDTH:3 * B_WIDTH], p["rk_w0"][l],
                p["rk_a0"][l], p["rk_kk"][l], p["rk_ka"][l], p["rk_rk"][l].reshape(B_WIDTH),
                p["rk_lnx_g"][l], p["rk_lnx_b"][l]]
    vecs = jnp.zeros((16, B_WIDTH), F32).at[:len(vec_rows)].set(jnp.stack(vec_rows))
    mu_l = jnp.zeros((SUBLANES, LORA_PAD), F32).at[0, :3 * LORA].set(mu[3 * B_WIDTH:])
    head_of = jnp.arange(LANES) // B_HEAD_DIM
    seg_ones = (head_of[:, None] == head_of[None, :]).astype(BF16)
    tril = jnp.tril(jnp.ones((CHUNK, CHUNK), bool))
    ws = jnp.where(tril[None], p["gm_ws"][l], 0.0)
    bs = p["gm_bs"][l]
    return {
        "g_mix_pre": p["norm_mix_pre"][l][None], "g_mix_post": p["norm_mix_post"][l][None],
        "g_ffn_pre": p["norm_ffn_pre"][l][None], "g_ffn_post": p["norm_ffn_post"][l][None],
        "g_mem": p["norm_mem"][l][None],
        "w_in": w_in_t, "w_in_tail": w_in_tail,
        "w_out": p["w_out"][l].astype(BF16),
        "w_mkv": jnp.concatenate([p["w_mem_k"][l], p["w_mem_v"][l]], axis=1).astype(BF16),
        "gm_ln_g": p["gm_ln_g"][l][None], "gm_ln_b": p["gm_ln_b"][l][None],
        "gm_ws": ws, "gm_bs": bs,
        "rk_vecs": vecs, "rk_mu_l": mu_l,
        "rk_w2": _pad_rows(p["rk_w2"][l], 0, LORA_PAD).astype(BF16),
        "rk_a2": _pad_rows(p["rk_a2"][l], LORA, LORA_PAD).astype(BF16),
        "rk_g2": _pad_rows(p["rk_g2"][l], 2 * LORA, LORA_PAD).astype(BF16),
        "seg_ones": seg_ones,
        "w_up": p["ffn_w_up"][l].astype(BF16),
        "conv_w": p["ffn_conv_w"][l], "conv_b": p["ffn_conv_b"][l][None],
        "w_down": p["ffn_w_down"][l].astype(BF16),
    }


def _spatial_weights(lw, seq):
    ws, bs = lw["gm_ws"], lw["gm_bs"]
    if seq >= CHUNK:
        w_blk, b_rows = ws, bs
    else:
        reps = CHUNK // seq
        eye = jnp.eye(reps, dtype=ws.dtype)
        w_blk = jnp.einsum("ab,hts->hatbs", eye, ws[:, :seq, :seq]).reshape(A_HEADS, CHUNK, CHUNK)
        b_rows = jnp.tile(bs[:, :seq], (1, reps))
    bias_full = jnp.repeat(b_rows.T, A_HEAD_DIM, axis=1)
    return w_blk.astype(BF16), bias_full


def _decoder_layer(x, mem_k, mem_v, shift_prev, wkv_prev, conv_prev, lw):
    batch, seq, _ = x.shape
    x2 = x.reshape(batch * seq, D_MODEL)
    proj, proj_tail = _in_proj(x2, lw["g_mix_pre"], lw["w_in"], lw["w_in_tail"])
    w_sp, bias_full = _spatial_weights(lw, seq)
    a_out, a_v = _group_a(proj, lw["gm_ln_g"], lw["gm_ln_b"], w_sp, bias_full)
    shift_parts = [
        shift_prev[:, None, :B_WIDTH], shift_prev[:, None, B_WIDTH:2 * B_WIDTH],
        shift_prev[:, None, 2 * B_WIDTH:3 * B_WIDTH],
        jnp.pad(shift_prev[:, None, 3 * B_WIDTH:], ((0, 0), (0, 0), (0, LORA_PAD - 3 * LORA))),
    ]
    b_out, wkv_new, shift_new = _rwkv(proj, proj_tail, shift_parts, wkv_prev, lw, batch=batch, seq=seq)
    c_out = _attention(proj_tail, mem_k, mem_v, batch=batch, seq=seq)
    x1, h = _out_proj(a_out, b_out, c_out, lw["w_out"], x2, lw["g_mix_post"], lw["g_ffn_pre"])
    act, conv_g, conv_v = _ffn_up(h, lw["w_up"], lw["conv_w"], lw["conv_b"], conv_prev, batch=batch, seq=seq)
    y = _ffn_down(act, lw["w_down"], x1, lw["g_ffn_post"])

    chunk_start = ((seq - 1) // CHUNK) * CHUNK
    chunk_v = a_v.reshape(batch, seq, A_WIDTH)[:, chunk_start:].reshape(batch, -1, A_HEADS, A_HEAD_DIM)
    conv_new = jnp.concatenate([conv_g, conv_v], axis=-1)
    return y.reshape(batch, seq, D_MODEL), chunk_v, shift_new, wkv_new, conv_new


def kernel(x_prompt, x_sample, mem_prompt, cache_mem_k, cache_mem_v, state_shift, state_wkv, state_conv,
           norm_mix_pre, norm_mix_post, norm_ffn_pre, norm_ffn_post, norm_mem, w_in, w_out, w_mem_k, w_mem_v,
           gm_ln_g, gm_ln_b, gm_ws, gm_bs, rk_mu, rk_w0, rk_w2, rk_a0, rk_a2, rk_g2, rk_kk, rk_ka, rk_rk,
           rk_lnx_g, rk_lnx_b, ffn_w_up, ffn_conv_w, ffn_conv_b, ffn_w_down):
    params = dict(
        norm_mix_pre=norm_mix_pre, norm_mix_post=norm_mix_post, norm_ffn_pre=norm_ffn_pre,
        norm_ffn_post=norm_ffn_post, norm_mem=norm_mem, w_in=w_in, w_out=w_out, w_mem_k=w_mem_k,
        w_mem_v=w_mem_v, gm_ln_g=gm_ln_g, gm_ln_b=gm_ln_b, gm_ws=gm_ws, gm_bs=gm_bs, rk_mu=rk_mu,
        rk_w0=rk_w0, rk_w2=rk_w2, rk_a0=rk_a0, rk_a2=rk_a2, rk_g2=rk_g2, rk_kk=rk_kk, rk_ka=rk_ka,
        rk_rk=rk_rk, rk_lnx_g=rk_lnx_g, rk_lnx_b=rk_lnx_b, ffn_w_up=ffn_w_up, ffn_conv_w=ffn_conv_w,
        ffn_conv_b=ffn_conv_b, ffn_w_down=ffn_w_down)
    depth = w_in.shape[0]
    bp = x_prompt.shape[0]
    y_p, y_s = x_prompt, x_sample
    outs = [[] for _ in range(10)]
    for l in range(depth):
        lw = _prepare_layer(params, l)
        mem2 = mem_prompt.reshape(bp * MEM_LEN, D_MODEL)
        mkv = _norm_matmul(mem2, lw["g_mem"], lw["w_mkv"], tm=512, tn=2 * C_WIDTH)
        mk = mkv[:, :C_WIDTH].reshape(bp, MEM_LEN, C_WIDTH)
        mv = mkv[:, C_WIDTH:].reshape(bp, MEM_LEN, C_WIDTH)
        zero_shift = jnp.zeros((bp, B_PROJ), x_prompt.dtype)
        zero_wkv = jnp.zeros((bp, B_HEADS, B_HEAD_DIM, B_HEAD_DIM), F32)
        zero_conv = jnp.zeros((bp, 2, 2 * D_FF), x_prompt.dtype)
        y_p, cv, sh, wkv, conv = _decoder_layer(y_p, mk, mv, zero_shift, zero_wkv, zero_conv, lw)
        mem_shape = (bp, MEM_LEN, C_HEADS, C_HEAD_DIM)
        for lst, val in zip(outs[:6], (mk.reshape(mem_shape), mv.reshape(mem_shape), cv, sh, wkv, conv)):
            lst.append(val)
        n_s = x_sample.shape[0]
        cache_k = cache_mem_k[l].reshape(n_s, MEM_LEN * C_HEADS, C_HEAD_DIM)
        cache_v = cache_mem_v[l].reshape(n_s, MEM_LEN * C_HEADS, C_HEAD_DIM)
        y_s, cv, sh, wkv, conv = _decoder_layer(y_s, cache_k, cache_v, state_shift[l],
                                                state_wkv[l], state_conv[l], lw)
        for lst, val in zip(outs[6:], (cv, sh, wkv, conv)):
            lst.append(val)
    return (y_p, y_s) + tuple(jnp.stack(o) for o in outs)
```

```python
import functools
import math

import jax
import jax.numpy as jnp
from jax import lax
from jax.experimental import pallas as pl
from jax.experimental.pallas import tpu as pltpu

D_MODEL = 2048
MEM_LEN = 256
CHUNK = 128
A_HEADS, A_HEAD_DIM = 4, 128
A_WIDTH = A_HEADS * A_HEAD_DIM
B_HEADS, B_HEAD_DIM = 16, 64
B_WIDTH = B_HEADS * B_HEAD_DIM
LORA = 64
B_PROJ = 3 * B_WIDTH + 3 * LORA
C_HEADS, C_HEAD_DIM = 4, 128
C_WIDTH = C_HEADS * C_HEAD_DIM
D_FF = 5632
RMS_EPS = 1e-6
LN_EPS = 1e-5
GN_EPS = 64e-5
DECAY_OFFSET = 0.5

LANES = 128
SUBLANES = 8
VMEM_LIMIT_BYTES = 56 * 1024 * 1024

LORA_PAD = 256
COL_A = 0
COL_R = 2 * A_WIDTH
MAIN_COLS = COL_R + 3 * B_WIDTH
COL_Q = 0
COL_L = C_WIDTH
TAIL_COLS = COL_L + LORA_PAD
PROJ_TN = 1024

WKV_CHUNK = 64
FFN_N_TILES = 2
FFN_TN = D_FF // FFN_N_TILES
FFN_SUB = 256

F32 = jnp.float32
BF16 = jnp.bfloat16


def _cparams(sem):
    return pltpu.CompilerParams(dimension_semantics=sem, vmem_limit_bytes=VMEM_LIMIT_BYTES)


def _rms(x, g):
    return x * lax.rsqrt(jnp.mean(x * x, axis=-1, keepdims=True) + RMS_EPS) * g


def _norm_matmul_kernel(x_ref, g_ref, w_ref, o_ref, h_ref):
    @pl.when(pl.program_id(1) == 0)
    def _():
        h_ref[...] = _rms(x_ref[...], g_ref[...]).astype(BF16)

    o_ref[...] = jnp.dot(h_ref[...], w_ref[...], preferred_element_type=F32)


def _norm_matmul(x, g, w, *, tm, tn):
    t, k = x.shape
    n = w.shape[1]
    return pl.pallas_call(
        _norm_matmul_kernel,
        grid=(t // tm, n // tn),
        in_specs=[
            pl.BlockSpec((tm, k), lambda i, j: (i, 0)),
            pl.BlockSpec((1, k), lambda i, j: (0, 0)),
            pl.BlockSpec((k, tn), lambda i, j: (0, j)),
        ],
        out_specs=pl.BlockSpec((tm, tn), lambda i, j: (i, j)),
        out_shape=jax.ShapeDtypeStruct((t, n), F32),
        scratch_shapes=[pltpu.VMEM((tm, k), BF16)],
        compiler_params=_cparams(("parallel", "arbitrary")),
        name="norm_matmul",
    )(x, g, w)


def _in_proj_kernel(x_ref, g_ref, wm_ref, wt_ref, om_ref, ot_ref, h_ref, *, n_main):
    j = pl.program_id(1)

    @pl.when(j == 0)
    def _():
        h_ref[...] = _rms(x_ref[...], g_ref[...]).astype(BF16)

    nt = (((1,), (1,)), ((), ()))

    @pl.when(j < n_main)
    def _():
        om_ref[...] = lax.dot_general(h_ref[...], wm_ref[...], nt, preferred_element_type=F32)

    @pl.when(j == n_main)
    def _():
        ot_ref[...] = lax.dot_general(h_ref[...], wt_ref[...], nt, preferred_element_type=F32)


def _in_proj(x, g, w_main, w_tail, *, tm=1024):
    t, k = x.shape
    n_main = MAIN_COLS // PROJ_TN
    main_tile = lambda j: jnp.minimum(j, n_main - 1)
    return pl.pallas_call(
        functools.partial(_in_proj_kernel, n_main=n_main),
        grid=(t // tm, n_main + 1),
        in_specs=[
            pl.BlockSpec((tm, k), lambda i, j: (i, 0)),
            pl.BlockSpec((1, k), lambda i, j: (0, 0)),
            pl.BlockSpec((PROJ_TN, k), lambda i, j: (main_tile(j), 0)),
            pl.BlockSpec((TAIL_COLS, k), lambda i, j: (0, 0)),
        ],
        out_specs=[
            pl.BlockSpec((tm, PROJ_TN), lambda i, j: (i, main_tile(j))),
            pl.BlockSpec((tm, TAIL_COLS), lambda i, j: (i, 0)),
        ],
        out_shape=[jax.ShapeDtypeStruct((t, MAIN_COLS), F32), jax.ShapeDtypeStruct((t, TAIL_COLS), F32)],
        scratch_shapes=[pltpu.VMEM((tm, k), BF16)],
        compiler_params=_cparams(("parallel", "arbitrary")),
        name="in_proj",
    )(x, g, w_main, w_tail)


def _group_a_kernel(u_ref, v_ref, g_ref, b_ref, w_ref, bias_ref, o_ref, vout_ref):
    u = jax.nn.gelu(u_ref[...], approximate=True)
    v = jax.nn.gelu(v_ref[...], approximate=True)
    mean = jnp.mean(v, axis=-1, keepdims=True)
    d = v - mean
    var = jnp.mean(d * d, axis=-1, keepdims=True)
    vn = d * lax.rsqrt(var + LN_EPS) * g_ref[...] + b_ref[...]
    vout_ref[...] = vn
    vb = vn.astype(BF16)
    for c in range(u.shape[0] // CHUNK):
        rs = slice(c * CHUNK, (c + 1) * CHUNK)
        for h in range(A_HEADS):
            hs = slice(h * A_HEAD_DIM, (h + 1) * A_HEAD_DIM)
            mixed = jnp.dot(w_ref[h], vb[rs, hs], preferred_element_type=F32) + bias_ref[:, hs]
            o_ref[rs, hs] = (u[rs, hs] * mixed).astype(BF16)


def _group_a(proj, ln_g, ln_b, w_sp, bias_full, *, rows=4 * CHUNK):
    t = proj.shape[0]
    return pl.pallas_call(
        _group_a_kernel,
        grid=(t // rows,),
        in_specs=[
            pl.BlockSpec((rows, A_WIDTH), lambda i: (i, COL_A // A_WIDTH)),
            pl.BlockSpec((rows, A_WIDTH), lambda i: (i, COL_A // A_WIDTH + 1)),
            pl.BlockSpec((1, A_WIDTH), lambda i: (0, 0)),
            pl.BlockSpec((1, A_WIDTH), lambda i: (0, 0)),
            pl.BlockSpec((A_HEADS, CHUNK, CHUNK), lambda i: (0, 0, 0)),
            pl.BlockSpec((CHUNK, A_WIDTH), lambda i: (0, 0)),
        ],
        out_specs=[
            pl.BlockSpec((rows, A_WIDTH), lambda i: (i, 0)),
            pl.BlockSpec((rows, A_WIDTH), lambda i: (i, 0)),
        ],
        out_shape=[
            jax.ShapeDtypeStruct((t, A_WIDTH), BF16),
            jax.ShapeDtypeStruct((t, A_WIDTH), F32),
        ],
        compiler_params=_cparams(("parallel",)),
        name="group_a",
    )(proj, proj, ln_g, ln_b, w_sp, bias_full)


def _softmax_rows(s):
    e = jnp.exp(s - jnp.max(s, axis=-1, keepdims=True))
    return e / jnp.sum(e, axis=-1, keepdims=True)


def _attn_kernel(q_ref, k_ref, v_ref, o_ref, *, n_b, tq, head_major_rows):
    scale = C_HEAD_DIM ** -0.5
    heads = range(C_HEADS)
    lanes = lambda h: slice(h * C_HEAD_DIM, (h + 1) * C_HEAD_DIM)
    if not head_major_rows:
        chains = [(g, h) for g in range(n_b) for h in heads]
        rows = lambda g: slice(g * tq, (g + 1) * tq)
        s = [lax.dot_general(q_ref[rows(g), lanes(h)].astype(BF16), k_ref[g, :, lanes(h)].astype(BF16), _NT,
                             preferred_element_type=F32) * scale for g, h in chains]
        p = [_softmax_rows(m).astype(BF16) for m in s]
        o = [jnp.dot(m, v_ref[g, :, lanes(h)].astype(BF16), preferred_element_type=F32)
             for m, (g, h) in zip(p, chains)]
        for m, (g, h) in zip(o, chains):
            o_ref[rows(g), lanes(h)] = m.astype(BF16)
        return
    row = lax.broadcasted_iota(jnp.int32, (C_HEADS * tq, 1), 0)
    col = lax.broadcasted_iota(jnp.int32, (1, C_HEADS * MEM_LEN), 1)
    own = (col & (C_HEADS - 1)) == (row >> int(math.log2(tq)))
    q = [q_ref[g * tq:(g + 1) * tq, :] for g in range(n_b)]
    qs = [jnp.concatenate([m[:, lanes(h)] for h in heads], axis=0).astype(BF16) for m in q]
    s = [lax.dot_general(m, k_ref[g].astype(BF16), _NT, preferred_element_type=F32) * scale
         for g, m in enumerate(qs)]
    p = [_softmax_rows(jnp.where(own, m, -1e30)).astype(BF16) for m in s]
    o = [jnp.dot(m, v_ref[g].astype(BF16), preferred_element_type=F32) for g, m in enumerate(p)]
    for g, m in enumerate(o):
        for h in heads:
            o_ref[g * tq:(g + 1) * tq, lanes(h)] = m[h * tq:(h + 1) * tq].astype(BF16)


def _attention(proj, mem_k, mem_v, *, batch, seq):
    head_major_rows = mem_k.shape[-1] == C_HEAD_DIM
    if head_major_rows:
        tq, n_b = seq, 8
    else:
        tq, n_b = 512, 1
    n_q = seq // tq
    mem_blk = (n_b,) + mem_k.shape[1:]
    return pl.pallas_call(
        functools.partial(_attn_kernel, n_b=n_b, tq=tq, head_major_rows=head_major_rows),
        grid=(batch // n_b, n_q),
        in_specs=[
            pl.BlockSpec((n_b * tq, C_WIDTH), lambda b, i: (b * n_q + i, COL_Q // C_WIDTH)),
            pl.BlockSpec(mem_blk, lambda b, i: (b, 0, 0)),
            pl.BlockSpec(mem_blk, lambda b, i: (b, 0, 0)),
        ],
        out_specs=pl.BlockSpec((n_b * tq, C_WIDTH), lambda b, i: (b * n_q + i, 0)),
        out_shape=jax.ShapeDtypeStruct((batch * seq, C_WIDTH), BF16),
        compiler_params=_cparams(("parallel", "arbitrary")),
        name="mem_attention",
    )(proj, mem_k, mem_v)


_V_MU_R, _V_MU_K, _V_MU_V, _V_W0, _V_A0, _V_KK, _V_KA, _V_RK, _V_LNG, _V_LNB = range(10)


_NN = (((1,), (0,)), ((), ()))
_NT = (((1,), (1,)), ((), ()))
_TN = (((0,), (0,)), ((), ()))
HEAD_PAIRS = B_HEADS // 2


def _split(x):
    hi = x.astype(BF16)
    return hi, (x - hi.astype(F32)).astype(BF16)


def _b(x):
    return x.astype(BF16)


def _bdot(a, b, dims=_NN):
    return lax.dot_general(a, b, dims, preferred_element_type=F32)


def _rwkv_kernel(pr_ref, pk_ref, pv_ref, pl_ref, sr_ref, sk_ref, sv_ref, sl_ref, wkv_ref,
                 vec_ref, mul_ref, w2_ref, a2_ref, g2_ref, seg_ref,
                 o_ref, so_ref, nr_ref, nk_ref, nv_ref, nl_ref,
                 cr_ref, ck_ref, cv_ref, cl_ref, sbd_ref, *, n_blk, n_sc, groups, chunk, whole_seq):
    rows = groups * chunk
    n_items = n_blk * n_sc
    block_of = lambda w: w % n_blk
    rows_of = lambda w: slice((w // n_blk) * rows, (w // n_blk + 1) * rows)
    first = pl.program_id(1) == 0
    last = pl.program_id(1) == pl.num_programs(1) - 1
    hd = B_HEAD_DIM
    n_seq = n_blk * groups

    @pl.when(first)
    def _():
        zero = jnp.zeros((hd, hd), F32)
        for g in range(n_seq):
            for q in range(HEAD_PAIRS):
                top = jnp.concatenate([wkv_ref[g, 2 * q], zero], axis=1)
                bot = jnp.concatenate([zero, wkv_ref[g, 2 * q + 1]], axis=1)
                sbd_ref[g, q] = jnp.concatenate([top, bot], axis=0)
        if not whole_seq:
            for s_ref, carry_ref in ((sr_ref, cr_ref), (sk_ref, ck_ref), (sv_ref, cv_ref), (sl_ref, cl_ref)):
                for blk in range(n_blk):
                    carry_ref[blk] = jnp.broadcast_to(s_ref[blk], (SUBLANES, s_ref.shape[-1]))

    row_id = lax.broadcasted_iota(jnp.int32, (rows, 1), 0)
    col_id = lax.broadcasted_iota(jnp.int32, (1, rows), 1)
    chunk_bits = int(math.log2(chunk))
    incl_b = (((row_id >> chunk_bits) == (col_id >> chunk_bits)) & (col_id <= row_id)).astype(BF16)
    vec = lambda i: vec_ref[i:i + 1, :]
    seg_ones = seg_ref[...]

    def seg_sum(x):
        slabs = jnp.concatenate([x[:, q * LANES:(q + 1) * LANES] for q in range(HEAD_PAIRS)], axis=0)
        hi, lo = _split(slabs)
        s = (jnp.dot(hi, seg_ones, preferred_element_type=F32)
             + jnp.dot(lo, seg_ones, preferred_element_type=F32))
        return jnp.concatenate([s[q * rows:(q + 1) * rows] for q in range(HEAD_PAIRS)], axis=1)

    def prologue(item, out):
        blk, rs = block_of(item), rows_of(item)
        seqs = slice(blk * groups, (blk + 1) * groups)

        def prev_rows(x, s_ref, carry_ref):
            width = x.shape[1]
            if whole_seq:
                start = jnp.broadcast_to(s_ref[seqs], (groups, chunk, width)).reshape(rows, width)
                return jnp.where((row_id & (chunk - 1)) == 0, start, pltpu.roll(x, 1, 0))

            ext = jnp.concatenate([carry_ref[blk], x], axis=0)
            prev = pltpu.roll(ext, 1, 0)[SUBLANES:]
            carry_ref[blk] = x[rows - SUBLANES:]
            return prev

        def shifted(p_ref, s_ref, carry_ref, n_ref, mu):
            x = p_ref[blk, rs]
            if whole_seq:
                n_ref[seqs] = x.reshape(groups, chunk, x.shape[1])[:, chunk - 1:, :]
            else:
                n_ref[blk] = x[rows - 1:]
            return x + (prev_rows(x, s_ref, carry_ref) - x) * mu

        lo = shifted(pl_ref, sl_ref, cl_ref, nl_ref, mul_ref[0:1, :])
        dw = jnp.dot(jnp.tanh(lo).astype(BF16), w2_ref[...], preferred_element_type=F32)
        da = jnp.dot(lo.astype(BF16), a2_ref[...], preferred_element_type=F32)
        gate = jnp.dot(jax.nn.sigmoid(lo).astype(BF16), g2_ref[...], preferred_element_type=F32)
        yield
        w_log = -jax.nn.softplus(-(vec(_V_W0) + dw)) - DECAY_OFFSET
        log_decay = -jnp.exp(w_log)
        ld_hi, ld_mid = _split(log_decay)
        ld_lo = (log_decay - ld_hi.astype(F32) - ld_mid.astype(F32)).astype(BF16)
        cum = (jnp.dot(incl_b, ld_hi, preferred_element_type=F32)
               + jnp.dot(incl_b, ld_mid, preferred_element_type=F32)
               + jnp.dot(incl_b, ld_lo, preferred_element_type=F32))
        yield
        a = jax.nn.sigmoid(vec(_V_A0) + da)
        k = shifted(pk_ref, sk_ref, ck_ref, nk_ref, vec(_V_MU_K))
        kk = k * vec(_V_KK)
        kk = kk / jnp.maximum(jnp.sqrt(seg_sum(kk * kk)), 1e-12)
        yield
        p_incl = jnp.exp(cum)
        p_inv = jnp.exp(-cum)
        a_t = -kk * jnp.exp(cum - log_decay)
        b_t = kk * a * p_inv
        yield
        k = k * (1.0 + (a - 1.0) * vec(_V_KA))
        k_t = k * p_inv
        r = shifted(pr_ref, sr_ref, cr_ref, nr_ref, vec(_V_MU_R))
        r_t = r * p_incl
        yield
        v = shifted(pv_ref, sv_ref, cv_ref, nv_ref, vec(_V_MU_V))
        bonus = seg_sum(r * k * vec(_V_RK)) * v
        out.update(a_t=a_t, b_t=b_t, k_t=k_t, r_t=r_t, v=v, p_incl=p_incl, bonus=bonus, gate=gate)

    pr = 2 * rows
    row2 = lax.broadcasted_iota(jnp.int32, (pr, 1), 0)
    col2 = lax.broadcasted_iota(jnp.int32, (1, pr), 1)
    t2, s2 = row2 & (rows - 1), col2 & (rows - 1)
    same2 = ((row2 >> chunk_bits) == (col2 >> chunk_bits))
    incl2 = same2 & (s2 <= t2)
    strict2 = same2 & (s2 < t2)
    eye2 = (row2 == col2).astype(F32)
    left = lax.broadcasted_iota(jnp.int32, (rows, LANES), 1) < hd

    def bd(x):
        zero = jnp.zeros_like(x)
        return jnp.concatenate([jnp.where(left, x, zero), jnp.where(left, zero, x)], axis=0)

    def group_rows(mats, g):
        starts = [hh * rows + g * chunk for hh in (0, 1)]
        return jnp.concatenate([m[i:i + chunk] for m in mats for i in starts], axis=0)

    n_sq = chunk_bits - 1
    col4 = lax.broadcasted_iota(jnp.int32, (1, 2 * pr), 1)
    incl4 = ((row2 >> chunk_bits) == ((col4 & (pr - 1)) >> chunk_bits)) & ((col4 & (rows - 1)) <= t2)
    pairs = range(HEAD_PAIRS)
    lanes_of = [slice(q * LANES, (q + 1) * LANES) for q in pairs]

    def recurrence(item, pro, qs):
        s0 = block_of(item) * groups
        p_incl = pro["p_incl"]
        pairs = range(len(qs))
        bds = [[bd(pro[name][:, lanes_of[q]]) for name in ("a_t", "r_t", "b_t", "k_t", "v")] for q in qs]
        ar_s = [_b(jnp.concatenate([m[0], m[1]], axis=0)) for m in bds]
        bk_s = [_b(jnp.concatenate([m[2], m[3]], axis=0)) for m in bds]
        v_s = [_b(m[4]) for m in bds]
        if groups == 1:
            ms = [_bdot(ar_s[q], jnp.concatenate([bk_s[q], _b(sbd_ref[s0, qs[q]])], axis=0), _NT)
                  for q in pairs]
        else:
            ms = [_bdot(ar_s[q], bk_s[q], _NT) for q in pairs]
        yield
        a_ab = [jnp.where(strict2, m[:pr, :pr], 0.0) for m in ms]
        a_ak = [jnp.where(strict2, m[:pr, pr:2 * pr], 0.0) for m in ms]
        a_r = [jnp.where(incl4, m[pr:, :2 * pr], 0.0) for m in ms]
        inv = [eye2 + n for n in a_ab]
        pw_s = [_b(_bdot(_b(n), _b(n))) for n in a_ab]
        yield
        for _ in range(1, n_sq):
            both = [_bdot(p, jnp.concatenate([p, _b(i)], axis=1)) for p, i in zip(pw_s, inv)]
            inv = [i + m[:, pr:] for i, m in zip(inv, both)]
            pw_s = [_b(m[:, :pr]) for m in both]
            yield
        inv = [i + _bdot(p, _b(i)) for i, p in zip(inv, pw_s)]
        yield
        if groups == 1:
            x0, y0 = [m[:pr, 2 * pr:] for m in ms], [m[pr:, 2 * pr:] for m in ms]
        else:
            x0, y0 = [], []
            for q in pairs:
                x_parts, y_parts = [None] * (2 * groups), [None] * (2 * groups)
                for g in range(groups):
                    xy = _bdot(_b(group_rows(bds[q][:2], g)), _b(sbd_ref[s0 + g, qs[q]]), _NT)
                    for hh in (0, 1):
                        x_parts[hh * groups + g] = xy[hh * chunk:(hh + 1) * chunk]
                        y_parts[hh * groups + g] = xy[(2 + hh) * chunk:(3 + hh) * chunk]
                x0.append(jnp.concatenate(x_parts, axis=0))
                y0.append(jnp.concatenate(y_parts, axis=0))
        yield
        rhs = [_b(x0[q] + _bdot(_b(a_ak[q]), v_s[q])) for q in pairs]
        u = [_bdot(_b(inv[q]), rhs[q]) for q in pairs]
        yield
        uv_s = [jnp.concatenate([_b(u[q]), v_s[q]], axis=0) for q in pairs]
        y2 = [y0[q] + _bdot(_b(a_r[q]), uv_s[q]) for q in pairs]
        for q in pairs:
            pro["y"][qs[q]] = y2[q][:rows] + y2[q][rows:]
        yield
        for q in pairs:
            ls = lanes_of[qs[q]]
            if groups == 1:
                ds = _bdot(uv_s[q], bk_s[q], _TN)
                sbd_ref[s0, qs[q]] = (sbd_ref[s0, qs[q]] + ds) * p_incl[rows - 1:rows, ls]
            else:
                for g in range(groups):
                    ds = _bdot(_b(group_rows((u[q], bds[q][4]), g)),
                               _b(group_rows((bds[q][2], bds[q][3]), g)), _TN)
                    end = (g + 1) * chunk - 1
                    sbd_ref[s0 + g, qs[q]] = (sbd_ref[s0 + g, qs[q]] + ds) * p_incl[end:end + 1, ls]

    def finish(item, pro):
        inv_n = 1.0 / B_HEAD_DIM
        y = jnp.concatenate([pro["y"][q] for q in range(HEAD_PAIRS)], axis=1)
        d = y - seg_sum(y) * inv_n
        yield
        var = seg_sum(d * d) * inv_n
        yield
        yn = d * lax.rsqrt(var + GN_EPS) * vec(_V_LNG) + vec(_V_LNB)
        o_ref[block_of(item), rows_of(item)] = ((yn + pro["bonus"]) * pro["gate"]).astype(BF16)

    def interleave(*gens):
        live = list(gens)
        while live:
            for gen in list(live):
                if next(gen, StopIteration) is StopIteration:
                    live.remove(gen)

    pros = [dict(y={}) for _ in range(n_items)]
    interleave(prologue(0, pros[0]))
    for item in range(n_items):
        work = [recurrence(item, pros[item], list(pairs))]
        if item + 1 < n_items:
            work.append(prologue(item + 1, pros[item + 1]))
        if item > 0:
            work.append(finish(item - 1, pros[item - 1]))
        interleave(*work)
    interleave(finish(n_items - 1, pros[n_items - 1]))

    @pl.when(last)
    def _():
        for g in range(n_seq):
            for q in range(HEAD_PAIRS):
                sq = sbd_ref[g, q]
                so_ref[g, 2 * q] = sq[:hd, :hd]
                so_ref[g, 2 * q + 1] = sq[hd:, hd:]


def _rwkv(proj, proj_tail, shift_parts, wkv_prev, lw, *, batch, seq):
    whole_seq = seq <= WKV_CHUNK
    if whole_seq:
        assert seq == SUBLANES, "whole-sequence blocks rely on one sublane tile per sequence"
        chunk, groups = seq, WKV_CHUNK // seq
        n_blk, n_sc, n_chunks = 2, 1, 1
        n_outer = batch // (groups * n_blk)
        lead = (batch // groups, WKV_CHUNK)
    else:
        chunk, groups = WKV_CHUNK, 1
        n_blk, n_sc, n_outer = batch, 2, 1
        n_chunks = seq // (n_sc * WKV_CHUNK)
        lead = (batch, seq)
    proj3, tail3 = proj.reshape(lead + (MAIN_COLS,)), proj_tail.reshape(lead + (TAIL_COLS,))
    n_seq = n_blk * groups
    act = lambda col_blk, width: pl.BlockSpec((n_blk, n_sc * WKV_CHUNK, width), lambda o, c: (o, c, col_blk))
    per_seq = lambda *tail: pl.BlockSpec((n_seq,) + tail, lambda o, c: (o,) + (0,) * len(tail))
    fixed = lambda *shape: pl.BlockSpec(shape, lambda o, c: (0,) * len(shape))
    state = (B_HEADS, B_HEAD_DIM, B_HEAD_DIM)
    kern = functools.partial(_rwkv_kernel, n_blk=n_blk, n_sc=n_sc, groups=groups, chunk=chunk,
                             whole_seq=whole_seq)
    shift_specs = [per_seq(1, B_WIDTH), per_seq(1, B_WIDTH), per_seq(1, B_WIDTH), per_seq(1, LORA_PAD)]
    b_out, wkv_new, *last_rows = pl.pallas_call(
        kern,
        grid=(n_outer, n_chunks),
        in_specs=[
            act(COL_R // B_WIDTH, B_WIDTH), act(COL_R // B_WIDTH + 1, B_WIDTH),
            act(COL_R // B_WIDTH + 2, B_WIDTH), act(COL_L // LORA_PAD, LORA_PAD),
            *shift_specs,
            per_seq(*state),
            fixed(16, B_WIDTH), fixed(SUBLANES, LORA_PAD),
            fixed(LORA_PAD, B_WIDTH), fixed(LORA_PAD, B_WIDTH), fixed(LORA_PAD, B_WIDTH),
            fixed(LANES, LANES),
        ],
        out_specs=[act(0, B_WIDTH), per_seq(*state), *shift_specs],
        out_shape=[
            jax.ShapeDtypeStruct(lead + (B_WIDTH,), BF16),
            jax.ShapeDtypeStruct((batch,) + state, F32),
            *(jax.ShapeDtypeStruct(p.shape, F32) for p in shift_parts),
        ],
        scratch_shapes=[
            pltpu.VMEM((n_blk, SUBLANES, B_WIDTH), F32),
            pltpu.VMEM((n_blk, SUBLANES, B_WIDTH), F32),
            pltpu.VMEM((n_blk, SUBLANES, B_WIDTH), F32),
            pltpu.VMEM((n_blk, SUBLANES, LORA_PAD), F32),
            pltpu.VMEM((n_seq, HEAD_PAIRS, LANES, LANES), F32),
        ],
        compiler_params=_cparams(("parallel", "arbitrary")),
        name="rwkv7",
    )(proj3, proj3, proj3, tail3, *shift_parts, wkv_prev,
      lw["rk_vecs"], lw["rk_mu_l"], lw["rk_w2"], lw["rk_a2"], lw["rk_g2"], lw["seg_ones"])
    shift_new = jnp.concatenate([p[:, 0] for p in last_rows[:3]] + [last_rows[3][:, 0, :3 * LORA]], axis=1)
    return b_out.reshape(batch * seq, B_WIDTH), wkv_new, shift_new


def _out_proj_kernel(a_ref, b_ref, c_ref, w_ref, x_ref, gpost_ref, gffn_ref, x1_ref, h_ref):
    n_sub = 4
    sub = x_ref.shape[0] // n_sub
    halves = [slice(s * sub, (s + 1) * sub) for s in range(n_sub)]

    def project(rs):
        acc = jnp.dot(a_ref[rs, :], w_ref[0:A_WIDTH, :], preferred_element_type=F32)
        acc += jnp.dot(b_ref[rs, :], w_ref[A_WIDTH:A_WIDTH + B_WIDTH, :], preferred_element_type=F32)
        return acc + jnp.dot(c_ref[rs, :], w_ref[A_WIDTH + B_WIDTH:, :], preferred_element_type=F32)

    accs = [project(rs) for rs in halves]
    for rs, acc in zip(halves, accs):
        x1 = x_ref[rs, :] + _rms(acc, gpost_ref[...])
        x1_ref[rs, :] = x1
        h_ref[rs, :] = _rms(x1, gffn_ref[...]).astype(BF16)


def _out_proj(a_out, b_out, c_out, w_out, x, g_post, g_ffn, *, tm=512):
    t = x.shape[0]
    row = lambda i: (i, 0)
    fixed = lambda i: (0, 0)
    return pl.pallas_call(
        _out_proj_kernel,
        grid=(t // tm,),
        in_specs=[
            pl.BlockSpec((tm, A_WIDTH), row),
            pl.BlockSpec((tm, B_WIDTH), row),
            pl.BlockSpec((tm, C_WIDTH), row),
            pl.BlockSpec((D_MODEL, D_MODEL), fixed, pipeline_mode=pl.Buffered(1)),
            pl.BlockSpec((tm, D_MODEL), row),
            pl.BlockSpec((1, D_MODEL), fixed),
            pl.BlockSpec((1, D_MODEL), fixed),
        ],
        out_specs=[pl.BlockSpec((tm, D_MODEL), row), pl.BlockSpec((tm, D_MODEL), row)],
        out_shape=[jax.ShapeDtypeStruct((t, D_MODEL), F32), jax.ShapeDtypeStruct((t, D_MODEL), BF16)],
        compiler_params=_cparams(("parallel",)),
        name="out_proj",
    )(a_out, b_out, c_out, w_out, x, g_post, g_ffn)


def _chunked_dots(h_ref, w_refs, cs, between):
    accs = [None] * len(w_refs)
    for kc in range(h_ref.shape[1] // FFN_SUB):
        ks = slice(kc * FFN_SUB, (kc + 1) * FFN_SUB)
        for n, w_ref in enumerate(w_refs):
            part = jnp.dot(h_ref[:, ks], w_ref[ks, cs], preferred_element_type=F32)
            accs[n] = part if accs[n] is None else accs[n] + part
        between(kc)
    return accs


def _ffn_up_kernel(h_ref, wg_ref, wv_ref, cwg_ref, cwv_ref, cbg_ref, cbv_ref, pg_ref, pv_ref,
                   act_ref, ng_ref, nv_ref, *, tm, seq):
    h = h_ref[...]
    tn = cwg_ref.shape[1]
    n_seq = tm // seq
    tau = lax.broadcasted_iota(jnp.int32, (tm, 1), 0) & (seq - 1)

    def conv(up, cs, cw_ref, cb_ref, p_ref, n_ref):
        width = up.shape[1]
        prev = p_ref[:, :, cs]
        e0 = jnp.broadcast_to(prev[:, 0:1, :], (n_seq, seq, width)).reshape(tm, width)
        e1 = jnp.broadcast_to(prev[:, 1:2, :], (n_seq, seq, width)).reshape(tm, width)
        m1 = jnp.where(tau == 0, e1, pltpu.roll(up, 1, 0))
        m2 = jnp.where(tau == 0, e0, jnp.where(tau == 1, e1, pltpu.roll(up, 2, 0)))
        n_ref[:, :, cs] = up.reshape(n_seq, seq, width)[:, seq - 2:, :]
        return cb_ref[:, cs] + m2 * cw_ref[0:1, cs] + m1 * cw_ref[1:2, cs] + up * cw_ref[2:3, cs]

    subs = [slice(s * FFN_SUB, (s + 1) * FFN_SUB) for s in range(tn // FFN_SUB)]
    ups = [(jnp.dot(h, wg_ref[:, cs], preferred_element_type=F32),
            jnp.dot(h, wv_ref[:, cs], preferred_element_type=F32)) for cs in subs]
    for cs, (up_g, up_v) in zip(subs, ups):
        gate = conv(up_g, cs, cwg_ref, cbg_ref, pg_ref, ng_ref)
        val = conv(up_v, cs, cwv_ref, cbv_ref, pv_ref, nv_ref)
        act_ref[:, cs] = (jax.nn.gelu(gate, approximate=True) * val).astype(BF16)


def _ffn_up_skew_kernel(h_ref, wg_ref, wv_ref, cwg_ref, cwv_ref, cbg_ref, cbv_ref, pg_ref, pv_ref,
                        act_ref, ng_ref, nv_ref, cg_ref, cv_ref, u0_ref, u1_ref,
                        *, tm, tiles_per_seq, n_row_tiles):
    i = pl.program_id(1)
    tn = cwg_ref.shape[1]
    n_sub = tn // FFN_SUB

    @pl.when(i == 0)
    def _():
        for ref in (u1_ref, cg_ref, cv_ref):
            ref[...] = jnp.zeros_like(ref)

    @pl.when(i % tiles_per_seq == 1 % tiles_per_seq)
    def _():
        for carry_ref, p_ref in ((cg_ref, pg_ref), (cv_ref, pv_ref)):
            carry_ref[...] = jnp.concatenate([jnp.zeros((SUBLANES - 2, tn), F32), p_ref[0]], axis=0)

    piece = 64

    def conv(u_ref, r0, us, cs, cw_ref, cb_ref, carry_ref):
        if r0 == 0:
            ext = jnp.concatenate([carry_ref[:, cs], u_ref[0:piece, us]], axis=0)
        else:
            ext = u_ref[r0 - SUBLANES:r0 + piece, us]
        m1 = pltpu.roll(ext, 1, 0)[SUBLANES:]
        m2 = pltpu.roll(ext, 2, 0)[SUBLANES:]
        return (cb_ref[:, cs] + m2 * cw_ref[0:1, cs] + m1 * cw_ref[1:2, cs]
                + ext[SUBLANES:] * cw_ref[2:3, cs])

    n_k = D_MODEL // FFN_SUB
    chunks_per_piece = n_k // (tm // piece)

    def step(new_ref, old_ref):
        for s in range(n_sub):
            cs = slice(s * FFN_SUB, (s + 1) * FFN_SUB)
            gs, vs = cs, slice(tn + s * FFN_SUB, tn + (s + 1) * FFN_SUB)

            def conv_piece(kc):
                if (kc + 1) % chunks_per_piece == 0:
                    r0 = (kc // chunks_per_piece) * piece
                    gate = conv(old_ref, r0, gs, cs, cwg_ref, cbg_ref, cg_ref)
                    val = conv(old_ref, r0, vs, cs, cwv_ref, cbv_ref, cv_ref)
                    act_ref[r0:r0 + piece, cs] = (jax.nn.gelu(gate, approximate=True) * val).astype(BF16)

            if new_ref is None:
                for kc in range(n_k):
                    conv_piece(kc)
            else:
                new_ref[:, gs], new_ref[:, vs] = _chunked_dots(h_ref, (wg_ref, wv_ref), cs, conv_piece)
            for us, carry_ref, n_ref in ((gs, cg_ref, ng_ref), (vs, cv_ref, nv_ref)):
                carry_ref[:, cs] = old_ref[tm - SUBLANES:, us]
                n_ref[0, :, cs] = old_ref[tm - 2:, us]

    bufs = (u0_ref, u1_ref)

    @pl.when((i % 2 == 0) & (i < n_row_tiles))
    def _():
        step(u0_ref, u1_ref)

    @pl.when((i % 2 == 1) & (i < n_row_tiles))
    def _():
        step(u1_ref, u0_ref)

    @pl.when(i == n_row_tiles)
    def _():
        step(None, bufs[(n_row_tiles - 1) % 2])


def _ffn_up_skew(h, w_up, conv_w, conv_b, conv_prev, *, batch, seq, tm=256):
    t = h.shape[0]
    tn, n_tiles = FFN_TN, FFN_N_TILES
    n_m = t // tm
    tiles_per_seq = seq // tm
    prev_tile = lambda i: jnp.maximum(i - 1, 0)
    col = lambda half: (lambda j, i: (0, j + half * n_tiles))
    state_blk = (1, 2, tn)
    state_idx = lambda half: (lambda j, i: (prev_tile(i) // tiles_per_seq, 0, j + half * n_tiles))
    state_out = jax.ShapeDtypeStruct((batch, 2, D_FF), F32)
    up_buf = pltpu.VMEM((tm, 2 * tn), F32)
    return pl.pallas_call(
        functools.partial(_ffn_up_skew_kernel, tm=tm, tiles_per_seq=tiles_per_seq, n_row_tiles=n_m),
        grid=(n_tiles, n_m + 1),
        in_specs=[
            pl.BlockSpec((tm, D_MODEL), lambda j, i: (jnp.minimum(i, n_m - 1), 0)),
            pl.BlockSpec((D_MODEL, tn), col(0), pipeline_mode=pl.Buffered(1)),
            pl.BlockSpec((D_MODEL, tn), col(1), pipeline_mode=pl.Buffered(1)),
            pl.BlockSpec((3, tn), col(0)),
            pl.BlockSpec((3, tn), col(1)),
            pl.BlockSpec((1, tn), col(0)),
            pl.BlockSpec((1, tn), col(1)),
            pl.BlockSpec(state_blk, state_idx(0)),
            pl.BlockSpec(state_blk, state_idx(1)),
        ],
        out_specs=[
            pl.BlockSpec((tm, tn), lambda j, i: (prev_tile(i), j)),
            pl.BlockSpec(state_blk, state_idx(0)),
            pl.BlockSpec(state_blk, state_idx(0)),
        ],
        out_shape=[jax.ShapeDtypeStruct((t, D_FF), BF16), state_out, state_out],
        scratch_shapes=[pltpu.VMEM((SUBLANES, tn), F32), pltpu.VMEM((SUBLANES, tn), F32), up_buf, up_buf],
        compiler_params=_cparams(("parallel", "arbitrary")),
        name="ffn_up_conv",
    )(h, w_up, w_up, conv_w, conv_w, conv_b, conv_b, conv_prev, conv_prev)


def _ffn_up(h, w_up, conv_w, conv_b, conv_prev, *, batch, seq):
    t = h.shape[0]
    if seq > SUBLANES:
        return _ffn_up_skew(h, w_up, conv_w, conv_b, conv_prev, batch=batch, seq=seq)
    assert seq == SUBLANES, "whole-sequence tiles rely on one sublane tile per sequence"
    tm, tn = t, 2 * FFN_SUB
    n_tiles = D_FF // tn
    state_blk = (batch, 2, tn)
    state_idx = lambda half: (lambda j, i: (0, 0, j + half * n_tiles))
    col = lambda half: (lambda j, i: (0, j + half * n_tiles))
    kern = functools.partial(_ffn_up_kernel, tm=tm, seq=seq)
    state_out = jax.ShapeDtypeStruct((batch, 2, D_FF), F32)
    return pl.pallas_call(
        kern,
        grid=(n_tiles, t // tm),
        in_specs=[
            pl.BlockSpec((tm, D_MODEL), lambda j, i: (i, 0)),
            pl.BlockSpec((D_MODEL, tn), col(0)),
            pl.BlockSpec((D_MODEL, tn), col(1)),
            pl.BlockSpec((3, tn), col(0)),
            pl.BlockSpec((3, tn), col(1)),
            pl.BlockSpec((1, tn), col(0)),
            pl.BlockSpec((1, tn), col(1)),
            pl.BlockSpec(state_blk, state_idx(0)),
            pl.BlockSpec(state_blk, state_idx(1)),
        ],
        out_specs=[
            pl.BlockSpec((tm, tn), lambda j, i: (i, j)),
            pl.BlockSpec(state_blk, state_idx(0)),
            pl.BlockSpec(state_blk, state_idx(0)),
        ],
        out_shape=[jax.ShapeDtypeStruct((t, D_FF), BF16), state_out, state_out],
        compiler_params=_cparams(("parallel", "arbitrary")),
        name="ffn_up_conv",
    )(h, w_up, w_up, conv_w, conv_w, conv_b, conv_b, conv_prev, conv_prev)


def _ffn_down_kernel(a_ref, w_ref, x_ref, g_ref, o_ref):
    acc = jnp.dot(a_ref[...], w_ref[...], preferred_element_type=F32)
    o_ref[...] = x_ref[...] + _rms(acc, g_ref[...])


def _ffn_down(act, w_down, x1, g_post, *, tm=256):
    t = x1.shape[0]
    return pl.pallas_call(
        _ffn_down_kernel,
        grid=(t // tm,),
        in_specs=[
            pl.BlockSpec((tm, D_FF), lambda i: (i, 0)),
            pl.BlockSpec((D_FF, D_MODEL), lambda i: (0, 0), pipeline_mode=pl.Buffered(1)),
            pl.BlockSpec((tm, D_MODEL), lambda i: (i, 0)),
            pl.BlockSpec((1, D_MODEL), lambda i: (0, 0)),
        ],
        out_specs=pl.BlockSpec((tm, D_MODEL), lambda i: (i, 0)),
        out_shape=jax.ShapeDtypeStruct((t, D_MODEL), F32),
        compiler_params=_cparams(("parallel",)),
        name="ffn_down",
    )(act, w_down, x1, g_post)


def _pad_rows(w, row0, total):
    return jnp.zeros((total, w.shape[1]), w.dtype).at[row0:row0 + w.shape[0]].set(w)


def _prepare_layer(p, l):
    w_in_t = p["w_in"][l].T.astype(BF16)
    q0 = MAIN_COLS + 3 * LORA
    w_in_tail = jnp.concatenate([
        w_in_t[q0:], w_in_t[MAIN_COLS:q0], jnp.zeros((LORA_PAD - 3 * LORA, D_MODEL), BF16)], axis=0)
    mu = p["rk_mu"][l]
    vec_rows = [mu[:B_WIDTH], mu[B_WIDTH:2 * B_WIDTH], mu[2 * B_WIDTH:3 * B_WIDTH], p["rk_w0"][l],
                p["rk_a0"][l], p["rk_kk"][l], p["rk_ka"][l], p["rk_rk"][l].reshape(B_WIDTH),
                p["rk_lnx_g"][l], p["rk_lnx_b"][l]]
    vecs = jnp.zeros((16, B_WIDTH), F32).at[:len(vec_rows)].set(jnp.stack(vec_rows))
    mu_l = jnp.zeros((SUBLANES, LORA_PAD), F32).at[0, :3 * LORA].set(mu[3 * B_WIDTH:])
    head_of = jnp.arange(LANES) // B_HEAD_DIM
    seg_ones = (head_of[:, None] == head_of[None, :]).astype(BF16)
    tril = jnp.tril(jnp.ones((CHUNK, CHUNK), bool))
    ws = jnp.where(tril[None], p["gm_ws"][l], 0.0)
    bs = p["gm_bs"][l]
    return {
        "g_mix_pre": p["norm_mix_pre"][l][None], "g_mix_post": p["norm_mix_post"][l][None],
        "g_ffn_pre": p["norm_ffn_pre"][l][None], "g_ffn_post": p["norm_ffn_post"][l][None],
        "g_mem": p["norm_mem"][l][None],
        "w_in": w_in_t, "w_in_tail": w_in_tail,
        "w_out": p["w_out"][l].astype(BF16),
        "w_mkv": jnp.concatenate([p["w_mem_k"][l], p["w_mem_v"][l]], axis=1).astype(BF16),
        "gm_ln_g": p["gm_ln_g"][l][None], "gm_ln_b": p["gm_ln_b"][l][None],
        "gm_ws": ws, "gm_bs": bs,
        "rk_vecs": vecs, "rk_mu_l": mu_l,
        "rk_w2": _pad_rows(p["rk_w2"][l], 0, LORA_PAD).astype(BF16),
        "rk_a2": _pad_rows(p["rk_a2"][l], LORA, LORA_PAD).astype(BF16),
        "rk_g2": _pad_rows(p["rk_g2"][l], 2 * LORA, LORA_PAD).astype(BF16),
        "seg_ones": seg_ones,
        "w_up": p["ffn_w_up"][l].astype(BF16),
        "conv_w": p["ffn_conv_w"][l], "conv_b": p["ffn_conv_b"][l][None],
        "w_down": p["ffn_w_down"][l].astype(BF16),
    }


def _spatial_weights(lw, seq):
    ws, bs = lw["gm_ws"], lw["gm_bs"]
    if seq >= CHUNK:
        w_blk, b_rows = ws, bs
    else:
        reps = CHUNK // seq
        eye = jnp.eye(reps, dtype=ws.dtype)
        w_blk = jnp.einsum("ab,hts->hatbs", eye, ws[:, :seq, :seq]).reshape(A_HEADS, CHUNK, CHUNK)
        b_rows = jnp.tile(bs[:, :seq], (1, reps))
    bias_full = jnp.repeat(b_rows.T, A_HEAD_DIM, axis=1)
    return w_blk.astype(BF16), bias_full


def _decoder_layer(x, mem_k, mem_v, shift_prev, wkv_prev, conv_prev, lw):
    batch, seq, _ = x.shape
    x2 = x.reshape(batch * seq, D_MODEL)
    proj, proj_tail = _in_proj(x2, lw["g_mix_pre"], lw["w_in"], lw["w_in_tail"])
    w_sp, bias_full = _spatial_weights(lw, seq)
    a_out, a_v = _group_a(proj, lw["gm_ln_g"], lw["gm_ln_b"], w_sp, bias_full)
    shift_parts = [
        shift_prev[:, None, :B_WIDTH], shift_prev[:, None, B_WIDTH:2 * B_WIDTH],
        shift_prev[:, None, 2 * B_WIDTH:3 * B_WIDTH],
        jnp.pad(shift_prev[:, None, 3 * B_WIDTH:], ((0, 0), (0, 0), (0, LORA_PAD - 3 * LORA))),
    ]
    b_out, wkv_new, shift_new = _rwkv(proj, proj_tail, shift_parts, wkv_prev, lw, batch=batch, seq=seq)
    c_out = _attention(proj_tail, mem_k, mem_v, batch=batch, seq=seq)
    x1, h = _out_proj(a_out, b_out, c_out, lw["w_out"], x2, lw["g_mix_post"], lw["g_ffn_pre"])
    act, conv_g, conv_v = _ffn_up(h, lw["w_up"], lw["conv_w"], lw["conv_b"], conv_prev, batch=batch, seq=seq)
    y = _ffn_down(act, lw["w_down"], x1, lw["g_ffn_post"])

    chunk_start = ((seq - 1) // CHUNK) * CHUNK
    chunk_v = a_v.reshape(batch, seq, A_WIDTH)[:, chunk_start:].reshape(batch, -1, A_HEADS, A_HEAD_DIM)
    conv_new = jnp.concatenate([conv_g, conv_v], axis=-1)
    return y.reshape(batch, seq, D_MODEL), chunk_v, shift_new, wkv_new, conv_new


def kernel(x_prompt, x_sample, mem_prompt, cache_mem_k, cache_mem_v, state_shift, state_wkv, state_conv,
           norm_mix_pre, norm_mix_post, norm_ffn_pre, norm_ffn_post, norm_mem, w_in, w_out, w_mem_k, w_mem_v,
           gm_ln_g, gm_ln_b, gm_ws, gm_bs, rk_mu, rk_w0, rk_w2, rk_a0, rk_a2, rk_g2, rk_kk, rk_ka, rk_rk,
           rk_lnx_g, rk_lnx_b, ffn_w_up, ffn_conv_w, ffn_conv_b, ffn_w_down):
    params = dict(
        norm_mix_pre=norm_mix_pre, norm_mix_post=norm_mix_post, norm_ffn_pre=norm_ffn_pre,
        norm_ffn_post=norm_ffn_post, norm_mem=norm_mem, w_in=w_in, w_out=w_out, w_mem_k=w_mem_k,
        w_mem_v=w_mem_v, gm_ln_g=gm_ln_g, gm_ln_b=gm_ln_b, gm_ws=gm_ws, gm_bs=gm_bs, rk_mu=rk_mu,
        rk_w0=rk_w0, rk_w2=rk_w2, rk_a0=rk_a0, rk_a2=rk_a2, rk_g2=rk_g2, rk_kk=rk_kk, rk_ka=rk_ka,
        rk_rk=rk_rk, rk_lnx_g=rk_lnx_g, rk_lnx_b=rk_lnx_b, ffn_w_up=ffn_w_up, ffn_conv_w=ffn_conv_w,
        ffn_conv_b=ffn_conv_b, ffn_w_down=ffn_w_down)
    depth = w_in.shape[0]
    bp = x_prompt.shape[0]
    y_p, y_s = x_prompt, x_sample
    outs = [[] for _ in range(10)]
    for l in range(depth):
        lw = _prepare_layer(params, l)
        mem2 = mem_prompt.reshape(bp * MEM_LEN, D_MODEL)
        mkv = _norm_matmul(mem2, lw["g_mem"], lw["w_mkv"], tm=512, tn=2 * C_WIDTH)
        mk = mkv[:, :C_WIDTH].reshape(bp, MEM_LEN, C_WIDTH)
        mv = mkv[:, C_WIDTH:].reshape(bp, MEM_LEN, C_WIDTH)
        zero_shift = jnp.zeros((bp, B_PROJ), x_prompt.dtype)
        zero_wkv = jnp.zeros((bp, B_HEADS, B_HEAD_DIM, B_HEAD_DIM), F32)
        zero_conv = jnp.zeros((bp, 2, 2 * D_FF), x_prompt.dtype)
        y_p, cv, sh, wkv, conv = _decoder_layer(y_p, mk, mv, zero_shift, zero_wkv, zero_conv, lw)
        mem_shape = (bp, MEM_LEN, C_HEADS, C_HEAD_DIM)
        for lst, val in zip(outs[:6], (mk.reshape(mem_shape), mv.reshape(mem_shape), cv, sh, wkv, conv)):
            lst.append(val)
        n_s = x_sample.shape[0]
        cache_k = cache_mem_k[l].reshape(n_s, MEM_LEN * C_HEADS, C_HEAD_DIM)
        cache_v = cache_mem_v[l].reshape(n_s, MEM_LEN * C_HEADS, C_HEAD_DIM)
        y_s, cv, sh, wkv, conv = _decoder_layer(y_s, cache_k, cache_v, state_shift[l],
                                                state_wkv[l], state_conv[l], lw)
        for lst, val in zip(outs[6:], (cv, sh, wkv, conv)):
            lst.append(val)
    return (y_p, y_s) + tuple(jnp.stack(o) for o in outs)
```

```python
import functools
import math

import jax
import jax.numpy as jnp
from jax import lax
from jax.experimental import pallas as pl
from jax.experimental.pallas import tpu as pltpu

D_MODEL = 2048
MEM_LEN = 256
CHUNK = 128
A_HEADS, A_HEAD_DIM = 4, 128
A_WIDTH = A_HEADS * A_HEAD_DIM
B_HEADS, B_HEAD_DIM = 16, 64
B_WIDTH = B_HEADS * B_HEAD_DIM
LORA = 64
B_PROJ = 3 * B_WIDTH + 3 * LORA
C_HEADS, C_HEAD_DIM = 4, 128
C_WIDTH = C_HEADS * C_HEAD_DIM
D_FF = 5632
RMS_EPS = 1e-6
LN_EPS = 1e-5
GN_EPS = 64e-5
DECAY_OFFSET = 0.5

LANES = 128
SUBLANES = 8
VMEM_LIMIT_BYTES = 56 * 1024 * 1024

LORA_PAD = 256
COL_A = 0
COL_R = 2 * A_WIDTH
MAIN_COLS = COL_R + 3 * B_WIDTH
COL_Q = 0
COL_L = C_WIDTH
TAIL_COLS = COL_L + LORA_PAD
PROJ_TN = 1024

WKV_CHUNK = 64
FFN_N_TILES = 2
FFN_TN = D_FF // FFN_N_TILES
FFN_SUB = 256

F32 = jnp.float32
BF16 = jnp.bfloat16


def _cparams(sem):
    return pltpu.CompilerParams(dimension_semantics=sem, vmem_limit_bytes=VMEM_LIMIT_BYTES)


def _rms(x, g):
    return x * lax.rsqrt(jnp.mean(x * x, axis=-1, keepdims=True) + RMS_EPS) * g


def _norm_matmul_kernel(x_ref, g_ref, w_ref, o_ref, h_ref):
    @pl.when(pl.program_id(1) == 0)
    def _():
        h_ref[...] = _rms(x_ref[...], g_ref[...]).astype(BF16)

    o_ref[...] = jnp.dot(h_ref[...], w_ref[...], preferred_element_type=F32)


def _norm_matmul(x, g, w, *, tm, tn):
    t, k = x.shape
    n = w.shape[1]
    return pl.pallas_call(
        _norm_matmul_kernel,
        grid=(t // tm, n // tn),
        in_specs=[
            pl.BlockSpec((tm, k), lambda i, j: (i, 0)),
            pl.BlockSpec((1, k), lambda i, j: (0, 0)),
            pl.BlockSpec((k, tn), lambda i, j: (0, j)),
        ],
        out_specs=pl.BlockSpec((tm, tn), lambda i, j: (i, j)),
        out_shape=jax.ShapeDtypeStruct((t, n), F32),
        scratch_shapes=[pltpu.VMEM((tm, k), BF16)],
        compiler_params=_cparams(("parallel", "arbitrary")),
        name="norm_matmul",
    )(x, g, w)


def _in_proj_kernel(x_ref, g_ref, wm_ref, wt_ref, om_ref, ot_ref, h_ref, *, n_main):
    j = pl.program_id(1)

    @pl.when(j == 0)
    def _():
        h_ref[...] = _rms(x_ref[...], g_ref[...]).astype(BF16)

    nt = (((1,), (1,)), ((), ()))

    @pl.when(j < n_main)
    def _():
        om_ref[...] = lax.dot_general(h_ref[...], wm_ref[...], nt, preferred_element_type=F32)

    @pl.when(j == n_main)
    def _():
        ot_ref[...] = lax.dot_general(h_ref[...], wt_ref[...], nt, preferred_element_type=F32)


def _in_proj(x, g, w_main, w_tail, *, tm=1024):
    t, k = x.shape
    n_main = MAIN_COLS // PROJ_TN
    main_tile = lambda j: jnp.minimum(j, n_main - 1)
    return pl.pallas_call(
        functools.partial(_in_proj_kernel, n_main=n_main),
        grid=(t // tm, n_main + 1),
        in_specs=[
            pl.BlockSpec((tm, k), lambda i, j: (i, 0)),
            pl.BlockSpec((1, k), lambda i, j: (0, 0)),
            pl.BlockSpec((PROJ_TN, k), lambda i, j: (main_tile(j), 0)),
            pl.BlockSpec((TAIL_COLS, k), lambda i, j: (0, 0)),
        ],
        out_specs=[
            pl.BlockSpec((tm, PROJ_TN), lambda i, j: (i, main_tile(j))),
            pl.BlockSpec((tm, TAIL_COLS), lambda i, j: (i, 0)),
        ],
        out_shape=[jax.ShapeDtypeStruct((t, MAIN_COLS), F32), jax.ShapeDtypeStruct((t, TAIL_COLS), F32)],
        scratch_shapes=[pltpu.VMEM((tm, k), BF16)],
        compiler_params=_cparams(("parallel", "arbitrary")),
        name="in_proj",
    )(x, g, w_main, w_tail)


def _group_a_kernel(u_ref, v_ref, g_ref, b_ref, w_ref, bias_ref, o_ref, vout_ref):
    u = jax.nn.gelu(u_ref[...], approximate=True)
    v = jax.nn.gelu(v_ref[...], approximate=True)
    mean = jnp.mean(v, axis=-1, keepdims=True)
    d = v - mean
    var = jnp.mean(d * d, axis=-1, keepdims=True)
    vn = d * lax.rsqrt(var + LN_EPS) * g_ref[...] + b_ref[...]
    vout_ref[...] = vn
    vb = vn.astype(BF16)
    for c in range(u.shape[0] // CHUNK):
        rs = slice(c * CHUNK, (c + 1) * CHUNK)
        for h in range(A_HEADS):
            hs = slice(h * A_HEAD_DIM, (h + 1) * A_HEAD_DIM)
            mixed = jnp.dot(w_ref[h], vb[rs, hs], preferred_element_type=F32) + bias_ref[:, hs]
            o_ref[rs, hs] = (u[rs, hs] * mixed).astype(BF16)


def _group_a(proj, ln_g, ln_b, w_sp, bias_full, *, rows=4 * CHUNK):
    t = proj.shape[0]
    return pl.pallas_call(
        _group_a_kernel,
        grid=(t // rows,),
        in_specs=[
            pl.BlockSpec((rows, A_WIDTH), lambda i: (i, COL_A // A_WIDTH)),
            pl.BlockSpec((rows, A_WIDTH), lambda i: (i, COL_A // A_WIDTH + 1)),
            pl.BlockSpec((1, A_WIDTH), lambda i: (0, 0)),
            pl.BlockSpec((1, A_WIDTH), lambda i: (0, 0)),
            pl.BlockSpec((A_HEADS, CHUNK, CHUNK), lambda i: (0, 0, 0)),
            pl.BlockSpec((CHUNK, A_WIDTH), lambda i: (0, 0)),
        ],
        out_specs=[
            pl.BlockSpec((rows, A_WIDTH), lambda i: (i, 0)),
            pl.BlockSpec((rows, A_WIDTH), lambda i: (i, 0)),
        ],
        out_shape=[
            jax.ShapeDtypeStruct((t, A_WIDTH), BF16),
            jax.ShapeDtypeStruct((t, A_WIDTH), F32),
        ],
        compiler_params=_cparams(("parallel",)),
        name="group_a",
    )(proj, proj, ln_g, ln_b, w_sp, bias_full)


def _softmax_rows(s):
    e = jnp.exp(s - jnp.max(s, axis=-1, keepdims=True))
    return e / jnp.sum(e, axis=-1, keepdims=True)


def _attn_kernel(q_ref, k_ref, v_ref, o_ref, *, n_b, tq, head_major_rows):
    scale = C_HEAD_DIM ** -0.5
    heads = range(C_HEADS)
    lanes = lambda h: slice(h * C_HEAD_DIM, (h + 1) * C_HEAD_DIM)
    if not head_major_rows:
        chains = [(g, h) for g in range(n_b) for h in heads]
        rows = lambda g: slice(g * tq, (g + 1) * tq)
        s = [lax.dot_general(q_ref[rows(g), lanes(h)].astype(BF16), k_ref[g, :, lanes(h)].astype(BF16), _NT,
                             preferred_element_type=F32) * scale for g, h in chains]
        p = [_softmax_rows(m).astype(BF16) for m in s]
        o = [jnp.dot(m, v_ref[g, :, lanes(h)].astype(BF16), preferred_element_type=F32)
             for m, (g, h) in zip(p, chains)]
        for m, (g, h) in zip(o, chains):
            o_ref[rows(g), lanes(h)] = m.astype(BF16)
        return
    row = lax.broadcasted_iota(jnp.int32, (C_HEADS * tq, 1), 0)
    col = lax.broadcasted_iota(jnp.int32, (1, C_HEADS * MEM_LEN), 1)
    own = (col & (C_HEADS - 1)) == (row >> int(math.log2(tq)))
    q = [q_ref[g * tq:(g + 1) * tq, :] for g in range(n_b)]
    qs = [jnp.concatenate([m[:, lanes(h)] for h in heads], axis=0).astype(BF16) for m in q]
    s = [lax.dot_general(m, k_ref[g].astype(BF16), _NT, preferred_element_type=F32) * scale
         for g, m in enumerate(qs)]
    p = [_softmax_rows(jnp.where(own, m, -1e30)).astype(BF16) for m in s]
    o = [jnp.dot(m, v_ref[g].astype(BF16), preferred_element_type=F32) for g, m in enumerate(p)]
    for g, m in enumerate(o):
        for h in heads:
            o_ref[g * tq:(g + 1) * tq, lanes(h)] = m[h * tq:(h + 1) * tq].astype(BF16)


def _attention(proj, mem_k, mem_v, *, batch, seq):
    head_major_rows = mem_k.shape[-1] == C_HEAD_DIM
    if head_major_rows:
        tq, n_b = seq, 8
    else:
        tq, n_b = 512, 1
    n_q = seq // tq
    mem_blk = (n_b,) + mem_k.shape[1:]
    return pl.pallas_call(
        functools.partial(_attn_kernel, n_b=n_b, tq=tq, head_major_rows=head_major_rows),
        grid=(batch // n_b, n_q),
        in_specs=[
            pl.BlockSpec((n_b * tq, C_WIDTH), lambda b, i: (b * n_q + i, COL_Q // C_WIDTH)),
            pl.BlockSpec(mem_blk, lambda b, i: (b, 0, 0)),
            pl.BlockSpec(mem_blk, lambda b, i: (b, 0, 0)),
        ],
        out_specs=pl.BlockSpec((n_b * tq, C_WIDTH), lambda b, i: (b * n_q + i, 0)),
        out_shape=jax.ShapeDtypeStruct((batch * seq, C_WIDTH), BF16),
        compiler_params=_cparams(("parallel", "arbitrary")),
        name="mem_attention",
    )(proj, mem_k, mem_v)


_V_MU_R, _V_MU_K, _V_MU_V, _V_W0, _V_A0, _V_KK, _V_KA, _V_RK, _V_LNG, _V_LNB = range(10)


_NN = (((1,), (0,)), ((), ()))
_NT = (((1,), (1,)), ((), ()))
_TN = (((0,), (0,)), ((), ()))
HEAD_PAIRS = B_HEADS // 2


def _split(x):
    hi = x.astype(BF16)
    return hi, (x - hi.astype(F32)).astype(BF16)


def _b(x):
    return x.astype(BF16)


def _bdot(a, b, dims=_NN):
    return lax.dot_general(a, b, dims, preferred_element_type=F32)


def _rwkv_kernel(pr_ref, pk_ref, pv_ref, pl_ref, sr_ref, sk_ref, sv_ref, sl_ref, wkv_ref,
                 vec_ref, mul_ref, w2_ref, a2_ref, g2_ref, seg_ref,
                 o_ref, so_ref, nr_ref, nk_ref, nv_ref, nl_ref,
                 cr_ref, ck_ref, cv_ref, cl_ref, sbd_ref, *, n_blk, n_sc, groups, chunk, whole_seq):
    rows = groups * chunk
    n_items = n_blk * n_sc
    block_of = lambda w: w % n_blk
    rows_of = lambda w: slice((w // n_blk) * rows, (w // n_blk + 1) * rows)
    first = pl.program_id(1) == 0
    last = pl.program_id(1) == pl.num_programs(1) - 1
    hd = B_HEAD_DIM
    n_seq = n_blk * groups

    @pl.when(first)
    def _():
        zero = jnp.zeros((hd, hd), F32)
        for g in range(n_seq):
            for q in range(HEAD_PAIRS):
                top = jnp.concatenate([wkv_ref[g, 2 * q], zero], axis=1)
                bot = jnp.concatenate([zero, wkv_ref[g, 2 * q + 1]], axis=1)
                sbd_ref[g, q] = jnp.concatenate([top, bot], axis=0)
        if not whole_seq:
            for s_ref, carry_ref in ((sr_ref, cr_ref), (sk_ref, ck_ref), (sv_ref, cv_ref), (sl_ref, cl_ref)):
                for blk in range(n_blk):
                    carry_ref[blk] = jnp.broadcast_to(s_ref[blk], (SUBLANES, s_ref.shape[-1]))

    row_id = lax.broadcasted_iota(jnp.int32, (rows, 1), 0)
    col_id = lax.broadcasted_iota(jnp.int32, (1, rows), 1)
    chunk_bits = int(math.log2(chunk))
    incl_b = (((row_id >> chunk_bits) == (col_id >> chunk_bits)) & (col_id <= row_id)).astype(BF16)
    vec = lambda i: vec_ref[i:i + 1, :]
    seg_ones = seg_ref[...]

    def seg_sum(x):
        slabs = jnp.concatenate([x[:, q * LANES:(q + 1) * LANES] for q in range(HEAD_PAIRS)], axis=0)
        hi, lo = _split(slabs)
        s = (jnp.dot(hi, seg_ones, preferred_element_type=F32)
             + jnp.dot(lo, seg_ones, preferred_element_type=F32))
        return jnp.concatenate([s[q * rows:(q + 1) * rows] for q in range(HEAD_PAIRS)], axis=1)

    def prologue(item, out):
        blk, rs = block_of(item), rows_of(item)
        seqs = slice(blk * groups, (blk + 1) * groups)

        def prev_rows(x, s_ref, carry_ref):
            width = x.shape[1]
            if whole_seq:
                start = jnp.broadcast_to(s_ref[seqs], (groups, chunk, width)).reshape(rows, width)
                return jnp.where((row_id & (chunk - 1)) == 0, start, pltpu.roll(x, 1, 0))

            ext = jnp.concatenate([carry_ref[blk], x], axis=0)
            prev = pltpu.roll(ext, 1, 0)[SUBLANES:]
            carry_ref[blk] = x[rows - SUBLANES:]
            return prev

        def shifted(p_ref, s_ref, carry_ref, n_ref, mu):
            x = p_ref[blk, rs]
            if whole_seq:
                n_ref[seqs] = x.reshape(groups, chunk, x.shape[1])[:, chunk - 1:, :]
            else:
                n_ref[blk] = x[rows - 1:]
            return x + (prev_rows(x, s_ref, carry_ref) - x) * mu

        lo = shifted(pl_ref, sl_ref, cl_ref, nl_ref, mul_ref[0:1, :])
        dw = jnp.dot(jnp.tanh(lo).astype(BF16), w2_ref[...], preferred_element_type=F32)
        da = jnp.dot(lo.astype(BF16), a2_ref[...], preferred_element_type=F32)
        gate = jnp.dot(jax.nn.sigmoid(lo).astype(BF16), g2_ref[...], preferred_element_type=F32)
        yield
        w_log = -jax.nn.softplus(-(vec(_V_W0) + dw)) - DECAY_OFFSET
        log_decay = -jnp.exp(w_log)
        ld_hi, ld_mid = _split(log_decay)
        ld_lo = (log_decay - ld_hi.astype(F32) - ld_mid.astype(F32)).astype(BF16)
        cum = (jnp.dot(incl_b, ld_hi, preferred_element_type=F32)
               + jnp.dot(incl_b, ld_mid, preferred_element_type=F32)
               + jnp.dot(incl_b, ld_lo, preferred_element_type=F32))
        yield
        a = jax.nn.sigmoid(vec(_V_A0) + da)
        k = shifted(pk_ref, sk_ref, ck_ref, nk_ref, vec(_V_MU_K))
        kk = k * vec(_V_KK)
        kk = kk / jnp.maximum(jnp.sqrt(seg_sum(kk * kk)), 1e-12)
        yield
        p_incl = jnp.exp(cum)
        p_inv = jnp.exp(-cum)
        a_t = -kk * jnp.exp(cum - log_decay)
        b_t = kk * a * p_inv
        yield
        k = k * (1.0 + (a - 1.0) * vec(_V_KA))
        k_t = k * p_inv
        r = shifted(pr_ref, sr_ref, cr_ref, nr_ref, vec(_V_MU_R))
        r_t = r * p_incl
        yield
        v = shifted(pv_ref, sv_ref, cv_ref, nv_ref, vec(_V_MU_V))
        bonus = seg_sum(r * k * vec(_V_RK)) * v
        out.update(a_t=a_t, b_t=b_t, k_t=k_t, r_t=r_t, v=v, p_incl=p_incl, bonus=bonus, gate=gate)

    pr = 2 * rows
    row2 = lax.broadcasted_iota(jnp.int32, (pr, 1), 0)
    col2 = lax.broadcasted_iota(jnp.int32, (1, pr), 1)
    t2, s2 = row2 & (rows - 1), col2 & (rows - 1)
    same2 = ((row2 >> chunk_bits) == (col2 >> chunk_bits))
    incl2 = same2 & (s2 <= t2)
    strict2 = same2 & (s2 < t2)
    eye2 = (row2 == col2).astype(F32)
    left = lax.broadcasted_iota(jnp.int32, (rows, LANES), 1) < hd

    def bd(x):
        zero = jnp.zeros_like(x)
        return jnp.concatenate([jnp.where(left, x, zero), jnp.where(left, zero, x)], axis=0)

    def group_rows(mats, g):
        starts = [hh * rows + g * chunk for hh in (0, 1)]
        return jnp.concatenate([m[i:i + chunk] for m in mats for i in starts], axis=0)

    n_sq = chunk_bits - 1
    col4 = lax.broadcasted_iota(jnp.int32, (1, 2 * pr), 1)
    incl4 = ((row2 >> chunk_bits) == ((col4 & (pr - 1)) >> chunk_bits)) & ((col4 & (rows - 1)) <= t2)
    pairs = range(HEAD_PAIRS)
    lanes_of = [slice(q * LANES, (q + 1) * LANES) for q in pairs]

    def recurrence(item, pro, qs):
        s0 = block_of(item) * groups
        p_incl = pro["p_incl"]
        pairs = range(len(qs))
        bds = [[bd(pro[name][:, lanes_of[q]]) for name in ("a_t", "r_t", "b_t", "k_t", "v")] for q in qs]
        ar_s = [_b(jnp.concatenate([m[0], m[1]], axis=0)) for m in bds]
        bk_s = [_b(jnp.concatenate([m[2], m[3]], axis=0)) for m in bds]
        v_s = [_b(m[4]) for m in bds]
        if groups == 1:
            ms = [_bdot(ar_s[q], jnp.concatenate([bk_s[q], _b(sbd_ref[s0, qs[q]])], axis=0), _NT)
                  for q in pairs]
        else:
            ms = [_bdot(ar_s[q], bk_s[q], _NT) for q in pairs]
        yield
        a_ab = [jnp.where(strict2, m[:pr, :pr], 0.0) for m in ms]
        a_ak = [jnp.where(strict2, m[:pr, pr:2 * pr], 0.0) for m in ms]
        a_r = [jnp.where(incl4, m[pr:, :2 * pr], 0.0) for m in ms]
        inv = [eye2 + n for n in a_ab]
        pw_s = [_b(_bdot(_b(n), _b(n))) for n in a_ab]
        yield
        for _ in range(1, n_sq):
            both = [_bdot(p, jnp.concatenate([p, _b(i)], axis=1)) for p, i in zip(pw_s, inv)]
            inv = [i + m[:, pr:] for i, m in zip(inv, both)]
            pw_s = [_b(m[:, :pr]) for m in both]
            yield
        inv = [i + _bdot(p, _b(i)) for i, p in zip(inv, pw_s)]
        yield
        if groups == 1:
            x0, y0 = [m[:pr, 2 * pr:] for m in ms], [m[pr:, 2 * pr:] for m in ms]
        else:
            x0, y0 = [], []
            for q in pairs:
                x_parts, y_parts = [None] * (2 * groups), [None] * (2 * groups)
                for g in range(groups):
                    xy = _bdot(_b(group_rows(bds[q][:2], g)), _b(sbd_ref[s0 + g, qs[q]]), _NT)
                    for hh in (0, 1):
                        x_parts[hh * groups + g] = xy[hh * chunk:(hh + 1) * chunk]
                        y_parts[hh * groups + g] = xy[(2 + hh) * chunk:(3 + hh) * chunk]
                x0.append(jnp.concatenate(x_parts, axis=0))
                y0.append(jnp.concatenate(y_parts, axis=0))
        yield
        rhs = [_b(x0[q] + _bdot(_b(a_ak[q]), v_s[q])) for q in pairs]
        u = [_bdot(_b(inv[q]), rhs[q]) for q in pairs]
        yield
        uv_s = [jnp.concatenate([_b(u[q]), v_s[q]], axis=0) for q in pairs]
        y2 = [y0[q] + _bdot(_b(a_r[q]), uv_s[q]) for q in pairs]
        for q in pairs:
            pro["y"][qs[q]] = y2[q][:rows] + y2[q][rows:]
        yield
        for q in pairs:
            ls = lanes_of[qs[q]]
            if groups == 1:
                ds = _bdot(uv_s[q], bk_s[q], _TN)
                sbd_ref[s0, qs[q]] = (sbd_ref[s0, qs[q]] + ds) * p_incl[rows - 1:rows, ls]
            else:
                for g in range(groups):
                    ds = _bdot(_b(group_rows((u[q], bds[q][4]), g)),
                               _b(group_rows((bds[q][2], bds[q][3]), g)), _TN)
                    end = (g + 1) * chunk - 1
                    sbd_ref[s0 + g, qs[q]] = (sbd_ref[s0 + g, qs[q]] + ds) * p_incl[end:end + 1, ls]

    def finish(item, pro):
        inv_n = 1.0 / B_HEAD_DIM
        y = jnp.concatenate([pro["y"][q] for q in range(HEAD_PAIRS)], axis=1)
        d = y - seg_sum(y) * inv_n
        yield
        var = seg_sum(d * d) * inv_n
        yield
        yn = d * lax.rsqrt(var + GN_EPS) * vec(_V_LNG) + vec(_V_LNB)
        o_ref[block_of(item), rows_of(item)] = ((yn + pro["bonus"]) * pro["gate"]).astype(BF16)

    def interleave(*gens):
        live = list(gens)
        while live:
            for gen in list(live):
                if next(gen, StopIteration) is StopIteration:
                    live.remove(gen)

    pros = [dict(y={}) for _ in range(n_items)]
    interleave(prologue(0, pros[0]))
    for item in range(n_items):
        work = [recurrence(item, pros[item], list(pairs))]
        if item + 1 < n_items:
            work.append(prologue(item + 1, pros[item + 1]))
        if item > 0:
            work.append(finish(item - 1, pros[item - 1]))
        interleave(*work)
    interleave(finish(n_items - 1, pros[n_items - 1]))

    @pl.when(last)
    def _():
        for g in range(n_seq):
            for q in range(HEAD_PAIRS):
                sq = sbd_ref[g, q]
                so_ref[g, 2 * q] = sq[:hd, :hd]
                so_ref[g, 2 * q + 1] = sq[hd:, hd:]


def _rwkv(proj, proj_tail, shift_parts, wkv_prev, lw, *, batch, seq):
    whole_seq = seq <= WKV_CHUNK
    if whole_seq:
        assert seq == SUBLANES, "whole-sequence blocks rely on one sublane tile per sequence"
        chunk, groups = seq, WKV_CHUNK // seq
        n_blk, n_sc, n_chunks = 2, 1, 1
        n_outer = batch // (groups * n_blk)
        lead = (batch // groups, WKV_CHUNK)
    else:
        chunk, groups = WKV_CHUNK, 1
        n_blk, n_sc, n_outer = batch, 2, 1
        n_chunks = seq // (n_sc * WKV_CHUNK)
        lead = (batch, seq)
    proj3, tail3 = proj.reshape(lead + (MAIN_COLS,)), proj_tail.reshape(lead + (TAIL_COLS,))
    n_seq = n_blk * groups
    act = lambda col_blk, width: pl.BlockSpec((n_blk, n_sc * WKV_CHUNK, width), lambda o, c: (o, c, col_blk))
    per_seq = lambda *tail: pl.BlockSpec((n_seq,) + tail, lambda o, c: (o,) + (0,) * len(tail))
    fixed = lambda *shape: pl.BlockSpec(shape, lambda o, c: (0,) * len(shape))
    state = (B_HEADS, B_HEAD_DIM, B_HEAD_DIM)
    kern = functools.partial(_rwkv_kernel, n_blk=n_blk, n_sc=n_sc, groups=groups, chunk=chunk,
                             whole_seq=whole_seq)
    shift_specs = [per_seq(1, B_WIDTH), per_seq(1, B_WIDTH), per_seq(1, B_WIDTH), per_seq(1, LORA_PAD)]
    b_out, wkv_new, *last_rows = pl.pallas_call(
        kern,
        grid=(n_outer, n_chunks),
        in_specs=[
            act(COL_R // B_WIDTH, B_WIDTH), act(COL_R // B_WIDTH + 1, B_WIDTH),
            act(COL_R // B_WIDTH + 2, B_WIDTH), act(COL_L // LORA_PAD, LORA_PAD),
            *shift_specs,
            per_seq(*state),
            fixed(16, B_WIDTH), fixed(SUBLANES, LORA_PAD),
            fixed(LORA_PAD, B_WIDTH), fixed(LORA_PAD, B_WIDTH), fixed(LORA_PAD, B_WIDTH),
            fixed(LANES, LANES),
        ],
        out_specs=[act(0, B_WIDTH), per_seq(*state), *shift_specs],
        out_shape=[
            jax.ShapeDtypeStruct(lead + (B_WIDTH,), BF16),
            jax.ShapeDtypeStruct((batch,) + state, F32),
            *(jax.ShapeDtypeStruct(p.shape, F32) for p in shift_parts),
        ],
        scratch_shapes=[
            pltpu.VMEM((n_blk, SUBLANES, B_WIDTH), F32),
            pltpu.VMEM((n_blk, SUBLANES, B_WIDTH), F32),
            pltpu.VMEM((n_blk, SUBLANES, B_WIDTH), F32),
            pltpu.VMEM((n_blk, SUBLANES, LORA_PAD), F32),
            pltpu.VMEM((n_seq, HEAD_PAIRS, LANES, LANES), F32),
        ],
        compiler_params=_cparams(("parallel", "arbitrary")),
        name="rwkv7",
    )(proj3, proj3, proj3, tail3, *shift_parts, wkv_prev,
      lw["rk_vecs"], lw["rk_mu_l"], lw["rk_w2"], lw["rk_a2"], lw["rk_g2"], lw["seg_ones"])
    shift_new = jnp.concatenate([p[:, 0] for p in last_rows[:3]] + [last_rows[3][:, 0, :3 * LORA]], axis=1)
    return b_out.reshape(batch * seq, B_WIDTH), wkv_new, shift_new


def _out_proj_kernel(a_ref, b_ref, c_ref, w_ref, x_ref, gpost_ref, gffn_ref, x1_ref, h_ref):
    n_sub = 4
    sub = x_ref.shape[0] // n_sub
    halves = [slice(s * sub, (s + 1) * sub) for s in range(n_sub)]

    def project(rs):
        acc = jnp.dot(a_ref[rs, :], w_ref[0:A_WIDTH, :], preferred_element_type=F32)
        acc += jnp.dot(b_ref[rs, :], w_ref[A_WIDTH:A_WIDTH + B_WIDTH, :], preferred_element_type=F32)
        return acc + jnp.dot(c_ref[rs, :], w_ref[A_WIDTH + B_WIDTH:, :], preferred_element_type=F32)

    accs = [project(rs) for rs in halves]
    for rs, acc in zip(halves, accs):
        x1 = x_ref[rs, :] + _rms(acc, gpost_ref[...])
        x1_ref[rs, :] = x1
        h_ref[rs, :] = _rms(x1, gffn_ref[...]).astype(BF16)


def _out_proj(a_out, b_out, c_out, w_out, x, g_post, g_ffn, *, tm=512):
    t = x.shape[0]
    row = lambda i: (i, 0)
    fixed = lambda i: (0, 0)
    return pl.pallas_call(
        _out_proj_kernel,
        grid=(t // tm,),
        in_specs=[
            pl.BlockSpec((tm, A_WIDTH), row),
            pl.BlockSpec((tm, B_WIDTH), row),
            pl.BlockSpec((tm, C_WIDTH), row),
            pl.BlockSpec((D_MODEL, D_MODEL), fixed, pipeline_mode=pl.Buffered(1)),
            pl.BlockSpec((tm, D_MODEL), row),
            pl.BlockSpec((1, D_MODEL), fixed),
            pl.BlockSpec((1, D_MODEL), fixed),
        ],
        out_specs=[pl.BlockSpec((tm, D_MODEL), row), pl.BlockSpec((tm, D_MODEL), row)],
        out_shape=[jax.ShapeDtypeStruct((t, D_MODEL), F32), jax.ShapeDtypeStruct((t, D_MODEL), BF16)],
        compiler_params=_cparams(("parallel",)),
        name="out_proj",
    )(a_out, b_out, c_out, w_out, x, g_post, g_ffn)


def _chunked_dots(h_ref, w_refs, cs, between):
    accs = [None] * len(w_refs)
    for kc in range(h_ref.shape[1] // FFN_SUB):
        ks = slice(kc * FFN_SUB, (kc + 1) * FFN_SUB)
        for n, w_ref in enumerate(w_refs):
            part = jnp.dot(h_ref[:, ks], w_ref[ks, cs], preferred_element_type=F32)
            accs[n] = part if accs[n] is None else accs[n] + part
        between(kc)
    return accs


def _ffn_up_kernel(h_ref, wg_ref, wv_ref, cwg_ref, cwv_ref, cbg_ref, cbv_ref, pg_ref, pv_ref,
                   act_ref, ng_ref, nv_ref, u0_ref, u1_ref, *, tm, seq, n_col_tiles):
    j = pl.program_id(0)
    tn = cwg_ref.shape[1]
    piece = 128
    n_seq = piece // seq
    tau = lax.broadcasted_iota(jnp.int32, (piece, 1), 0) & (seq - 1)
    subs = [slice(s * FFN_SUB, (s + 1) * FFN_SUB) for s in range(tn // FFN_SUB)]

    @pl.when(j == 0)
    def _():
        u1_ref[...] = jnp.zeros_like(u1_ref)

    def conv(old_ref, r0, us, cs, cw_ref, cb_ref, p_ref, n_ref):
        up = old_ref[r0:r0 + piece, us]
        seqs = slice(r0 // seq, (r0 + piece) // seq)
        prev = p_ref[seqs, :, cs]
        e0 = jnp.broadcast_to(prev[:, 0:1, :], (n_seq, seq, FFN_SUB)).reshape(piece, FFN_SUB)
        e1 = jnp.broadcast_to(prev[:, 1:2, :], (n_seq, seq, FFN_SUB)).reshape(piece, FFN_SUB)
        m1 = jnp.where(tau == 0, e1, pltpu.roll(up, 1, 0))
        m2 = jnp.where(tau == 0, e0, jnp.where(tau == 1, e1, pltpu.roll(up, 2, 0)))
        n_ref[seqs, :, cs] = up.reshape(n_seq, seq, FFN_SUB)[:, seq - 2:, :]
        return cb_ref[:, cs] + m2 * cw_ref[0:1, cs] + m1 * cw_ref[1:2, cs] + up * cw_ref[2:3, cs]

    def step(new_ref, old_ref):
        for cs in subs:
            vs = slice(tn + cs.start, tn + cs.stop)
            if new_ref is not None:
                new_ref[:, cs] = jnp.dot(h_ref[...], wg_ref[:, cs], preferred_element_type=F32)
                new_ref[:, vs] = jnp.dot(h_ref[...], wv_ref[:, cs], preferred_element_type=F32)
            for r0 in range(0, tm, piece):
                gate = conv(old_ref, r0, cs, cs, cwg_ref, cbg_ref, pg_ref, ng_ref)
                val = conv(old_ref, r0, vs, cs, cwv_ref, cbv_ref, pv_ref, nv_ref)
                act_ref[r0:r0 + piece, cs] = (jax.nn.gelu(gate, approximate=True) * val).astype(BF16)

    bufs = (u0_ref, u1_ref)

    @pl.when((j % 2 == 0) & (j < n_col_tiles))
    def _():
        step(u0_ref, u1_ref)

    @pl.when((j % 2 == 1) & (j < n_col_tiles))
    def _():
        step(u1_ref, u0_ref)

    @pl.when(j == n_col_tiles)
    def _():
        step(None, bufs[(n_col_tiles - 1) % 2])


def _ffn_up_skew_kernel(h_ref, wg_ref, wv_ref, cwg_ref, cwv_ref, cbg_ref, cbv_ref, pg_ref, pv_ref,
                        act_ref, ng_ref, nv_ref, cg_ref, cv_ref, u0_ref, u1_ref,
                        *, tm, tiles_per_seq, n_row_tiles):
    i = pl.program_id(1)
    tn = cwg_ref.shape[1]
    n_sub = tn // FFN_SUB

    @pl.when(i == 0)
    def _():
        for ref in (u1_ref, cg_ref, cv_ref):
            ref[...] = jnp.zeros_like(ref)

    @pl.when(i % tiles_per_seq == 1 % tiles_per_seq)
    def _():
        for carry_ref, p_ref in ((cg_ref, pg_ref), (cv_ref, pv_ref)):
            carry_ref[...] = jnp.concatenate([jnp.zeros((SUBLANES - 2, tn), F32), p_ref[0]], axis=0)

    piece = 64

    def conv(u_ref, r0, us, cs, cw_ref, cb_ref, carry_ref):
        if r0 == 0:
            ext = jnp.concatenate([carry_ref[:, cs], u_ref[0:piece, us]], axis=0)
        else:
            ext = u_ref[r0 - SUBLANES:r0 + piece, us]
        m1 = pltpu.roll(ext, 1, 0)[SUBLANES:]
        m2 = pltpu.roll(ext, 2, 0)[SUBLANES:]
        return (cb_ref[:, cs] + m2 * cw_ref[0:1, cs] + m1 * cw_ref[1:2, cs]
                + ext[SUBLANES:] * cw_ref[2:3, cs])

    n_k = D_MODEL // FFN_SUB
    chunks_per_piece = n_k // (tm // piece)

    def step(new_ref, old_ref):
        for s in range(n_sub):
            cs = slice(s * FFN_SUB, (s + 1) * FFN_SUB)
            gs, vs = cs, slice(tn + s * FFN_SUB, tn + (s + 1) * FFN_SUB)

            def conv_piece(kc):
                if (kc + 1) % chunks_per_piece == 0:
                    r0 = (kc // chunks_per_piece) * piece
                    gate = conv(old_ref, r0, gs, cs, cwg_ref, cbg_ref, cg_ref)
                    val = conv(old_ref, r0, vs, cs, cwv_ref, cbv_ref, cv_ref)
                    act_ref[r0:r0 + piece, cs] = (jax.nn.gelu(gate, approximate=True) * val).astype(BF16)

            if new_ref is None:
                for kc in range(n_k):
                    conv_piece(kc)
            else:
                new_ref[:, gs], new_ref[:, vs] = _chunked_dots(h_ref, (wg_ref, wv_ref), cs, conv_piece)
            for us, carry_ref, n_ref in ((gs, cg_ref, ng_ref), (vs, cv_ref, nv_ref)):
                carry_ref[:, cs] = old_ref[tm - SUBLANES:, us]
                n_ref[0, :, cs] = old_ref[tm - 2:, us]

    bufs = (u0_ref, u1_ref)

    @pl.when((i % 2 == 0) & (i < n_row_tiles))
    def _():
        step(u0_ref, u1_ref)

    @pl.when((i % 2 == 1) & (i < n_row_tiles))
    def _():
        step(u1_ref, u0_ref)

    @pl.when(i == n_row_tiles)
    def _():
        step(None, bufs[(n_row_tiles - 1) % 2])


def _ffn_up_skew(h, w_up, conv_w, conv_b, conv_prev, *, batch, seq, tm=256):
    t = h.shape[0]
    tn, n_tiles = FFN_TN, FFN_N_TILES
    n_m = t // tm
    tiles_per_seq = seq // tm
    prev_tile = lambda i: jnp.maximum(i - 1, 0)
    col = lambda half: (lambda j, i: (0, j + half * n_tiles))
    state_blk = (1, 2, tn)
    state_idx = lambda half: (lambda j, i: (prev_tile(i) // tiles_per_seq, 0, j + half * n_tiles))
    state_out = jax.ShapeDtypeStruct((batch, 2, D_FF), F32)
    up_buf = pltpu.VMEM((tm, 2 * tn), F32)
    return pl.pallas_call(
        functools.partial(_ffn_up_skew_kernel, tm=tm, tiles_per_seq=tiles_per_seq, n_row_tiles=n_m),
        grid=(n_tiles, n_m + 1),
        in_specs=[
            pl.BlockSpec((tm, D_MODEL), lambda j, i: (jnp.minimum(i, n_m - 1), 0)),
            pl.BlockSpec((D_MODEL, tn), col(0), pipeline_mode=pl.Buffered(1)),
            pl.BlockSpec((D_MODEL, tn), col(1), pipeline_mode=pl.Buffered(1)),
            pl.BlockSpec((3, tn), col(0)),
            pl.BlockSpec((3, tn), col(1)),
            pl.BlockSpec((1, tn), col(0)),
            pl.BlockSpec((1, tn), col(1)),
            pl.BlockSpec(state_blk, state_idx(0)),
            pl.BlockSpec(state_blk, state_idx(1)),
        ],
        out_specs=[
            pl.BlockSpec((tm, tn), lambda j, i: (prev_tile(i), j)),
            pl.BlockSpec(state_blk, state_idx(0)),
            pl.BlockSpec(state_blk, state_idx(0)),
        ],
        out_shape=[jax.ShapeDtypeStruct((t, D_FF), BF16), state_out, state_out],
        scratch_shapes=[pltpu.VMEM((SUBLANES, tn), F32), pltpu.VMEM((SUBLANES, tn), F32), up_buf, up_buf],
        compiler_params=_cparams(("parallel", "arbitrary")),
        name="ffn_up_conv",
    )(h, w_up, w_up, conv_w, conv_w, conv_b, conv_b, conv_prev, conv_prev)


def _ffn_up(h, w_up, conv_w, conv_b, conv_prev, *, batch, seq):
    t = h.shape[0]
    if seq > SUBLANES:
        return _ffn_up_skew(h, w_up, conv_w, conv_b, conv_prev, batch=batch, seq=seq)
    assert seq == SUBLANES, "whole-sequence tiles rely on one sublane tile per sequence"
    tm, tn = t, 2 * FFN_SUB
    n_tiles = D_FF // tn
    state_blk = (batch, 2, tn)
    new_col = lambda half: (lambda j: (0, jnp.minimum(j, n_tiles - 1) + half * n_tiles))
    old_col = lambda half: (lambda j: (0, jnp.maximum(j - 1, 0) + half * n_tiles))
    old_state = lambda half: (lambda j: (0, 0, jnp.maximum(j - 1, 0) + half * n_tiles))
    kern = functools.partial(_ffn_up_kernel, tm=tm, seq=seq, n_col_tiles=n_tiles)
    state_out = jax.ShapeDtypeStruct((batch, 2, D_FF), F32)
    up_buf = pltpu.VMEM((tm, 2 * tn), F32)
    return pl.pallas_call(
        kern,
        grid=(n_tiles + 1,),
        in_specs=[
            pl.BlockSpec((tm, D_MODEL), lambda j: (0, 0)),
            pl.BlockSpec((D_MODEL, tn), new_col(0)),
            pl.BlockSpec((D_MODEL, tn), new_col(1)),
            pl.BlockSpec((3, tn), old_col(0)),
            pl.BlockSpec((3, tn), old_col(1)),
            pl.BlockSpec((1, tn), old_col(0)),
            pl.BlockSpec((1, tn), old_col(1)),
            pl.BlockSpec(state_blk, old_state(0)),
            pl.BlockSpec(state_blk, old_state(1)),
        ],
        out_specs=[
            pl.BlockSpec((tm, tn), old_col(0)),
            pl.BlockSpec(state_blk, old_state(0)),
            pl.BlockSpec(state_blk, old_state(0)),
        ],
        out_shape=[jax.ShapeDtypeStruct((t, D_FF), BF16), state_out, state_out],
        scratch_shapes=[up_buf, up_buf],
        compiler_params=_cparams(("arbitrary",)),
        name="ffn_up_conv",
    )(h, w_up, w_up, conv_w, conv_w, conv_b, conv_b, conv_prev, conv_prev)


def _ffn_down_kernel(a_ref, w_ref, x_ref, g_ref, o_ref):
    acc = jnp.dot(a_ref[...], w_ref[...], preferred_element_type=F32)
    o_ref[...] = x_ref[...] + _rms(acc, g_ref[...])


def _ffn_down(act, w_down, x1, g_post, *, tm=256):
    t = x1.shape[0]
    return pl.pallas_call(
        _ffn_down_kernel,
        grid=(t // tm,),
        in_specs=[
            pl.BlockSpec((tm, D_FF), lambda i: (i, 0)),
            pl.BlockSpec((D_FF, D_MODEL), lambda i: (0, 0), pipeline_mode=pl.Buffered(1)),
            pl.BlockSpec((tm, D_MODEL), lambda i: (i, 0)),
            pl.BlockSpec((1, D_MODEL), lambda i: (0, 0)),
        ],
        out_specs=pl.BlockSpec((tm, D_MODEL), lambda i: (i, 0)),
        out_shape=jax.ShapeDtypeStruct((t, D_MODEL), F32),
        compiler_params=_cparams(("parallel",)),
        name="ffn_down",
    )(act, w_down, x1, g_post)


def _pad_rows(w, row0, total):
    return jnp.zeros((total, w.shape[1]), w.dtype).at[row0:row0 + w.shape[0]].set(w)


def _prepare_layer(p, l):
    w_in_t = p["w_in"][l].T.astype(BF16)
    q0 = MAIN_COLS + 3 * LORA
    w_in_tail = jnp.concatenate([
        w_in_t[q0:], w_in_t[MAIN_COLS:q0], jnp.zeros((LORA_PAD - 3 * LORA, D_MODEL), BF16)], axis=0)
    mu = p["rk_mu"][l]
    vec_rows = [mu[:B_WIDTH], mu[B_WIDTH:2 * B_WIDTH], mu[2 * B_WIDTH:3 * B_WIDTH], p["rk_w0"][l],
                p["rk_a0"][l], p["rk_kk"][l], p["rk_ka"][l], p["rk_rk"][l].reshape(B_WIDTH),
                p["rk_lnx_g"][l], p["rk_lnx_b"][l]]
    vecs = jnp.zeros((16, B_WIDTH), F32).at[:len(vec_rows)].set(jnp.stack(vec_rows))
    mu_l = jnp.zeros((SUBLANES, LORA_PAD), F32).at[0, :3 * LORA].set(mu[3 * B_WIDTH:])
    head_of = jnp.arange(LANES) // B_HEAD_DIM
    seg_ones = (head_of[:, None] == head_of[None, :]).astype(BF16)
    tril = jnp.tril(jnp.ones((CHUNK, CHUNK), bool))
    ws = jnp.where(tril[None], p["gm_ws"][l], 0.0)
    bs = p["gm_bs"][l]
    return {
        "g_mix_pre": p["norm_mix_pre"][l][None], "g_mix_post": p["norm_mix_post"][l][None],
        "g_ffn_pre": p["norm_ffn_pre"][l][None], "g_ffn_post": p["norm_ffn_post"][l][None],
        "g_mem": p["norm_mem"][l][None],
        "w_in": w_in_t, "w_in_tail": w_in_tail,
        "w_out": p["w_out"][l].astype(BF16),
        "w_mkv": jnp.concatenate([p["w_mem_k"][l], p["w_mem_v"][l]], axis=1).astype(BF16),
        "gm_ln_g": p["gm_ln_g"][l][None], "gm_ln_b": p["gm_ln_b"][l][None],
        "gm_ws": ws, "gm_bs": bs,
        "rk_vecs": vecs, "rk_mu_l": mu_l,
        "rk_w2": _pad_rows(p["rk_w2"][l], 0, LORA_PAD).astype(BF16),
        "rk_a2": _pad_rows(p["rk_a2"][l], LORA, LORA_PAD).astype(BF16),
        "rk_g2": _pad_rows(p["rk_g2"][l], 2 * LORA, LORA_PAD).astype(BF16),
        "seg_ones": seg_ones,
        "w_up": p["ffn_w_up"][l].astype(BF16),
        "conv_w": p["ffn_conv_w"][l], "conv_b": p["ffn_conv_b"][l][None],
        "w_down": p["ffn_w_down"][l].astype(BF16),
    }


def _spatial_weights(lw, seq):
    ws, bs = lw["gm_ws"], lw["gm_bs"]
    if seq >= CHUNK:
        w_blk, b_rows = ws, bs
    else:
        reps = CHUNK // seq
        eye = jnp.eye(reps, dtype=ws.dtype)
        w_blk = jnp.einsum("ab,hts->hatbs", eye, ws[:, :seq, :seq]).reshape(A_HEADS, CHUNK, CHUNK)
        b_rows = jnp.tile(bs[:, :seq], (1, reps))
    bias_full = jnp.repeat(b_rows.T, A_HEAD_DIM, axis=1)
    return w_blk.astype(BF16), bias_full


def _decoder_layer(x, mem_k, mem_v, shift_prev, wkv_prev, conv_prev, lw):
    batch, seq, _ = x.shape
    x2 = x.reshape(batch * seq, D_MODEL)
    proj, proj_tail = _in_proj(x2, lw["g_mix_pre"], lw["w_in"], lw["w_in_tail"])
    w_sp, bias_full = _spatial_weights(lw, seq)
    a_out, a_v = _group_a(proj, lw["gm_ln_g"], lw["gm_ln_b"], w_sp, bias_full)
    shift_parts = [
        shift_prev[:, None, :B_WIDTH], shift_prev[:, None, B_WIDTH:2 * B_WIDTH],
        shift_prev[:, None, 2 * B_WIDTH:3 * B_WIDTH],
        jnp.pad(shift_prev[:, None, 3 * B_WIDTH:], ((0, 0), (0, 0), (0, LORA_PAD - 3 * LORA))),
    ]
    b_out, wkv_new, shift_new = _rwkv(proj, proj_tail, shift_parts, wkv_prev, lw, batch=batch, seq=seq)
    c_out = _attention(proj_tail, mem_k, mem_v, batch=batch, seq=seq)
    x1, h = _out_proj(a_out, b_out, c_out, lw["w_out"], x2, lw["g_mix_post"], lw["g_ffn_pre"])
    act, conv_g, conv_v = _ffn_up(h, lw["w_up"], lw["conv_w"], lw["conv_b"], conv_prev, batch=batch, seq=seq)
    y = _ffn_down(act, lw["w_down"], x1, lw["g_ffn_post"])

    chunk_start = ((seq - 1) // CHUNK) * CHUNK
    chunk_v = a_v.reshape(batch, seq, A_WIDTH)[:, chunk_start:].reshape(batch, -1, A_HEADS, A_HEAD_DIM)
    conv_new = jnp.concatenate([conv_g, conv_v], axis=-1)
    return y.reshape(batch, seq, D_MODEL), chunk_v, shift_new, wkv_new, conv_new


def kernel(x_prompt, x_sample, mem_prompt, cache_mem_k, cache_mem_v, state_shift, state_wkv, state_conv,
           norm_mix_pre, norm_mix_post, norm_ffn_pre, norm_ffn_post, norm_mem, w_in, w_out, w_mem_k, w_mem_v,
           gm_ln_g, gm_ln_b, gm_ws, gm_bs, rk_mu, rk_w0, rk_w2, rk_a0, rk_a2, rk_g2, rk_kk, rk_ka, rk_rk,
           rk_lnx_g, rk_lnx_b, ffn_w_up, ffn_conv_w, ffn_conv_b, ffn_w_down):
    params = dict(
        norm_mix_pre=norm_mix_pre, norm_mix_post=norm_mix_post, norm_ffn_pre=norm_ffn_pre,
        norm_ffn_post=norm_ffn_post, norm_mem=norm_mem, w_in=w_in, w_out=w_out, w_mem_k=w_mem_k,
        w_mem_v=w_mem_v, gm_ln_g=gm_ln_g, gm_ln_b=gm_ln_b, gm_ws=gm_ws, gm_bs=gm_bs, rk_mu=rk_mu,
        rk_w0=rk_w0, rk_w2=rk_w2, rk_a0=rk_a0, rk_a2=rk_a2, rk_g2=rk_g2, rk_kk=rk_kk, rk_ka=rk_ka,
        rk_rk=rk_rk, rk_lnx_g=rk_lnx_g, rk_lnx_b=rk_lnx_b, ffn_w_up=ffn_w_up, ffn_conv_w=ffn_conv_w,
        ffn_conv_b=ffn_conv_b, ffn_w_down=ffn_w_down)
    depth = w_in.shape[0]
    bp = x_prompt.shape[0]
    y_p, y_s = x_prompt, x_sample
    outs = [[] for _ in range(10)]
    for l in range(depth):
        lw = _prepare_layer(params, l)
        mem2 = mem_prompt.reshape(bp * MEM_LEN, D_MODEL)
        mkv = _norm_matmul(mem2, lw["g_mem"], lw["w_mkv"], tm=512, tn=2 * C_WIDTH)
        mk = mkv[:, :C_WIDTH].reshape(bp, MEM_LEN, C_WIDTH)
        mv = mkv[:, C_WIDTH:].reshape(bp, MEM_LEN, C_WIDTH)
        zero_shift = jnp.zeros((bp, B_PROJ), x_prompt.dtype)
        zero_wkv = jnp.zeros((bp, B_HEADS, B_HEAD_DIM, B_HEAD_DIM), F32)
        zero_conv = jnp.zeros((bp, 2, 2 * D_FF), x_prompt.dtype)
        y_p, cv, sh, wkv, conv = _decoder_layer(y_p, mk, mv, zero_shift, zero_wkv, zero_conv, lw)
        mem_shape = (bp, MEM_LEN, C_HEADS, C_HEAD_DIM)
        for lst, val in zip(outs[:6], (mk.reshape(mem_shape), mv.reshape(mem_shape), cv, sh, wkv, conv)):
            lst.append(val)
        n_s = x_sample.shape[0]
        cache_k = cache_mem_k[l].reshape(n_s, MEM_LEN * C_HEADS, C_HEAD_DIM)
        cache_v = cache_mem_v[l].reshape(n_s, MEM_LEN * C_HEADS, C_HEAD_DIM)
        y_s, cv, sh, wkv, conv = _decoder_layer(y_s, cache_k, cache_v, state_shift[l],
                                                state_wkv[l], state_conv[l], lw)
        for lst, val in zip(outs[6:], (cv, sh, wkv, conv)):
            lst.append(val)
    return (y_p, y_s) + tuple(jnp.stack(o) for o in outs)
```

```python
import functools
import math

import jax
import jax.numpy as jnp
from jax import lax
from jax.experimental import pallas as pl
from jax.experimental.pallas import tpu as pltpu

D_MODEL = 2048
MEM_LEN = 256
CHUNK = 128
A_HEADS, A_HEAD_DIM = 4, 128
A_WIDTH = A_HEADS * A_HEAD_DIM
B_HEADS, B_HEAD_DIM = 16, 64
B_WIDTH = B_HEADS * B_HEAD_DIM
LORA = 64
B_PROJ = 3 * B_WIDTH + 3 * LORA
C_HEADS, C_HEAD_DIM = 4, 128
C_WIDTH = C_HEADS * C_HEAD_DIM
D_FF = 5632
RMS_EPS = 1e-6
LN_EPS = 1e-5
GN_EPS = 64e-5
DECAY_OFFSET = 0.5

LANES = 128
SUBLANES = 8
VMEM_LIMIT_BYTES = 56 * 1024 * 1024

LORA_PAD = 256
COL_A = 0
COL_R = 2 * A_WIDTH
MAIN_COLS = COL_R + 3 * B_WIDTH
COL_Q = 0
COL_L = C_WIDTH
TAIL_COLS = COL_L + LORA_PAD
PROJ_TN = 1024

WKV_CHUNK = 64
FFN_N_TILES = 2
FFN_TN = D_FF // FFN_N_TILES
FFN_SUB = 256

F32 = jnp.float32
BF16 = jnp.bfloat16


def _cparams(sem):
    return pltpu.CompilerParams(dimension_semantics=sem, vmem_limit_bytes=VMEM_LIMIT_BYTES)


def _rms(x, g):
    return x * lax.rsqrt(jnp.mean(x * x, axis=-1, keepdims=True) + RMS_EPS) * g


def _norm_matmul_kernel(x_ref, g_ref, w_ref, o_ref, h_ref):
    @pl.when(pl.program_id(1) == 0)
    def _():
        h_ref[...] = _rms(x_ref[...], g_ref[...]).astype(BF16)

    o_ref[...] = jnp.dot(h_ref[...], w_ref[...], preferred_element_type=F32)


def _norm_matmul(x, g, w, *, tm, tn):
    t, k = x.shape
    n = w.shape[1]
    return pl.pallas_call(
        _norm_matmul_kernel,
        grid=(t // tm, n // tn),
        in_specs=[
            pl.BlockSpec((tm, k), lambda i, j: (i, 0)),
            pl.BlockSpec((1, k), lambda i, j: (0, 0)),
            pl.BlockSpec((k, tn), lambda i, j: (0, j)),
        ],
        out_specs=pl.BlockSpec((tm, tn), lambda i, j: (i, j)),
        out_shape=jax.ShapeDtypeStruct((t, n), F32),
        scratch_shapes=[pltpu.VMEM((tm, k), BF16)],
        compiler_params=_cparams(("parallel", "arbitrary")),
        name="norm_matmul",
    )(x, g, w)


def _in_proj_kernel(x_ref, g_ref, wm_ref, wt_ref, om_ref, ot_ref, h_ref, *, n_main):
    j = pl.program_id(1)

    @pl.when(j == 0)
    def _():
        h_ref[...] = _rms(x_ref[...], g_ref[...]).astype(BF16)

    nt = (((1,), (1,)), ((), ()))

    @pl.when(j < n_main)
    def _():
        om_ref[...] = lax.dot_general(h_ref[...], wm_ref[...], nt, preferred_element_type=F32)

    @pl.when(j == n_main)
    def _():
        ot_ref[...] = lax.dot_general(h_ref[...], wt_ref[...], nt, preferred_element_type=F32)


def _in_proj(x, g, w_main, w_tail, *, tm=1024):
    t, k = x.shape
    n_main = MAIN_COLS // PROJ_TN
    main_tile = lambda j: jnp.minimum(j, n_main - 1)
    return pl.pallas_call(
        functools.partial(_in_proj_kernel, n_main=n_main),
        grid=(t // tm, n_main + 1),
        in_specs=[
            pl.BlockSpec((tm, k), lambda i, j: (i, 0)),
            pl.BlockSpec((1, k), lambda i, j: (0, 0)),
            pl.BlockSpec((PROJ_TN, k), lambda i, j: (main_tile(j), 0)),
            pl.BlockSpec((TAIL_COLS, k), lambda i, j: (0, 0)),
        ],
        out_specs=[
            pl.BlockSpec((tm, PROJ_TN), lambda i, j: (i, main_tile(j))),
            pl.BlockSpec((tm, TAIL_COLS), lambda i, j: (i, 0)),
        ],
        out_shape=[jax.ShapeDtypeStruct((t, MAIN_COLS), F32), jax.ShapeDtypeStruct((t, TAIL_COLS), F32)],
        scratch_shapes=[pltpu.VMEM((tm, k), BF16)],
        compiler_params=_cparams(("parallel", "arbitrary")),
        name="in_proj",
    )(x, g, w_main, w_tail)


def _group_a_kernel(u_ref, v_ref, g_ref, b_ref, w_ref, bias_ref, o_ref, vout_ref):
    u = jax.nn.gelu(u_ref[...], approximate=True)
    v = jax.nn.gelu(v_ref[...], approximate=True)
    mean = jnp.mean(v, axis=-1, keepdims=True)
    d = v - mean
    var = jnp.mean(d * d, axis=-1, keepdims=True)
    vn = d * lax.rsqrt(var + LN_EPS) * g_ref[...] + b_ref[...]
    vout_ref[...] = vn
    vb = vn.astype(BF16)
    for c in range(u.shape[0] // CHUNK):
        rs = slice(c * CHUNK, (c + 1) * CHUNK)
        for h in range(A_HEADS):
            hs = slice(h * A_HEAD_DIM, (h + 1) * A_HEAD_DIM)
            mixed = jnp.dot(w_ref[h], vb[rs, hs], preferred_element_type=F32) + bias_ref[:, hs]
            o_ref[rs, hs] = (u[rs, hs] * mixed).astype(BF16)


def _group_a(proj, ln_g, ln_b, w_sp, bias_full, *, rows=8 * CHUNK):
    t = proj.shape[0]
    return pl.pallas_call(
        _group_a_kernel,
        grid=(t // rows,),
        in_specs=[
            pl.BlockSpec((rows, A_WIDTH), lambda i: (i, COL_A // A_WIDTH)),
            pl.BlockSpec((rows, A_WIDTH), lambda i: (i, COL_A // A_WIDTH + 1)),
            pl.BlockSpec((1, A_WIDTH), lambda i: (0, 0)),
            pl.BlockSpec((1, A_WIDTH), lambda i: (0, 0)),
            pl.BlockSpec((A_HEADS, CHUNK, CHUNK), lambda i: (0, 0, 0)),
            pl.BlockSpec((CHUNK, A_WIDTH), lambda i: (0, 0)),
        ],
        out_specs=[
            pl.BlockSpec((rows, A_WIDTH), lambda i: (i, 0)),
            pl.BlockSpec((rows, A_WIDTH), lambda i: (i, 0)),
        ],
        out_shape=[
            jax.ShapeDtypeStruct((t, A_WIDTH), BF16),
            jax.ShapeDtypeStruct((t, A_WIDTH), F32),
        ],
        compiler_params=_cparams(("parallel",)),
        name="group_a",
    )(proj, proj, ln_g, ln_b, w_sp, bias_full)


def _softmax_rows(s):
    e = jnp.exp(s - jnp.max(s, axis=-1, keepdims=True))
    return e / jnp.sum(e, axis=-1, keepdims=True)


def _attn_kernel(q_ref, k_ref, v_ref, o_ref, *, n_b, tq, head_major_rows):
    scale = C_HEAD_DIM ** -0.5
    heads = range(C_HEADS)
    lanes = lambda h: slice(h * C_HEAD_DIM, (h + 1) * C_HEAD_DIM)
    if not head_major_rows:
        chains = [(g, h) for g in range(n_b) for h in heads]
        rows = lambda g: slice(g * tq, (g + 1) * tq)
        s = [lax.dot_general(q_ref[rows(g), lanes(h)].astype(BF16), k_ref[g, :, lanes(h)].astype(BF16), _NT,
                             preferred_element_type=F32) * scale for g, h in chains]
        p = [_softmax_rows(m).astype(BF16) for m in s]
        o = [jnp.dot(m, v_ref[g, :, lanes(h)].astype(BF16), preferred_element_type=F32)
             for m, (g, h) in zip(p, chains)]
        for m, (g, h) in zip(o, chains):
            o_ref[rows(g), lanes(h)] = m.astype(BF16)
        return
    row = lax.broadcasted_iota(jnp.int32, (C_HEADS * tq, 1), 0)
    col = lax.broadcasted_iota(jnp.int32, (1, C_HEADS * MEM_LEN), 1)
    own = (col & (C_HEADS - 1)) == (row >> int(math.log2(tq)))
    q = [q_ref[g * tq:(g + 1) * tq, :] for g in range(n_b)]
    qs = [jnp.concatenate([m[:, lanes(h)] for h in heads], axis=0).astype(BF16) for m in q]
    s = [lax.dot_general(m, k_ref[g].astype(BF16), _NT, preferred_element_type=F32) * scale
         for g, m in enumerate(qs)]
    p = [_softmax_rows(jnp.where(own, m, -1e30)).astype(BF16) for m in s]
    o = [jnp.dot(m, v_ref[g].astype(BF16), preferred_element_type=F32) for g, m in enumerate(p)]
    for g, m in enumerate(o):
        for h in heads:
            o_ref[g * tq:(g + 1) * tq, lanes(h)] = m[h * tq:(h + 1) * tq].astype(BF16)


def _attention(proj, mem_k, mem_v, *, batch, seq):
    head_major_rows = mem_k.shape[-1] == C_HEAD_DIM
    if head_major_rows:
        tq, n_b = seq, 8
    else:
        tq, n_b = 1024, 1
    n_q = seq // tq
    mem_blk = (n_b,) + mem_k.shape[1:]
    return pl.pallas_call(
        functools.partial(_attn_kernel, n_b=n_b, tq=tq, head_major_rows=head_major_rows),
        grid=(batch // n_b, n_q),
        in_specs=[
            pl.BlockSpec((n_b * tq, C_WIDTH), lambda b, i: (b * n_q + i, COL_Q // C_WIDTH)),
            pl.BlockSpec(mem_blk, lambda b, i: (b, 0, 0)),
            pl.BlockSpec(mem_blk, lambda b, i: (b, 0, 0)),
        ],
        out_specs=pl.BlockSpec((n_b * tq, C_WIDTH), lambda b, i: (b * n_q + i, 0)),
        out_shape=jax.ShapeDtypeStruct((batch * seq, C_WIDTH), BF16),
        compiler_params=_cparams(("parallel", "arbitrary")),
        name="mem_attention",
    )(proj, mem_k, mem_v)


_V_MU_R, _V_MU_K, _V_MU_V, _V_W0, _V_A0, _V_KK, _V_KA, _V_RK, _V_LNG, _V_LNB = range(10)


_NN = (((1,), (0,)), ((), ()))
_NT = (((1,), (1,)), ((), ()))
_TN = (((0,), (0,)), ((), ()))
HEAD_PAIRS = B_HEADS // 2


def _split(x):
    hi = x.astype(BF16)
    return hi, (x - hi.astype(F32)).astype(BF16)


def _b(x):
    return x.astype(BF16)


def _bdot(a, b, dims=_NN):
    return lax.dot_general(a, b, dims, preferred_element_type=F32)


def _rwkv_kernel(pr_ref, pk_ref, pv_ref, pl_ref, sr_ref, sk_ref, sv_ref, sl_ref, wkv_ref,
                 vec_ref, mul_ref, w2_ref, a2_ref, g2_ref, seg_ref,
                 o_ref, so_ref, nr_ref, nk_ref, nv_ref, nl_ref,
                 cr_ref, ck_ref, cv_ref, cl_ref, sbd_ref, *, n_blk, n_sc, groups, chunk, whole_seq):
    rows = groups * chunk
    n_items = n_blk * n_sc
    block_of = lambda w: w % n_blk
    rows_of = lambda w: slice((w // n_blk) * rows, (w // n_blk + 1) * rows)
    first = pl.program_id(1) == 0
    last = pl.program_id(1) == pl.num_programs(1) - 1
    hd = B_HEAD_DIM
    n_seq = n_blk * groups

    @pl.when(first)
    def _():
        zero = jnp.zeros((hd, hd), F32)
        for g in range(n_seq):
            for q in range(HEAD_PAIRS):
                top = jnp.concatenate([wkv_ref[g, 2 * q], zero], axis=1)
                bot = jnp.concatenate([zero, wkv_ref[g, 2 * q + 1]], axis=1)
                sbd_ref[g, q] = jnp.concatenate([top, bot], axis=0)
        if not whole_seq:
            for s_ref, carry_ref in ((sr_ref, cr_ref), (sk_ref, ck_ref), (sv_ref, cv_ref), (sl_ref, cl_ref)):
                for blk in range(n_blk):
                    carry_ref[blk] = jnp.broadcast_to(s_ref[blk], (SUBLANES, s_ref.shape[-1]))

    row_id = lax.broadcasted_iota(jnp.int32, (rows, 1), 0)
    col_id = lax.broadcasted_iota(jnp.int32, (1, rows), 1)
    chunk_bits = int(math.log2(chunk))
    incl_b = (((row_id >> chunk_bits) == (col_id >> chunk_bits)) & (col_id <= row_id)).astype(BF16)
    vec = lambda i: vec_ref[i:i + 1, :]
    seg_ones = seg_ref[...]

    def seg_sum(x):
        slabs = jnp.concatenate([x[:, q * LANES:(q + 1) * LANES] for q in range(HEAD_PAIRS)], axis=0)
        hi, lo = _split(slabs)
        s = (jnp.dot(hi, seg_ones, preferred_element_type=F32)
             + jnp.dot(lo, seg_ones, preferred_element_type=F32))
        return jnp.concatenate([s[q * rows:(q + 1) * rows] for q in range(HEAD_PAIRS)], axis=1)

    def prologue(item, out):
        blk, rs = block_of(item), rows_of(item)
        seqs = slice(blk * groups, (blk + 1) * groups)

        def prev_rows(x, s_ref, carry_ref):
            width = x.shape[1]
            if whole_seq:
                start = jnp.broadcast_to(s_ref[seqs], (groups, chunk, width)).reshape(rows, width)
                return jnp.where((row_id & (chunk - 1)) == 0, start, pltpu.roll(x, 1, 0))

            ext = jnp.concatenate([carry_ref[blk], x], axis=0)
            prev = pltpu.roll(ext, 1, 0)[SUBLANES:]
            carry_ref[blk] = x[rows - SUBLANES:]
            return prev

        def shifted(p_ref, s_ref, carry_ref, n_ref, mu):
            x = p_ref[blk, rs]
            if whole_seq:
                n_ref[seqs] = x.reshape(groups, chunk, x.shape[1])[:, chunk - 1:, :]
            else:
                n_ref[blk] = x[rows - 1:]
            return x + (prev_rows(x, s_ref, carry_ref) - x) * mu

        lo = shifted(pl_ref, sl_ref, cl_ref, nl_ref, mul_ref[0:1, :])
        dw = jnp.dot(jnp.tanh(lo).astype(BF16), w2_ref[...], preferred_element_type=F32)
        da = jnp.dot(lo.astype(BF16), a2_ref[...], preferred_element_type=F32)
        gate = jnp.dot(jax.nn.sigmoid(lo).astype(BF16), g2_ref[...], preferred_element_type=F32)
        yield
        w_log = -jax.nn.softplus(-(vec(_V_W0) + dw)) - DECAY_OFFSET
        log_decay = -jnp.exp(w_log)
        ld_hi, ld_mid = _split(log_decay)
        ld_lo = (log_decay - ld_hi.astype(F32) - ld_mid.astype(F32)).astype(BF16)
        cum = (jnp.dot(incl_b, ld_hi, preferred_element_type=F32)
               + jnp.dot(incl_b, ld_mid, preferred_element_type=F32)
               + jnp.dot(incl_b, ld_lo, preferred_element_type=F32))
        yield
        a = jax.nn.sigmoid(vec(_V_A0) + da)
        k = shifted(pk_ref, sk_ref, ck_ref, nk_ref, vec(_V_MU_K))
        kk = k * vec(_V_KK)
        kk = kk / jnp.maximum(jnp.sqrt(seg_sum(kk * kk)), 1e-12)
        yield
        p_incl = jnp.exp(cum)
        p_inv = jnp.exp(-cum)
        a_t = -kk * jnp.exp(cum - log_decay)
        b_t = kk * a * p_inv
        yield
        k = k * (1.0 + (a - 1.0) * vec(_V_KA))
        k_t = k * p_inv
        r = shifted(pr_ref, sr_ref, cr_ref, nr_ref, vec(_V_MU_R))
        r_t = r * p_incl
        yield
        v = shifted(pv_ref, sv_ref, cv_ref, nv_ref, vec(_V_MU_V))
        bonus = seg_sum(r * k * vec(_V_RK)) * v
        out.update(a_t=a_t, b_t=b_t, k_t=k_t, r_t=r_t, v=v, p_incl=p_incl, bonus=bonus, gate=gate)

    pr = 2 * rows
    row2 = lax.broadcasted_iota(jnp.int32, (pr, 1), 0)
    col2 = lax.broadcasted_iota(jnp.int32, (1, pr), 1)
    t2, s2 = row2 & (rows - 1), col2 & (rows - 1)
    same2 = ((row2 >> chunk_bits) == (col2 >> chunk_bits))
    incl2 = same2 & (s2 <= t2)
    strict2 = same2 & (s2 < t2)
    eye2 = (row2 == col2).astype(F32)
    left = lax.broadcasted_iota(jnp.int32, (rows, LANES), 1) < hd

    def bd(x):
        zero = jnp.zeros_like(x)
        return jnp.concatenate([jnp.where(left, x, zero), jnp.where(left, zero, x)], axis=0)

    def group_rows(mats, g):
        starts = [hh * rows + g * chunk for hh in (0, 1)]
        return jnp.concatenate([m[i:i + chunk] for m in mats for i in starts], axis=0)

    n_sq = chunk_bits - 1
    col4 = lax.broadcasted_iota(jnp.int32, (1, 2 * pr), 1)
    incl4 = ((row2 >> chunk_bits) == ((col4 & (pr - 1)) >> chunk_bits)) & ((col4 & (rows - 1)) <= t2)
    pairs = range(HEAD_PAIRS)
    lanes_of = [slice(q * LANES, (q + 1) * LANES) for q in pairs]

    def recurrence(item, pro, qs):
        s0 = block_of(item) * groups
        p_incl = pro["p_incl"]
        pairs = range(len(qs))
        bds = [[bd(pro[name][:, lanes_of[q]]) for name in ("a_t", "r_t", "b_t", "k_t", "v")] for q in qs]
        ar_s = [_b(jnp.concatenate([m[0], m[1]], axis=0)) for m in bds]
        bk_s = [_b(jnp.concatenate([m[2], m[3]], axis=0)) for m in bds]
        v_s = [_b(m[4]) for m in bds]
        if groups == 1:
            ms = [_bdot(ar_s[q], jnp.concatenate([bk_s[q], _b(sbd_ref[s0, qs[q]])], axis=0), _NT)
                  for q in pairs]
        else:
            ms = [_bdot(ar_s[q], bk_s[q], _NT) for q in pairs]
        yield
        a_ab = [jnp.where(strict2, m[:pr, :pr], 0.0) for m in ms]
        a_ak = [jnp.where(strict2, m[:pr, pr:2 * pr], 0.0) for m in ms]
        a_r = [jnp.where(incl4, m[pr:, :2 * pr], 0.0) for m in ms]
        inv = [eye2 + n for n in a_ab]
        pw_s = [_b(_bdot(_b(n), _b(n))) for n in a_ab]
        yield
        for _ in range(1, n_sq):
            both = [_bdot(p, jnp.concatenate([p, _b(i)], axis=1)) for p, i in zip(pw_s, inv)]
            inv = [i + m[:, pr:] for i, m in zip(inv, both)]
            pw_s = [_b(m[:, :pr]) for m in both]
            yield
        inv = [i + _bdot(p, _b(i)) for i, p in zip(inv, pw_s)]
        yield
        if groups == 1:
            x0, y0 = [m[:pr, 2 * pr:] for m in ms], [m[pr:, 2 * pr:] for m in ms]
        else:
            x0, y0 = [], []
            for q in pairs:
                x_parts, y_parts = [None] * (2 * groups), [None] * (2 * groups)
                for g in range(groups):
                    xy = _bdot(_b(group_rows(bds[q][:2], g)), _b(sbd_ref[s0 + g, qs[q]]), _NT)
                    for hh in (0, 1):
                        x_parts[hh * groups + g] = xy[hh * chunk:(hh + 1) * chunk]
                        y_parts[hh * groups + g] = xy[(2 + hh) * chunk:(3 + hh) * chunk]
                x0.append(jnp.concatenate(x_parts, axis=0))
                y0.append(jnp.concatenate(y_parts, axis=0))
        yield
        rhs = [_b(x0[q] + _bdot(_b(a_ak[q]), v_s[q])) for q in pairs]
        u = [_bdot(_b(inv[q]), rhs[q]) for q in pairs]
        yield
        uv_s = [jnp.concatenate([_b(u[q]), v_s[q]], axis=0) for q in pairs]
        y2 = [y0[q] + _bdot(_b(a_r[q]), uv_s[q]) for q in pairs]
        for q in pairs:
            pro["y"][qs[q]] = y2[q][:rows] + y2[q][rows:]
        yield
        for q in pairs:
            ls = lanes_of[qs[q]]
            if groups == 1:
                ds = _bdot(uv_s[q], bk_s[q], _TN)
                sbd_ref[s0, qs[q]] = (sbd_ref[s0, qs[q]] + ds) * p_incl[rows - 1:rows, ls]
            else:
                for g in range(groups):
                    ds = _bdot(_b(group_rows((u[q], bds[q][4]), g)),
                               _b(group_rows((bds[q][2], bds[q][3]), g)), _TN)
                    end = (g + 1) * chunk - 1
                    sbd_ref[s0 + g, qs[q]] = (sbd_ref[s0 + g, qs[q]] + ds) * p_incl[end:end + 1, ls]

    def finish(item, pro):
        inv_n = 1.0 / B_HEAD_DIM
        y = jnp.concatenate([pro["y"][q] for q in range(HEAD_PAIRS)], axis=1)
        d = y - seg_sum(y) * inv_n
        yield
        var = seg_sum(d * d) * inv_n
        yield
        yn = d * lax.rsqrt(var + GN_EPS) * vec(_V_LNG) + vec(_V_LNB)
        o_ref[block_of(item), rows_of(item)] = ((yn + pro["bonus"]) * pro["gate"]).astype(BF16)

    def interleave(*gens):
        live = list(gens)
        while live:
            for gen in list(live):
                if next(gen, StopIteration) is StopIteration:
                    live.remove(gen)

    pros = [dict(y={}) for _ in range(n_items)]
    interleave(prologue(0, pros[0]))
    for item in range(n_items):
        work = [recurrence(item, pros[item], list(pairs))]
        if item + 1 < n_items:
            work.append(prologue(item + 1, pros[item + 1]))
        if item > 0:
            work.append(finish(item - 1, pros[item - 1]))
        interleave(*work)
    interleave(finish(n_items - 1, pros[n_items - 1]))

    @pl.when(last)
    def _():
        for g in range(n_seq):
            for q in range(HEAD_PAIRS):
                sq = sbd_ref[g, q]
                so_ref[g, 2 * q] = sq[:hd, :hd]
                so_ref[g, 2 * q + 1] = sq[hd:, hd:]


def _rwkv(proj, proj_tail, shift_parts, wkv_prev, lw, *, batch, seq):
    whole_seq = seq <= WKV_CHUNK
    if whole_seq:
        assert seq == SUBLANES, "whole-sequence blocks rely on one sublane tile per sequence"
        chunk, groups = seq, WKV_CHUNK // seq
        n_blk, n_sc, n_chunks = 2, 1, 1
        n_outer = batch // (groups * n_blk)
        lead = (batch // groups, WKV_CHUNK)
    else:
        chunk, groups = WKV_CHUNK, 1
        n_blk, n_sc, n_outer = batch, 2, 1
        n_chunks = seq // (n_sc * WKV_CHUNK)
        lead = (batch, seq)
    proj3, tail3 = proj.reshape(lead + (MAIN_COLS,)), proj_tail.reshape(lead + (TAIL_COLS,))
    n_seq = n_blk * groups
    act = lambda col_blk, width: pl.BlockSpec((n_blk, n_sc * WKV_CHUNK, width), lambda o, c: (o, c, col_blk))
    per_seq = lambda *tail: pl.BlockSpec((n_seq,) + tail, lambda o, c: (o,) + (0,) * len(tail))
    fixed = lambda *shape: pl.BlockSpec(shape, lambda o, c: (0,) * len(shape))
    state = (B_HEADS, B_HEAD_DIM, B_HEAD_DIM)
    kern = functools.partial(_rwkv_kernel, n_blk=n_blk, n_sc=n_sc, groups=groups, chunk=chunk,
                             whole_seq=whole_seq)
    shift_specs = [per_seq(1, B_WIDTH), per_seq(1, B_WIDTH), per_seq(1, B_WIDTH), per_seq(1, LORA_PAD)]
    b_out, wkv_new, *last_rows = pl.pallas_call(
        kern,
        grid=(n_outer, n_chunks),
        in_specs=[
            act(COL_R // B_WIDTH, B_WIDTH), act(COL_R // B_WIDTH + 1, B_WIDTH),
            act(COL_R // B_WIDTH + 2, B_WIDTH), act(COL_L // LORA_PAD, LORA_PAD),
            *shift_specs,
            per_seq(*state),
            fixed(16, B_WIDTH), fixed(SUBLANES, LORA_PAD),
            fixed(LORA_PAD, B_WIDTH), fixed(LORA_PAD, B_WIDTH), fixed(LORA_PAD, B_WIDTH),
            fixed(LANES, LANES),
        ],
        out_specs=[act(0, B_WIDTH), per_seq(*state), *shift_specs],
        out_shape=[
            jax.ShapeDtypeStruct(lead + (B_WIDTH,), BF16),
            jax.ShapeDtypeStruct((batch,) + state, F32),
            *(jax.ShapeDtypeStruct(p.shape, F32) for p in shift_parts),
        ],
        scratch_shapes=[
            pltpu.VMEM((n_blk, SUBLANES, B_WIDTH), F32),
            pltpu.VMEM((n_blk, SUBLANES, B_WIDTH), F32),
            pltpu.VMEM((n_blk, SUBLANES, B_WIDTH), F32),
            pltpu.VMEM((n_blk, SUBLANES, LORA_PAD), F32),
            pltpu.VMEM((n_seq, HEAD_PAIRS, LANES, LANES), F32),
        ],
        compiler_params=_cparams(("parallel", "arbitrary")),
        name="rwkv7",
    )(proj3, proj3, proj3, tail3, *shift_parts, wkv_prev,
      lw["rk_vecs"], lw["rk_mu_l"], lw["rk_w2"], lw["rk_a2"], lw["rk_g2"], lw["seg_ones"])
    shift_new = jnp.concatenate([p[:, 0] for p in last_rows[:3]] + [last_rows[3][:, 0, :3 * LORA]], axis=1)
    return b_out.reshape(batch * seq, B_WIDTH), wkv_new, shift_new


def _out_proj_kernel(a_ref, b_ref, c_ref, w_ref, x_ref, gpost_ref, gffn_ref, x1_ref, h_ref):
    n_sub = 4
    sub = x_ref.shape[0] // n_sub
    halves = [slice(s * sub, (s + 1) * sub) for s in range(n_sub)]

    def project(rs):
        acc = jnp.dot(a_ref[rs, :], w_ref[0:A_WIDTH, :], preferred_element_type=F32)
        acc += jnp.dot(b_ref[rs, :], w_ref[A_WIDTH:A_WIDTH + B_WIDTH, :], preferred_element_type=F32)
        return acc + jnp.dot(c_ref[rs, :], w_ref[A_WIDTH + B_WIDTH:, :], preferred_element_type=F32)

    accs = [project(rs) for rs in halves]
    for rs, acc in zip(halves, accs):
        x1 = x_ref[rs, :] + _rms(acc, gpost_ref[...])
        x1_ref[rs, :] = x1
        h_ref[rs, :] = _rms(x1, gffn_ref[...]).astype(BF16)


def _out_proj(a_out, b_out, c_out, w_out, x, g_post, g_ffn, *, tm=512):
    t = x.shape[0]
    row = lambda i: (i, 0)
    fixed = lambda i: (0, 0)
    return pl.pallas_call(
        _out_proj_kernel,
        grid=(t // tm,),
        in_specs=[
            pl.BlockSpec((tm, A_WIDTH), row),
            pl.BlockSpec((tm, B_WIDTH), row),
            pl.BlockSpec((tm, C_WIDTH), row),
            pl.BlockSpec((D_MODEL, D_MODEL), fixed, pipeline_mode=pl.Buffered(1)),
            pl.BlockSpec((tm, D_MODEL), row),
            pl.BlockSpec((1, D_MODEL), fixed),
            pl.BlockSpec((1, D_MODEL), fixed),
        ],
        out_specs=[pl.BlockSpec((tm, D_MODEL), row), pl.BlockSpec((tm, D_MODEL), row)],
        out_shape=[jax.ShapeDtypeStruct((t, D_MODEL), F32), jax.ShapeDtypeStruct((t, D_MODEL), BF16)],
        compiler_params=_cparams(("parallel",)),
        name="out_proj",
    )(a_out, b_out, c_out, w_out, x, g_post, g_ffn)


def _chunked_dots(h_ref, w_refs, cs, between):
    accs = [None] * len(w_refs)
    for kc in range(h_ref.shape[1] // FFN_SUB):
        ks = slice(kc * FFN_SUB, (kc + 1) * FFN_SUB)
        for n, w_ref in enumerate(w_refs):
            part = jnp.dot(h_ref[:, ks], w_ref[ks, cs], preferred_element_type=F32)
            accs[n] = part if accs[n] is None else accs[n] + part
        between(kc)
    return accs


def _ffn_up_kernel(h_ref, wg_ref, wv_ref, cwg_ref, cwv_ref, cbg_ref, cbv_ref, pg_ref, pv_ref,
                   act_ref, ng_ref, nv_ref, *, tm, seq):
    h = h_ref[...]
    tn = cwg_ref.shape[1]
    n_seq = tm // seq
    tau = lax.broadcasted_iota(jnp.int32, (tm, 1), 0) & (seq - 1)

    def conv(up, cs, cw_ref, cb_ref, p_ref, n_ref):
        width = up.shape[1]
        prev = p_ref[:, :, cs]
        e0 = jnp.broadcast_to(prev[:, 0:1, :], (n_seq, seq, width)).reshape(tm, width)
        e1 = jnp.broadcast_to(prev[:, 1:2, :], (n_seq, seq, width)).reshape(tm, width)
        m1 = jnp.where(tau == 0, e1, pltpu.roll(up, 1, 0))
        m2 = jnp.where(tau == 0, e0, jnp.where(tau == 1, e1, pltpu.roll(up, 2, 0)))
        n_ref[:, :, cs] = up.reshape(n_seq, seq, width)[:, seq - 2:, :]
        return cb_ref[:, cs] + m2 * cw_ref[0:1, cs] + m1 * cw_ref[1:2, cs] + up * cw_ref[2:3, cs]

    subs = [slice(s * FFN_SUB, (s + 1) * FFN_SUB) for s in range(tn // FFN_SUB)]
    ups = [(jnp.dot(h, wg_ref[:, cs], preferred_element_type=F32),
            jnp.dot(h, wv_ref[:, cs], preferred_element_type=F32)) for cs in subs]
    for cs, (up_g, up_v) in zip(subs, ups):
        gate = conv(up_g, cs, cwg_ref, cbg_ref, pg_ref, ng_ref)
        val = conv(up_v, cs, cwv_ref, cbv_ref, pv_ref, nv_ref)
        act_ref[:, cs] = (jax.nn.gelu(gate, approximate=True) * val).astype(BF16)


def _ffn_up_skew_kernel(h_ref, wg_ref, wv_ref, cwg_ref, cwv_ref, cbg_ref, cbv_ref, pg_ref, pv_ref,
                        act_ref, ng_ref, nv_ref, cg_ref, cv_ref, u0_ref, u1_ref,
                        *, tm, tiles_per_seq, n_row_tiles):
    i = pl.program_id(1)
    tn = cwg_ref.shape[1]
    n_sub = tn // FFN_SUB

    @pl.when(i == 0)
    def _():
        for ref in (u1_ref, cg_ref, cv_ref):
            ref[...] = jnp.zeros_like(ref)

    @pl.when(i % tiles_per_seq == 1 % tiles_per_seq)
    def _():
        for carry_ref, p_ref in ((cg_ref, pg_ref), (cv_ref, pv_ref)):
            carry_ref[...] = jnp.concatenate([jnp.zeros((SUBLANES - 2, tn), F32), p_ref[0]], axis=0)

    piece = 64

    def conv(u_ref, r0, us, cs, cw_ref, cb_ref, carry_ref):
        if r0 == 0:
            ext = jnp.concatenate([carry_ref[:, cs], u_ref[0:piece, us]], axis=0)
        else:
            ext = u_ref[r0 - SUBLANES:r0 + piece, us]
        m1 = pltpu.roll(ext, 1, 0)[SUBLANES:]
        m2 = pltpu.roll(ext, 2, 0)[SUBLANES:]
        return (cb_ref[:, cs] + m2 * cw_ref[0:1, cs] + m1 * cw_ref[1:2, cs]
                + ext[SUBLANES:] * cw_ref[2:3, cs])

    n_k = D_MODEL // FFN_SUB
    chunks_per_piece = n_k // (tm // piece)

    def step(new_ref, old_ref):
        for s in range(n_sub):
            cs = slice(s * FFN_SUB, (s + 1) * FFN_SUB)
            gs, vs = cs, slice(tn + s * FFN_SUB, tn + (s + 1) * FFN_SUB)

            def conv_piece(kc):
                if (kc + 1) % chunks_per_piece == 0:
                    r0 = (kc // chunks_per_piece) * piece
                    gate = conv(old_ref, r0, gs, cs, cwg_ref, cbg_ref, cg_ref)
                    val = conv(old_ref, r0, vs, cs, cwv_ref, cbv_ref, cv_ref)
                    act_ref[r0:r0 + piece, cs] = (jax.nn.gelu(gate, approximate=True) * val).astype(BF16)

            if new_ref is None:
                for kc in range(n_k):
                    conv_piece(kc)
            else:
                new_ref[:, gs], new_ref[:, vs] = _chunked_dots(h_ref, (wg_ref, wv_ref), cs, conv_piece)
            for us, carry_ref, n_ref in ((gs, cg_ref, ng_ref), (vs, cv_ref, nv_ref)):
                carry_ref[:, cs] = old_ref[tm - SUBLANES:, us]
                n_ref[0, :, cs] = old_ref[tm - 2:, us]

    bufs = (u0_ref, u1_ref)

    @pl.when((i % 2 == 0) & (i < n_row_tiles))
    def _():
        step(u0_ref, u1_ref)

    @pl.when((i % 2 == 1) & (i < n_row_tiles))
    def _():
        step(u1_ref, u0_ref)

    @pl.when(i == n_row_tiles)
    def _():
        step(None, bufs[(n_row_tiles - 1) % 2])


def _ffn_up_skew(h, w_up, conv_w, conv_b, conv_prev, *, batch, seq, tm=256):
    t = h.shape[0]
    tn, n_tiles = FFN_TN, FFN_N_TILES
    n_m = t // tm
    tiles_per_seq = seq // tm
    prev_tile = lambda i: jnp.maximum(i - 1, 0)
    col = lambda half: (lambda j, i: (0, j + half * n_tiles))
    state_blk = (1, 2, tn)
    state_idx = lambda half: (lambda j, i: (prev_tile(i) // tiles_per_seq, 0, j + half * n_tiles))
    state_out = jax.ShapeDtypeStruct((batch, 2, D_FF), F32)
    up_buf = pltpu.VMEM((tm, 2 * tn), F32)
    return pl.pallas_call(
        functools.partial(_ffn_up_skew_kernel, tm=tm, tiles_per_seq=tiles_per_seq, n_row_tiles=n_m),
        grid=(n_tiles, n_m + 1),
        in_specs=[
            pl.BlockSpec((tm, D_MODEL), lambda j, i: (jnp.minimum(i, n_m - 1), 0)),
            pl.BlockSpec((D_MODEL, tn), col(0), pipeline_mode=pl.Buffered(1)),
            pl.BlockSpec((D_MODEL, tn), col(1), pipeline_mode=pl.Buffered(1)),
            pl.BlockSpec((3, tn), col(0)),
            pl.BlockSpec((3, tn), col(1)),
            pl.BlockSpec((1, tn), col(0)),
            pl.BlockSpec((1, tn), col(1)),
            pl.BlockSpec(state_blk, state_idx(0)),
            pl.BlockSpec(state_blk, state_idx(1)),
        ],
        out_specs=[
            pl.BlockSpec((tm, tn), lambda j, i: (prev_tile(i), j)),
            pl.BlockSpec(state_blk, state_idx(0)),
            pl.BlockSpec(state_blk, state_idx(0)),
        ],
        out_shape=[jax.ShapeDtypeStruct((t, D_FF), BF16), state_out, state_out],
        scratch_shapes=[pltpu.VMEM((SUBLANES, tn), F32), pltpu.VMEM((SUBLANES, tn), F32), up_buf, up_buf],
        compiler_params=_cparams(("parallel", "arbitrary")),
        name="ffn_up_conv",
    )(h, w_up, w_up, conv_w, conv_w, conv_b, conv_b, conv_prev, conv_prev)


def _ffn_up(h, w_up, conv_w, conv_b, conv_prev, *, batch, seq):
    t = h.shape[0]
    if seq > SUBLANES:
        return _ffn_up_skew(h, w_up, conv_w, conv_b, conv_prev, batch=batch, seq=seq)
    assert seq == SUBLANES, "whole-sequence tiles rely on one sublane tile per sequence"
    tm, tn = t, 2 * FFN_SUB
    n_tiles = D_FF // tn
    state_blk = (batch, 2, tn)
    state_idx = lambda half: (lambda j, i: (0, 0, j + half * n_tiles))
    col = lambda half: (lambda j, i: (0, j + half * n_tiles))
    kern = functools.partial(_ffn_up_kernel, tm=tm, seq=seq)
    state_out = jax.ShapeDtypeStruct((batch, 2, D_FF), F32)
    return pl.pallas_call(
        kern,
        grid=(n_tiles, t // tm),
        in_specs=[
            pl.BlockSpec((tm, D_MODEL), lambda j, i: (i, 0)),
            pl.BlockSpec((D_MODEL, tn), col(0)),
            pl.BlockSpec((D_MODEL, tn), col(1)),
            pl.BlockSpec((3, tn), col(0)),
            pl.BlockSpec((3, tn), col(1)),
            pl.BlockSpec((1, tn), col(0)),
            pl.BlockSpec((1, tn), col(1)),
            pl.BlockSpec(state_blk, state_idx(0)),
            pl.BlockSpec(state_blk, state_idx(1)),
        ],
        out_specs=[
            pl.BlockSpec((tm, tn), lambda j, i: (i, j)),
            pl.BlockSpec(state_blk, state_idx(0)),
            pl.BlockSpec(state_blk, state_idx(0)),
        ],
        out_shape=[jax.ShapeDtypeStruct((t, D_FF), BF16), state_out, state_out],
        compiler_params=_cparams(("parallel", "arbitrary")),
        name="ffn_up_conv",
    )(h, w_up, w_up, conv_w, conv_w, conv_b, conv_b, conv_prev, conv_prev)


def _ffn_down_kernel(a_ref, w_ref, x_ref, g_ref, o_ref):
    acc = jnp.dot(a_ref[...], w_ref[...], preferred_element_type=F32)
    o_ref[...] = x_ref[...] + _rms(acc, g_ref[...])


def _ffn_down(act, w_down, x1, g_post, *, tm=256):
    t = x1.shape[0]
    return pl.pallas_call(
        _ffn_down_kernel,
        grid=(t // tm,),
        in_specs=[
            pl.BlockSpec((tm, D_FF), lambda i: (i, 0)),
            pl.BlockSpec((D_FF, D_MODEL), lambda i: (0, 0), pipeline_mode=pl.Buffered(1)),
            pl.BlockSpec((tm, D_MODEL), lambda i: (i, 0)),
            pl.BlockSpec((1, D_MODEL), lambda i: (0, 0)),
        ],
        out_specs=pl.BlockSpec((tm, D_MODEL), lambda i: (i, 0)),
        out_shape=jax.ShapeDtypeStruct((t, D_MODEL), F32),
        compiler_params=_cparams(("parallel",)),
        name="ffn_down",
    )(act, w_down, x1, g_post)


def _pad_rows(w, row0, total):
    return jnp.zeros((total, w.shape[1]), w.dtype).at[row0:row0 + w.shape[0]].set(w)


def _prepare_layer(p, l):
    w_in_t = p["w_in"][l].T.astype(BF16)
    q0 = MAIN_COLS + 3 * LORA
    w_in_tail = jnp.concatenate([
        w_in_t[q0:], w_in_t[MAIN_COLS:q0], jnp.zeros((LORA_PAD - 3 * LORA, D_MODEL), BF16)], axis=0)
    mu = p["rk_mu"][l]
    vec_rows = [mu[:B_WIDTH], mu[B_WIDTH:2 * B_WIDTH], mu[2 * B_WIDTH:3 * B_WIDTH], p["rk_w0"][l],
                p["rk_a0"][l], p["rk_kk"][l], p["rk_ka"][l], p["rk_rk"][l].reshape(B_WIDTH),
                p["rk_lnx_g"][l], p["rk_lnx_b"][l]]
    vecs = jnp.zeros((16, B_WIDTH), F32).at[:len(vec_rows)].set(jnp.stack(vec_rows))
    mu_l = jnp.zeros((SUBLANES, LORA_PAD), F32).at[0, :3 * LORA].set(mu[3 * B_WIDTH:])
    head_of = jnp.arange(LANES) // B_HEAD_DIM
    seg_ones = (head_of[:, None] == head_of[None, :]).astype(BF16)
    tril = jnp.tril(jnp.ones((CHUNK, CHUNK), bool))
    ws = jnp.where(tril[None], p["gm_ws"][l], 0.0)
    bs = p["gm_bs"][l]
    return {
        "g_mix_pre": p["norm_mix_pre"][l][None], "g_mix_post": p["norm_mix_post"][l][None],
        "g_ffn_pre": p["norm_ffn_pre"][l][None], "g_ffn_post": p["norm_ffn_post"][l][None],
        "g_mem": p["norm_mem"][l][None],
        "w_in": w_in_t, "w_in_tail": w_in_tail,
        "w_out": p["w_out"][l].astype(BF16),
        "w_mkv": jnp.concatenate([p["w_mem_k"][l], p["w_mem_v"][l]], axis=1).astype(BF16),
        "gm_ln_g": p["gm_ln_g"][l][None], "gm_ln_b": p["gm_ln_b"][l][None],
        "gm_ws": ws, "gm_bs": bs,
        "rk_vecs": vecs, "rk_mu_l": mu_l,
        "rk_w2": _pad_rows(p["rk_w2"][l], 0, LORA_PAD).astype(BF16),
        "rk_a2": _pad_rows(p["rk_a2"][l], LORA, LORA_PAD).astype(BF16),
        "rk_g2": _pad_rows(p["rk_g2"][l], 2 * LORA, LORA_PAD).astype(BF16),
        "seg_ones": seg_ones,
        "w_up": p["ffn_w_up"][l].astype(BF16),
        "conv_w": p["ffn_conv_w"][l], "conv_b": p["ffn_conv_b"][l][None],
        "w_down": p["ffn_w_down"][l].astype(BF16),
    }


def _spatial_weights(lw, seq):
    ws, bs = lw["gm_ws"], lw["gm_bs"]
    if seq >= CHUNK:
        w_blk, b_rows = ws, bs
    else:
        reps = CHUNK // seq
        eye = jnp.eye(reps, dtype=ws.dtype)
        w_blk = jnp.einsum("ab,hts->hatbs", eye, ws[:, :seq, :seq]).reshape(A_HEADS, CHUNK, CHUNK)
        b_rows = jnp.tile(bs[:, :seq], (1, reps))
    bias_full = jnp.repeat(b_rows.T, A_HEAD_DIM, axis=1)
    return w_blk.astype(BF16), bias_full


def _decoder_layer(x, mem_k, mem_v, shift_prev, wkv_prev, conv_prev, lw):
    batch, seq, _ = x.shape
    x2 = x.reshape(batch * seq, D_MODEL)
    proj, proj_tail = _in_proj(x2, lw["g_mix_pre"], lw["w_in"], lw["w_in_tail"])
    w_sp, bias_full = _spatial_weights(lw, seq)
    a_out, a_v = _group_a(proj, lw["gm_ln_g"], lw["gm_ln_b"], w_sp, bias_full)
    shift_parts = [
        shift_prev[:, None, :B_WIDTH], shift_prev[:, None, B_WIDTH:2 * B_WIDTH],
        shift_prev[:, None, 2 * B_WIDTH:3 * B_WIDTH],
        jnp.pad(shift_prev[:, None, 3 * B_WIDTH:], ((0, 0), (0, 0), (0, LORA_PAD - 3 * LORA))),
    ]
    b_out, wkv_new, shift_new = _rwkv(proj, proj_tail, shift_parts, wkv_prev, lw, batch=batch, seq=seq)
    c_out = _attention(proj_tail, mem_k, mem_v, batch=batch, seq=seq)
    x1, h = _out_proj(a_out, b_out, c_out, lw["w_out"], x2, lw["g_mix_post"], lw["g_ffn_pre"])
    act, conv_g, conv_v = _ffn_up(h, lw["w_up"], lw["conv_w"], lw["conv_b"], conv_prev, batch=batch, seq=seq)
    y = _ffn_down(act, lw["w_down"], x1, lw["g_ffn_post"])

    chunk_start = ((seq - 1) // CHUNK) * CHUNK
    chunk_v = a_v.reshape(batch, seq, A_WIDTH)[:, chunk_start:].reshape(batch, -1, A_HEADS, A_HEAD_DIM)
    conv_new = jnp.concatenate([conv_g, conv_v], axis=-1)
    return y.reshape(batch, seq, D_MODEL), chunk_v, shift_new, wkv_new, conv_new


def kernel(x_prompt, x_sample, mem_prompt, cache_mem_k, cache_mem_v, state_shift, state_wkv, state_conv,
           norm_mix_pre, norm_mix_post, norm_ffn_pre, norm_ffn_post, norm_mem, w_in, w_out, w_mem_k, w_mem_v,
           gm_ln_g, gm_ln_b, gm_ws, gm_bs, rk_mu, rk_w0, rk_w2, rk_a0, rk_a2, rk_g2, rk_kk, rk_ka, rk_rk,
           rk_lnx_g, rk_lnx_b, ffn_w_up, ffn_conv_w, ffn_conv_b, ffn_w_down):
    params = dict(
        norm_mix_pre=norm_mix_pre, norm_mix_post=norm_mix_post, norm_ffn_pre=norm_ffn_pre,
        norm_ffn_post=norm_ffn_post, norm_mem=norm_mem, w_in=w_in, w_out=w_out, w_mem_k=w_mem_k,
        w_mem_v=w_mem_v, gm_ln_g=gm_ln_g, gm_ln_b=gm_ln_b, gm_ws=gm_ws, gm_bs=gm_bs, rk_mu=rk_mu,
        rk_w0=rk_w0, rk_w2=rk_w2, rk_a0=rk_a0, rk_a2=rk_a2, rk_g2=rk_g2, rk_kk=rk_kk, rk_ka=rk_ka,
        rk_rk=rk_rk, rk_lnx_g=rk_lnx_g, rk_lnx_b=rk_lnx_b, ffn_w_up=ffn_w_up, ffn_conv_w=ffn_conv_w,
        ffn_conv_b=ffn_conv_b, ffn_w_down=ffn_w_down)
    depth = w_in.shape[0]
    bp = x_prompt.shape[0]
    y_p, y_s = x_prompt, x_sample
    outs = [[] for _ in range(10)]
    for l in range(depth):
        lw = _prepare_layer(params, l)
        mem2 = mem_prompt.reshape(bp * MEM_LEN, D_MODEL)
        mkv = _norm_matmul(mem2, lw["g_mem"], lw["w_mkv"], tm=512, tn=2 * C_WIDTH)
        mk = mkv[:, :C_WIDTH].reshape(bp, MEM_LEN, C_WIDTH)
        mv = mkv[:, C_WIDTH:].reshape(bp, MEM_LEN, C_WIDTH)
        zero_shift = jnp.zeros((bp, B_PROJ), x_prompt.dtype)
        zero_wkv = jnp.zeros((bp, B_HEADS, B_HEAD_DIM, B_HEAD_DIM), F32)
        zero_conv = jnp.zeros((bp, 2, 2 * D_FF), x_prompt.dtype)
        y_p, cv, sh, wkv, conv = _decoder_layer(y_p, mk, mv, zero_shift, zero_wkv, zero_conv, lw)
        mem_shape = (bp, MEM_LEN, C_HEADS, C_HEAD_DIM)
        for lst, val in zip(outs[:6], (mk.reshape(mem_shape), mv.reshape(mem_shape), cv, sh, wkv, conv)):
            lst.append(val)
        n_s = x_sample.shape[0]
        cache_k = cache_mem_k[l].reshape(n_s, MEM_LEN * C_HEADS, C_HEAD_DIM)
        cache_v = cache_mem_v[l].reshape(n_s, MEM_LEN * C_HEADS, C_HEAD_DIM)
        y_s, cv, sh, wkv, conv = _decoder_layer(y_s, cache_k, cache_v, state_shift[l],
                                                state_wkv[l], state_conv[l], lw)
        for lst, val in zip(outs[6:], (cv, sh, wkv, conv)):
            lst.append(val)
    return (y_p, y_s) + tuple(jnp.stack(o) for o in outs)
```
